```python
import jax, jax.numpy as jnp
from jax import lax
import numpy as np

D_MODEL = 1024
BATCH = 16
SEQ = 256
DEPTH = 4
DEC_BATCH = 2
DEC_SEQ = 1024
PAST_LEN = 512

GRID_W = 64
HEAD_DIM = 64
BRANCH_W = D_MODEL // 2
GLA_HEADS = 4
GLA_DV = BRANCH_W // GLA_HEADS
GLA_DK = GLA_DV // 2
GLA_LOWRANK = 16
GLA_TAU = 16.0
GLA_CHUNK = 32
SWA_Q_HEADS = BRANCH_W // HEAD_DIM
SWA_KV_HEADS = 2
SWA_GROUP = SWA_Q_HEADS // SWA_KV_HEADS
SWA_WINDOW = 128
SWA_BLOCK = 128
NA_HEADS = BRANCH_W // HEAD_DIM
NA_ROWS = 8
NA_COLS = 16
MLP_HIDDEN = 4 * D_MODEL
ROPE_BASE = 10000.0
EPS = 1e-6
Q_BLOCK = 128

SPLIT_SIZES = (
    GLA_HEADS * GLA_DK, GLA_HEADS * GLA_DK, GLA_HEADS * GLA_DV, GLA_HEADS * GLA_DV,
    GLA_LOWRANK, GLA_LOWRANK,
    SWA_Q_HEADS * HEAD_DIM, SWA_KV_HEADS * HEAD_DIM, SWA_KV_HEADS * HEAD_DIM,
    NA_HEADS * HEAD_DIM, NA_HEADS * HEAD_DIM, NA_HEADS * HEAD_DIM,
    D_MODEL, D_MODEL, D_MODEL,
)
IN_W = sum(SPLIT_SIZES)

kernel_name = "hybrid_prefix_diffusion_trunk_step"


def rms_norm(x, g):
    xf = x.astype(jnp.float32)
    y = xf * lax.rsqrt(jnp.mean(xf * xf, axis=-1, keepdims=True) + EPS)
    return (y * g.astype(jnp.float32)).astype(x.dtype)


def split_proj(z):
    idx = np.cumsum(np.array(SPLIT_SIZES))[:-1].tolist()
    return jnp.split(z, idx, axis=-1)


def axial_rope(x):
    n = x.shape[1]
    t = jnp.arange(n)
    row = (t // GRID_W).astype(jnp.float32)
    col = (t % GRID_W).astype(jnp.float32)
    half = x.shape[-1] // 2
    nf = half // 2
    inv_freq = ROPE_BASE ** (-jnp.arange(nf, dtype=jnp.float32) / nf)

    def rot(xh, pos):
        ang = pos[:, None] * inv_freq[None, :]
        cos = jnp.cos(ang)[None, :, None, :].astype(x.dtype)
        sin = jnp.sin(ang)[None, :, None, :].astype(x.dtype)
        x1, x2 = xh[..., :nf], xh[..., nf:]
        return jnp.concatenate([x1 * cos - x2 * sin, x1 * sin + x2 * cos], axis=-1)

    return jnp.concatenate([rot(x[..., :half], row), rot(x[..., half:], col)], axis=-1)


def softmax_attend(q, parts, sink):
    scale = q.shape[-1] ** -0.5
    scores = []
    for k, _, bias in parts:
        s = jnp.einsum('bqhgd,bkhd->bhgqk', q, k).astype(jnp.float32) * scale
        if bias is not None:
            s = s + bias
        scores.append(s)
    if sink is not None:
        scores.append(jnp.broadcast_to(sink.astype(jnp.float32)[None, :, :, None, None], scores[0].shape[:-1] + (1,)))
    p = jax.nn.softmax(jnp.concatenate(scores, axis=-1), axis=-1)
    out = None
    off = 0
    for k, v, _ in parts:
        tk = k.shape[1]
        o = jnp.einsum('bhgqk,bkhd->bqhgd', p[..., off:off + tk].astype(v.dtype), v)
        out = o if out is None else out + o
        off += tk
    return out


def dense_ctx_attention(q, k, v, sink):
    b, s = q.shape[:2]
    nb = s // Q_BLOCK
    qb = q.reshape((b, nb, Q_BLOCK) + q.shape[2:]).swapaxes(0, 1)
    out = lax.map(lambda qi: softmax_attend(qi, [(k, v, None)], sink), qb)
    return out.swapaxes(0, 1).reshape(q.shape)


def windowed_attention(q, k, v, k_ctx, v_ctx, sink):
    b, n = q.shape[:2]
    nb = n // SWA_BLOCK
    span = SWA_BLOCK + 2 * SWA_WINDOW
    pad = ((0, 0), (SWA_WINDOW, SWA_WINDOW), (0, 0), (0, 0))
    kp = jnp.pad(k, pad)
    vp = jnp.pad(v, pad)
    i = jnp.arange(SWA_BLOCK)[:, None]
    j = jnp.arange(span)[None, :]
    rel = j - i
    band = (rel >= 0) & (rel <= 2 * SWA_WINDOW)

    def blk(args):
        qi, nidx = args
        start = nidx * SWA_BLOCK
        ki = lax.dynamic_slice_in_dim(kp, start, span, axis=1)
        vi = lax.dynamic_slice_in_dim(vp, start, span, axis=1)
        kpos = start - SWA_WINDOW + j
        valid = band & (kpos >= 0) & (kpos < n)
        bias = jnp.where(valid, 0.0, -jnp.inf).astype(jnp.float32)
        return softmax_attend(qi, [(ki, vi, bias), (k_ctx, v_ctx, None)], sink)

    qb = q.reshape((b, nb, SWA_BLOCK) + q.shape[2:]).swapaxes(0, 1)
    out = lax.map(blk, (qb, jnp.arange(nb)))
    return out.swapaxes(0, 1).reshape(q.shape)


def neighbourhood_attention(q, k, v, k_ctx, v_ctx, rpb):
    b, n, h, dh = q.shape
    rows = n // GRID_W
    wr = min(NA_ROWS, rows)
    qg = q.reshape(b, rows, GRID_W, h, 1, dh).swapaxes(0, 1)
    kg = k.reshape(b, rows, GRID_W, h, dh)
    vg = v.reshape(b, rows, GRID_W, h, dh)
    cq = jnp.arange(GRID_W)[:, None]
    ck = jnp.arange(GRID_W)[None, :]
    cs = jnp.clip(cq - NA_COLS // 2, 0, GRID_W - NA_COLS)
    col_ok = (ck >= cs) & (ck < cs + NA_COLS)
    dc_idx = jnp.clip(ck - cq + NA_COLS - 1, 0, 2 * NA_COLS - 2)

    def row_blk(args):
        qi, r = args
        rs = jnp.clip(r - NA_ROWS // 2, 0, rows - wr)
        ki = lax.dynamic_slice_in_dim(kg, rs, wr, axis=1).reshape(b, wr * GRID_W, h, dh)
        vi = lax.dynamic_slice_in_dim(vg, rs, wr, axis=1).reshape(b, wr * GRID_W, h, dh)
        dr_idx = rs + jnp.arange(wr) - r + NA_ROWS - 1
        bias = rpb[:, dr_idx[None, :, None], dc_idx[:, None, :]].astype(jnp.float32)
        bias = jnp.where(col_ok[:, None, :], bias, -jnp.inf).reshape(h, GRID_W, wr * GRID_W)
        return softmax_attend(qi, [(ki, vi, bias[:, None]), (k_ctx, v_ctx, None)], None)

    out = lax.map(row_blk, (qg, jnp.arange(rows)))
    return out.swapaxes(0, 1).reshape(b, n, h, dh)


def gla_chunked(q, k, v, log_a, s0):
    b, n, h, dk = q.shape
    dv = v.shape[-1]
    c = GLA_CHUNK
    nc = n // c
    f32 = jnp.float32
    qc = q.astype(f32).reshape(b, nc, c, h, dk)
    kc = k.astype(f32).reshape(b, nc, c, h, dk)
    vc = v.astype(f32).reshape(b, nc, c, h, dv)
    cum = jnp.cumsum(log_a.astype(f32).reshape(b, nc, c, h, dk), axis=2)
    tri = jnp.tril(jnp.ones((c, c), dtype=bool))
    diff = cum[:, :, :, None] - cum[:, :, None, :]
    decay = jnp.exp(jnp.where(tri[None, None, :, :, None, None], diff, -jnp.inf))
    attn = jnp.einsum('bnthd,bnshd,bntshd->bnhts', qc, kc, decay)
    o_intra = jnp.einsum('bnhts,bnshv->bnthv', attn, vc)
    last = cum[:, :, -1]
    q_in = qc * jnp.exp(cum)
    k_in = kc * jnp.exp(last[:, :, None] - cum)
    a_last = jnp.exp(last)

    def step(state, xs):
        qi, ki, vi, ai = xs
        o = jnp.einsum('bthk,bhkv->bthv', qi, state)
        state = state * ai[..., None] + jnp.einsum('bthk,bthv->bhkv', ki, vi)
        return state, o

    xs = (q_in.swapaxes(0, 1), k_in.swapaxes(0, 1), vc.swapaxes(0, 1), a_last.swapaxes(0, 1))
    s_fin, o_inter = lax.scan(step, s0.astype(f32), xs)
    o = o_intra + o_inter.swapaxes(0, 1)
    return o.reshape(b, n, h, dv).astype(v.dtype), s_fin


def mixer(h, l, w, ctx):
    b, n, _ = h.shape
    (aq, ak, av, ar, alf, alb, bq, bk, bv, cq, ck, cv, ga, gb, gc) = split_proj(h @ w['w_in'][l])

    qa = aq.reshape(b, n, GLA_HEADS, GLA_DK) * (GLA_DK ** -0.5)
    ka = ak.reshape(b, n, GLA_HEADS, GLA_DK)
    va = av.reshape(b, n, GLA_HEADS, GLA_DV)
    log_f = (jax.nn.log_sigmoid((alf @ w['w_a2_f'][l] + w['b_a_f'][l]).astype(jnp.float32)) / GLA_TAU).reshape(b, n, GLA_HEADS, GLA_DK)
    log_b = (jax.nn.log_sigmoid((alb @ w['w_a2_b'][l] + w['b_a_b'][l]).astype(jnp.float32)) / GLA_TAU).reshape(b, n, GLA_HEADS, GLA_DK)
    if ctx is None:
        s0 = jnp.zeros((b, 2, GLA_HEADS, GLA_DK, GLA_DV), jnp.float32)
    else:
        s0 = ctx[0]
    o_f, s_f = gla_chunked(qa, ka, va, log_f, s0[:, 0])
    o_b, s_b = gla_chunked(jnp.flip(qa, 1), jnp.flip(ka, 1), jnp.flip(va, 1), jnp.flip(log_b, 1), s0[:, 1])
    o_a = rms_norm(o_f + jnp.flip(o_b, 1), w['gla_onorm'][l]).reshape(b, n, GLA_HEADS * GLA_DV) * jax.nn.silu(ar)

    qb = rms_norm(bq.reshape(b, n, SWA_Q_HEADS, HEAD_DIM), w['qn_swa'][l])
    kb = rms_norm(bk.reshape(b, n, SWA_KV_HEADS, HEAD_DIM), w['kn_swa'][l])
    vb = bv.reshape(b, n, SWA_KV_HEADS, HEAD_DIM)
    sink = w['sink_swa'][l].reshape(SWA_KV_HEADS, SWA_GROUP)
    if ctx is None:
        o_b_attn = dense_ctx_attention(qb.reshape(b, n, SWA_KV_HEADS, SWA_GROUP, HEAD_DIM), kb, vb, sink)
    else:
        qr = axial_rope(qb).reshape(b, n, SWA_KV_HEADS, SWA_GROUP, HEAD_DIM)
        o_b_attn = windowed_attention(qr, axial_rope(kb), vb, ctx[1], ctx[2], sink)

    qn = rms_norm(cq.reshape(b, n, NA_HEADS, HEAD_DIM), w['qn_na'][l])
    kn = rms_norm(ck.reshape(b, n, NA_HEADS, HEAD_DIM), w['kn_na'][l])
    vn = cv.reshape(b, n, NA_HEADS, HEAD_DIM)
    if ctx is None:
        o_c = dense_ctx_attention(qn[:, :, :, None, :], kn, vn, None)[:, :, :, 0]
    else:
        o_c = neighbourhood_attention(qn, kn, vn, ctx[3], ctx[4], w['rpb_na'][l])

    ya = o_a @ w['w_pa'][l]
    yb = o_b_attn.reshape(b, n, SWA_Q_HEADS * HEAD_DIM) @ w['w_pb'][l]
    yc = o_c.reshape(b, n, NA_HEADS * HEAD_DIM) @ w['w_pc'][l]
    merged = jax.nn.sigmoid(ga) * ya + jax.nn.sigmoid(gb) * yb + jax.nn.sigmoid(gc) * yc
    out = merged @ w['w_o'][l]
    if ctx is None:
        return out, (jnp.stack([s_f, s_b], axis=1), kb, vb, kn, vn)
    return out, None


def trunk_layer(x, cond, l, w, ctx):
    mod = jax.nn.silu(cond) @ w['w_mod'][l] + w['b_mod'][l]
    sh1, sc1, g1, sh2, sc2, g2 = jnp.split(mod[:, None, :], 6, axis=-1)
    h = rms_norm(x, w['norm1'][l]) * (1 + sc1) + sh1
    mix, new_ctx = mixer(h, l, w, ctx)
    x = x + g1 * mix
    h = rms_norm(x, w['norm2'][l]) * (1 + sc2) + sh2
    x = x + g2 * (jnp.square(jax.nn.relu(h @ w['w_fc1'][l])) @ w['w_fc2'][l])
    return x, new_ctx


def setup_inputs(seed: int = 0) -> dict:
    key = jax.random.key(seed)
    ks = jax.random.split(key, 32)

    def nrm(k, shape, scale):
        return jax.random.normal(k, shape, jnp.float32) * scale

    d = D_MODEL
    return {
        "x_prompt": nrm(ks[0], (BATCH, SEQ, d), 1.0),
        "x_sample": nrm(ks[1], (DEC_BATCH, DEC_SEQ, d), 1.0),
        "state_gla": nrm(ks[2], (DEC_BATCH, DEPTH, 2, GLA_HEADS, GLA_DK, GLA_DV), 0.5),
        "cache_swa_k": nrm(ks[3], (DEC_BATCH, DEPTH, PAST_LEN, SWA_KV_HEADS, HEAD_DIM), 1.0),
        "cache_swa_v": nrm(ks[4], (DEC_BATCH, DEPTH, PAST_LEN, SWA_KV_HEADS, HEAD_DIM), 1.0),
        "cache_na_k": nrm(ks[5], (DEC_BATCH, DEPTH, PAST_LEN, NA_HEADS, HEAD_DIM), 1.0),
        "cache_na_v": nrm(ks[6], (DEC_BATCH, DEPTH, PAST_LEN, NA_HEADS, HEAD_DIM), 1.0),
        "c": nrm(ks[7], (DEC_BATCH, d), 1.0),
        "c_ctx": nrm(ks[8], (d,), 1.0),
        "w_mod": nrm(ks[9], (DEPTH, d, 6 * d), d ** -0.5),
        "b_mod": nrm(ks[10], (DEPTH, 6 * d), 0.02),
        "norm1": 1.0 + nrm(ks[11], (DEPTH, d), 0.05),
        "norm2": 1.0 + nrm(ks[12], (DEPTH, d), 0.05),
        "w_in": nrm(ks[13], (DEPTH, d, IN_W), d ** -0.5),
        "w_a2_f": nrm(ks[14], (DEPTH, GLA_LOWRANK, GLA_HEADS * GLA_DK), GLA_LOWRANK ** -0.5),
        "b_a_f": nrm(ks[15], (DEPTH, GLA_HEADS * GLA_DK), 0.1),
        "w_a2_b": nrm(ks[16], (DEPTH, GLA_LOWRANK, GLA_HEADS * GLA_DK), GLA_LOWRANK ** -0.5),
        "b_a_b": nrm(ks[17], (DEPTH, GLA_HEADS * GLA_DK), 0.1),
        "gla_onorm": 1.0 + nrm(ks[18], (DEPTH, GLA_DV), 0.05),
        "qn_swa": 1.0 + nrm(ks[19], (DEPTH, HEAD_DIM), 0.05),
        "kn_swa": 1.0 + nrm(ks[20], (DEPTH, HEAD_DIM), 0.05),
        "sink_swa": nrm(ks[21], (DEPTH, SWA_Q_HEADS), 0.5),
        "qn_na": 1.0 + nrm(ks[22], (DEPTH, HEAD_DIM), 0.05),
        "kn_na": 1.0 + nrm(ks[23], (DEPTH, HEAD_DIM), 0.05),
        "rpb_na": nrm(ks[24], (DEPTH, NA_HEADS, 2 * NA_ROWS - 1, 2 * NA_COLS - 1), 0.5),
        "w_pa": nrm(ks[25], (DEPTH, GLA_HEADS * GLA_DV, d), (GLA_HEADS * GLA_DV) ** -0.5),
        "w_pb": nrm(ks[26], (DEPTH, SWA_Q_HEADS * HEAD_DIM, d), (SWA_Q_HEADS * HEAD_DIM) ** -0.5),
        "w_pc": nrm(ks[27], (DEPTH, NA_HEADS * HEAD_DIM, d), (NA_HEADS * HEAD_DIM) ** -0.5),
        "w_o": nrm(ks[28], (DEPTH, d, d), d ** -0.5),
        "w_fc1": nrm(ks[29], (DEPTH, d, MLP_HIDDEN), d ** -0.5),
        "w_fc2": nrm(ks[30], (DEPTH, MLP_HIDDEN, d), MLP_HIDDEN ** -0.5),
    }


def reference(x_prompt, x_sample, state_gla, cache_swa_k, cache_swa_v, cache_na_k, cache_na_v, c,
              c_ctx, w_mod, b_mod, norm1, norm2, w_in, w_a2_f, b_a_f, w_a2_b, b_a_b, gla_onorm,
              qn_swa, kn_swa, sink_swa, qn_na, kn_na, rpb_na, w_pa, w_pb, w_pc, w_o, w_fc1, w_fc2):
    w = dict(w_mod=w_mod, b_mod=b_mod, norm1=norm1, norm2=norm2, w_in=w_in, w_a2_f=w_a2_f, b_a_f=b_a_f,
             w_a2_b=w_a2_b, b_a_b=b_a_b, gla_onorm=gla_onorm, qn_swa=qn_swa, kn_swa=kn_swa,
             sink_swa=sink_swa, qn_na=qn_na, kn_na=kn_na, rpb_na=rpb_na, w_pa=w_pa, w_pb=w_pb,
             w_pc=w_pc, w_o=w_o, w_fc1=w_fc1, w_fc2=w_fc2)

    xp = x_prompt
    st_l, bk_l, bv_l, nk_l, nv_l = [], [], [], [], []
    for l in range(DEPTH):
        xp, (st, bk, bv, nk, nv) = trunk_layer(xp, c_ctx[None, :], l, w, None)
        st_l.append(st)
        bk_l.append(bk)
        bv_l.append(bv)
        nk_l.append(nk)
        nv_l.append(nv)

    xs = x_sample
    for l in range(DEPTH):
        ctx = (state_gla[:, l], cache_swa_k[:, l], cache_swa_v[:, l], cache_na_k[:, l], cache_na_v[:, l])
        xs, _ = trunk_layer(xs, c, l, w, ctx)

    new_state_gla = jnp.stack(st_l, axis=1)
    new_swa_k = jnp.stack(bk_l, axis=1)
    new_swa_v = jnp.stack(bv_l, axis=1)
    new_na_k = jnp.stack(nk_l, axis=1)
    new_na_v = jnp.stack(nv_l, axis=1)
    return (xp, xs, new_state_gla, new_swa_k, new_swa_v, new_na_k, new_na_v)
```

```python
import functools

import numpy as np
import jax
import jax.numpy as jnp
from jax import lax
from jax.experimental import pallas as pl
from jax.experimental.pallas import tpu as pltpu

F32 = jnp.float32
BF16 = jnp.bfloat16

D_MODEL = 1024
DEPTH = 4
N_CTX_SEQ = 16
CTX_LEN = 256
N_LAT_SEQ = 2
LAT_LEN = 1024
PAST_LEN = 512
T_CTX = N_CTX_SEQ * CTX_LEN
T_LAT = N_LAT_SEQ * LAT_LEN
T_ALL = T_CTX + T_LAT
GRID_W = 64
HEAD_DIM = 64
GLA_HEADS = 4
GLA_DK = 64
GLA_DV = 128
GLA_LOWRANK = 16
GLA_TAU = 16.0
SWA_Q_HEADS = 8
SWA_KV_HEADS = 2
SWA_GROUP = 4
SWA_WINDOW = 128
NA_HEADS = 8
NA_ROWS = 8
NA_COLS = 16
MLP_HIDDEN = 4 * D_MODEL
ROPE_BASE = 10000.0
EPS = 1e-6

Z_GATES = 0
Z_GLA = 3072
Z_AQ, Z_AK, Z_AV, Z_AR = 3072, 3328, 3584, 4096
Z_ATT = 4608
ATT_W = 2304
A_BQ, A_BK, A_BV, A_CQ, A_CK, A_CV = 0, 512, 640, 768, 1280, 1792
Z_LR = 6912
Z_W = 7168

GLA_C = 128
GLA_LEVELS = 7
GLA_GROWS = (GLA_LEVELS + 2) * GLA_C + 16

VMEM_LIMIT = 56 * 1024 * 1024


def _cparams(sem):
    return pltpu.CompilerParams(dimension_semantics=sem, vmem_limit_bytes=VMEM_LIMIT)


def _sigmoid(x):
    return 1.0 / (1.0 + jnp.exp(-x))


def _silu(x):
    return x * _sigmoid(x)


def _nt_dot(a, b):
    return lax.dot_general(a, b, (((1,), (1,)), ((), ())), preferred_element_type=F32)


def _tn_dot(a, b):
    return lax.dot_general(a, b, (((0,), (0,)), ((), ())), preferred_element_type=F32)


def _dot(a, b):
    return jnp.dot(a, b, preferred_element_type=F32)


def _mod_kernel(cond_ref, w_ref, b_ref, o_ref):
    s = _silu(cond_ref[...]).astype(BF16)
    o_ref[...] = _dot(s, w_ref[...].astype(BF16)) + b_ref[...]


def _modulation(cond8, w_mod, b_mod):
    tn = 1024
    return pl.pallas_call(
        _mod_kernel,
        grid=(DEPTH, 6 * D_MODEL // tn),
        in_specs=[
            pl.BlockSpec((8, D_MODEL), lambda l, j: (0, 0)),
            pl.BlockSpec((None, D_MODEL, tn), lambda l, j: (l, 0, j)),
            pl.BlockSpec((None, 1, tn), lambda l, j: (l, 0, j)),
        ],
        out_specs=pl.BlockSpec((None, 8, tn), lambda l, j: (l, 0, j)),
        out_shape=jax.ShapeDtypeStruct((DEPTH, 8, 6 * D_MODEL), F32),
        compiler_params=_cparams(("arbitrary", "arbitrary")),
        name="modulation",
    )(cond8, w_mod, b_mod.reshape(DEPTH, 1, 6 * D_MODEL))


def _group_of_rows(row_block, rows_per_block):
    first = row_block * rows_per_block
    return jnp.maximum(first - T_CTX + LAT_LEN, 0) // LAT_LEN


def _norm_mod(x, g, shift, scale):
    ms = jnp.mean(x * x, axis=-1, keepdims=True)
    y = x * lax.rsqrt(ms + EPS) * g
    return y * (1.0 + scale) + shift


def _inproj_kernel(x_ref, g_ref, mod_ref, w_ref, z_ref, h_scr):
    @pl.when(pl.program_id(1) == 0)
    def _():
        h = _norm_mod(x_ref[...], g_ref[...],
                      mod_ref[:, 0:D_MODEL], mod_ref[:, D_MODEL:2 * D_MODEL])
        h_scr[...] = h.astype(BF16)

    z_ref[...] = _dot(h_scr[...], w_ref[...])


def _inproj(x, norm_w, mod4, w_in_p, l):
    tm, tn = 1024, 1024
    return pl.pallas_call(
        _inproj_kernel,
        grid=(T_ALL // tm, Z_W // tn),
        in_specs=[
            pl.BlockSpec((tm, D_MODEL), lambda i, j: (i, 0)),
            pl.BlockSpec((None, 1, D_MODEL), lambda i, j: (l, 0, 0)),
            pl.BlockSpec((None, None, 1, 6 * D_MODEL),
                         lambda i, j: (l, _group_of_rows(i, tm), 0, 0)),
            pl.BlockSpec((None, D_MODEL, tn), lambda i, j: (l, 0, j)),
        ],
        out_specs=pl.BlockSpec((tm, tn), lambda i, j: (i, j)),
        out_shape=jax.ShapeDtypeStruct((T_ALL, Z_W), F32),
        scratch_shapes=[pltpu.VMEM((tm, D_MODEL), BF16)],
        compiler_params=_cparams(("arbitrary", "arbitrary")),
        name="inproj",
    )(x, norm_w, mod4, w_in_p)


def _pair_norm(x, w):
    sq = x * x
    low = lax.broadcasted_iota(jnp.int32, (1, 128), 1) < HEAD_DIM
    s_low = jnp.sum(jnp.where(low, sq, 0.0), axis=-1, keepdims=True)
    s_all = jnp.sum(sq, axis=-1, keepdims=True)
    ms = jnp.where(low, s_low, s_all - s_low) * (1.0 / HEAD_DIM)
    return x * lax.rsqrt(ms + EPS) * w


def _rope(x, cos, sin_signed):
    n = x.shape[-1]
    lane = lax.broadcasted_iota(jnp.int32, (1, n), 1)
    first = (lane % 32) < 16
    partner = jnp.where(first, pltpu.roll(x, n - 16, 1), pltpu.roll(x, 16, 1))
    return x * cos + partner * sin_signed


def _prep_kernel(z_ref, wqb_ref, wkb_ref, wqc_ref, wkc_ref, cos_ref, sin_ref,
                 qb_ref, kb_ref, vb_ref, qc_ref, kc_ref, vc_ref, kbf_ref, kcf_ref):
    latent = pl.program_id(0) >= T_CTX // z_ref.shape[0]
    qscale = HEAD_DIM ** -0.5

    def normed(col, width, w_ref):
        tiles = []
        for p in range(width // 128):
            x = z_ref[:, col + 128 * p: col + 128 * (p + 1)]
            tiles.append(_pair_norm(x, w_ref[:, 128 * p:128 * (p + 1)]))
        return tiles

    qb = normed(A_BQ, 512, wqb_ref)
    kb = normed(A_BK, 128, wkb_ref)
    qc = normed(A_CQ, 512, wqc_ref)
    kc = normed(A_CK, 512, wkc_ref)

    vb_ref[...] = z_ref[:, A_BV:A_BV + 128].astype(BF16)
    vc_ref[...] = z_ref[:, A_CV:A_CV + 512].astype(BF16)
    for p in range(4):
        qc_ref[:, 128 * p:128 * (p + 1)] = (qc[p] * qscale).astype(BF16)
        kc_ref[:, 128 * p:128 * (p + 1)] = kc[p].astype(BF16)
        kcf_ref[:, 128 * p:128 * (p + 1)] = kc[p]

    @pl.when(jnp.logical_not(latent))
    def _():
        for p in range(4):
            qb_ref[:, 128 * p:128 * (p + 1)] = (qb[p] * qscale).astype(BF16)
        kb_ref[...] = kb[0].astype(BF16)
        kbf_ref[...] = kb[0]

    @pl.when(latent)
    def _():
        cos = cos_ref[...]
        sin = sin_ref[...]
        for p in range(4):
            qb_ref[:, 128 * p:128 * (p + 1)] = (_rope(qb[p], cos, sin) * qscale).astype(BF16)
        kr = _rope(kb[0], cos, sin)
        kb_ref[...] = kr.astype(BF16)
        kbf_ref[...] = kr


def _attn_prep(z, wqb, wkb, wqc, wkc, cos_t, sin_t):
    tm = 512
    nblk = T_ALL // tm
    lat_blocks = LAT_LEN // tm

    def rope_idx(i):
        return (jnp.maximum(i - T_CTX // tm, 0) % lat_blocks, 0)

    row = lambda w: pl.BlockSpec((tm, w), lambda i: (i, 0))
    cst = lambda w: pl.BlockSpec((1, w), lambda i: (0, 0))
    sds = lambda w, dt: jax.ShapeDtypeStruct((T_ALL, w), dt)
    return pl.pallas_call(
        _prep_kernel,
        grid=(nblk,),
        in_specs=[
            pl.BlockSpec((tm, ATT_W), lambda i: (i, Z_ATT // ATT_W)),
            cst(512), cst(128), cst(512), cst(512),
            pl.BlockSpec((tm, 128), rope_idx),
            pl.BlockSpec((tm, 128), rope_idx),
        ],
        out_specs=[row(512), row(128), row(128), row(512), row(512), row(512), row(128), row(512)],
        out_shape=[sds(512, BF16), sds(128, BF16), sds(128, BF16), sds(512, BF16),
                   sds(512, BF16), sds(512, BF16), sds(128, F32), sds(512, F32)],
        compiler_params=_cparams(("arbitrary",)),
        name="attn_prep",
    )(z, wqb, wkb, wqc, wkc, cos_t, sin_t)


def _softmax_parts(scores, values, sink=None):
    mx = scores[0].max(axis=-1, keepdims=True)
    for s in scores[1:]:
        mx = jnp.maximum(mx, s.max(axis=-1, keepdims=True))
    if sink is not None:
        mx = jnp.maximum(mx, sink)
    den = None
    acc = None
    for s, v in zip(scores, values):
        p = jnp.exp(s - mx)
        d = p.sum(axis=-1, keepdims=True)
        o = _dot(p.astype(BF16), v)
        den = d if den is None else den + d
        acc = o if acc is None else acc + o
    if sink is not None:
        den = den + jnp.exp(sink - mx)
    return acc / den


def _ctx_attn_kernel(sink_ref, qb_ref, kb_ref, vb_ref, qc_ref, kc_ref, vc_ref, ob_ref, oc_ref, *, layer):
    n = qb_ref.shape[0]
    outs = [None] * SWA_Q_HEADS
    for kvh in range(SWA_KV_HEADS):
        k = kb_ref[:, HEAD_DIM * kvh:HEAD_DIM * (kvh + 1)]
        v = vb_ref[:, HEAD_DIM * kvh:HEAD_DIM * (kvh + 1)]
        heads = [SWA_GROUP * kvh + g for g in range(SWA_GROUP)]
        q = jnp.concatenate([qb_ref[:, HEAD_DIM * h:HEAD_DIM * (h + 1)] for h in heads], axis=0)
        sink = jnp.concatenate(
            [jnp.full((n, 1), sink_ref[layer, h], F32) for h in heads], axis=0)
        o = _softmax_parts([_nt_dot(q, k)], [v], sink)
        for g, h in enumerate(heads):
            outs[h] = o[n * g:n * (g + 1)]
    ob_ref[...] = jnp.concatenate(outs, axis=1).astype(BF16)

    outs = []
    for h in range(NA_HEADS):
        sl = slice(HEAD_DIM * h, HEAD_DIM * (h + 1))
        outs.append(_softmax_parts([_nt_dot(qc_ref[:, sl], kc_ref[:, sl])], [vc_ref[:, sl]]))
    oc_ref[...] = jnp.concatenate(outs, axis=1).astype(BF16)


def _ctx_attention(sink, qb, kb, vb, qc, kc, vc, l):
    blk = lambda w: pl.BlockSpec((CTX_LEN, w), lambda b: (b, 0))
    return pl.pallas_call(
        functools.partial(_ctx_attn_kernel, layer=l),
        grid=(N_CTX_SEQ,),
        in_specs=[pl.BlockSpec(memory_space=pltpu.SMEM),
                  blk(512), blk(128), blk(128), blk(512), blk(512), blk(512)],
        out_specs=[blk(512), blk(512)],
        out_shape=[jax.ShapeDtypeStruct((T_CTX, 512), BF16)] * 2,
        compiler_params=_cparams(("arbitrary",)),
        name="ctx_attention",
    )(sink, qb, kb, vb, qc, kc, vc)


SWA_QBLK = 128
SWA_SPAN = SWA_QBLK + 2 * SWA_WINDOW


def _swa_kernel(sink_ref, q_ref, k_ref, v_ref, kc_ref, vc_ref, o_ref, *, layer):
    i = pl.program_id(1)
    start = pl.multiple_of(jnp.clip(SWA_QBLK * (i - 1), 0, LAT_LEN - SWA_SPAN), SWA_QBLK)
    rows = SWA_GROUP * SWA_QBLK
    qpos = SWA_QBLK * i + lax.broadcasted_iota(jnp.int32, (rows, SWA_SPAN), 0) % SWA_QBLK
    kpos = start + lax.broadcasted_iota(jnp.int32, (rows, SWA_SPAN), 1)
    valid = jnp.abs(kpos - qpos) <= SWA_WINDOW
    kwin = k_ref[pl.ds(start, SWA_SPAN), :]
    vwin = v_ref[pl.ds(start, SWA_SPAN), :]
    kctx = kc_ref[...].astype(BF16)
    vctx = vc_ref[...].astype(BF16)
    outs = [None] * SWA_Q_HEADS
    for kvh in range(SWA_KV_HEADS):
        sl = slice(HEAD_DIM * kvh, HEAD_DIM * (kvh + 1))
        heads = [SWA_GROUP * kvh + g for g in range(SWA_GROUP)]
        q = jnp.concatenate([q_ref[:, HEAD_DIM * h:HEAD_DIM * (h + 1)] for h in heads], axis=0)
        sink = jnp.concatenate(
            [jnp.full((SWA_QBLK, 1), sink_ref[layer, h], F32) for h in heads], axis=0)
        s_lat = jnp.where(valid, _nt_dot(q, kwin[:, sl]), -jnp.inf)
        s_ctx = _nt_dot(q, kctx[:, sl])
        o = _softmax_parts([s_lat, s_ctx], [vwin[:, sl], vctx[:, sl]], sink)
        for g, h in enumerate(heads):
            outs[h] = o[SWA_QBLK * g:SWA_QBLK * (g + 1)]
    o_ref[...] = jnp.concatenate(outs, axis=1).astype(BF16)


def _swa_attention(sink, qb, kb, vb, cache_k, cache_v, l):
    nq = LAT_LEN // SWA_QBLK
    lat0 = T_CTX // LAT_LEN
    q0 = T_CTX // SWA_QBLK
    kv = pl.BlockSpec((LAT_LEN, 128), lambda b, i: (lat0 + b, 0))
    cache = pl.BlockSpec((None, None, PAST_LEN, 128), lambda b, i: (b, l, 0, 0))
    return pl.pallas_call(
        functools.partial(_swa_kernel, layer=l),
        grid=(N_LAT_SEQ, nq),
        in_specs=[pl.BlockSpec(memory_space=pltpu.SMEM),
                  pl.BlockSpec((SWA_QBLK, 512), lambda b, i: (q0 + b * nq + i, 0)),
                  kv, kv, cache, cache],
        out_specs=pl.BlockSpec((SWA_QBLK, 512), lambda b, i: (b * nq + i, 0)),
        out_shape=jax.ShapeDtypeStruct((T_LAT, 512), BF16),
        compiler_params=_cparams(("arbitrary", "arbitrary")),
        name="swa_attention",
    )(sink, qb, kb, vb, cache_k, cache_v)


LAT_ROWS = LAT_LEN // GRID_W
NA_WIN_ROWS = min(NA_ROWS, LAT_ROWS)
NA_KEYS = NA_WIN_ROWS * GRID_W


def _na_kernel(q_ref, k_ref, v_ref, kc_ref, vc_ref, bias_ref, o_ref):
    r = pl.program_id(1)
    rs = jnp.clip(r - NA_ROWS // 2, 0, LAT_ROWS - NA_WIN_ROWS)
    start = pl.multiple_of(rs * GRID_W, GRID_W)
    base = rs - r + NA_ROWS - 1
    low = lax.broadcasted_iota(jnp.int32, (1, 128), 1) < HEAD_DIM
    tiles = []
    for p in range(NA_HEADS // 2):
        sl = slice(128 * p, 128 * (p + 1))
        q = q_ref[:, sl]
        kwin = k_ref[pl.ds(start, NA_KEYS), sl]
        vwin = v_ref[pl.ds(start, NA_KEYS), sl]
        kctx = kc_ref[:, sl].astype(BF16)
        vctx = vc_ref[:, sl].astype(BF16)
        halves = []
        for hh in range(2):
            h = 2 * p + hh
            qm = jnp.where(low if hh == 0 else jnp.logical_not(low), q, jnp.zeros_like(q))
            bias = jnp.concatenate(
                [bias_ref[h, base + 2 * w] for w in range(NA_WIN_ROWS // 2)], axis=1)
            s_lat = _nt_dot(qm, kwin) + bias
            s_ctx = _nt_dot(qm, kctx)
            halves.append(_softmax_parts([s_lat, s_ctx], [vwin, vctx]))
        tiles.append(jnp.where(low, halves[0], halves[1]))
    o_ref[...] = jnp.concatenate(tiles, axis=1).astype(BF16)


def _na_attention(qc, kc, vc, cache_k, cache_v, bias_tab, l):
    lat0 = T_CTX // LAT_LEN
    q0 = T_CTX // GRID_W
    kv = pl.BlockSpec((LAT_LEN, 512), lambda b, r: (lat0 + b, 0))
    cache = pl.BlockSpec((None, None, PAST_LEN, 512), lambda b, r: (b, l, 0, 0))
    return pl.pallas_call(
        _na_kernel,
        grid=(N_LAT_SEQ, LAT_ROWS),
        in_specs=[pl.BlockSpec((GRID_W, 512), lambda b, r: (q0 + b * LAT_ROWS + r, 0)),
                  kv, kv, cache, cache,
                  pl.BlockSpec((None, NA_HEADS, 2 * NA_ROWS - 2, GRID_W, 128),
                               lambda b, r: (l, 0, 0, 0, 0))],
        out_specs=pl.BlockSpec((GRID_W, 512), lambda b, r: (b * LAT_ROWS + r, 0)),
        out_shape=jax.ShapeDtypeStruct((T_LAT, 512), BF16),
        compiler_params=_cparams(("arbitrary", "arbitrary")),
        name="na_attention",
    )(qc, kc, vc, cache_k, cache_v, bias_tab)


def _na_bias_tables(rpb):
    cq = np.arange(GRID_W)[:, None]
    ck = np.arange(GRID_W)[None, :]
    cs = np.clip(cq - NA_COLS // 2, 0, GRID_W - NA_COLS)
    col_ok = (ck >= cs) & (ck < cs + NA_COLS)
    dc_idx = np.clip(ck - cq + NA_COLS - 1, 0, 2 * NA_COLS - 2)
    t = rpb[:, :, :, dc_idx]
    t = jnp.where(col_ok, t, -jnp.inf)
    return jnp.concatenate([t[:, :, :-1], t[:, :, 1:]], axis=-1)


def _gla_constants():
    c = GLA_C
    t = np.arange(c)[:, None]
    u = np.arange(c)[None, :]
    blocks = []
    for k in range(GLA_LEVELS):
        b = 1 << k
        m = ((t >> k) | 1) * b - 1
        query = ((t >> k) & 1) == 1
        blocks.append(np.where(query, (u > m) & (u <= t), (u > t) & (u <= m)))
    blocks.append(u <= t)
    blocks.append(u > t)
    blocks.append(np.ones((16, c), bool))
    fwd = np.concatenate(blocks, axis=0).astype(np.float32)
    bwd_blocks = [blk[::-1, ::-1] for blk in blocks]
    bwd = np.concatenate(bwd_blocks, axis=0).astype(np.float32)
    seg = np.stack([fwd, bwd])
    seg = np.concatenate([seg, seg], axis=-1)

    s = np.arange(c)[None, :]
    x = t ^ s
    lev = np.where(x == 0, GLA_LEVELS, np.floor(np.log2(np.maximum(x, 1))).astype(np.int64))
    lev_f = np.where(s <= t, lev, -1)
    lev_b = np.where(s >= t, lev, -1)
    levmap = np.stack([np.tile(lev_f, (1, GLA_HEADS)), np.tile(lev_b, (1, GLA_HEADS))])
    return seg, levmap.astype(np.int32)


def _gla_kernel(q_ref, k_ref, v_ref, lr_ref, w2_ref, ba_ref, seg_ref, lev_ref, s0_ref,
                o_ref, sfin_ref, st_ref):
    c = GLA_C
    d = pl.program_id(0)
    step = pl.program_id(1)
    nchunk = pl.num_programs(1)
    cc = step + d * (nchunk - 1 - 2 * step)
    ctx_chunks = T_CTX // c
    is_lat = cc >= ctx_chunks
    per_seq = jnp.where(is_lat, LAT_LEN // c, CTX_LEN // c)
    pos = jnp.where(is_lat, cc - ctx_chunks, cc) % per_seq
    first = jnp.where(d == 0, pos == 0, pos == per_seq - 1)

    @pl.when(first)
    def _():
        for h in range(GLA_HEADS):
            blk = jnp.where(is_lat, s0_ref[h], 0.0)
            row = [jnp.zeros((GLA_DV, GLA_DK), F32)] * GLA_HEADS
            row[h] = blk
            st_ref[GLA_DV * h:GLA_DV * (h + 1), :] = jnp.concatenate(row, axis=1)

    x = _dot(lr_ref[...].astype(BF16), w2_ref[...]) + ba_ref[...]
    la = (jnp.minimum(x, 0.0) - jnp.log1p(jnp.exp(-jnp.abs(x)))) * (1.0 / GLA_TAU)
    la_hi = la.astype(BF16)
    la_lo = (la - la_hi.astype(F32)).astype(BF16)
    seg = _dot(seg_ref[...], jnp.concatenate([la_hi, la_lo], axis=0))

    q = q_ref[...] * (GLA_DK ** -0.5)
    k = k_ref[...]
    v = v_ref[...].astype(BF16)
    lev = lev_ref[...]
    lane_head = lax.broadcasted_iota(jnp.int32, (1, GLA_HEADS * GLA_DK), 1) // GLA_DK

    def pair_scores(qs, ks):
        ksb = ks.astype(BF16)
        kbd = jnp.concatenate(
            [jnp.where(lane_head == h, ksb, jnp.zeros_like(ksb)) for h in range(GLA_HEADS)], axis=0)
        return _nt_dot(qs.astype(BF16), kbd)

    attn = jnp.where(lev == GLA_LEVELS, pair_scores(q, k), 0.0)
    for lvl in range(GLA_LEVELS):
        e = jnp.exp(seg[c * lvl:c * (lvl + 1)])
        attn = jnp.where(lev == lvl, pair_scores(q * e, k * e), attn)
    attn = attn.astype(BF16)

    cum = seg[c * GLA_LEVELS:c * (GLA_LEVELS + 1)]
    rem = seg[c * (GLA_LEVELS + 1):c * (GLA_LEVELS + 2)]
    tot = seg[c * (GLA_LEVELS + 2):c * (GLA_LEVELS + 2) + 1]
    q_in = (q * jnp.exp(cum)).astype(BF16)
    k_in = (k * jnp.exp(rem)).astype(BF16)
    state = st_ref[...]
    o_inter = _nt_dot(q_in, state.astype(BF16))
    for h in range(GLA_HEADS):
        sl = slice(GLA_DV * h, GLA_DV * (h + 1))
        o_ref[:, sl] = o_inter[:, sl] + _dot(attn[:, c * h:c * (h + 1)], v[:, sl])

    upd = _tn_dot(v, k_in)
    row_head = lax.broadcasted_iota(jnp.int32, (GLA_HEADS * GLA_DV, 1), 0) // GLA_DV
    new_state = state * jnp.exp(tot) + jnp.where(row_head == lane_head, upd, 0.0)
    st_ref[...] = new_state
    for h in range(GLA_HEADS):
        sfin_ref[h] = new_state[GLA_DV * h:GLA_DV * (h + 1), GLA_DK * h:GLA_DK * (h + 1)]


def _gla(z, w2p, ba, seg, levmap, s0t, l):
    c = GLA_C
    nchunk = T_ALL // c
    ctx_chunks = T_CTX // c
    nseq = N_CTX_SEQ + N_LAT_SEQ

    def cc_of(d, s):
        return s + d * (nchunk - 1 - 2 * s)

    def seq_of(d, s):
        cc = cc_of(d, s)
        return jnp.where(cc < ctx_chunks, cc // (CTX_LEN // c),
                         N_CTX_SEQ + (cc - ctx_chunks) // (LAT_LEN // c))

    def lat_of(d, s):
        return jnp.clip((cc_of(d, s) - ctx_chunks) // (LAT_LEN // c), 0, N_LAT_SEQ - 1)

    return pl.pallas_call(
        _gla_kernel,
        grid=(2, nchunk),
        in_specs=[
            pl.BlockSpec((c, 256), lambda d, s: (cc_of(d, s), Z_AQ // 256)),
            pl.BlockSpec((c, 256), lambda d, s: (cc_of(d, s), Z_AK // 256)),
            pl.BlockSpec((c, 512), lambda d, s: (cc_of(d, s), Z_AV // 512)),
            pl.BlockSpec((c, 128), lambda d, s: (cc_of(d, s), Z_LR // 128)),
            pl.BlockSpec((None, None, 128, 256), lambda d, s: (l, d, 0, 0)),
            pl.BlockSpec((None, None, 1, 256), lambda d, s: (l, d, 0, 0)),
            pl.BlockSpec((None, GLA_GROWS, 2 * c), lambda d, s: (d, 0, 0)),
            pl.BlockSpec((None, c, GLA_HEADS * c), lambda d, s: (d, 0, 0)),
            pl.BlockSpec((None, None, None, GLA_HEADS, GLA_DV, GLA_DK),
                         lambda d, s: (lat_of(d, s), l, d, 0, 0, 0)),
        ],
        out_specs=[
            pl.BlockSpec((None, c, 512), lambda d, s: (d, cc_of(d, s), 0)),
            pl.BlockSpec((None, None, GLA_HEADS, GLA_DV, GLA_DK),
                         lambda d, s: (d, seq_of(d, s), 0, 0, 0)),
        ],
        out_shape=[jax.ShapeDtypeStruct((2, T_ALL, 512), F32),
                   jax.ShapeDtypeStruct((2, nseq, GLA_HEADS, GLA_DV, GLA_DK), F32)],
        scratch_shapes=[pltpu.VMEM((GLA_HEADS * GLA_DV, GLA_HEADS * GLA_DK), F32)],
        compiler_params=_cparams(("arbitrary", "arbitrary")),
        name="gla",
    )(z, z, z, z, w2p, ba, seg, levmap, s0t)


def _merge_kernel(x_ref, of_ref, obk_ref, ar_ref, gates_ref, bc_ref, bl_ref, cc_ref, cl_ref,
                  wpa_ref, wpb_ref, wpc_ref, wo_ref, gn_ref, mod_ref, y_ref):
    ctx = pl.program_id(0) < T_CTX // x_ref.shape[0]
    o = of_ref[...] + obk_ref[...]
    heads = []
    for h in range(GLA_HEADS):
        oh = o[:, GLA_DV * h:GLA_DV * (h + 1)]
        ms = jnp.mean(oh * oh, axis=-1, keepdims=True)
        heads.append(oh * lax.rsqrt(ms + EPS) * gn_ref[...])
    oa = (jnp.concatenate(heads, axis=1) * _silu(ar_ref[...])).astype(BF16)
    ob = jnp.where(ctx, bc_ref[...], bl_ref[...])
    oc = jnp.where(ctx, cc_ref[...], cl_ref[...])
    d = D_MODEL
    merged = (_sigmoid(gates_ref[:, 0:d]) * _dot(oa, wpa_ref[...])
              + _sigmoid(gates_ref[:, d:2 * d]) * _dot(ob, wpb_ref[...])
              + _sigmoid(gates_ref[:, 2 * d:3 * d]) * _dot(oc, wpc_ref[...]))
    out = _dot(merged.astype(BF16), wo_ref[...])
    y_ref[...] = x_ref[...] + mod_ref[:, 2 * d:3 * d] * out


def _merge(x, ogla, z, ob_ctx, ob_lat, oc_ctx, oc_lat, wpa, wpb, wpc, wo, gn, mod4, l):
    tm = 512
    ctx_blocks = T_CTX // tm
    ctx_idx = lambda i: (jnp.minimum(i, ctx_blocks - 1), 0)
    lat_idx = lambda i: (jnp.maximum(i - ctx_blocks, 0), 0)
    wspec = lambda k: pl.BlockSpec((None, k, D_MODEL), lambda i: (l, 0, 0))
    return pl.pallas_call(
        _merge_kernel,
        grid=(T_ALL // tm,),
        in_specs=[
            pl.BlockSpec((tm, D_MODEL), lambda i: (i, 0)),
            pl.BlockSpec((None, tm, 512), lambda i: (0, i, 0)),
            pl.BlockSpec((None, tm, 512), lambda i: (1, i, 0)),
            pl.BlockSpec((tm, 512), lambda i: (i, Z_AR // 512)),
            pl.BlockSpec((tm, 3 * D_MODEL), lambda i: (i, 0)),
            pl.BlockSpec((tm, 512), ctx_idx), pl.BlockSpec((tm, 512), lat_idx),
            pl.BlockSpec((tm, 512), ctx_idx), pl.BlockSpec((tm, 512), lat_idx),
            wspec(512), wspec(512), wspec(512), wspec(D_MODEL),
            pl.BlockSpec((None, 1, GLA_DV), lambda i: (l, 0, 0)),
            pl.BlockSpec((None, None, 1, 6 * D_MODEL), lambda i: (l, _group_of_rows(i, tm), 0, 0)),
        ],
        out_specs=pl.BlockSpec((tm, D_MODEL), lambda i: (i, 0)),
        out_shape=jax.ShapeDtypeStruct((T_ALL, D_MODEL), F32),
        compiler_params=_cparams(("arbitrary",)),
        name="merge",
    )(x, ogla, ogla, z, z, ob_ctx, ob_lat, oc_ctx, oc_lat, wpa, wpb, wpc, wo, gn, mod4)


def _mlp_kernel(x_ref, g_ref, mod_ref, w1_ref, w2_ref, y_ref, h_scr, acc_scr):
    k = pl.program_id(1)
    d = D_MODEL

    @pl.when(k == 0)
    def _():
        h = _norm_mod(x_ref[...], g_ref[...], mod_ref[:, 3 * d:4 * d], mod_ref[:, 4 * d:5 * d])
        h_scr[...] = h.astype(BF16)

    hid = jnp.square(jnp.maximum(_dot(h_scr[...], w1_ref[...]), 0.0)).astype(BF16)
    part = _dot(hid, w2_ref[...])

    @pl.when(k == 0)
    def _():
        acc_scr[...] = part

    @pl.when(k > 0)
    def _():
        acc_scr[...] += part

    @pl.when(k == pl.num_programs(1) - 1)
    def _():
        y_ref[...] = x_ref[...] + mod_ref[:, 5 * d:6 * d] * acc_scr[...]


def _mlp(x, norm_w, mod4, w1, w2, l):
    tm, th = 1024, 1024
    return pl.pallas_call(
        _mlp_kernel,
        grid=(T_ALL // tm, MLP_HIDDEN // th),
        in_specs=[
            pl.BlockSpec((tm, D_MODEL), lambda i, k: (i, 0)),
            pl.BlockSpec((None, 1, D_MODEL), lambda i, k: (l, 0, 0)),
            pl.BlockSpec((None, None, 1, 6 * D_MODEL),
                         lambda i, k: (l, _group_of_rows(i, tm), 0, 0)),
            pl.BlockSpec((None, D_MODEL, th), lambda i, k: (l, 0, k)),
            pl.BlockSpec((None, th, D_MODEL), lambda i, k: (l, k, 0)),
        ],
        out_specs=pl.BlockSpec((tm, D_MODEL), lambda i, k: (i, 0)),
        out_shape=jax.ShapeDtypeStruct((T_ALL, D_MODEL), F32),
        scratch_shapes=[pltpu.VMEM((tm, D_MODEL), BF16), pltpu.VMEM((tm, D_MODEL), F32)],
        compiler_params=_cparams(("arbitrary", "arbitrary")),
        name="mlp",
    )(x, norm_w, mod4, w1, w2)


def _rope_tables():
    t = jnp.arange(LAT_LEN)
    row = (t // GRID_W).astype(F32)
    col = (t % GRID_W).astype(F32)
    nf = HEAD_DIM // 4
    inv_freq = ROPE_BASE ** (-jnp.arange(nf, dtype=F32) / nf)
    ang_r = row[:, None] * inv_freq[None, :]
    ang_c = col[:, None] * inv_freq[None, :]
    cos = jnp.concatenate([jnp.cos(ang_r)] * 2 + [jnp.cos(ang_c)] * 2, axis=1)
    sin = jnp.concatenate([-jnp.sin(ang_r), jnp.sin(ang_r), -jnp.sin(ang_c), jnp.sin(ang_c)], axis=1)
    return jnp.tile(cos, (1, 2)), jnp.tile(sin, (1, 2))


def kernel(x_prompt, x_sample, state_gla, cache_swa_k, cache_swa_v, cache_na_k, cache_na_v, c,
           c_ctx, w_mod, b_mod, norm1, norm2, w_in, w_a2_f, b_a_f, w_a2_b, b_a_b, gla_onorm,
           qn_swa, kn_swa, sink_swa, qn_na, kn_na, rpb_na, w_pa, w_pb, w_pc, w_o, w_fc1, w_fc2):
    d = D_MODEL
    x = jnp.concatenate([x_prompt.reshape(T_CTX, d), x_sample.reshape(T_LAT, d)], axis=0)

    cond8 = jnp.zeros((8, d), F32).at[0].set(c_ctx).at[1:1 + N_LAT_SEQ].set(c)
    mod4 = _modulation(cond8, w_mod, b_mod).reshape(DEPTH, 8, 1, 6 * d)

    w_in_p = jnp.concatenate(
        [w_in[:, :, 3872:6944], w_in[:, :, 0:1536], w_in[:, :, 1568:3872], w_in[:, :, 1536:1568],
         jnp.zeros((DEPTH, d, Z_W - 6944), F32)], axis=-1).astype(BF16)
    wpa, wpb, wpc, wo = (w.astype(BF16) for w in (w_pa, w_pb, w_pc, w_o))
    w1, w2 = w_fc1.astype(BF16), w_fc2.astype(BF16)
    norm1r = norm1.reshape(DEPTH, 1, d)
    norm2r = norm2.reshape(DEPTH, 1, d)
    gnr = gla_onorm.reshape(DEPTH, 1, GLA_DV)
    w2p = jnp.zeros((DEPTH, 2, 128, GLA_HEADS * GLA_DK), F32)
    w2p = w2p.at[:, 0, 0:GLA_LOWRANK].set(w_a2_f).at[:, 1, GLA_LOWRANK:2 * GLA_LOWRANK].set(w_a2_b)
    w2p = w2p.astype(BF16)
    ba = jnp.stack([b_a_f, b_a_b], axis=1).reshape(DEPTH, 2, 1, GLA_HEADS * GLA_DK)
    seg_np, lev_np = _gla_constants()
    seg = jnp.asarray(seg_np, BF16)
    levmap = jnp.asarray(lev_np)
    s0t = jnp.swapaxes(state_gla, -1, -2)
    cos_t, sin_t = _rope_tables()
    bias_tab = _na_bias_tables(rpb_na)
    csk = cache_swa_k.reshape(N_LAT_SEQ, DEPTH, PAST_LEN, 128)
    csv = cache_swa_v.reshape(N_LAT_SEQ, DEPTH, PAST_LEN, 128)
    cnk = cache_na_k.reshape(N_LAT_SEQ, DEPTH, PAST_LEN, 512)
    cnv = cache_na_v.reshape(N_LAT_SEQ, DEPTH, PAST_LEN, 512)

    st_l, bk_l, bv_l, nk_l, nv_l = [], [], [], [], []
    for l in range(DEPTH):
        z = _inproj(x, norm1r, mod4, w_in_p, l)
        wqb = jnp.tile(qn_swa[l], 8)[None, :]
        wkb = jnp.tile(kn_swa[l], 2)[None, :]
        wqc = jnp.tile(qn_na[l], 8)[None, :]
        wkc = jnp.tile(kn_na[l], 8)[None, :]
        qb, kb, vb, qc, kc, vc, kbf, kcf = _attn_prep(z, wqb, wkb, wqc, wkc, cos_t, sin_t)
        ogla, sfin = _gla(z, w2p, ba, seg, levmap, s0t, l)
        ob_ctx, oc_ctx = _ctx_attention(sink_swa, qb, kb, vb, qc, kc, vc, l)
        ob_lat = _swa_attention(sink_swa, qb, kb, vb, csk, csv, l)
        oc_lat = _na_attention(qc, kc, vc, cnk, cnv, bias_tab, l)
        x = _merge(x, ogla, z, ob_ctx, ob_lat, oc_ctx, oc_lat, wpa, wpb, wpc, wo, gnr, mod4, l)
        x = _mlp(x, norm2r, mod4, w1, w2, l)

        st_l.append(jnp.swapaxes(sfin[:, :N_CTX_SEQ], -1, -2))
        bk_l.append(kbf[:T_CTX])
        bv_l.append(z[:T_CTX, Z_ATT + A_BV:Z_ATT + A_BV + 128])
        nk_l.append(kcf[:T_CTX])
        nv_l.append(z[:T_CTX, Z_ATT + A_CV:Z_ATT + A_CV + 512])

    y_prompt = x[:T_CTX].reshape(N_CTX_SEQ, CTX_LEN, d)
    y_sample = x[T_CTX:].reshape(N_LAT_SEQ, LAT_LEN, d)
    new_state = jnp.stack(st_l, axis=0).transpose(2, 0, 1, 3, 4, 5)

    def stack_kv(parts, heads):
        a = jnp.stack(parts, axis=0).reshape(DEPTH, N_CTX_SEQ, CTX_LEN, heads, HEAD_DIM)
        return a.transpose(1, 0, 2, 3, 4)

    return (y_prompt, y_sample, new_state,
            stack_kv(bk_l, SWA_KV_HEADS), stack_kv(bv_l, SWA_KV_HEADS),
            stack_kv(nk_l, NA_HEADS), stack_kv(nv_l, NA_HEADS))
```

```python
import functools

import numpy as np
import jax
import jax.numpy as jnp
from jax import lax
from jax.experimental import pallas as pl
from jax.experimental.pallas import tpu as pltpu

F32 = jnp.float32
BF16 = jnp.bfloat16

D_MODEL = 1024
DEPTH = 4
N_CTX_SEQ = 16
CTX_LEN = 256
N_LAT_SEQ = 2
LAT_LEN = 1024
PAST_LEN = 512
T_CTX = N_CTX_SEQ * CTX_LEN
T_LAT = N_LAT_SEQ * LAT_LEN
T_ALL = T_CTX + T_LAT
GRID_W = 64
HEAD_DIM = 64
GLA_HEADS = 4
GLA_DK = 64
GLA_DV = 128
GLA_LOWRANK = 16
GLA_TAU = 16.0
SWA_Q_HEADS = 8
SWA_KV_HEADS = 2
SWA_GROUP = 4
SWA_WINDOW = 128
NA_HEADS = 8
NA_ROWS = 8
NA_COLS = 16
MLP_HIDDEN = 4 * D_MODEL
ROPE_BASE = 10000.0
EPS = 1e-6

Z_GATES = 0
Z_GLA = 3072
Z_AQ, Z_AK, Z_AV, Z_AR = 3072, 3328, 3584, 4096
Z_ATT = 4608
ATT_W = 2304
A_BQ, A_BK, A_BV, A_CQ, A_CK, A_CV = 0, 512, 640, 768, 1280, 1792
Z_LR = 6912
Z_W = 7168

GLA_C = 128
GLA_LEVELS = 7
GLA_GROWS = (GLA_LEVELS + 2) * GLA_C + 16

VMEM_LIMIT = 56 * 1024 * 1024


def _cparams(sem):
    return pltpu.CompilerParams(dimension_semantics=sem, vmem_limit_bytes=VMEM_LIMIT)


def _sigmoid(x):
    return 1.0 / (1.0 + jnp.exp(-x))


def _silu(x):
    return x * _sigmoid(x)


def _nt_dot(a, b):
    return lax.dot_general(a, b, (((1,), (1,)), ((), ())), preferred_element_type=F32)


def _tn_dot(a, b):
    return lax.dot_general(a, b, (((0,), (0,)), ((), ())), preferred_element_type=F32)


def _dot(a, b):
    return jnp.dot(a, b, preferred_element_type=F32)


def _mod_kernel(cond_ref, w_ref, b_ref, o_ref):
    s = _silu(cond_ref[...]).astype(BF16)
    o_ref[...] = _dot(s, w_ref[...].astype(BF16)) + b_ref[...]


def _modulation(cond8, w_mod, b_mod):
    tn = 1024
    return pl.pallas_call(
        _mod_kernel,
        grid=(DEPTH, 6 * D_MODEL // tn),
        in_specs=[
            pl.BlockSpec((8, D_MODEL), lambda l, j: (0, 0)),
            pl.BlockSpec((None, D_MODEL, tn), lambda l, j: (l, 0, j)),
            pl.BlockSpec((None, 1, tn), lambda l, j: (l, 0, j)),
        ],
        out_specs=pl.BlockSpec((None, 8, tn), lambda l, j: (l, 0, j)),
        out_shape=jax.ShapeDtypeStruct((DEPTH, 8, 6 * D_MODEL), F32),
        compiler_params=_cparams(("arbitrary", "arbitrary")),
        name="modulation",
    )(cond8, w_mod, b_mod.reshape(DEPTH, 1, 6 * D_MODEL))


def _group_of_rows(row_block, rows_per_block):
    first = row_block * rows_per_block
    return jnp.maximum(first - T_CTX + LAT_LEN, 0) // LAT_LEN


def _norm_mod(x, g, shift, scale):
    ms = jnp.mean(x * x, axis=-1, keepdims=True)
    y = x * lax.rsqrt(ms + EPS) * g
    return y * (1.0 + scale) + shift


def _inproj_kernel(x_ref, g_ref, mod_ref, w_ref, z_ref, h_scr):
    @pl.when(pl.program_id(1) == 0)
    def _():
        h = _norm_mod(x_ref[...], g_ref[...],
                      mod_ref[:, 0:D_MODEL], mod_ref[:, D_MODEL:2 * D_MODEL])
        h_scr[...] = h.astype(BF16)

    z_ref[...] = _dot(h_scr[...], w_ref[...]).astype(z_ref.dtype)


def _inproj(x, norm_w, mod4, w_in_p, l):
    tm, tn = 1024, 1024
    return pl.pallas_call(
        _inproj_kernel,
        grid=(T_ALL // tm, Z_W // tn),
        in_specs=[
            pl.BlockSpec((tm, D_MODEL), lambda i, j: (i, 0)),
            pl.BlockSpec((None, 1, D_MODEL), lambda i, j: (l, 0, 0)),
            pl.BlockSpec((None, None, 1, 6 * D_MODEL),
                         lambda i, j: (l, _group_of_rows(i, tm), 0, 0)),
            pl.BlockSpec((None, D_MODEL, tn), lambda i, j: (l, 0, j)),
        ],
        out_specs=pl.BlockSpec((tm, tn), lambda i, j: (i, j)),
        out_shape=jax.ShapeDtypeStruct((T_ALL, Z_W), BF16),
        scratch_shapes=[pltpu.VMEM((tm, D_MODEL), BF16)],
        compiler_params=_cparams(("arbitrary", "arbitrary")),
        name="inproj",
    )(x, norm_w, mod4, w_in_p)


def _pair_norm(x, w):
    sq = x * x
    low = lax.broadcasted_iota(jnp.int32, (1, 128), 1) < HEAD_DIM
    s_low = jnp.sum(jnp.where(low, sq, 0.0), axis=-1, keepdims=True)
    s_all = jnp.sum(sq, axis=-1, keepdims=True)
    ms = jnp.where(low, s_low, s_all - s_low) * (1.0 / HEAD_DIM)
    return x * lax.rsqrt(ms + EPS) * w


def _rope(x, cos, sin_signed):
    n = x.shape[-1]
    lane = lax.broadcasted_iota(jnp.int32, (1, n), 1)
    first = (lane % 32) < 16
    partner = jnp.where(first, pltpu.roll(x, n - 16, 1), pltpu.roll(x, 16, 1))
    return x * cos + partner * sin_signed


def _prep_kernel(z_ref, wqb_ref, wkb_ref, wqc_ref, wkc_ref, cos_ref, sin_ref,
                 qb_ref, kb_ref, vb_ref, qc_ref, kc_ref, vc_ref, kbf_ref, kcf_ref):
    latent = pl.program_id(0) >= T_CTX // z_ref.shape[0]
    qscale = HEAD_DIM ** -0.5

    def normed(col, width, w_ref):
        tiles = []
        for p in range(width // 128):
            x = z_ref[:, col + 128 * p: col + 128 * (p + 1)].astype(F32)
            tiles.append(_pair_norm(x, w_ref[:, 128 * p:128 * (p + 1)]))
        return tiles

    qb = normed(A_BQ, 512, wqb_ref)
    kb = normed(A_BK, 128, wkb_ref)
    qc = normed(A_CQ, 512, wqc_ref)
    kc = normed(A_CK, 512, wkc_ref)

    vb_ref[...] = z_ref[:, A_BV:A_BV + 128]
    vc_ref[...] = z_ref[:, A_CV:A_CV + 512]
    for p in range(4):
        qc_ref[:, 128 * p:128 * (p + 1)] = (qc[p] * qscale).astype(BF16)
        kc_ref[:, 128 * p:128 * (p + 1)] = kc[p].astype(BF16)
        kcf_ref[:, 128 * p:128 * (p + 1)] = kc[p]

    @pl.when(jnp.logical_not(latent))
    def _():
        for p in range(4):
            qb_ref[:, 128 * p:128 * (p + 1)] = (qb[p] * qscale).astype(BF16)
        kb_ref[...] = kb[0].astype(BF16)
        kbf_ref[...] = kb[0]

    @pl.when(latent)
    def _():
        cos = cos_ref[...]
        sin = sin_ref[...]
        for p in range(4):
            qb_ref[:, 128 * p:128 * (p + 1)] = (_rope(qb[p], cos, sin) * qscale).astype(BF16)
        kr = _rope(kb[0], cos, sin)
        kb_ref[...] = kr.astype(BF16)
        kbf_ref[...] = kr


def _attn_prep(z, wqb, wkb, wqc, wkc, cos_t, sin_t):
    tm = 512
    nblk = T_ALL // tm
    lat_blocks = LAT_LEN // tm

    def rope_idx(i):
        return (jnp.maximum(i - T_CTX // tm, 0) % lat_blocks, 0)

    row = lambda w: pl.BlockSpec((tm, w), lambda i: (i, 0))
    cst = lambda w: pl.BlockSpec((1, w), lambda i: (0, 0))
    sds = lambda w, dt: jax.ShapeDtypeStruct((T_ALL, w), dt)
    return pl.pallas_call(
        _prep_kernel,
        grid=(nblk,),
        in_specs=[
            pl.BlockSpec((tm, ATT_W), lambda i: (i, Z_ATT // ATT_W)),
            cst(512), cst(128), cst(512), cst(512),
            pl.BlockSpec((tm, 128), rope_idx),
            pl.BlockSpec((tm, 128), rope_idx),
        ],
        out_specs=[row(512), row(128), row(128), row(512), row(512), row(512), row(128), row(512)],
        out_shape=[sds(512, BF16), sds(128, BF16), sds(128, BF16), sds(512, BF16),
                   sds(512, BF16), sds(512, BF16), sds(128, F32), sds(512, F32)],
        compiler_params=_cparams(("arbitrary",)),
        name="attn_prep",
    )(z, wqb, wkb, wqc, wkc, cos_t, sin_t)


def _softmax_parts(scores, values, sink=None):
    mx = scores[0].max(axis=-1, keepdims=True)
    for s in scores[1:]:
        mx = jnp.maximum(mx, s.max(axis=-1, keepdims=True))
    if sink is not None:
        mx = jnp.maximum(mx, sink)
    den = None
    acc = None
    for s, v in zip(scores, values):
        p = jnp.exp(s - mx)
        d = p.sum(axis=-1, keepdims=True)
        o = _dot(p.astype(BF16), v)
        den = d if den is None else den + d
        acc = o if acc is None else acc + o
    if sink is not None:
        den = den + jnp.exp(sink - mx)
    return acc / den


def _ctx_attn_kernel(sink_ref, qb_ref, kb_ref, vb_ref, qc_ref, kc_ref, vc_ref, ob_ref, oc_ref, *, layer):
    n = qb_ref.shape[0]
    outs = [None] * SWA_Q_HEADS
    for kvh in range(SWA_KV_HEADS):
        k = kb_ref[:, HEAD_DIM * kvh:HEAD_DIM * (kvh + 1)]
        v = vb_ref[:, HEAD_DIM * kvh:HEAD_DIM * (kvh + 1)]
        heads = [SWA_GROUP * kvh + g for g in range(SWA_GROUP)]
        q = jnp.concatenate([qb_ref[:, HEAD_DIM * h:HEAD_DIM * (h + 1)] for h in heads], axis=0)
        sink = jnp.concatenate(
            [jnp.full((n, 1), sink_ref[layer, h], F32) for h in heads], axis=0)
        o = _softmax_parts([_nt_dot(q, k)], [v], sink)
        for g, h in enumerate(heads):
            outs[h] = o[n * g:n * (g + 1)]
    ob_ref[...] = jnp.concatenate(outs, axis=1).astype(BF16)

    outs = []
    for h in range(NA_HEADS):
        sl = slice(HEAD_DIM * h, HEAD_DIM * (h + 1))
        outs.append(_softmax_parts([_nt_dot(qc_ref[:, sl], kc_ref[:, sl])], [vc_ref[:, sl]]))
    oc_ref[...] = jnp.concatenate(outs, axis=1).astype(BF16)


def _ctx_attention(sink, qb, kb, vb, qc, kc, vc, l):
    blk = lambda w: pl.BlockSpec((CTX_LEN, w), lambda b: (b, 0))
    return pl.pallas_call(
        functools.partial(_ctx_attn_kernel, layer=l),
        grid=(N_CTX_SEQ,),
        in_specs=[pl.BlockSpec(memory_space=pltpu.SMEM),
                  blk(512), blk(128), blk(128), blk(512), blk(512), blk(512)],
        out_specs=[blk(512), blk(512)],
        out_shape=[jax.ShapeDtypeStruct((T_CTX, 512), BF16)] * 2,
        compiler_params=_cparams(("arbitrary",)),
        name="ctx_attention",
    )(sink, qb, kb, vb, qc, kc, vc)


SWA_QBLK = 128
SWA_SPAN = SWA_QBLK + 2 * SWA_WINDOW


def _swa_kernel(sink_ref, q_ref, k_ref, v_ref, kc_ref, vc_ref, o_ref, *, layer):
    i = pl.program_id(1)
    start = pl.multiple_of(jnp.clip(SWA_QBLK * (i - 1), 0, LAT_LEN - SWA_SPAN), SWA_QBLK)
    rows = SWA_GROUP * SWA_QBLK
    qpos = SWA_QBLK * i + lax.broadcasted_iota(jnp.int32, (rows, SWA_SPAN), 0) % SWA_QBLK
    kpos = start + lax.broadcasted_iota(jnp.int32, (rows, SWA_SPAN), 1)
    valid = jnp.abs(kpos - qpos) <= SWA_WINDOW
    kwin = k_ref[pl.ds(start, SWA_SPAN), :]
    vwin = v_ref[pl.ds(start, SWA_SPAN), :]
    kctx = kc_ref[...].astype(BF16)
    vctx = vc_ref[...].astype(BF16)
    outs = [None] * SWA_Q_HEADS
    for kvh in range(SWA_KV_HEADS):
        sl = slice(HEAD_DIM * kvh, HEAD_DIM * (kvh + 1))
        heads = [SWA_GROUP * kvh + g for g in range(SWA_GROUP)]
        q = jnp.concatenate([q_ref[:, HEAD_DIM * h:HEAD_DIM * (h + 1)] for h in heads], axis=0)
        sink = jnp.concatenate(
            [jnp.full((SWA_QBLK, 1), sink_ref[layer, h], F32) for h in heads], axis=0)
        s_lat = jnp.where(valid, _nt_dot(q, kwin[:, sl]), -jnp.inf)
        s_ctx = _nt_dot(q, kctx[:, sl])
        o = _softmax_parts([s_lat, s_ctx], [vwin[:, sl], vctx[:, sl]], sink)
        for g, h in enumerate(heads):
            outs[h] = o[SWA_QBLK * g:SWA_QBLK * (g + 1)]
    o_ref[...] = jnp.concatenate(outs, axis=1).astype(BF16)


def _swa_attention(sink, qb, kb, vb, cache_k, cache_v, l):
    nq = LAT_LEN // SWA_QBLK
    lat0 = T_CTX // LAT_LEN
    q0 = T_CTX // SWA_QBLK
    kv = pl.BlockSpec((LAT_LEN, 128), lambda b, i: (lat0 + b, 0))
    cache = pl.BlockSpec((None, None, PAST_LEN, 128), lambda b, i: (b, l, 0, 0))
    return pl.pallas_call(
        functools.partial(_swa_kernel, layer=l),
        grid=(N_LAT_SEQ, nq),
        in_specs=[pl.BlockSpec(memory_space=pltpu.SMEM),
                  pl.BlockSpec((SWA_QBLK, 512), lambda b, i: (q0 + b * nq + i, 0)),
                  kv, kv, cache, cache],
        out_specs=pl.BlockSpec((SWA_QBLK, 512), lambda b, i: (b * nq + i, 0)),
        out_shape=jax.ShapeDtypeStruct((T_LAT, 512), BF16),
        compiler_params=_cparams(("arbitrary", "arbitrary")),
        name="swa_attention",
    )(sink, qb, kb, vb, cache_k, cache_v)


LAT_ROWS = LAT_LEN // GRID_W
NA_WIN_ROWS = min(NA_ROWS, LAT_ROWS)
NA_KEYS = NA_WIN_ROWS * GRID_W


def _na_kernel(q_ref, k_ref, v_ref, kc_ref, vc_ref, bias_ref, o_ref):
    r = pl.program_id(1)
    rs = jnp.clip(r - NA_ROWS // 2, 0, LAT_ROWS - NA_WIN_ROWS)
    start = pl.multiple_of(rs * GRID_W, GRID_W)
    base = rs - r + NA_ROWS - 1
    low = lax.broadcasted_iota(jnp.int32, (1, 128), 1) < HEAD_DIM
    tiles = []
    for p in range(NA_HEADS // 2):
        sl = slice(128 * p, 128 * (p + 1))
        q = q_ref[:, sl]
        kwin = k_ref[pl.ds(start, NA_KEYS), sl]
        vwin = v_ref[pl.ds(start, NA_KEYS), sl]
        kctx = kc_ref[:, sl].astype(BF16)
        vctx = vc_ref[:, sl].astype(BF16)
        halves = []
        for hh in range(2):
            h = 2 * p + hh
            qm = jnp.where(low if hh == 0 else jnp.logical_not(low), q, jnp.zeros_like(q))
            bias = jnp.concatenate(
                [bias_ref[h, base + 2 * w] for w in range(NA_WIN_ROWS // 2)], axis=1)
            s_lat = _nt_dot(qm, kwin) + bias
            s_ctx = _nt_dot(qm, kctx)
            halves.append(_softmax_parts([s_lat, s_ctx], [vwin, vctx]))
        tiles.append(jnp.where(low, halves[0], halves[1]))
    o_ref[...] = jnp.concatenate(tiles, axis=1).astype(BF16)


def _na_attention(qc, kc, vc, cache_k, cache_v, bias_tab, l):
    lat0 = T_CTX // LAT_LEN
    q0 = T_CTX // GRID_W
    kv = pl.BlockSpec((LAT_LEN, 512), lambda b, r: (lat0 + b, 0))
    cache = pl.BlockSpec((None, None, PAST_LEN, 512), lambda b, r: (b, l, 0, 0))
    return pl.pallas_call(
        _na_kernel,
        grid=(N_LAT_SEQ, LAT_ROWS),
        in_specs=[pl.BlockSpec((GRID_W, 512), lambda b, r: (q0 + b * LAT_ROWS + r, 0)),
                  kv, kv, cache, cache,
                  pl.BlockSpec((None, NA_HEADS, 2 * NA_ROWS - 2, GRID_W, 128),
                               lambda b, r: (l, 0, 0, 0, 0))],
        out_specs=pl.BlockSpec((GRID_W, 512), lambda b, r: (b * LAT_ROWS + r, 0)),
        out_shape=jax.ShapeDtypeStruct((T_LAT, 512), BF16),
        compiler_params=_cparams(("arbitrary", "arbitrary")),
        name="na_attention",
    )(qc, kc, vc, cache_k, cache_v, bias_tab)


def _na_bias_tables(rpb):
    cq = np.arange(GRID_W)[:, None]
    ck = np.arange(GRID_W)[None, :]
    cs = np.clip(cq - NA_COLS // 2, 0, GRID_W - NA_COLS)
    col_ok = (ck >= cs) & (ck < cs + NA_COLS)
    period = 2 * GRID_W + 1
    v = jnp.zeros(rpb.shape[:-1] + (period,), F32)
    v = v.at[..., 0:NA_COLS].set(rpb[..., NA_COLS - 1:])
    v = v.at[..., period - (NA_COLS - 1):].set(rpb[..., :NA_COLS - 1])
    flat = jnp.tile(v, (1, 1, 1, GRID_W))[..., :GRID_W * (period - 1)]
    t = flat.reshape(rpb.shape[:-1] + (GRID_W, period - 1))[..., :GRID_W]
    t = jnp.where(col_ok, t, -jnp.inf)
    return jnp.concatenate([t[:, :, :-1], t[:, :, 1:]], axis=-1)


def _gla_constants():
    c = GLA_C
    t = np.arange(c)[:, None]
    u = np.arange(c)[None, :]
    blocks = []
    for k in range(GLA_LEVELS):
        b = 1 << k
        m = ((t >> k) | 1) * b - 1
        query = ((t >> k) & 1) == 1
        blocks.append(np.where(query, (u > m) & (u <= t), (u > t) & (u <= m)))
    blocks.append(u <= t)
    blocks.append(u > t)
    blocks.append(np.ones((16, c), bool))
    fwd = np.concatenate(blocks, axis=0).astype(np.float32)
    bwd_blocks = [blk[::-1, ::-1] for blk in blocks]
    bwd = np.concatenate(bwd_blocks, axis=0).astype(np.float32)
    seg = np.stack([fwd, bwd])
    seg = np.concatenate([seg, seg], axis=-1)

    s = np.arange(c)[None, :]
    x = t ^ s
    lev = np.where(x == 0, GLA_LEVELS, np.floor(np.log2(np.maximum(x, 1))).astype(np.int64))
    lev_f = np.where(s <= t, lev, -1)
    lev_b = np.where(s >= t, lev, -1)
    levmap = np.stack([np.tile(lev_f, (1, GLA_HEADS)), np.tile(lev_b, (1, GLA_HEADS))])
    return seg, levmap.astype(np.int32)


def _gla_state_init(d, cc, s0_ref, st_ref):
    c = GLA_C
    ctx_chunks = T_CTX // c
    is_lat = cc >= ctx_chunks
    per_seq = jnp.where(is_lat, LAT_LEN // c, CTX_LEN // c)
    pos = jnp.where(is_lat, cc - ctx_chunks, cc) % per_seq
    first = (pos == 0) if d == 0 else (pos == per_seq - 1)

    @pl.when(first)
    def _():
        for h in range(GLA_HEADS):
            blk = jnp.where(is_lat, s0_ref[h], 0.0)
            row = [jnp.zeros((GLA_DV, GLA_DK), F32)] * GLA_HEADS
            row[h] = blk
            st_ref[GLA_DV * h:GLA_DV * (h + 1), :] = jnp.concatenate(row, axis=1)


def _gla_chunk(q_ref, k_ref, v_ref, lr_ref, w2_ref, ba_ref, seg_ref, lev_ref, o_ref, sfin_ref, st_ref):
    c = GLA_C
    x = _dot(lr_ref[...], w2_ref[...]) + ba_ref[...]
    la = (jnp.minimum(x, 0.0) - jnp.log1p(jnp.exp(-jnp.abs(x)))) * (1.0 / GLA_TAU)
    la_hi = la.astype(BF16)
    la_lo = (la - la_hi.astype(F32)).astype(BF16)
    seg = _dot(seg_ref[...], jnp.concatenate([la_hi, la_lo], axis=0))

    q = q_ref[...].astype(F32) * (GLA_DK ** -0.5)
    k = k_ref[...].astype(F32)
    v = v_ref[...]
    lev = lev_ref[...]
    lane_head = lax.broadcasted_iota(jnp.int32, (1, GLA_HEADS * GLA_DK), 1) // GLA_DK
    head_sel = [jnp.where(lane_head == h, 1.0, 0.0).astype(BF16) for h in range(GLA_HEADS)]

    def pair_scores(qs, ks):
        ksb = ks.astype(BF16)
        kbd = jnp.concatenate([ksb * head_sel[h] for h in range(GLA_HEADS)], axis=0)
        return _nt_dot(qs.astype(BF16), kbd)

    attn = jnp.where(lev == GLA_LEVELS, pair_scores(q, k), 0.0)
    for lvl in range(GLA_LEVELS):
        e = jnp.exp(seg[c * lvl:c * (lvl + 1)])
        attn = jnp.where(lev == lvl, pair_scores(q * e, k * e), attn)
    attn = attn.astype(BF16)

    cum = seg[c * GLA_LEVELS:c * (GLA_LEVELS + 1)]
    rem = seg[c * (GLA_LEVELS + 1):c * (GLA_LEVELS + 2)]
    tot = seg[c * (GLA_LEVELS + 2):c * (GLA_LEVELS + 2) + 1]
    q_in = (q * jnp.exp(cum)).astype(BF16)
    k_in = (k * jnp.exp(rem)).astype(BF16)
    state = st_ref[...]
    o_inter = _nt_dot(q_in, state.astype(BF16))
    for h in range(GLA_HEADS):
        sl = slice(GLA_DV * h, GLA_DV * (h + 1))
        o_ref[:, sl] = o_inter[:, sl] + _dot(attn[:, c * h:c * (h + 1)], v[:, sl])

    upd = _tn_dot(v, k_in)
    row_head = lax.broadcasted_iota(jnp.int32, (GLA_HEADS * GLA_DV, 1), 0) // GLA_DV
    new_state = state * jnp.exp(tot) + jnp.where(row_head == lane_head, upd, 0.0)
    st_ref[...] = new_state
    for h in range(GLA_HEADS):
        sfin_ref[h] = new_state[GLA_DV * h:GLA_DV * (h + 1), GLA_DK * h:GLA_DK * (h + 1)]


def _gla_kernel(qf, kf, vf, lrf, qb, kb, vb, lrb, w2_ref, ba_ref, seg_ref, lev_ref, s0f, s0b,
                of_ref, ob_ref, sff_ref, sfb_ref, stf, stb):
    step = pl.program_id(0)
    last = pl.num_programs(0) - 1
    _gla_state_init(0, step, s0f, stf)
    _gla_state_init(1, last - step, s0b, stb)
    _gla_chunk(qf, kf, vf, lrf, w2_ref.at[0], ba_ref.at[0], seg_ref.at[0], lev_ref.at[0],
               of_ref, sff_ref, stf)
    _gla_chunk(qb, kb, vb, lrb, w2_ref.at[1], ba_ref.at[1], seg_ref.at[1], lev_ref.at[1],
               ob_ref, sfb_ref, stb)


def _gla(z, w2p, ba, seg, levmap, s0t, l):
    c = GLA_C
    nchunk = T_ALL // c
    ctx_chunks = T_CTX // c
    nseq = N_CTX_SEQ + N_LAT_SEQ
    fwd = lambda s: s
    bwd = lambda s: nchunk - 1 - s

    def seq_of(cc):
        return jnp.where(cc < ctx_chunks, cc // (CTX_LEN // c),
                         N_CTX_SEQ + (cc - ctx_chunks) // (LAT_LEN // c))

    def lat_of(cc):
        return jnp.clip((cc - ctx_chunks) // (LAT_LEN // c), 0, N_LAT_SEQ - 1)

    def chunk_specs(pos):
        return [pl.BlockSpec((c, 256), lambda s: (pos(s), Z_AQ // 256)),
                pl.BlockSpec((c, 256), lambda s: (pos(s), Z_AK // 256)),
                pl.BlockSpec((c, 512), lambda s: (pos(s), Z_AV // 512)),
                pl.BlockSpec((c, 128), lambda s: (pos(s), Z_LR // 128))]

    def s0_spec(d, pos):
        return pl.BlockSpec((None, None, None, GLA_HEADS, GLA_DV, GLA_DK),
                            lambda s: (lat_of(pos(s)), l, d, 0, 0, 0))

    def sfin_spec(pos):
        return pl.BlockSpec((None, GLA_HEADS, GLA_DV, GLA_DK), lambda s: (seq_of(pos(s)), 0, 0, 0))

    state_sds = jax.ShapeDtypeStruct((nseq, GLA_HEADS, GLA_DV, GLA_DK), F32)
    out_sds = jax.ShapeDtypeStruct((T_ALL, 512), F32)
    state_scr = pltpu.VMEM((GLA_HEADS * GLA_DV, GLA_HEADS * GLA_DK), F32)
    return pl.pallas_call(
        _gla_kernel,
        grid=(nchunk,),
        in_specs=chunk_specs(fwd) + chunk_specs(bwd) + [
            pl.BlockSpec((None, 2, 128, 256), lambda s: (l, 0, 0, 0)),
            pl.BlockSpec((None, 2, 1, 256), lambda s: (l, 0, 0, 0)),
            pl.BlockSpec((2, GLA_GROWS, 2 * c), lambda s: (0, 0, 0)),
            pl.BlockSpec((2, c, GLA_HEADS * c), lambda s: (0, 0, 0)),
            s0_spec(0, fwd), s0_spec(1, bwd),
        ],
        out_specs=[pl.BlockSpec((c, 512), lambda s: (fwd(s), 0)),
                   pl.BlockSpec((c, 512), lambda s: (bwd(s), 0)),
                   sfin_spec(fwd), sfin_spec(bwd)],
        out_shape=[out_sds, out_sds, state_sds, state_sds],
        scratch_shapes=[state_scr, state_scr],
        compiler_params=_cparams(("arbitrary",)),
        name="gla",
    )(z, z, z, z, z, z, z, z, w2p, ba, seg, levmap, s0t, s0t)


def _merge_kernel(x_ref, of_ref, obk_ref, ar_ref, gates_ref, bc_ref, bl_ref, cc_ref, cl_ref,
                  wpa_ref, wpb_ref, wpc_ref, wo_ref, gn_ref, mod_ref, y_ref):
    ctx = pl.program_id(0) < T_CTX // x_ref.shape[0]
    o = of_ref[...] + obk_ref[...]
    heads = []
    for h in range(GLA_HEADS):
        oh = o[:, GLA_DV * h:GLA_DV * (h + 1)]
        ms = jnp.mean(oh * oh, axis=-1, keepdims=True)
        heads.append(oh * lax.rsqrt(ms + EPS) * gn_ref[...])
    oa = (jnp.concatenate(heads, axis=1) * _silu(ar_ref[...].astype(F32))).astype(BF16)
    ob = jnp.where(ctx, bc_ref[...], bl_ref[...])
    oc = jnp.where(ctx, cc_ref[...], cl_ref[...])
    d = D_MODEL
    gate = lambda j: _sigmoid(gates_ref[:, j * d:(j + 1) * d].astype(F32))
    merged = (gate(0) * _dot(oa, wpa_ref[...]) + gate(1) * _dot(ob, wpb_ref[...])
              + gate(2) * _dot(oc, wpc_ref[...]))
    out = _dot(merged.astype(BF16), wo_ref[...])
    y_ref[...] = x_ref[...] + mod_ref[:, 2 * d:3 * d] * out


def _merge(x, o_fwd, o_bwd, z, ob_ctx, ob_lat, oc_ctx, oc_lat, wpa, wpb, wpc, wo, gn, mod4, l):
    tm = 512
    ctx_blocks = T_CTX // tm
    ctx_idx = lambda i: (jnp.minimum(i, ctx_blocks - 1), 0)
    lat_idx = lambda i: (jnp.maximum(i - ctx_blocks, 0), 0)
    wspec = lambda k: pl.BlockSpec((None, k, D_MODEL), lambda i: (l, 0, 0))
    return pl.pallas_call(
        _merge_kernel,
        grid=(T_ALL // tm,),
        in_specs=[
            pl.BlockSpec((tm, D_MODEL), lambda i: (i, 0)),
            pl.BlockSpec((tm, 512), lambda i: (i, 0)),
            pl.BlockSpec((tm, 512), lambda i: (i, 0)),
            pl.BlockSpec((tm, 512), lambda i: (i, Z_AR // 512)),
            pl.BlockSpec((tm, 3 * D_MODEL), lambda i: (i, 0)),
            pl.BlockSpec((tm, 512), ctx_idx), pl.BlockSpec((tm, 512), lat_idx),
            pl.BlockSpec((tm, 512), ctx_idx), pl.BlockSpec((tm, 512), lat_idx),
            wspec(512), wspec(512), wspec(512), wspec(D_MODEL),
            pl.BlockSpec((None, 1, GLA_DV), lambda i: (l, 0, 0)),
            pl.BlockSpec((None, None, 1, 6 * D_MODEL), lambda i: (l, _group_of_rows(i, tm), 0, 0)),
        ],
        out_specs=pl.BlockSpec((tm, D_MODEL), lambda i: (i, 0)),
        out_shape=jax.ShapeDtypeStruct((T_ALL, D_MODEL), F32),
        compiler_params=_cparams(("arbitrary",)),
        name="merge",
    )(x, o_fwd, o_bwd, z, z, ob_ctx, ob_lat, oc_ctx, oc_lat, wpa, wpb, wpc, wo, gn, mod4)


def _mlp_kernel(x_ref, g_ref, mod_ref, w1_ref, w2_ref, y_ref, h_scr, acc_scr):
    k = pl.program_id(1)
    d = D_MODEL

    @pl.when(k == 0)
    def _():
        h = _norm_mod(x_ref[...], g_ref[...], mod_ref[:, 3 * d:4 * d], mod_ref[:, 4 * d:5 * d])
        h_scr[...] = h.astype(BF16)

    hid = jnp.square(jnp.maximum(_dot(h_scr[...], w1_ref[...]), 0.0)).astype(BF16)
    part = _dot(hid, w2_ref[...])

    @pl.when(k == 0)
    def _():
        acc_scr[...] = part

    @pl.when(k > 0)
    def _():
        acc_scr[...] += part

    @pl.when(k == pl.num_programs(1) - 1)
    def _():
        y_ref[...] = x_ref[...] + mod_ref[:, 5 * d:6 * d] * acc_scr[...]


def _mlp(x, norm_w, mod4, w1, w2, l):
    tm, th = 1024, 1024
    return pl.pallas_call(
        _mlp_kernel,
        grid=(T_ALL // tm, MLP_HIDDEN // th),
        in_specs=[
            pl.BlockSpec((tm, D_MODEL), lambda i, k: (i, 0)),
            pl.BlockSpec((None, 1, D_MODEL), lambda i, k: (l, 0, 0)),
            pl.BlockSpec((None, None, 1, 6 * D_MODEL),
                         lambda i, k: (l, _group_of_rows(i, tm), 0, 0)),
            pl.BlockSpec((None, D_MODEL, th), lambda i, k: (l, 0, k)),
            pl.BlockSpec((None, th, D_MODEL), lambda i, k: (l, k, 0)),
        ],
        out_specs=pl.BlockSpec((tm, D_MODEL), lambda i, k: (i, 0)),
        out_shape=jax.ShapeDtypeStruct((T_ALL, D_MODEL), F32),
        scratch_shapes=[pltpu.VMEM((tm, D_MODEL), BF16), pltpu.VMEM((tm, D_MODEL), F32)],
        compiler_params=_cparams(("arbitrary", "arbitrary")),
        name="mlp",
    )(x, norm_w, mod4, w1, w2)


def _rope_tables():
    t = jnp.arange(LAT_LEN)
    row = (t // GRID_W).astype(F32)
    col = (t % GRID_W).astype(F32)
    nf = HEAD_DIM // 4
    inv_freq = ROPE_BASE ** (-jnp.arange(nf, dtype=F32) / nf)
    ang_r = row[:, None] * inv_freq[None, :]
    ang_c = col[:, None] * inv_freq[None, :]
    cos = jnp.concatenate([jnp.cos(ang_r)] * 2 + [jnp.cos(ang_c)] * 2, axis=1)
    sin = jnp.concatenate([-jnp.sin(ang_r), jnp.sin(ang_r), -jnp.sin(ang_c), jnp.sin(ang_c)], axis=1)
    return jnp.tile(cos, (1, 2)), jnp.tile(sin, (1, 2))


def kernel(x_prompt, x_sample, state_gla, cache_swa_k, cache_swa_v, cache_na_k, cache_na_v, c,
           c_ctx, w_mod, b_mod, norm1, norm2, w_in, w_a2_f, b_a_f, w_a2_b, b_a_b, gla_onorm,
           qn_swa, kn_swa, sink_swa, qn_na, kn_na, rpb_na, w_pa, w_pb, w_pc, w_o, w_fc1, w_fc2):
    d = D_MODEL
    x = jnp.concatenate([x_prompt.reshape(T_CTX, d), x_sample.reshape(T_LAT, d)], axis=0)

    cond8 = jnp.zeros((8, d), F32).at[0].set(c_ctx).at[1:1 + N_LAT_SEQ].set(c)
    mod4 = _modulation(cond8, w_mod, b_mod).reshape(DEPTH, 8, 1, 6 * d)

    w_in_p = jnp.concatenate(
        [w_in[:, :, 3872:6944], w_in[:, :, 0:1536], w_in[:, :, 1568:3872], w_in[:, :, 1536:1568],
         jnp.zeros((DEPTH, d, Z_W - 6944), F32)], axis=-1).astype(BF16)
    wpa, wpb, wpc, wo = (w.astype(BF16) for w in (w_pa, w_pb, w_pc, w_o))
    w1, w2 = w_fc1.astype(BF16), w_fc2.astype(BF16)
    norm1r = norm1.reshape(DEPTH, 1, d)
    norm2r = norm2.reshape(DEPTH, 1, d)
    gnr = gla_onorm.reshape(DEPTH, 1, GLA_DV)
    w2p = jnp.zeros((DEPTH, 2, 128, GLA_HEADS * GLA_DK), F32)
    w2p = w2p.at[:, 0, 0:GLA_LOWRANK].set(w_a2_f).at[:, 1, GLA_LOWRANK:2 * GLA_LOWRANK].set(w_a2_b)
    w2p = w2p.astype(BF16)
    ba = jnp.stack([b_a_f, b_a_b], axis=1).reshape(DEPTH, 2, 1, GLA_HEADS * GLA_DK)
    seg_np, lev_np = _gla_constants()
    seg = jnp.asarray(seg_np, BF16)
    levmap = jnp.asarray(lev_np)
    s0t = jnp.swapaxes(state_gla, -1, -2)
    cos_t, sin_t = _rope_tables()
    bias_tab = _na_bias_tables(rpb_na)
    csk = cache_swa_k.reshape(N_LAT_SEQ, DEPTH, PAST_LEN, 128)
    csv = cache_swa_v.reshape(N_LAT_SEQ, DEPTH, PAST_LEN, 128)
    cnk = cache_na_k.reshape(N_LAT_SEQ, DEPTH, PAST_LEN, 512)
    cnv = cache_na_v.reshape(N_LAT_SEQ, DEPTH, PAST_LEN, 512)

    st_l, bk_l, bv_l, nk_l, nv_l = [], [], [], [], []
    for l in range(DEPTH):
        z = _inproj(x, norm1r, mod4, w_in_p, l)
        wqb = jnp.tile(qn_swa[l], 8)[None, :]
        wkb = jnp.tile(kn_swa[l], 2)[None, :]
        wqc = jnp.tile(qn_na[l], 8)[None, :]
        wkc = jnp.tile(kn_na[l], 8)[None, :]
        qb, kb, vb, qc, kc, vc, kbf, kcf = _attn_prep(z, wqb, wkb, wqc, wkc, cos_t, sin_t)
        o_fwd, o_bwd, sfin_f, sfin_b = _gla(z, w2p, ba, seg, levmap, s0t, l)
        ob_ctx, oc_ctx = _ctx_attention(sink_swa, qb, kb, vb, qc, kc, vc, l)
        ob_lat = _swa_attention(sink_swa, qb, kb, vb, csk, csv, l)
        oc_lat = _na_attention(qc, kc, vc, cnk, cnv, bias_tab, l)
        x = _merge(x, o_fwd, o_bwd, z, ob_ctx, ob_lat, oc_ctx, oc_lat, wpa, wpb, wpc, wo, gnr, mod4, l)
        x = _mlp(x, norm2r, mod4, w1, w2, l)

        sfin = jnp.stack([sfin_f[:N_CTX_SEQ], sfin_b[:N_CTX_SEQ]], axis=0)
        st_l.append(jnp.swapaxes(sfin, -1, -2))
        bk_l.append(kbf[:T_CTX])
        bv_l.append(z[:T_CTX, Z_ATT + A_BV:Z_ATT + A_BV + 128].astype(F32))
        nk_l.append(kcf[:T_CTX])
        nv_l.append(z[:T_CTX, Z_ATT + A_CV:Z_ATT + A_CV + 512].astype(F32))

    y_prompt = x[:T_CTX].reshape(N_CTX_SEQ, CTX_LEN, d)
    y_sample = x[T_CTX:].reshape(N_LAT_SEQ, LAT_LEN, d)
    new_state = jnp.stack(st_l, axis=0).transpose(2, 0, 1, 3, 4, 5)

    def stack_kv(parts, heads):
        a = jnp.stack(parts, axis=0).reshape(DEPTH, N_CTX_SEQ, CTX_LEN, heads, HEAD_DIM)
        return a.transpose(1, 0, 2, 3, 4)

    return (y_prompt, y_sample, new_state,
            stack_kv(bk_l, SWA_KV_HEADS), stack_kv(bv_l, SWA_KV_HEADS),
            stack_kv(nk_l, NA_HEADS), stack_kv(nv_l, NA_HEADS))
```

```python
import functools

import numpy as np
import jax
import jax.numpy as jnp
from jax import lax
from jax.experimental import pallas as pl
from jax.experimental.pallas import tpu as pltpu

F32 = jnp.float32
BF16 = jnp.bfloat16

D_MODEL = 1024
DEPTH = 4
N_CTX_SEQ = 16
CTX_LEN = 256
N_LAT_SEQ = 2
LAT_LEN = 1024
PAST_LEN = 512
T_CTX = N_CTX_SEQ * CTX_LEN
T_LAT = N_LAT_SEQ * LAT_LEN
T_ALL = T_CTX + T_LAT
GRID_W = 64
HEAD_DIM = 64
GLA_HEADS = 4
GLA_DK = 64
GLA_DV = 128
GLA_LOWRANK = 16
GLA_TAU = 16.0
SWA_Q_HEADS = 8
SWA_KV_HEADS = 2
SWA_GROUP = 4
SWA_WINDOW = 128
NA_HEADS = 8
NA_ROWS = 8
NA_COLS = 16
MLP_HIDDEN = 4 * D_MODEL
ROPE_BASE = 10000.0
EPS = 1e-6

Z_GATES = 0
Z_GLA = 3072
Z_AQ, Z_AK, Z_AV, Z_AR = 3072, 3328, 3584, 4096
Z_ATT = 4608
ATT_W = 2304
A_BQ, A_BK, A_BV, A_CQ, A_CK, A_CV = 0, 512, 640, 768, 1280, 1792
Z_LR = 6912
Z_W = 7168

GLA_C = 128
GLA_LEVELS = 7
GLA_GROWS = (GLA_LEVELS + 2) * GLA_C + 16

VMEM_LIMIT = 56 * 1024 * 1024


def _cparams(sem):
    return pltpu.CompilerParams(dimension_semantics=sem, vmem_limit_bytes=VMEM_LIMIT)


def _sigmoid(x):
    return 1.0 / (1.0 + jnp.exp(-x))


def _silu(x):
    return x * _sigmoid(x)


def _nt_dot(a, b):
    return lax.dot_general(a, b, (((1,), (1,)), ((), ())), preferred_element_type=F32)


def _tn_dot(a, b):
    return lax.dot_general(a, b, (((0,), (0,)), ((), ())), preferred_element_type=F32)


def _dot(a, b):
    return jnp.dot(a, b, preferred_element_type=F32)


def _mod_kernel(cond_ref, w_ref, b_ref, o_ref):
    s = _silu(cond_ref[...]).astype(BF16)
    o_ref[...] = _dot(s, w_ref[...].astype(BF16)) + b_ref[...]


def _modulation(cond8, w_mod, b_mod):
    tn = 1024
    return pl.pallas_call(
        _mod_kernel,
        grid=(DEPTH, 6 * D_MODEL // tn),
        in_specs=[
            pl.BlockSpec((8, D_MODEL), lambda l, j: (0, 0)),
            pl.BlockSpec((None, D_MODEL, tn), lambda l, j: (l, 0, j)),
            pl.BlockSpec((None, 1, tn), lambda l, j: (l, 0, j)),
        ],
        out_specs=pl.BlockSpec((None, 8, tn), lambda l, j: (l, 0, j)),
        out_shape=jax.ShapeDtypeStruct((DEPTH, 8, 6 * D_MODEL), F32),
        compiler_params=_cparams(("arbitrary", "arbitrary")),
        name="modulation",
    )(cond8, w_mod, b_mod.reshape(DEPTH, 1, 6 * D_MODEL))


def _group_of_rows(row_block, rows_per_block):
    first = row_block * rows_per_block
    return jnp.maximum(first - T_CTX + LAT_LEN, 0) // LAT_LEN


def _norm_mod(x, g, shift, scale):
    ms = jnp.mean(x * x, axis=-1, keepdims=True)
    y = x * lax.rsqrt(ms + EPS) * g
    return y * (1.0 + scale) + shift


def _inproj_kernel(x_ref, g_ref, mod_ref, w_ref, z_ref, h_scr):
    @pl.when(pl.program_id(1) == 0)
    def _():
        h = _norm_mod(x_ref[...], g_ref[...],
                      mod_ref[:, 0:D_MODEL], mod_ref[:, D_MODEL:2 * D_MODEL])
        h_scr[...] = h.astype(BF16)

    z_ref[...] = _dot(h_scr[...], w_ref[...]).astype(z_ref.dtype)


def _inproj(x, norm_w, mod4, w_in_p, l):
    tm, tn = 1024, 1024
    return pl.pallas_call(
        _inproj_kernel,
        grid=(T_ALL // tm, Z_W // tn),
        in_specs=[
            pl.BlockSpec((tm, D_MODEL), lambda i, j: (i, 0)),
            pl.BlockSpec((None, 1, D_MODEL), lambda i, j: (l, 0, 0)),
            pl.BlockSpec((None, None, 1, 6 * D_MODEL),
                         lambda i, j: (l, _group_of_rows(i, tm), 0, 0)),
            pl.BlockSpec((None, D_MODEL, tn), lambda i, j: (l, 0, j)),
        ],
        out_specs=pl.BlockSpec((tm, tn), lambda i, j: (i, j)),
        out_shape=jax.ShapeDtypeStruct((T_ALL, Z_W), BF16),
        scratch_shapes=[pltpu.VMEM((tm, D_MODEL), BF16)],
        compiler_params=_cparams(("arbitrary", "arbitrary")),
        name="inproj",
    )(x, norm_w, mod4, w_in_p)


def _pair_norm(x, w):
    sq = x * x
    low = lax.broadcasted_iota(jnp.int32, (1, 128), 1) < HEAD_DIM
    s_low = jnp.sum(jnp.where(low, sq, 0.0), axis=-1, keepdims=True)
    s_all = jnp.sum(sq, axis=-1, keepdims=True)
    ms = jnp.where(low, s_low, s_all - s_low) * (1.0 / HEAD_DIM)
    return x * lax.rsqrt(ms + EPS) * w


def _rope(x, cos, sin_signed):
    n = x.shape[-1]
    lane = lax.broadcasted_iota(jnp.int32, (1, n), 1)
    first = (lane % 32) < 16
    partner = jnp.where(first, pltpu.roll(x, n - 16, 1), pltpu.roll(x, 16, 1))
    return x * cos + partner * sin_signed


def _prep_kernel(z_ref, wqb_ref, wkb_ref, wqc_ref, wkc_ref, cos_ref, sin_ref,
                 qb_ref, kb_ref, vb_ref, qc_ref, kc_ref, vc_ref):
    qscale = HEAD_DIM ** -0.5

    def normed(col, width, w_ref):
        tiles = []
        for p in range(width // 128):
            x = z_ref[:, col + 128 * p: col + 128 * (p + 1)].astype(F32)
            tiles.append(_pair_norm(x, w_ref[:, 128 * p:128 * (p + 1)]))
        return tiles

    qb = normed(A_BQ, 512, wqb_ref)
    kb = normed(A_BK, 128, wkb_ref)
    qc = normed(A_CQ, 512, wqc_ref)
    kc = normed(A_CK, 512, wkc_ref)

    vb_ref[...] = z_ref[:, A_BV:A_BV + 128].astype(F32).T.astype(BF16)
    vc_ref[...] = z_ref[:, A_CV:A_CV + 512]
    for p in range(4):
        qc_ref[:, 128 * p:128 * (p + 1)] = (qc[p] * qscale).astype(BF16)
        kc_ref[:, 128 * p:128 * (p + 1)] = kc[p].astype(BF16)

    cos = cos_ref[...]
    sin = sin_ref[...]
    for p in range(4):
        qb_ref[128 * p:128 * (p + 1), :] = (_rope(qb[p], cos, sin) * qscale).T.astype(BF16)
    kb_ref[...] = _rope(kb[0], cos, sin).astype(BF16)


def _attn_prep(z, wqb, wkb, wqc, wkc, cos_t, sin_t):
    tm = 512
    lat_blocks = LAT_LEN // tm
    rope_idx = lambda i: (i % lat_blocks, 0)
    row = lambda w: pl.BlockSpec((tm, w), lambda i: (i, 0))
    col = lambda w: pl.BlockSpec((w, tm), lambda i: (0, i))
    cst = lambda w: pl.BlockSpec((1, w), lambda i: (0, 0))
    sds = lambda w: jax.ShapeDtypeStruct((T_LAT, w), BF16)
    sds_t = lambda w: jax.ShapeDtypeStruct((w, T_LAT), BF16)
    return pl.pallas_call(
        _prep_kernel,
        grid=(T_LAT // tm,),
        in_specs=[
            pl.BlockSpec((tm, ATT_W), lambda i: (T_CTX // tm + i, Z_ATT // ATT_W)),
            cst(512), cst(128), cst(512), cst(512),
            pl.BlockSpec((tm, 128), rope_idx),
            pl.BlockSpec((tm, 128), rope_idx),
        ],
        out_specs=[col(512), row(128), col(128), row(512), row(512), row(512)],
        out_shape=[sds_t(512), sds(128), sds_t(128), sds(512), sds(512), sds(512)],
        compiler_params=_cparams(("arbitrary",)),
        name="attn_prep",
    )(z, wqb, wkb, wqc, wkc, cos_t, sin_t)


def _half_rows(t, half):
    z = jnp.zeros((HEAD_DIM, t.shape[1]), t.dtype)
    return jnp.concatenate([t[:HEAD_DIM], z] if half == 0 else [z, t[HEAD_DIM:]], axis=0)


def _ctx_attn_kernel(sink_ref, z_ref, wqb_ref, wkb_ref, wqc_ref, wkc_ref,
                     ob_ref, oc_ref, kbf_ref, kcf_ref, *, layer):
    n = z_ref.shape[0]
    qscale = HEAD_DIM ** -0.5

    def pair(col, p):
        return z_ref[:, col + 128 * p: col + 128 * (p + 1)]

    def q_t(col, w_ref, p):
        y = _pair_norm(pair(col, p).astype(F32), w_ref[:, 128 * p:128 * (p + 1)]) * qscale
        return y.T.astype(BF16)

    def v_t(col, p):
        return pair(col, p).astype(F32).T.astype(BF16)

    kb = _pair_norm(pair(A_BK, 0).astype(F32), wkb_ref[...])
    kbf_ref[...] = kb
    kb = kb.astype(BF16)
    qb_t = [q_t(A_BQ, wqb_ref, p) for p in range(4)]
    scores = []
    for kvh in range(SWA_KV_HEADS):
        placed = []
        for h in range(SWA_GROUP * kvh, SWA_GROUP * (kvh + 1)):
            rows = qb_t[h // 2][HEAD_DIM * (h % 2):HEAD_DIM * (h % 2 + 1)]
            z = jnp.zeros_like(rows)
            placed.append(jnp.concatenate([rows, z] if kvh == 0 else [z, rows], axis=0))
        scores.append(_dot(kb, jnp.concatenate(placed, axis=1)))
    for p in range(NA_HEADS // 2):
        kc = _pair_norm(pair(A_CK, p).astype(F32), wkc_ref[:, 128 * p:128 * (p + 1)])
        kcf_ref[:, 128 * p:128 * (p + 1)] = kc
        qc_t = q_t(A_CQ, wqc_ref, p)
        q2 = jnp.concatenate([_half_rows(qc_t, 0), _half_rows(qc_t, 1)], axis=1)
        scores.append(_dot(kc.astype(BF16), q2))
    s_t = jnp.concatenate(scores, axis=1)

    sink = jnp.concatenate(
        [jnp.full((1, n), sink_ref[layer, h], F32) for h in range(SWA_Q_HEADS)]
        + [jnp.full((1, n * NA_HEADS), -jnp.inf, F32)], axis=1)
    mx = jnp.maximum(s_t.max(axis=0, keepdims=True), sink)
    p_t = jnp.exp(s_t - mx)
    inv = 1.0 / (p_t.sum(axis=0, keepdims=True) + jnp.exp(sink - mx))
    p_t = p_t.astype(BF16)

    nb = SWA_Q_HEADS * n
    o = _dot(v_t(A_BV, 0), p_t[:, :nb]) * inv[:, :nb]
    outs = [o[HEAD_DIM * (h // SWA_GROUP):HEAD_DIM * (h // SWA_GROUP + 1), n * h:n * (h + 1)]
            for h in range(SWA_Q_HEADS)]
    ob_ref[...] = jnp.concatenate(outs, axis=0).T.astype(BF16)
    outs = []
    for p in range(NA_HEADS // 2):
        cols = slice(nb + 2 * n * p, nb + 2 * n * (p + 1))
        o = _dot(v_t(A_CV, p), p_t[:, cols]) * inv[:, cols]
        outs += [o[:HEAD_DIM, :n], o[HEAD_DIM:, n:]]
    oc_ref[...] = jnp.concatenate(outs, axis=0).T.astype(BF16)


def _ctx_attention(sink, z, wqb, wkb, wqc, wkc, l):
    blk = lambda w, j=0: pl.BlockSpec((CTX_LEN, w), lambda b: (b, j))
    cst = lambda w: pl.BlockSpec((1, w), lambda b: (0, 0))
    sds = lambda w, dt: jax.ShapeDtypeStruct((T_CTX, w), dt)
    return pl.pallas_call(
        functools.partial(_ctx_attn_kernel, layer=l),
        grid=(N_CTX_SEQ,),
        in_specs=[pl.BlockSpec(memory_space=pltpu.SMEM), blk(ATT_W, Z_ATT // ATT_W),
                  cst(512), cst(128), cst(512), cst(512)],
        out_specs=[blk(512), blk(512), blk(128), blk(512)],
        out_shape=[sds(512, BF16), sds(512, BF16), sds(128, F32), sds(512, F32)],
        compiler_params=_cparams(("arbitrary",)),
        name="ctx_attention",
    )(sink, z, wqb, wkb, wqc, wkc)


SWA_QBLK = 128
SWA_SPAN = SWA_QBLK + 2 * SWA_WINDOW


def _swa_kernel(sink_ref, qt_ref, k_ref, vt_ref, kc_ref, vc_ref, o_ref, kctx_scr, vctx_scr, *, layer):
    i = pl.program_id(1)
    nq = SWA_QBLK

    @pl.when(i == 0)
    def _():
        kctx_scr[...] = kc_ref[...].astype(BF16)
        vctx_scr[...] = vc_ref[...].T.astype(BF16)

    start = pl.multiple_of(jnp.clip(nq * (i - 1), 0, LAT_LEN - SWA_SPAN), nq)
    kwin = k_ref[pl.ds(start, SWA_SPAN), :]
    vwin_t = vt_ref[:, pl.ds(start, SWA_SPAN)]
    kctx = kctx_scr[...]
    s_lat, s_ctx = [], []
    for kvh in range(SWA_KV_HEADS):
        placed = []
        for h in range(SWA_GROUP * kvh, SWA_GROUP * (kvh + 1)):
            rows = qt_ref[HEAD_DIM * h:HEAD_DIM * (h + 1), :]
            z = jnp.zeros_like(rows)
            placed.append(jnp.concatenate([rows, z] if kvh == 0 else [z, rows], axis=0))
        q4 = jnp.concatenate(placed, axis=1)
        s_lat.append(_dot(kwin, q4))
        s_ctx.append(_dot(kctx, q4))
    s_lat = jnp.concatenate(s_lat, axis=1)
    s_ctx = jnp.concatenate(s_ctx, axis=1)

    kpos = start + lax.broadcasted_iota(jnp.int32, (SWA_SPAN, nq), 0)
    qpos = nq * i + lax.broadcasted_iota(jnp.int32, (SWA_SPAN, nq), 1)
    band = jnp.where(jnp.abs(kpos - qpos) <= SWA_WINDOW, 0.0, -jnp.inf)
    s_lat = s_lat + jnp.concatenate([band] * SWA_Q_HEADS, axis=1)
    sink = jnp.concatenate(
        [jnp.full((1, nq), sink_ref[layer, h], F32) for h in range(SWA_Q_HEADS)], axis=1)
    mx = jnp.maximum(jnp.maximum(s_lat.max(axis=0, keepdims=True),
                                 s_ctx.max(axis=0, keepdims=True)), sink)
    p_lat = jnp.exp(s_lat - mx)
    p_ctx = jnp.exp(s_ctx - mx)
    inv = 1.0 / (p_lat.sum(axis=0, keepdims=True) + p_ctx.sum(axis=0, keepdims=True)
                 + jnp.exp(sink - mx))
    o = (_dot(vwin_t, p_lat.astype(BF16)) + _dot(vctx_scr[...], p_ctx.astype(BF16))) * inv
    outs = [o[HEAD_DIM * (h // SWA_GROUP):HEAD_DIM * (h // SWA_GROUP + 1), nq * h:nq * (h + 1)]
            for h in range(SWA_Q_HEADS)]
    o_ref[...] = jnp.concatenate(outs, axis=0).T.astype(BF16)


def _swa_attention(sink, qb_t, kb, vb_t, cache_k, cache_v, l):
    nq = LAT_LEN // SWA_QBLK
    cache = pl.BlockSpec((None, None, PAST_LEN, 128), lambda b, i: (b, l, 0, 0))
    return pl.pallas_call(
        functools.partial(_swa_kernel, layer=l),
        grid=(N_LAT_SEQ, nq),
        in_specs=[pl.BlockSpec(memory_space=pltpu.SMEM),
                  pl.BlockSpec((512, SWA_QBLK), lambda b, i: (0, b * nq + i)),
                  pl.BlockSpec((LAT_LEN, 128), lambda b, i: (b, 0)),
                  pl.BlockSpec((128, LAT_LEN), lambda b, i: (0, b)),
                  cache, cache],
        out_specs=pl.BlockSpec((SWA_QBLK, 512), lambda b, i: (b * nq + i, 0)),
        out_shape=jax.ShapeDtypeStruct((T_LAT, 512), BF16),
        scratch_shapes=[pltpu.VMEM((PAST_LEN, 128), BF16), pltpu.VMEM((128, PAST_LEN), BF16)],
        compiler_params=_cparams(("arbitrary", "arbitrary")),
        name="swa_attention",
    )(sink, qb_t, kb, vb_t, cache_k, cache_v)


LAT_ROWS = LAT_LEN // GRID_W
NA_WIN_ROWS = min(NA_ROWS, LAT_ROWS)
NA_KEYS = NA_WIN_ROWS * GRID_W


def _na_kernel(q_ref, k_ref, v_ref, kc_ref, vc_ref, bias_ref, o_ref, kctx_scr, vctx_scr):
    r = pl.program_id(1)

    @pl.when(r == 0)
    def _():
        kctx_scr[...] = kc_ref[...].astype(BF16)
        vctx_scr[...] = vc_ref[...].astype(BF16)

    rs = jnp.clip(r - NA_ROWS // 2, 0, LAT_ROWS - NA_WIN_ROWS)
    start = pl.multiple_of(rs * GRID_W, GRID_W)
    base = rs - r + NA_ROWS - 1
    low = lax.broadcasted_iota(jnp.int32, (1, 128), 1) < HEAD_DIM
    npair = NA_HEADS // 2
    s_lat, s_ctx = [], []
    for p in range(npair):
        sl = slice(128 * p, 128 * (p + 1))
        q = q_ref[:, sl]
        zero = jnp.zeros_like(q)
        q2 = jnp.concatenate([jnp.where(low, q, zero), jnp.where(low, zero, q)], axis=0)
        bias = jnp.concatenate(
            [jnp.concatenate([bias_ref[2 * p + hh, base + 2 * w] for w in range(NA_WIN_ROWS // 2)],
                             axis=1) for hh in range(2)], axis=0)
        s_lat.append(_nt_dot(q2, k_ref[pl.ds(start, NA_KEYS), sl]) + bias)
        s_ctx.append(_nt_dot(q2, kctx_scr[:, sl]))
    s_lat = jnp.concatenate(s_lat, axis=0)
    s_ctx = jnp.concatenate(s_ctx, axis=0)
    mx = jnp.maximum(s_lat.max(axis=-1, keepdims=True), s_ctx.max(axis=-1, keepdims=True))
    p_lat = jnp.exp(s_lat - mx)
    p_ctx = jnp.exp(s_ctx - mx)
    inv = 1.0 / (p_lat.sum(axis=-1, keepdims=True) + p_ctx.sum(axis=-1, keepdims=True))
    p_lat = p_lat.astype(BF16)
    p_ctx = p_ctx.astype(BF16)
    tiles = []
    for p in range(npair):
        sl = slice(128 * p, 128 * (p + 1))
        rows = slice(2 * GRID_W * p, 2 * GRID_W * (p + 1))
        o2 = (_dot(p_lat[rows], v_ref[pl.ds(start, NA_KEYS), sl])
              + _dot(p_ctx[rows], vctx_scr[:, sl])) * inv[rows]
        tiles.append(jnp.where(low, o2[:GRID_W], o2[GRID_W:]))
    o_ref[...] = jnp.concatenate(tiles, axis=1).astype(BF16)


def _na_attention(qc, kc, vc, cache_k, cache_v, bias_tab, l):
    kv = pl.BlockSpec((LAT_LEN, 512), lambda b, r: (b, 0))
    cache = pl.BlockSpec((None, None, PAST_LEN, 512), lambda b, r: (b, l, 0, 0))
    return pl.pallas_call(
        _na_kernel,
        grid=(N_LAT_SEQ, LAT_ROWS),
        in_specs=[pl.BlockSpec((GRID_W, 512), lambda b, r: (b * LAT_ROWS + r, 0)),
                  kv, kv, cache, cache,
                  pl.BlockSpec((None, NA_HEADS, 2 * NA_ROWS - 2, GRID_W, 128),
                               lambda b, r: (l, 0, 0, 0, 0))],
        out_specs=pl.BlockSpec((GRID_W, 512), lambda b, r: (b * LAT_ROWS + r, 0)),
        out_shape=jax.ShapeDtypeStruct((T_LAT, 512), BF16),
        scratch_shapes=[pltpu.VMEM((PAST_LEN, 512), BF16), pltpu.VMEM((PAST_LEN, 512), BF16)],
        compiler_params=_cparams(("arbitrary", "arbitrary")),
        name="na_attention",
    )(qc, kc, vc, cache_k, cache_v, bias_tab)


def _na_bias_tables(rpb):
    cq = np.arange(GRID_W)[:, None]
    ck = np.arange(GRID_W)[None, :]
    cs = np.clip(cq - NA_COLS // 2, 0, GRID_W - NA_COLS)
    col_ok = (ck >= cs) & (ck < cs + NA_COLS)
    period = 2 * GRID_W + 1
    v = jnp.zeros(rpb.shape[:-1] + (period,), F32)
    v = v.at[..., 0:NA_COLS].set(rpb[..., NA_COLS - 1:])
    v = v.at[..., period - (NA_COLS - 1):].set(rpb[..., :NA_COLS - 1])
    flat = jnp.tile(v, (1, 1, 1, GRID_W))[..., :GRID_W * (period - 1)]
    t = flat.reshape(rpb.shape[:-1] + (GRID_W, period - 1))[..., :GRID_W]
    t = jnp.where(col_ok, t, -jnp.inf)
    return jnp.concatenate([t[:, :, :-1], t[:, :, 1:]], axis=-1)


def _gla_constants():
    c = GLA_C
    t = np.arange(c)[:, None]
    u = np.arange(c)[None, :]
    blocks = []
    for k in range(GLA_LEVELS):
        b = 1 << k
        m = ((t >> k) | 1) * b - 1
        query = ((t >> k) & 1) == 1
        blocks.append(np.where(query, (u > m) & (u <= t), (u > t) & (u <= m)))
    blocks.append(u <= t)
    blocks.append(u > t)
    blocks.append(np.ones((16, c), bool))
    fwd = np.concatenate(blocks, axis=0).astype(np.float32)
    bwd_blocks = [blk[::-1, ::-1] for blk in blocks]
    bwd = np.concatenate(bwd_blocks, axis=0).astype(np.float32)
    seg = np.stack([fwd, bwd])
    seg = np.concatenate([seg, seg], axis=-1)

    s = np.arange(c)[None, :]
    x = t ^ s
    lev = np.where(x == 0, GLA_LEVELS, np.floor(np.log2(np.maximum(x, 1))).astype(np.int64))
    lev_f = np.where(s <= t, lev, -1)
    lev_b = np.where(s >= t, lev, -1)
    levmap = np.stack([np.tile(lev_f, (1, GLA_HEADS)), np.tile(lev_b, (1, GLA_HEADS))])
    return seg, levmap.astype(np.int32)


def _gla_state_init(d, cc, s0_ref, st_ref):
    c = GLA_C
    ctx_chunks = T_CTX // c
    is_lat = cc >= ctx_chunks
    per_seq = jnp.where(is_lat, LAT_LEN // c, CTX_LEN // c)
    pos = jnp.where(is_lat, cc - ctx_chunks, cc) % per_seq
    first = (pos == 0) if d == 0 else (pos == per_seq - 1)

    @pl.when(first)
    def _():
        for h in range(GLA_HEADS):
            blk = jnp.where(is_lat, s0_ref[h], 0.0)
            row = [jnp.zeros((GLA_DV, GLA_DK), F32)] * GLA_HEADS
            row[h] = blk
            st_ref[GLA_DV * h:GLA_DV * (h + 1), :] = jnp.concatenate(row, axis=1)


def _gla_chunk(q_ref, k_ref, v_ref, lr_ref, w2_ref, ba_ref, seg_ref, lev_ref, o_ref, sfin_ref, st_ref):
    c = GLA_C
    x = _dot(lr_ref[...], w2_ref[...]) + ba_ref[...]
    la = (jnp.minimum(x, 0.0) - jnp.log1p(jnp.exp(-jnp.abs(x)))) * (1.0 / GLA_TAU)
    la_hi = la.astype(BF16)
    la_lo = (la - la_hi.astype(F32)).astype(BF16)
    seg = _dot(seg_ref[...], jnp.concatenate([la_hi, la_lo], axis=0))

    q = q_ref[...].astype(F32) * (GLA_DK ** -0.5)
    k = k_ref[...].astype(F32)
    v = v_ref[...]
    lev = lev_ref[...]
    lane_head = lax.broadcasted_iota(jnp.int32, (1, GLA_HEADS * GLA_DK), 1) // GLA_DK
    head_sel = [jnp.where(lane_head == h, 1.0, 0.0).astype(BF16) for h in range(GLA_HEADS)]

    def pair_scores(qs, ks):
        ksb = ks.astype(BF16)
        kbd = jnp.concatenate([ksb * head_sel[h] for h in range(GLA_HEADS)], axis=0)
        return _nt_dot(qs.astype(BF16), kbd)

    attn = jnp.where(lev == GLA_LEVELS, pair_scores(q, k), 0.0)
    for lvl in range(GLA_LEVELS):
        e = jnp.exp(seg[c * lvl:c * (lvl + 1)])
        attn = jnp.where(lev == lvl, pair_scores(q * e, k * e), attn)
    attn = attn.astype(BF16)

    cum = seg[c * GLA_LEVELS:c * (GLA_LEVELS + 1)]
    rem = seg[c * (GLA_LEVELS + 1):c * (GLA_LEVELS + 2)]
    tot = seg[c * (GLA_LEVELS + 2):c * (GLA_LEVELS + 2) + 1]
    q_in = (q * jnp.exp(cum)).astype(BF16)
    k_in = (k * jnp.exp(rem)).astype(BF16)
    state = st_ref[...]
    o_inter = _nt_dot(q_in, state.astype(BF16))
    for h in range(GLA_HEADS):
        sl = slice(GLA_DV * h, GLA_DV * (h + 1))
        o_ref[:, sl] = o_inter[:, sl] + _dot(attn[:, c * h:c * (h + 1)], v[:, sl])

    upd = _tn_dot(v, k_in)
    row_head = lax.broadcasted_iota(jnp.int32, (GLA_HEADS * GLA_DV, 1), 0) // GLA_DV
    new_state = state * jnp.exp(tot) + jnp.where(row_head == lane_head, upd, 0.0)
    st_ref[...] = new_state
    for h in range(GLA_HEADS):
        sfin_ref[h] = new_state[GLA_DV * h:GLA_DV * (h + 1), GLA_DK * h:GLA_DK * (h + 1)]


def _gla_kernel(qf, kf, vf, lrf, qb, kb, vb, lrb, w2_ref, ba_ref, seg_ref, lev_ref, s0f, s0b,
                of_ref, ob_ref, sff_ref, sfb_ref, stf, stb):
    step = pl.program_id(0)
    last = pl.num_programs(0) - 1
    _gla_state_init(0, step, s0f, stf)
    _gla_state_init(1, last - step, s0b, stb)
    _gla_chunk(qf, kf, vf, lrf, w2_ref.at[0], ba_ref.at[0], seg_ref.at[0], lev_ref.at[0],
               of_ref, sff_ref, stf)
    _gla_chunk(qb, kb, vb, lrb, w2_ref.at[1], ba_ref.at[1], seg_ref.at[1], lev_ref.at[1],
               ob_ref, sfb_ref, stb)


def _gla(z, w2p, ba, seg, levmap, s0t, l):
    c = GLA_C
    nchunk = T_ALL // c
    ctx_chunks = T_CTX // c
    nseq = N_CTX_SEQ + N_LAT_SEQ
    fwd = lambda s: s
    bwd = lambda s: nchunk - 1 - s

    def seq_of(cc):
        return jnp.where(cc < ctx_chunks, cc // (CTX_LEN // c),
                         N_CTX_SEQ + (cc - ctx_chunks) // (LAT_LEN // c))

    def lat_of(cc):
        return jnp.clip((cc - ctx_chunks) // (LAT_LEN // c), 0, N_LAT_SEQ - 1)

    def chunk_specs(pos):
        return [pl.BlockSpec((c, 256), lambda s: (pos(s), Z_AQ // 256)),
                pl.BlockSpec((c, 256), lambda s: (pos(s), Z_AK // 256)),
                pl.BlockSpec((c, 512), lambda s: (pos(s), Z_AV // 512)),
                pl.BlockSpec((c, 128), lambda s: (pos(s), Z_LR // 128))]

    def s0_spec(d, pos):
        return pl.BlockSpec((None, None, None, GLA_HEADS, GLA_DV, GLA_DK),
                            lambda s: (lat_of(pos(s)), l, d, 0, 0, 0))

    def sfin_spec(pos):
        return pl.BlockSpec((None, GLA_HEADS, GLA_DV, GLA_DK), lambda s: (seq_of(pos(s)), 0, 0, 0))

    state_sds = jax.ShapeDtypeStruct((nseq, GLA_HEADS, GLA_DV, GLA_DK), F32)
    out_sds = jax.ShapeDtypeStruct((T_ALL, 512), F32)
    state_scr = pltpu.VMEM((GLA_HEADS * GLA_DV, GLA_HEADS * GLA_DK), F32)
    return pl.pallas_call(
        _gla_kernel,
        grid=(nchunk,),
        in_specs=chunk_specs(fwd) + chunk_specs(bwd) + [
            pl.BlockSpec((None, 2, 128, 256), lambda s: (l, 0, 0, 0)),
            pl.BlockSpec((None, 2, 1, 256), lambda s: (l, 0, 0, 0)),
            pl.BlockSpec((2, GLA_GROWS, 2 * c), lambda s: (0, 0, 0)),
            pl.BlockSpec((2, c, GLA_HEADS * c), lambda s: (0, 0, 0)),
            s0_spec(0, fwd), s0_spec(1, bwd),
        ],
        out_specs=[pl.BlockSpec((c, 512), lambda s: (fwd(s), 0)),
                   pl.BlockSpec((c, 512), lambda s: (bwd(s), 0)),
                   sfin_spec(fwd), sfin_spec(bwd)],
        out_shape=[out_sds, out_sds, state_sds, state_sds],
        scratch_shapes=[state_scr, state_scr],
        compiler_params=_cparams(("arbitrary",)),
        name="gla",
    )(z, z, z, z, z, z, z, z, w2p, ba, seg, levmap, s0t, s0t)


def _merge_kernel(x_ref, of_ref, obk_ref, ar_ref, gates_ref, bc_ref, bl_ref, cc_ref, cl_ref,
                  wpa_ref, wpb_ref, wpc_ref, wo_ref, gn_ref, mod_ref, y_ref):
    ctx = pl.program_id(0) < T_CTX // x_ref.shape[0]
    o = of_ref[...] + obk_ref[...]
    heads = []
    for h in range(GLA_HEADS):
        oh = o[:, GLA_DV * h:GLA_DV * (h + 1)]
        ms = jnp.mean(oh * oh, axis=-1, keepdims=True)
        heads.append(oh * lax.rsqrt(ms + EPS) * gn_ref[...])
    oa = (jnp.concatenate(heads, axis=1) * _silu(ar_ref[...].astype(F32))).astype(BF16)
    ob = jnp.where(ctx, bc_ref[...], bl_ref[...])
    oc = jnp.where(ctx, cc_ref[...], cl_ref[...])
    d = D_MODEL
    gate = lambda j: _sigmoid(gates_ref[:, j * d:(j + 1) * d].astype(F32))
    merged = (gate(0) * _dot(oa, wpa_ref[...]) + gate(1) * _dot(ob, wpb_ref[...])
              + gate(2) * _dot(oc, wpc_ref[...]))
    out = _dot(merged.astype(BF16), wo_ref[...])
    y_ref[...] = x_ref[...] + mod_ref[:, 2 * d:3 * d] * out


def _merge(x, o_fwd, o_bwd, z, ob_ctx, ob_lat, oc_ctx, oc_lat, wpa, wpb, wpc, wo, gn, mod4, l):
    tm = 512
    ctx_blocks = T_CTX // tm
    ctx_idx = lambda i: (jnp.minimum(i, ctx_blocks - 1), 0)
    lat_idx = lambda i: (jnp.maximum(i - ctx_blocks, 0), 0)
    wspec = lambda k: pl.BlockSpec((None, k, D_MODEL), lambda i: (l, 0, 0))
    return pl.pallas_call(
        _merge_kernel,
        grid=(T_ALL // tm,),
        in_specs=[
            pl.BlockSpec((tm, D_MODEL), lambda i: (i, 0)),
            pl.BlockSpec((tm, 512), lambda i: (i, 0)),
            pl.BlockSpec((tm, 512), lambda i: (i, 0)),
            pl.BlockSpec((tm, 512), lambda i: (i, Z_AR // 512)),
            pl.BlockSpec((tm, 3 * D_MODEL), lambda i: (i, 0)),
            pl.BlockSpec((tm, 512), ctx_idx), pl.BlockSpec((tm, 512), lat_idx),
            pl.BlockSpec((tm, 512), ctx_idx), pl.BlockSpec((tm, 512), lat_idx),
            wspec(512), wspec(512), wspec(512), wspec(D_MODEL),
            pl.BlockSpec((None, 1, GLA_DV), lambda i: (l, 0, 0)),
            pl.BlockSpec((None, None, 1, 6 * D_MODEL), lambda i: (l, _group_of_rows(i, tm), 0, 0)),
        ],
        out_specs=pl.BlockSpec((tm, D_MODEL), lambda i: (i, 0)),
        out_shape=jax.ShapeDtypeStruct((T_ALL, D_MODEL), F32),
        compiler_params=_cparams(("arbitrary",)),
        name="merge",
    )(x, o_fwd, o_bwd, z, z, ob_ctx, ob_lat, oc_ctx, oc_lat, wpa, wpb, wpc, wo, gn, mod4)


def _mlp_kernel(x_ref, g_ref, mod_ref, w1_ref, w2_ref, y_ref, h_scr, acc_scr):
    k = pl.program_id(1)
    d = D_MODEL

    @pl.when(k == 0)
    def _():
        h = _norm_mod(x_ref[...], g_ref[...], mod_ref[:, 3 * d:4 * d], mod_ref[:, 4 * d:5 * d])
        h_scr[...] = h.astype(BF16)

    hid = jnp.square(jnp.maximum(_dot(h_scr[...], w1_ref[...]), 0.0)).astype(BF16)
    part = _dot(hid, w2_ref[...])

    @pl.when(k == 0)
    def _():
        acc_scr[...] = part

    @pl.when(k > 0)
    def _():
        acc_scr[...] += part

    @pl.when(k == pl.num_programs(1) - 1)
    def _():
        y_ref[...] = x_ref[...] + mod_ref[:, 5 * d:6 * d] * acc_scr[...]


def _mlp(x, norm_w, mod4, w1, w2, l):
    tm, th = 1024, 1024
    return pl.pallas_call(
        _mlp_kernel,
        grid=(T_ALL // tm, MLP_HIDDEN // th),
        in_specs=[
            pl.BlockSpec((tm, D_MODEL), lambda i, k: (i, 0)),
            pl.BlockSpec((None, 1, D_MODEL), lambda i, k: (l, 0, 0)),
            pl.BlockSpec((None, None, 1, 6 * D_MODEL),
                         lambda i, k: (l, _group_of_rows(i, tm), 0, 0)),
            pl.BlockSpec((None, D_MODEL, th), lambda i, k: (l, 0, k)),
            pl.BlockSpec((None, th, D_MODEL), lambda i, k: (l, k, 0)),
        ],
        out_specs=pl.BlockSpec((tm, D_MODEL), lambda i, k: (i, 0)),
        out_shape=jax.ShapeDtypeStruct((T_ALL, D_MODEL), F32),
        scratch_shapes=[pltpu.VMEM((tm, D_MODEL), BF16), pltpu.VMEM((tm, D_MODEL), F32)],
        compiler_params=_cparams(("arbitrary", "arbitrary")),
        name="mlp",
    )(x, norm_w, mod4, w1, w2)


def _rope_tables():
    t = jnp.arange(LAT_LEN)
    row = (t // GRID_W).astype(F32)
    col = (t % GRID_W).astype(F32)
    nf = HEAD_DIM // 4
    inv_freq = ROPE_BASE ** (-jnp.arange(nf, dtype=F32) / nf)
    ang_r = row[:, None] * inv_freq[None, :]
    ang_c = col[:, None] * inv_freq[None, :]
    cos = jnp.concatenate([jnp.cos(ang_r)] * 2 + [jnp.cos(ang_c)] * 2, axis=1)
    sin = jnp.concatenate([-jnp.sin(ang_r), jnp.sin(ang_r), -jnp.sin(ang_c), jnp.sin(ang_c)], axis=1)
    return jnp.tile(cos, (1, 2)), jnp.tile(sin, (1, 2))


def kernel(x_prompt, x_sample, state_gla, cache_swa_k, cache_swa_v, cache_na_k, cache_na_v, c,
           c_ctx, w_mod, b_mod, norm1, norm2, w_in, w_a2_f, b_a_f, w_a2_b, b_a_b, gla_onorm,
           qn_swa, kn_swa, sink_swa, qn_na, kn_na, rpb_na, w_pa, w_pb, w_pc, w_o, w_fc1, w_fc2):
    d = D_MODEL
    x = jnp.concatenate([x_prompt.reshape(T_CTX, d), x_sample.reshape(T_LAT, d)], axis=0)

    cond8 = jnp.zeros((8, d), F32).at[0].set(c_ctx).at[1:1 + N_LAT_SEQ].set(c)
    mod4 = _modulation(cond8, w_mod, b_mod).reshape(DEPTH, 8, 1, 6 * d)

    w_in_p = jnp.concatenate(
        [w_in[:, :, 3872:6944], w_in[:, :, 0:1536], w_in[:, :, 1568:3872], w_in[:, :, 1536:1568],
         jnp.zeros((DEPTH, d, Z_W - 6944), F32)], axis=-1).astype(BF16)
    wpa, wpb, wpc, wo = (w.astype(BF16) for w in (w_pa, w_pb, w_pc, w_o))
    w1, w2 = w_fc1.astype(BF16), w_fc2.astype(BF16)
    norm1r = norm1.reshape(DEPTH, 1, d)
    norm2r = norm2.reshape(DEPTH, 1, d)
    gnr = gla_onorm.reshape(DEPTH, 1, GLA_DV)
    w2p = jnp.zeros((DEPTH, 2, 128, GLA_HEADS * GLA_DK), F32)
    w2p = w2p.at[:, 0, 0:GLA_LOWRANK].set(w_a2_f).at[:, 1, GLA_LOWRANK:2 * GLA_LOWRANK].set(w_a2_b)
    w2p = w2p.astype(BF16)
    ba = jnp.stack([b_a_f, b_a_b], axis=1).reshape(DEPTH, 2, 1, GLA_HEADS * GLA_DK)
    seg_np, lev_np = _gla_constants()
    seg = jnp.asarray(seg_np, BF16)
    levmap = jnp.asarray(lev_np)
    s0t = jnp.swapaxes(state_gla, -1, -2)
    cos_t, sin_t = _rope_tables()
    bias_tab = _na_bias_tables(rpb_na)
    csk = cache_swa_k.reshape(N_LAT_SEQ, DEPTH, PAST_LEN, 128)
    csv = cache_swa_v.reshape(N_LAT_SEQ, DEPTH, PAST_LEN, 128)
    cnk = cache_na_k.reshape(N_LAT_SEQ, DEPTH, PAST_LEN, 512)
    cnv = cache_na_v.reshape(N_LAT_SEQ, DEPTH, PAST_LEN, 512)

    st_l, bk_l, bv_l, nk_l, nv_l = [], [], [], [], []
    for l in range(DEPTH):
        z = _inproj(x, norm1r, mod4, w_in_p, l)
        wqb = jnp.tile(qn_swa[l], 8)[None, :]
        wkb = jnp.tile(kn_swa[l], 2)[None, :]
        wqc = jnp.tile(qn_na[l], 8)[None, :]
        wkc = jnp.tile(kn_na[l], 8)[None, :]
        qb, kb, vb, qc, kc, vc = _attn_prep(z, wqb, wkb, wqc, wkc, cos_t, sin_t)
        o_fwd, o_bwd, sfin_f, sfin_b = _gla(z, w2p, ba, seg, levmap, s0t, l)
        ob_ctx, oc_ctx, kbf, kcf = _ctx_attention(sink_swa, z, wqb, wkb, wqc, wkc, l)
        ob_lat = _swa_attention(sink_swa, qb, kb, vb, csk, csv, l)
        oc_lat = _na_attention(qc, kc, vc, cnk, cnv, bias_tab, l)
        x = _merge(x, o_fwd, o_bwd, z, ob_ctx, ob_lat, oc_ctx, oc_lat, wpa, wpb, wpc, wo, gnr, mod4, l)
        x = _mlp(x, norm2r, mod4, w1, w2, l)

        sfin = jnp.stack([sfin_f[:N_CTX_SEQ], sfin_b[:N_CTX_SEQ]], axis=0)
        st_l.append(jnp.swapaxes(sfin, -1, -2))
        bk_l.append(kbf)
        bv_l.append(z[:T_CTX, Z_ATT + A_BV:Z_ATT + A_BV + 128].astype(F32))
        nk_l.append(kcf)
        nv_l.append(z[:T_CTX, Z_ATT + A_CV:Z_ATT + A_CV + 512].astype(F32))

    y_prompt = x[:T_CTX].reshape(N_CTX_SEQ, CTX_LEN, d)
    y_sample = x[T_CTX:].reshape(N_LAT_SEQ, LAT_LEN, d)
    new_state = jnp.stack(st_l, axis=0).transpose(2, 0, 1, 3, 4, 5)

    def stack_kv(parts, heads):
        a = jnp.stack(parts, axis=0).reshape(DEPTH, N_CTX_SEQ, CTX_LEN, heads, HEAD_DIM)
        return a.transpose(1, 0, 2, 3, 4)

    return (y_prompt, y_sample, new_state,
            stack_kv(bk_l, SWA_KV_HEADS), stack_kv(bv_l, SWA_KV_HEADS),
            stack_kv(nk_l, NA_HEADS), stack_kv(nv_l, NA_HEADS))
```

```python
import functools

import numpy as np
import jax
import jax.numpy as jnp
from jax import lax
from jax.experimental import pallas as pl
from jax.experimental.pallas import tpu as pltpu

F32 = jnp.float32
BF16 = jnp.bfloat16

D_MODEL = 1024
DEPTH = 4
N_CTX_SEQ = 16
CTX_LEN = 256
N_LAT_SEQ = 2
LAT_LEN = 1024
PAST_LEN = 512
T_CTX = N_CTX_SEQ * CTX_LEN
T_LAT = N_LAT_SEQ * LAT_LEN
T_ALL = T_CTX + T_LAT
GRID_W = 64
HEAD_DIM = 64
GLA_HEADS = 4
GLA_DK = 64
GLA_DV = 128
GLA_LOWRANK = 16
GLA_TAU = 16.0
SWA_Q_HEADS = 8
SWA_KV_HEADS = 2
SWA_GROUP = 4
SWA_WINDOW = 128
NA_HEADS = 8
NA_ROWS = 8
NA_COLS = 16
MLP_HIDDEN = 4 * D_MODEL
ROPE_BASE = 10000.0
EPS = 1e-6

Z_GATES = 0
Z_GLA = 3072
Z_AQ, Z_AK, Z_AV, Z_AR = 3072, 3328, 3584, 4096
Z_ATT = 4608
ATT_W = 2304
A_BQ, A_BK, A_BV, A_CQ, A_CK, A_CV = 0, 512, 640, 768, 1280, 1792
Z_LR = 6912
Z_W = 7168

GLA_C = 128
GLA_LEVELS = 7
GLA_GROWS = (GLA_LEVELS + 2) * GLA_C + 16

VMEM_LIMIT = 56 * 1024 * 1024


def _cparams(sem):
    return pltpu.CompilerParams(dimension_semantics=sem, vmem_limit_bytes=VMEM_LIMIT)


def _sigmoid(x):
    return 1.0 / (1.0 + jnp.exp(-x))


def _silu(x):
    return x * _sigmoid(x)


def _nt_dot(a, b):
    return lax.dot_general(a, b, (((1,), (1,)), ((), ())), preferred_element_type=F32)


def _tn_dot(a, b):
    return lax.dot_general(a, b, (((0,), (0,)), ((), ())), preferred_element_type=F32)


def _dot(a, b):
    return jnp.dot(a, b, preferred_element_type=F32)


def _mod_kernel(cond_ref, w_ref, b_ref, o_ref):
    s = _silu(cond_ref[...]).astype(BF16)
    o_ref[...] = _dot(s, w_ref[...].astype(BF16)) + b_ref[...]


def _modulation(cond8, w_mod, b_mod):
    tn = 1024
    return pl.pallas_call(
        _mod_kernel,
        grid=(DEPTH, 6 * D_MODEL // tn),
        in_specs=[
            pl.BlockSpec((8, D_MODEL), lambda l, j: (0, 0)),
            pl.BlockSpec((None, D_MODEL, tn), lambda l, j: (l, 0, j)),
            pl.BlockSpec((None, 1, tn), lambda l, j: (l, 0, j)),
        ],
        out_specs=pl.BlockSpec((None, 8, tn), lambda l, j: (l, 0, j)),
        out_shape=jax.ShapeDtypeStruct((DEPTH, 8, 6 * D_MODEL), F32),
        compiler_params=_cparams(("arbitrary", "arbitrary")),
        name="modulation",
    )(cond8, w_mod, b_mod.reshape(DEPTH, 1, 6 * D_MODEL))


def _group_of_rows(row_block, rows_per_block):
    first = row_block * rows_per_block
    return jnp.maximum(first - T_CTX + LAT_LEN, 0) // LAT_LEN


def _norm_mod(x, g, shift, scale):
    ms = jnp.mean(x * x, axis=-1, keepdims=True)
    y = x * lax.rsqrt(ms + EPS) * g
    return y * (1.0 + scale) + shift


W_IN_SEGMENTS = ((3872, 6944), (0, 1536), (1568, 3872), (1536, 1568))
W_IN_COLS = 6944


def _w_in_layout_kernel(w_ref, o_ref):
    dst = 0
    for lo, hi in W_IN_SEGMENTS:
        o_ref[:, dst:dst + hi - lo] = w_ref[:, lo:hi].astype(BF16)
        dst += hi - lo
    o_ref[:, dst:] = jnp.zeros((o_ref.shape[0], Z_W - dst), BF16)


def _w_in_layout(w_in):
    tr = 256
    return pl.pallas_call(
        _w_in_layout_kernel,
        grid=(DEPTH, D_MODEL // tr),
        in_specs=[pl.BlockSpec((None, tr, W_IN_COLS), lambda l, r: (l, r, 0))],
        out_specs=pl.BlockSpec((None, tr, Z_W), lambda l, r: (l, r, 0)),
        out_shape=jax.ShapeDtypeStruct((DEPTH, D_MODEL, Z_W), BF16),
        compiler_params=_cparams(("arbitrary", "arbitrary")),
        name="w_in_layout",
    )(w_in)


def _inproj_kernel(x_ref, g_ref, mod_ref, w_ref, z_ref, h_scr):
    @pl.when(pl.program_id(1) == 0)
    def _():
        h = _norm_mod(x_ref[...], g_ref[...],
                      mod_ref[:, 0:D_MODEL], mod_ref[:, D_MODEL:2 * D_MODEL])
        h_scr[...] = h.astype(BF16)

    tn = z_ref.shape[1]
    col = pl.multiple_of(pl.program_id(1) * tn, tn)
    z_ref[...] = _dot(h_scr[...], w_ref[:, pl.ds(col, tn)]).astype(z_ref.dtype)


def _inproj(x, norm_w, mod4, w_in_p, l):
    tm, tn = 1024, 1024
    return pl.pallas_call(
        _inproj_kernel,
        grid=(T_ALL // tm, Z_W // tn),
        in_specs=[
            pl.BlockSpec((tm, D_MODEL), lambda i, j: (i, 0)),
            pl.BlockSpec((None, 1, D_MODEL), lambda i, j: (l, 0, 0)),
            pl.BlockSpec((None, None, 1, 6 * D_MODEL),
                         lambda i, j: (l, _group_of_rows(i, tm), 0, 0)),
            pl.BlockSpec((None, D_MODEL, Z_W), lambda i, j: (l, 0, 0)),
        ],
        out_specs=pl.BlockSpec((tm, tn), lambda i, j: (i, j)),
        out_shape=jax.ShapeDtypeStruct((T_ALL, Z_W), BF16),
        scratch_shapes=[pltpu.VMEM((tm, D_MODEL), BF16)],
        compiler_params=_cparams(("arbitrary", "arbitrary")),
        name="inproj",
    )(x, norm_w, mod4, w_in_p)


def _pair_norm(x, w):
    sq = x * x
    low = lax.broadcasted_iota(jnp.int32, (1, 128), 1) < HEAD_DIM
    s_low = jnp.sum(jnp.where(low, sq, 0.0), axis=-1, keepdims=True)
    s_all = jnp.sum(sq, axis=-1, keepdims=True)
    ms = jnp.where(low, s_low, s_all - s_low) * (1.0 / HEAD_DIM)
    return x * lax.rsqrt(ms + EPS) * w


def _rope(x, cos, sin_signed):
    n = x.shape[-1]
    lane = lax.broadcasted_iota(jnp.int32, (1, n), 1)
    first = (lane % 32) < 16
    partner = jnp.where(first, pltpu.roll(x, n - 16, 1), pltpu.roll(x, 16, 1))
    return x * cos + partner * sin_signed


def _prep_kernel(z_ref, wqb_ref, wkb_ref, wqc_ref, wkc_ref, cos_ref, sin_ref,
                 qb_ref, kb_ref, vb_ref, qc_ref, kc_ref, vc_ref):
    qscale = HEAD_DIM ** -0.5

    def normed(col, width, w_ref):
        tiles = []
        for p in range(width // 128):
            x = z_ref[:, col + 128 * p: col + 128 * (p + 1)].astype(F32)
            tiles.append(_pair_norm(x, w_ref[:, 128 * p:128 * (p + 1)]))
        return tiles

    qb = normed(A_BQ, 512, wqb_ref)
    kb = normed(A_BK, 128, wkb_ref)
    qc = normed(A_CQ, 512, wqc_ref)
    kc = normed(A_CK, 512, wkc_ref)

    vb_ref[...] = z_ref[:, A_BV:A_BV + 128].astype(F32).T.astype(BF16)
    vc_ref[...] = z_ref[:, A_CV:A_CV + 512]
    for p in range(4):
        qc_ref[:, 128 * p:128 * (p + 1)] = (qc[p] * qscale).astype(BF16)
        kc_ref[:, 128 * p:128 * (p + 1)] = kc[p].astype(BF16)

    cos = cos_ref[...]
    sin = sin_ref[...]
    for p in range(4):
        qb_ref[128 * p:128 * (p + 1), :] = (_rope(qb[p], cos, sin) * qscale).T.astype(BF16)
    kb_ref[...] = _rope(kb[0], cos, sin).astype(BF16)


def _attn_prep(z, wqb, wkb, wqc, wkc, cos_t, sin_t):
    tm = 512
    lat_blocks = LAT_LEN // tm
    rope_idx = lambda i: (i % lat_blocks, 0)
    row = lambda w: pl.BlockSpec((tm, w), lambda i: (i, 0))
    col = lambda w: pl.BlockSpec((w, tm), lambda i: (0, i))
    cst = lambda w: pl.BlockSpec((1, w), lambda i: (0, 0))
    sds = lambda w: jax.ShapeDtypeStruct((T_LAT, w), BF16)
    sds_t = lambda w: jax.ShapeDtypeStruct((w, T_LAT), BF16)
    return pl.pallas_call(
        _prep_kernel,
        grid=(T_LAT // tm,),
        in_specs=[
            pl.BlockSpec((tm, ATT_W), lambda i: (T_CTX // tm + i, Z_ATT // ATT_W)),
            cst(512), cst(128), cst(512), cst(512),
            pl.BlockSpec((tm, 128), rope_idx),
            pl.BlockSpec((tm, 128), rope_idx),
        ],
        out_specs=[col(512), row(128), col(128), row(512), row(512), row(512)],
        out_shape=[sds_t(512), sds(128), sds_t(128), sds(512), sds(512), sds(512)],
        compiler_params=_cparams(("arbitrary",)),
        name="attn_prep",
    )(z, wqb, wkb, wqc, wkc, cos_t, sin_t)


def _half_rows(t, half):
    z = jnp.zeros((HEAD_DIM, t.shape[1]), t.dtype)
    return jnp.concatenate([t[:HEAD_DIM], z] if half == 0 else [z, t[HEAD_DIM:]], axis=0)


def _ctx_attn_kernel(sink_ref, z_ref, wqb_ref, wkb_ref, wqc_ref, wkc_ref, *refs, layer):
    ob_ref, oc_ref, kbf_ref, vbf_ref, kcf_ref, vcf_ref = refs[-6:]
    vbf_ref[...] = z_ref[:, A_BV:A_BV + 128].astype(F32)
    vcf_ref[...] = z_ref[:, A_CV:A_CV + 512].astype(F32)
    n = z_ref.shape[0]
    qscale = HEAD_DIM ** -0.5

    def pair(col, p):
        return z_ref[:, col + 128 * p: col + 128 * (p + 1)]

    def q_t(col, w_ref, p):
        y = _pair_norm(pair(col, p).astype(F32), w_ref[:, 128 * p:128 * (p + 1)]) * qscale
        return y.T.astype(BF16)

    def v_t(col, p):
        return pair(col, p).astype(F32).T.astype(BF16)

    kb = _pair_norm(pair(A_BK, 0).astype(F32), wkb_ref[...])
    kbf_ref[...] = kb
    kb = kb.astype(BF16)
    qb_t = [q_t(A_BQ, wqb_ref, p) for p in range(4)]
    scores = []
    for kvh in range(SWA_KV_HEADS):
        placed = []
        for h in range(SWA_GROUP * kvh, SWA_GROUP * (kvh + 1)):
            rows = qb_t[h // 2][HEAD_DIM * (h % 2):HEAD_DIM * (h % 2 + 1)]
            z = jnp.zeros_like(rows)
            placed.append(jnp.concatenate([rows, z] if kvh == 0 else [z, rows], axis=0))
        scores.append(_dot(kb, jnp.concatenate(placed, axis=1)))
    for p in range(NA_HEADS // 2):
        kc = _pair_norm(pair(A_CK, p).astype(F32), wkc_ref[:, 128 * p:128 * (p + 1)])
        kcf_ref[:, 128 * p:128 * (p + 1)] = kc
        qc_t = q_t(A_CQ, wqc_ref, p)
        q2 = jnp.concatenate([_half_rows(qc_t, 0), _half_rows(qc_t, 1)], axis=1)
        scores.append(_dot(kc.astype(BF16), q2))
    s_t = jnp.concatenate(scores, axis=1)

    sink = jnp.concatenate(
        [jnp.full((1, n), sink_ref[layer, h], F32) for h in range(SWA_Q_HEADS)]
        + [jnp.full((1, n * NA_HEADS), -jnp.inf, F32)], axis=1)
    mx = jnp.maximum(s_t.max(axis=0, keepdims=True), sink)
    p_t = jnp.exp(s_t - mx)
    inv = 1.0 / (p_t.sum(axis=0, keepdims=True) + jnp.exp(sink - mx))
    p_t = p_t.astype(BF16)

    nb = SWA_Q_HEADS * n
    o = _dot(v_t(A_BV, 0), p_t[:, :nb]) * inv[:, :nb]
    outs = [o[HEAD_DIM * (h // SWA_GROUP):HEAD_DIM * (h // SWA_GROUP + 1), n * h:n * (h + 1)]
            for h in range(SWA_Q_HEADS)]
    ob_ref[...] = jnp.concatenate(outs, axis=0).T.astype(BF16)
    outs = []
    for p in range(NA_HEADS // 2):
        cols = slice(nb + 2 * n * p, nb + 2 * n * (p + 1))
        o = _dot(v_t(A_CV, p), p_t[:, cols]) * inv[:, cols]
        outs += [o[:HEAD_DIM, :n], o[HEAD_DIM:, n:]]
    oc_ref[...] = jnp.concatenate(outs, axis=0).T.astype(BF16)


def _ctx_attention(sink, z, wqb, wkb, wqc, wkc, caches, l):
    blk = lambda w, j=0: pl.BlockSpec((CTX_LEN, w), lambda b: (b, j))
    cst = lambda w: pl.BlockSpec((1, w), lambda b: (0, 0))
    sds = lambda w: jax.ShapeDtypeStruct((T_CTX, w), BF16)
    cache_w = (128, 128, 512, 512)
    cache_blk = [pl.BlockSpec((None, None, CTX_LEN, w), lambda b: (b, l, 0, 0)) for w in cache_w]
    cache_sds = [jax.ShapeDtypeStruct((N_CTX_SEQ, DEPTH, CTX_LEN, w), F32) for w in cache_w]
    carried = [] if caches is None else list(caches)
    n_in = 6
    return pl.pallas_call(
        functools.partial(_ctx_attn_kernel, layer=l),
        grid=(N_CTX_SEQ,),
        in_specs=[pl.BlockSpec(memory_space=pltpu.SMEM), blk(ATT_W, Z_ATT // ATT_W),
                  cst(512), cst(128), cst(512), cst(512)]
                 + [pl.BlockSpec(memory_space=pl.ANY)] * len(carried),
        out_specs=[blk(512), blk(512)] + cache_blk,
        out_shape=[sds(512), sds(512)] + cache_sds,
        input_output_aliases={n_in + j: 2 + j for j in range(len(carried))},
        compiler_params=_cparams(("arbitrary",)),
        name="ctx_attention",
    )(sink, z, wqb, wkb, wqc, wkc, *carried)


SWA_QBLK = 128
SWA_SPAN = SWA_QBLK + 2 * SWA_WINDOW


def _swa_kernel(sink_ref, qt_ref, k_ref, vt_ref, kc_ref, vc_ref, o_ref, kctx_scr, vctx_scr, *, layer):
    i = pl.program_id(1)
    nq = SWA_QBLK

    @pl.when(i == 0)
    def _():
        kctx_scr[...] = kc_ref[...].astype(BF16)
        vctx_scr[...] = vc_ref[...].T.astype(BF16)

    start = pl.multiple_of(jnp.clip(nq * (i - 1), 0, LAT_LEN - SWA_SPAN), nq)
    kwin = k_ref[pl.ds(start, SWA_SPAN), :]
    vwin_t = vt_ref[:, pl.ds(start, SWA_SPAN)]
    kctx = kctx_scr[...]
    s_lat, s_ctx = [], []
    for kvh in range(SWA_KV_HEADS):
        placed = []
        for h in range(SWA_GROUP * kvh, SWA_GROUP * (kvh + 1)):
            rows = qt_ref[HEAD_DIM * h:HEAD_DIM * (h + 1), :]
            z = jnp.zeros_like(rows)
            placed.append(jnp.concatenate([rows, z] if kvh == 0 else [z, rows], axis=0))
        q4 = jnp.concatenate(placed, axis=1)
        s_lat.append(_dot(kwin, q4))
        s_ctx.append(_dot(kctx, q4))
    s_lat = jnp.concatenate(s_lat, axis=1)
    s_ctx = jnp.concatenate(s_ctx, axis=1)

    kpos = start + lax.broadcasted_iota(jnp.int32, (SWA_SPAN, nq), 0)
    qpos = nq * i + lax.broadcasted_iota(jnp.int32, (SWA_SPAN, nq), 1)
    band = jnp.where(jnp.abs(kpos - qpos) <= SWA_WINDOW, 0.0, -jnp.inf)
    s_lat = s_lat + jnp.concatenate([band] * SWA_Q_HEADS, axis=1)
    sink = jnp.concatenate(
        [jnp.full((1, nq), sink_ref[layer, h], F32) for h in range(SWA_Q_HEADS)], axis=1)
    mx = jnp.maximum(jnp.maximum(s_lat.max(axis=0, keepdims=True),
                                 s_ctx.max(axis=0, keepdims=True)), sink)
    p_lat = jnp.exp(s_lat - mx)
    p_ctx = jnp.exp(s_ctx - mx)
    inv = 1.0 / (p_lat.sum(axis=0, keepdims=True) + p_ctx.sum(axis=0, keepdims=True)
                 + jnp.exp(sink - mx))
    o = (_dot(vwin_t, p_lat.astype(BF16)) + _dot(vctx_scr[...], p_ctx.astype(BF16))) * inv
    outs = [o[HEAD_DIM * (h // SWA_GROUP):HEAD_DIM * (h // SWA_GROUP + 1), nq * h:nq * (h + 1)]
            for h in range(SWA_Q_HEADS)]
    o_ref[...] = jnp.concatenate(outs, axis=0).T.astype(BF16)


def _swa_attention(sink, qb_t, kb, vb_t, cache_k, cache_v, l):
    nq = LAT_LEN // SWA_QBLK
    cache = pl.BlockSpec((None, None, PAST_LEN, 128), lambda b, i: (b, l, 0, 0))
    return pl.pallas_call(
        functools.partial(_swa_kernel, layer=l),
        grid=(N_LAT_SEQ, nq),
        in_specs=[pl.BlockSpec(memory_space=pltpu.SMEM),
                  pl.BlockSpec((512, SWA_QBLK), lambda b, i: (0, b * nq + i)),
                  pl.BlockSpec((LAT_LEN, 128), lambda b, i: (b, 0)),
                  pl.BlockSpec((128, LAT_LEN), lambda b, i: (0, b)),
                  cache, cache],
        out_specs=pl.BlockSpec((SWA_QBLK, 512), lambda b, i: (b * nq + i, 0)),
        out_shape=jax.ShapeDtypeStruct((T_LAT, 512), BF16),
        scratch_shapes=[pltpu.VMEM((PAST_LEN, 128), BF16), pltpu.VMEM((128, PAST_LEN), BF16)],
        compiler_params=_cparams(("arbitrary", "arbitrary")),
        name="swa_attention",
    )(sink, qb_t, kb, vb_t, cache_k, cache_v)


LAT_ROWS = LAT_LEN // GRID_W
NA_WIN_ROWS = min(NA_ROWS, LAT_ROWS)
NA_KEYS = NA_WIN_ROWS * GRID_W


def _na_kernel(q_ref, k_ref, v_ref, kc_ref, vc_ref, bias_ref, o_ref, kctx_scr, vctx_scr):
    r = pl.program_id(1)

    @pl.when(r == 0)
    def _():
        kctx_scr[...] = kc_ref[...].astype(BF16)
        vctx_scr[...] = vc_ref[...].astype(BF16)

    rs = jnp.clip(r - NA_ROWS // 2, 0, LAT_ROWS - NA_WIN_ROWS)
    start = pl.multiple_of(rs * GRID_W, GRID_W)
    base = rs - r + NA_ROWS - 1
    low = lax.broadcasted_iota(jnp.int32, (1, 128), 1) < HEAD_DIM
    npair = NA_HEADS // 2
    s_lat, s_ctx = [], []
    for p in range(npair):
        sl = slice(128 * p, 128 * (p + 1))
        q = q_ref[:, sl]
        zero = jnp.zeros_like(q)
        q2 = jnp.concatenate([jnp.where(low, q, zero), jnp.where(low, zero, q)], axis=0)
        bias = jnp.concatenate(
            [jnp.concatenate([bias_ref[2 * p + hh, base + 2 * w] for w in range(NA_WIN_ROWS // 2)],
                             axis=1) for hh in range(2)], axis=0)
        s_lat.append(_nt_dot(q2, k_ref[pl.ds(start, NA_KEYS), sl]) + bias)
        s_ctx.append(_nt_dot(q2, kctx_scr[:, sl]))
    s_lat = jnp.concatenate(s_lat, axis=0)
    s_ctx = jnp.concatenate(s_ctx, axis=0)
    mx = jnp.maximum(s_lat.max(axis=-1, keepdims=True), s_ctx.max(axis=-1, keepdims=True))
    p_lat = jnp.exp(s_lat - mx)
    p_ctx = jnp.exp(s_ctx - mx)
    inv = 1.0 / (p_lat.sum(axis=-1, keepdims=True) + p_ctx.sum(axis=-1, keepdims=True))
    p_lat = p_lat.astype(BF16)
    p_ctx = p_ctx.astype(BF16)
    tiles = []
    for p in range(npair):
        sl = slice(128 * p, 128 * (p + 1))
        rows = slice(2 * GRID_W * p, 2 * GRID_W * (p + 1))
        o2 = (_dot(p_lat[rows], v_ref[pl.ds(start, NA_KEYS), sl])
              + _dot(p_ctx[rows], vctx_scr[:, sl])) * inv[rows]
        tiles.append(jnp.where(low, o2[:GRID_W], o2[GRID_W:]))
    o_ref[...] = jnp.concatenate(tiles, axis=1).astype(BF16)


def _na_attention(qc, kc, vc, cache_k, cache_v, bias_tab, l):
    kv = pl.BlockSpec((LAT_LEN, 512), lambda b, r: (b, 0))
    cache = pl.BlockSpec((None, None, PAST_LEN, 512), lambda b, r: (b, l, 0, 0))
    return pl.pallas_call(
        _na_kernel,
        grid=(N_LAT_SEQ, LAT_ROWS),
        in_specs=[pl.BlockSpec((GRID_W, 512), lambda b, r: (b * LAT_ROWS + r, 0)),
                  kv, kv, cache, cache,
                  pl.BlockSpec((None, NA_HEADS, 2 * NA_ROWS - 2, GRID_W, 128),
                               lambda b, r: (l, 0, 0, 0, 0))],
        out_specs=pl.BlockSpec((GRID_W, 512), lambda b, r: (b * LAT_ROWS + r, 0)),
        out_shape=jax.ShapeDtypeStruct((T_LAT, 512), BF16),
        scratch_shapes=[pltpu.VMEM((PAST_LEN, 512), BF16), pltpu.VMEM((PAST_LEN, 512), BF16)],
        compiler_params=_cparams(("arbitrary", "arbitrary")),
        name="na_attention",
    )(qc, kc, vc, cache_k, cache_v, bias_tab)


def _na_bias_tables(rpb):
    cq = np.arange(GRID_W)[:, None]
    ck = np.arange(GRID_W)[None, :]
    cs = np.clip(cq - NA_COLS // 2, 0, GRID_W - NA_COLS)
    col_ok = (ck >= cs) & (ck < cs + NA_COLS)
    period = 2 * GRID_W + 1
    v = jnp.zeros(rpb.shape[:-1] + (period,), F32)
    v = v.at[..., 0:NA_COLS].set(rpb[..., NA_COLS - 1:])
    v = v.at[..., period - (NA_COLS - 1):].set(rpb[..., :NA_COLS - 1])
    flat = jnp.tile(v, (1, 1, 1, GRID_W))[..., :GRID_W * (period - 1)]
    t = flat.reshape(rpb.shape[:-1] + (GRID_W, period - 1))[..., :GRID_W]
    t = jnp.where(col_ok, t, -jnp.inf)
    return jnp.concatenate([t[:, :, :-1], t[:, :, 1:]], axis=-1)


def _gla_constants():
    c = GLA_C
    t = np.arange(c)[:, None]
    u = np.arange(c)[None, :]
    blocks = []
    for k in range(GLA_LEVELS):
        b = 1 << k
        m = ((t >> k) | 1) * b - 1
        query = ((t >> k) & 1) == 1
        blocks.append(np.where(query, (u > m) & (u <= t), (u > t) & (u <= m)))
    blocks.append(u <= t)
    blocks.append(u > t)
    blocks.append(np.ones((16, c), bool))
    fwd = np.concatenate(blocks, axis=0).astype(np.float32)
    bwd_blocks = [blk[::-1, ::-1] for blk in blocks]
    bwd = np.concatenate(bwd_blocks, axis=0).astype(np.float32)
    seg = np.stack([fwd, bwd])
    seg = np.concatenate([seg, seg], axis=-1)

    s = np.arange(c)[None, :]
    x = t ^ s
    lev = np.where(x == 0, GLA_LEVELS, np.floor(np.log2(np.maximum(x, 1))).astype(np.int64))
    lev_f = np.where(s <= t, lev, -1)
    lev_b = np.where(s >= t, lev, -1)
    levmap = np.stack([np.tile(lev_f, (1, GLA_HEADS)), np.tile(lev_b, (1, GLA_HEADS))])
    return seg, levmap.astype(np.int32)


def _gla_state_init(d, cc, s0_ref, st_ref):
    c = GLA_C
    ctx_chunks = T_CTX // c
    is_lat = cc >= ctx_chunks
    per_seq = jnp.where(is_lat, LAT_LEN // c, CTX_LEN // c)
    pos = jnp.where(is_lat, cc - ctx_chunks, cc) % per_seq
    first = (pos == 0) if d == 0 else (pos == per_seq - 1)

    @pl.when(first)
    def _():
        for h in range(GLA_HEADS):
            blk = jnp.where(is_lat, s0_ref[h], 0.0)
            row = [jnp.zeros((GLA_DV, GLA_DK), F32)] * GLA_HEADS
            row[h] = blk
            st_ref[GLA_DV * h:GLA_DV * (h + 1), :] = jnp.concatenate(row, axis=1)


def _gla_chunk(q_ref, k_ref, v_ref, lr_ref, w2_ref, ba_ref, seg_ref, lev_ref, o_ref, sfin_ref, st_ref):
    c = GLA_C
    x = _dot(lr_ref[...], w2_ref[...]) + ba_ref[...]
    la = (jnp.minimum(x, 0.0) - jnp.log1p(jnp.exp(-jnp.abs(x)))) * (1.0 / GLA_TAU)
    la_hi = la.astype(BF16)
    la_lo = (la - la_hi.astype(F32)).astype(BF16)
    seg = _dot(seg_ref[...], jnp.concatenate([la_hi, la_lo], axis=0))

    q = q_ref[...].astype(F32) * (GLA_DK ** -0.5)
    k = k_ref[...].astype(F32)
    v = v_ref[...]
    lev = lev_ref[...]
    lane_head = lax.broadcasted_iota(jnp.int32, (1, GLA_HEADS * GLA_DK), 1) // GLA_DK
    head_sel = [jnp.where(lane_head == h, 1.0, 0.0).astype(BF16) for h in range(GLA_HEADS)]

    def pair_scores(qs, ks):
        ksb = ks.astype(BF16)
        kbd = jnp.concatenate([ksb * head_sel[h] for h in range(GLA_HEADS)], axis=0)
        return _nt_dot(qs.astype(BF16), kbd)

    attn = jnp.where(lev == GLA_LEVELS, pair_scores(q, k), 0.0)
    for lvl in range(GLA_LEVELS):
        e = jnp.exp(seg[c * lvl:c * (lvl + 1)])
        attn = jnp.where(lev == lvl, pair_scores(q * e, k * e), attn)
    attn = attn.astype(BF16)

    cum = seg[c * GLA_LEVELS:c * (GLA_LEVELS + 1)]
    rem = seg[c * (GLA_LEVELS + 1):c * (GLA_LEVELS + 2)]
    tot = seg[c * (GLA_LEVELS + 2):c * (GLA_LEVELS + 2) + 1]
    q_in = (q * jnp.exp(cum)).astype(BF16)
    k_in = (k * jnp.exp(rem)).astype(BF16)
    state = st_ref[...]
    o_inter = _nt_dot(q_in, state.astype(BF16))
    for h in range(GLA_HEADS):
        sl = slice(GLA_DV * h, GLA_DV * (h + 1))
        o_ref[:, sl] = o_inter[:, sl] + _dot(attn[:, c * h:c * (h + 1)], v[:, sl])

    upd = _tn_dot(v, k_in)
    row_head = lax.broadcasted_iota(jnp.int32, (GLA_HEADS * GLA_DV, 1), 0) // GLA_DV
    new_state = state * jnp.exp(tot) + jnp.where(row_head == lane_head, upd, 0.0)
    st_ref[...] = new_state
    for h in range(GLA_HEADS):
        sfin_ref[h] = new_state[GLA_DV * h:GLA_DV * (h + 1), GLA_DK * h:GLA_DK * (h + 1)]


def _gla_kernel(qf, kf, vf, lrf, qb, kb, vb, lrb, w2_ref, ba_ref, seg_ref, lev_ref, s0f, s0b,
                of_ref, ob_ref, sff_ref, sfb_ref, stf, stb):
    step = pl.program_id(0)
    last = pl.num_programs(0) - 1
    _gla_state_init(0, step, s0f, stf)
    _gla_state_init(1, last - step, s0b, stb)
    _gla_chunk(qf, kf, vf, lrf, w2_ref.at[0], ba_ref.at[0], seg_ref.at[0], lev_ref.at[0],
               of_ref, sff_ref, stf)
    _gla_chunk(qb, kb, vb, lrb, w2_ref.at[1], ba_ref.at[1], seg_ref.at[1], lev_ref.at[1],
               ob_ref, sfb_ref, stb)


def _gla(z, w2p, ba, seg, levmap, s0t, l):
    c = GLA_C
    nchunk = T_ALL // c
    ctx_chunks = T_CTX // c
    nseq = N_CTX_SEQ + N_LAT_SEQ
    fwd = lambda s: s
    bwd = lambda s: nchunk - 1 - s

    def seq_of(cc):
        return jnp.where(cc < ctx_chunks, cc // (CTX_LEN // c),
                         N_CTX_SEQ + (cc - ctx_chunks) // (LAT_LEN // c))

    def lat_of(cc):
        return jnp.clip((cc - ctx_chunks) // (LAT_LEN // c), 0, N_LAT_SEQ - 1)

    def chunk_specs(pos):
        return [pl.BlockSpec((c, 256), lambda s: (pos(s), Z_AQ // 256)),
                pl.BlockSpec((c, 256), lambda s: (pos(s), Z_AK // 256)),
                pl.BlockSpec((c, 512), lambda s: (pos(s), Z_AV // 512)),
                pl.BlockSpec((c, 128), lambda s: (pos(s), Z_LR // 128))]

    def s0_spec(d, pos):
        return pl.BlockSpec((None, None, None, GLA_HEADS, GLA_DV, GLA_DK),
                            lambda s: (lat_of(pos(s)), l, d, 0, 0, 0))

    def sfin_spec(pos):
        return pl.BlockSpec((None, GLA_HEADS, GLA_DV, GLA_DK), lambda s: (seq_of(pos(s)), 0, 0, 0))

    state_sds = jax.ShapeDtypeStruct((nseq, GLA_HEADS, GLA_DV, GLA_DK), F32)
    out_sds = jax.ShapeDtypeStruct((T_ALL, 512), F32)
    state_scr = pltpu.VMEM((GLA_HEADS * GLA_DV, GLA_HEADS * GLA_DK), F32)
    return pl.pallas_call(
        _gla_kernel,
        grid=(nchunk,),
        in_specs=chunk_specs(fwd) + chunk_specs(bwd) + [
            pl.BlockSpec((None, 2, 128, 256), lambda s: (l, 0, 0, 0)),
            pl.BlockSpec((None, 2, 1, 256), lambda s: (l, 0, 0, 0)),
            pl.BlockSpec((2, GLA_GROWS, 2 * c), lambda s: (0, 0, 0)),
            pl.BlockSpec((2, c, GLA_HEADS * c), lambda s: (0, 0, 0)),
            s0_spec(0, fwd), s0_spec(1, bwd),
        ],
        out_specs=[pl.BlockSpec((c, 512), lambda s: (fwd(s), 0)),
                   pl.BlockSpec((c, 512), lambda s: (bwd(s), 0)),
                   sfin_spec(fwd), sfin_spec(bwd)],
        out_shape=[out_sds, out_sds, state_sds, state_sds],
        scratch_shapes=[state_scr, state_scr],
        compiler_params=_cparams(("arbitrary",)),
        name="gla",
    )(z, z, z, z, z, z, z, z, w2p, ba, seg, levmap, s0t, s0t)


def _merge_kernel(x_ref, of_ref, obk_ref, ar_ref, gates_ref, bc_ref, bl_ref, cc_ref, cl_ref,
                  wpa_f32, wpb_f32, wpc_f32, wo_f32, gn_ref, mod_ref, y_ref,
                  wpa_ref, wpb_ref, wpc_ref, wo_ref):
    @pl.when(pl.program_id(0) == 0)
    def _():
        for src, dst in ((wpa_f32, wpa_ref), (wpb_f32, wpb_ref), (wpc_f32, wpc_ref), (wo_f32, wo_ref)):
            dst[...] = src[...].astype(BF16)

    ctx = pl.program_id(0) < T_CTX // x_ref.shape[0]
    o = of_ref[...] + obk_ref[...]
    heads = []
    for h in range(GLA_HEADS):
        oh = o[:, GLA_DV * h:GLA_DV * (h + 1)]
        ms = jnp.mean(oh * oh, axis=-1, keepdims=True)
        heads.append(oh * lax.rsqrt(ms + EPS) * gn_ref[...])
    oa = (jnp.concatenate(heads, axis=1) * _silu(ar_ref[...].astype(F32))).astype(BF16)
    ob = jnp.where(ctx, bc_ref[...], bl_ref[...])
    oc = jnp.where(ctx, cc_ref[...], cl_ref[...])
    d = D_MODEL
    gate = lambda j: _sigmoid(gates_ref[:, j * d:(j + 1) * d].astype(F32))
    merged = (gate(0) * _dot(oa, wpa_ref[...]) + gate(1) * _dot(ob, wpb_ref[...])
              + gate(2) * _dot(oc, wpc_ref[...]))
    out = _dot(merged.astype(BF16), wo_ref[...])
    y_ref[...] = x_ref[...] + mod_ref[:, 2 * d:3 * d] * out


def _merge(x, o_fwd, o_bwd, z, ob_ctx, ob_lat, oc_ctx, oc_lat, wpa, wpb, wpc, wo, gn, mod4, l):
    tm = 512
    ctx_blocks = T_CTX // tm
    ctx_idx = lambda i: (jnp.minimum(i, ctx_blocks - 1), 0)
    lat_idx = lambda i: (jnp.maximum(i - ctx_blocks, 0), 0)
    wspec = lambda k: pl.BlockSpec((None, k, D_MODEL), lambda i: (l, 0, 0))
    return pl.pallas_call(
        _merge_kernel,
        grid=(T_ALL // tm,),
        in_specs=[
            pl.BlockSpec((tm, D_MODEL), lambda i: (i, 0)),
            pl.BlockSpec((tm, 512), lambda i: (i, 0)),
            pl.BlockSpec((tm, 512), lambda i: (i, 0)),
            pl.BlockSpec((tm, 512), lambda i: (i, Z_AR // 512)),
            pl.BlockSpec((tm, 3 * D_MODEL), lambda i: (i, 0)),
            pl.BlockSpec((tm, 512), ctx_idx), pl.BlockSpec((tm, 512), lat_idx),
            pl.BlockSpec((tm, 512), ctx_idx), pl.BlockSpec((tm, 512), lat_idx),
            wspec(512), wspec(512), wspec(512), wspec(D_MODEL),
            pl.BlockSpec((None, 1, GLA_DV), lambda i: (l, 0, 0)),
            pl.BlockSpec((None, None, 1, 6 * D_MODEL), lambda i: (l, _group_of_rows(i, tm), 0, 0)),
        ],
        out_specs=pl.BlockSpec((tm, D_MODEL), lambda i: (i, 0)),
        out_shape=jax.ShapeDtypeStruct((T_ALL, D_MODEL), F32),
        scratch_shapes=[pltpu.VMEM((512, D_MODEL), BF16)] * 3 + [pltpu.VMEM((D_MODEL, D_MODEL), BF16)],
        compiler_params=_cparams(("arbitrary",)),
        name="merge",
    )(x, o_fwd, o_bwd, z, z, ob_ctx, ob_lat, oc_ctx, oc_lat, wpa, wpb, wpc, wo, gn, mod4)


def _mlp_kernel(x_ref, g_ref, mod_ref, w1_ref, w2_ref, y_ref, w1_scr, w2_scr, h_scr, acc_scr):
    k = pl.program_id(1)
    d = D_MODEL

    @pl.when(pl.program_id(0) == 0)
    def _():
        w1_scr[k] = w1_ref[...].astype(BF16)
        w2_scr[k] = w2_ref[...].astype(BF16)

    @pl.when(k == 0)
    def _():
        h = _norm_mod(x_ref[...], g_ref[...], mod_ref[:, 3 * d:4 * d], mod_ref[:, 4 * d:5 * d])
        h_scr[...] = h.astype(BF16)

    hid = jnp.square(jnp.maximum(_dot(h_scr[...], w1_scr[k]), 0.0)).astype(BF16)
    part = _dot(hid, w2_scr[k])

    @pl.when(k == 0)
    def _():
        acc_scr[...] = part

    @pl.when(k > 0)
    def _():
        acc_scr[...] += part

    @pl.when(k == pl.num_programs(1) - 1)
    def _():
        y_ref[...] = x_ref[...] + mod_ref[:, 5 * d:6 * d] * acc_scr[...]


def _mlp(x, norm_w, mod4, w1, w2, l):
    tm, th = 1024, 512
    nk = MLP_HIDDEN // th
    wk = lambda i, k: jnp.where(i == 0, k, nk - 1)
    return pl.pallas_call(
        _mlp_kernel,
        grid=(T_ALL // tm, nk),
        in_specs=[
            pl.BlockSpec((tm, D_MODEL), lambda i, k: (i, 0)),
            pl.BlockSpec((None, 1, D_MODEL), lambda i, k: (l, 0, 0)),
            pl.BlockSpec((None, None, 1, 6 * D_MODEL),
                         lambda i, k: (l, _group_of_rows(i, tm), 0, 0)),
            pl.BlockSpec((None, D_MODEL, th), lambda i, k: (l, 0, wk(i, k))),
            pl.BlockSpec((None, th, D_MODEL), lambda i, k: (l, wk(i, k), 0)),
        ],
        out_specs=pl.BlockSpec((tm, D_MODEL), lambda i, k: (i, 0)),
        out_shape=jax.ShapeDtypeStruct((T_ALL, D_MODEL), F32),
        scratch_shapes=[pltpu.VMEM((nk, D_MODEL, th), BF16), pltpu.VMEM((nk, th, D_MODEL), BF16),
                        pltpu.VMEM((tm, D_MODEL), BF16), pltpu.VMEM((tm, D_MODEL), F32)],
        compiler_params=_cparams(("arbitrary", "arbitrary")),
        name="mlp",
    )(x, norm_w, mod4, w1, w2)


def _rope_tables():
    t = jnp.arange(LAT_LEN)
    row = (t // GRID_W).astype(F32)
    col = (t % GRID_W).astype(F32)
    nf = HEAD_DIM // 4
    inv_freq = ROPE_BASE ** (-jnp.arange(nf, dtype=F32) / nf)
    ang_r = row[:, None] * inv_freq[None, :]
    ang_c = col[:, None] * inv_freq[None, :]
    cos = jnp.concatenate([jnp.cos(ang_r)] * 2 + [jnp.cos(ang_c)] * 2, axis=1)
    sin = jnp.concatenate([-jnp.sin(ang_r), jnp.sin(ang_r), -jnp.sin(ang_c), jnp.sin(ang_c)], axis=1)
    return jnp.tile(cos, (1, 2)), jnp.tile(sin, (1, 2))


def kernel(x_prompt, x_sample, state_gla, cache_swa_k, cache_swa_v, cache_na_k, cache_na_v, c,
           c_ctx, w_mod, b_mod, norm1, norm2, w_in, w_a2_f, b_a_f, w_a2_b, b_a_b, gla_onorm,
           qn_swa, kn_swa, sink_swa, qn_na, kn_na, rpb_na, w_pa, w_pb, w_pc, w_o, w_fc1, w_fc2):
    d = D_MODEL
    x = jnp.concatenate([x_prompt.reshape(T_CTX, d), x_sample.reshape(T_LAT, d)], axis=0)

    cond8 = jnp.zeros((8, d), F32).at[0].set(c_ctx).at[1:1 + N_LAT_SEQ].set(c)
    mod4 = _modulation(cond8, w_mod, b_mod).reshape(DEPTH, 8, 1, 6 * d)

    w_in_p = _w_in_layout(w_in)
    norm1r = norm1.reshape(DEPTH, 1, d)
    norm2r = norm2.reshape(DEPTH, 1, d)
    gnr = gla_onorm.reshape(DEPTH, 1, GLA_DV)
    w2p = jnp.zeros((DEPTH, 2, 128, GLA_HEADS * GLA_DK), F32)
    w2p = w2p.at[:, 0, 0:GLA_LOWRANK].set(w_a2_f).at[:, 1, GLA_LOWRANK:2 * GLA_LOWRANK].set(w_a2_b)
    w2p = w2p.astype(BF16)
    ba = jnp.stack([b_a_f, b_a_b], axis=1).reshape(DEPTH, 2, 1, GLA_HEADS * GLA_DK)
    seg_np, lev_np = _gla_constants()
    seg = jnp.asarray(seg_np, BF16)
    levmap = jnp.asarray(lev_np)
    s0t = jnp.swapaxes(state_gla, -1, -2)
    cos_t, sin_t = _rope_tables()
    bias_tab = _na_bias_tables(rpb_na)
    csk = cache_swa_k.reshape(N_LAT_SEQ, DEPTH, PAST_LEN, 128)
    csv = cache_swa_v.reshape(N_LAT_SEQ, DEPTH, PAST_LEN, 128)
    cnk = cache_na_k.reshape(N_LAT_SEQ, DEPTH, PAST_LEN, 512)
    cnv = cache_na_v.reshape(N_LAT_SEQ, DEPTH, PAST_LEN, 512)

    st_l = []
    caches = None
    for l in range(DEPTH):
        z = _inproj(x, norm1r, mod4, w_in_p, l)
        wqb = jnp.tile(qn_swa[l], 8)[None, :]
        wkb = jnp.tile(kn_swa[l], 2)[None, :]
        wqc = jnp.tile(qn_na[l], 8)[None, :]
        wkc = jnp.tile(kn_na[l], 8)[None, :]
        qb, kb, vb, qc, kc, vc = _attn_prep(z, wqb, wkb, wqc, wkc, cos_t, sin_t)
        o_fwd, o_bwd, sfin_f, sfin_b = _gla(z, w2p, ba, seg, levmap, s0t, l)
        ob_ctx, oc_ctx, *caches = _ctx_attention(sink_swa, z, wqb, wkb, wqc, wkc, caches, l)
        ob_lat = _swa_attention(sink_swa, qb, kb, vb, csk, csv, l)
        oc_lat = _na_attention(qc, kc, vc, cnk, cnv, bias_tab, l)
        x = _merge(x, o_fwd, o_bwd, z, ob_ctx, ob_lat, oc_ctx, oc_lat, w_pa, w_pb, w_pc, w_o, gnr, mod4, l)
        x = _mlp(x, norm2r, mod4, w_fc1, w_fc2, l)

        sfin = jnp.stack([sfin_f[:N_CTX_SEQ], sfin_b[:N_CTX_SEQ]], axis=0)
        st_l.append(jnp.swapaxes(sfin, -1, -2))

    y_prompt = x[:T_CTX].reshape(N_CTX_SEQ, CTX_LEN, d)
    y_sample = x[T_CTX:].reshape(N_LAT_SEQ, LAT_LEN, d)
    new_state = jnp.stack(st_l, axis=0).transpose(2, 0, 1, 3, 4, 5)

    swa_k, swa_v, na_k, na_v = caches
    kv_shape = lambda heads: (N_CTX_SEQ, DEPTH, CTX_LEN, heads, HEAD_DIM)
    return (y_prompt, y_sample, new_state,
            swa_k.reshape(kv_shape(SWA_KV_HEADS)), swa_v.reshape(kv_shape(SWA_KV_HEADS)),
            na_k.reshape(kv_shape(NA_HEADS)), na_v.reshape(kv_shape(NA_HEADS)))
```

```python
import functools

import numpy as np
import jax
import jax.numpy as jnp
from jax import lax
from jax.experimental import pallas as pl
from jax.experimental.pallas import tpu as pltpu

F32 = jnp.float32
BF16 = jnp.bfloat16

D_MODEL = 1024
DEPTH = 4
N_CTX_SEQ = 16
CTX_LEN = 256
N_LAT_SEQ = 2
LAT_LEN = 1024
PAST_LEN = 512
T_CTX = N_CTX_SEQ * CTX_LEN
T_LAT = N_LAT_SEQ * LAT_LEN
T_ALL = T_CTX + T_LAT
GRID_W = 64
HEAD_DIM = 64
GLA_HEADS = 4
GLA_DK = 64
GLA_DV = 128
GLA_LOWRANK = 16
GLA_TAU = 16.0
SWA_Q_HEADS = 8
SWA_KV_HEADS = 2
SWA_GROUP = 4
SWA_WINDOW = 128
NA_HEADS = 8
NA_ROWS = 8
NA_COLS = 16
MLP_HIDDEN = 4 * D_MODEL
ROPE_BASE = 10000.0
EPS = 1e-6

Z_GATES = 0
Z_GLA = 3072
Z_AQ, Z_AK, Z_AV, Z_AR = 3072, 3328, 3584, 4096
Z_ATT = 4608
ATT_W = 2304
A_BQ, A_BK, A_BV, A_CQ, A_CK, A_CV = 0, 512, 640, 768, 1280, 1792
Z_LR = 6912
Z_W = 7168

GLA_C = 128
GLA_LEVELS = 7
GLA_GROWS = (GLA_LEVELS + 2) * GLA_C + 16

VMEM_LIMIT = 56 * 1024 * 1024


def _cparams(sem):
    return pltpu.CompilerParams(dimension_semantics=sem, vmem_limit_bytes=VMEM_LIMIT)


def _sigmoid(x):
    return 1.0 / (1.0 + jnp.exp(-x))


def _silu(x):
    return x * _sigmoid(x)


def _nt_dot(a, b):
    return lax.dot_general(a, b, (((1,), (1,)), ((), ())), preferred_element_type=F32)


def _tn_dot(a, b):
    return lax.dot_general(a, b, (((0,), (0,)), ((), ())), preferred_element_type=F32)


def _dot(a, b):
    return jnp.dot(a, b, preferred_element_type=F32)


def _mod_kernel(cond_ref, w_ref, b_ref, o_ref):
    s = _silu(cond_ref[...]).astype(BF16)
    o_ref[...] = _dot(s, w_ref[...].astype(BF16)) + b_ref[...]


def _modulation(cond8, w_mod, b_mod):
    tn = 1024
    return pl.pallas_call(
        _mod_kernel,
        grid=(DEPTH, 6 * D_MODEL // tn),
        in_specs=[
            pl.BlockSpec((8, D_MODEL), lambda l, j: (0, 0)),
            pl.BlockSpec((None, D_MODEL, tn), lambda l, j: (l, 0, j)),
            pl.BlockSpec((None, 1, tn), lambda l, j: (l, 0, j)),
        ],
        out_specs=pl.BlockSpec((None, 8, tn), lambda l, j: (l, 0, j)),
        out_shape=jax.ShapeDtypeStruct((DEPTH, 8, 6 * D_MODEL), F32),
        compiler_params=_cparams(("arbitrary", "arbitrary")),
        name="modulation",
    )(cond8, w_mod, b_mod.reshape(DEPTH, 1, 6 * D_MODEL))


def _group_of_rows(row_block, rows_per_block):
    first = row_block * rows_per_block
    return jnp.maximum(first - T_CTX + LAT_LEN, 0) // LAT_LEN


def _norm_mod(x, g, shift, scale):
    ms = jnp.mean(x * x, axis=-1, keepdims=True)
    y = x * lax.rsqrt(ms + EPS) * g
    return y * (1.0 + scale) + shift


W_IN_SEGMENTS = ((3872, 6944), (0, 1536), (1568, 3872), (1536, 1568))
W_IN_COLS = 6944


def _w_in_layout_kernel(w_ref, o_ref):
    dst = 0
    for lo, hi in W_IN_SEGMENTS:
        o_ref[dst:dst + hi - lo, :] = w_ref[lo:hi, :].astype(BF16)
        dst += hi - lo
    o_ref[dst:, :] = jnp.zeros((Z_W - dst, o_ref.shape[1]), BF16)


def _w_in_layout(w_in_t):
    tk = 256
    return pl.pallas_call(
        _w_in_layout_kernel,
        grid=(DEPTH, D_MODEL // tk),
        in_specs=[pl.BlockSpec((None, W_IN_COLS, tk), lambda l, r: (l, 0, r))],
        out_specs=pl.BlockSpec((None, Z_W, tk), lambda l, r: (l, 0, r)),
        out_shape=jax.ShapeDtypeStruct((DEPTH, Z_W, D_MODEL), BF16),
        compiler_params=_cparams(("arbitrary", "arbitrary")),
        name="w_in_layout",
    )(w_in_t)


def _inproj_kernel(x_ref, g_ref, mod_ref, w_ref, z_ref, h_scr):
    @pl.when(pl.program_id(1) == 0)
    def _():
        h = _norm_mod(x_ref[...], g_ref[...],
                      mod_ref[:, 0:D_MODEL], mod_ref[:, D_MODEL:2 * D_MODEL])
        h_scr[...] = h.astype(BF16)

    tn = z_ref.shape[1]
    row = pl.multiple_of(pl.program_id(1) * tn, tn)
    z_ref[...] = _nt_dot(h_scr[...], w_ref[pl.ds(row, tn), :]).astype(z_ref.dtype)


def _inproj(x, norm_w, mod4, w_in_p, l):
    tm, tn = 1024, 1024
    return pl.pallas_call(
        _inproj_kernel,
        grid=(T_ALL // tm, Z_W // tn),
        in_specs=[
            pl.BlockSpec((tm, D_MODEL), lambda i, j: (i, 0)),
            pl.BlockSpec((None, 1, D_MODEL), lambda i, j: (l, 0, 0)),
            pl.BlockSpec((None, None, 1, 6 * D_MODEL),
                         lambda i, j: (l, _group_of_rows(i, tm), 0, 0)),
            pl.BlockSpec((None, Z_W, D_MODEL), lambda i, j: (l, 0, 0)),
        ],
        out_specs=pl.BlockSpec((tm, tn), lambda i, j: (i, j)),
        out_shape=jax.ShapeDtypeStruct((T_ALL, Z_W), BF16),
        scratch_shapes=[pltpu.VMEM((tm, D_MODEL), BF16)],
        compiler_params=_cparams(("arbitrary", "arbitrary")),
        name="inproj",
    )(x, norm_w, mod4, w_in_p)


def _pair_norm(x, w):
    sq = x * x
    low = lax.broadcasted_iota(jnp.int32, (1, 128), 1) < HEAD_DIM
    s_low = jnp.sum(jnp.where(low, sq, 0.0), axis=-1, keepdims=True)
    s_all = jnp.sum(sq, axis=-1, keepdims=True)
    ms = jnp.where(low, s_low, s_all - s_low) * (1.0 / HEAD_DIM)
    return x * lax.rsqrt(ms + EPS) * w


def _rope(x, cos, sin_signed):
    n = x.shape[-1]
    lane = lax.broadcasted_iota(jnp.int32, (1, n), 1)
    first = (lane % 32) < 16
    partner = jnp.where(first, pltpu.roll(x, n - 16, 1), pltpu.roll(x, 16, 1))
    return x * cos + partner * sin_signed


def _prep_kernel(z_ref, wqb_ref, wkb_ref, wqc_ref, wkc_ref, cos_ref, sin_ref,
                 qb_ref, kb_ref, vb_ref, qc_ref, kc_ref, vc_ref):
    qscale = HEAD_DIM ** -0.5

    def normed(col, width, w_ref):
        tiles = []
        for p in range(width // 128):
            x = z_ref[:, col + 128 * p: col + 128 * (p + 1)].astype(F32)
            tiles.append(_pair_norm(x, w_ref[:, 128 * p:128 * (p + 1)]))
        return tiles

    qb = normed(A_BQ, 512, wqb_ref)
    kb = normed(A_BK, 128, wkb_ref)
    qc = normed(A_CQ, 512, wqc_ref)
    kc = normed(A_CK, 512, wkc_ref)

    vb_ref[...] = z_ref[:, A_BV:A_BV + 128].astype(F32).T.astype(BF16)
    vc_ref[...] = z_ref[:, A_CV:A_CV + 512]
    for p in range(4):
        qc_ref[:, 128 * p:128 * (p + 1)] = (qc[p] * qscale).astype(BF16)
        kc_ref[:, 128 * p:128 * (p + 1)] = kc[p].astype(BF16)

    cos = cos_ref[...]
    sin = sin_ref[...]
    for p in range(4):
        qb_ref[128 * p:128 * (p + 1), :] = (_rope(qb[p], cos, sin) * qscale).T.astype(BF16)
    kb_ref[...] = _rope(kb[0], cos, sin).astype(BF16)


def _attn_prep(z, wqb, wkb, wqc, wkc, cos_t, sin_t):
    tm = 512
    lat_blocks = LAT_LEN // tm
    rope_idx = lambda i: (i % lat_blocks, 0)
    row = lambda w: pl.BlockSpec((tm, w), lambda i: (i, 0))
    col = lambda w: pl.BlockSpec((w, tm), lambda i: (0, i))
    cst = lambda w: pl.BlockSpec((1, w), lambda i: (0, 0))
    sds = lambda w: jax.ShapeDtypeStruct((T_LAT, w), BF16)
    sds_t = lambda w: jax.ShapeDtypeStruct((w, T_LAT), BF16)
    return pl.pallas_call(
        _prep_kernel,
        grid=(T_LAT // tm,),
        in_specs=[
            pl.BlockSpec((tm, ATT_W), lambda i: (T_CTX // tm + i, Z_ATT // ATT_W)),
            cst(512), cst(128), cst(512), cst(512),
            pl.BlockSpec((tm, 128), rope_idx),
            pl.BlockSpec((tm, 128), rope_idx),
        ],
        out_specs=[col(512), row(128), col(128), row(512), row(512), row(512)],
        out_shape=[sds_t(512), sds(128), sds_t(128), sds(512), sds(512), sds(512)],
        compiler_params=_cparams(("arbitrary",)),
        name="attn_prep",
    )(z, wqb, wkb, wqc, wkc, cos_t, sin_t)


def _half_rows(t, half):
    z = jnp.zeros((HEAD_DIM, t.shape[1]), t.dtype)
    return jnp.concatenate([t[:HEAD_DIM], z] if half == 0 else [z, t[HEAD_DIM:]], axis=0)


def _ctx_attn_kernel(sink_ref, z_ref, wqb_ref, wkb_ref, wqc_ref, wkc_ref, *refs, layer):
    ob_ref, oc_ref = refs[-6:-4]
    cache_refs = refs[-4:]
    if len(refs) == 6:
        for ref in cache_refs:
            ref[...] = jnp.zeros(ref.shape, F32)
        cache_refs = [ref.at[layer] for ref in cache_refs]
    kbf_ref, vbf_ref, kcf_ref, vcf_ref = cache_refs
    vbf_ref[...] = z_ref[:, A_BV:A_BV + 128].astype(F32)
    vcf_ref[...] = z_ref[:, A_CV:A_CV + 512].astype(F32)
    n = z_ref.shape[0]
    qscale = HEAD_DIM ** -0.5

    def pair(col, p):
        return z_ref[:, col + 128 * p: col + 128 * (p + 1)]

    def q_t(col, w_ref, p):
        y = _pair_norm(pair(col, p).astype(F32), w_ref[:, 128 * p:128 * (p + 1)]) * qscale
        return y.T.astype(BF16)

    def v_t(col, p):
        return pair(col, p).astype(F32).T.astype(BF16)

    kb = _pair_norm(pair(A_BK, 0).astype(F32), wkb_ref[...])
    kbf_ref[...] = kb
    kb = kb.astype(BF16)
    qb_t = [q_t(A_BQ, wqb_ref, p) for p in range(4)]
    scores = []
    for kvh in range(SWA_KV_HEADS):
        placed = []
        for h in range(SWA_GROUP * kvh, SWA_GROUP * (kvh + 1)):
            rows = qb_t[h // 2][HEAD_DIM * (h % 2):HEAD_DIM * (h % 2 + 1)]
            z = jnp.zeros_like(rows)
            placed.append(jnp.concatenate([rows, z] if kvh == 0 else [z, rows], axis=0))
        scores.append(_dot(kb, jnp.concatenate(placed, axis=1)))
    for p in range(NA_HEADS // 2):
        kc = _pair_norm(pair(A_CK, p).astype(F32), wkc_ref[:, 128 * p:128 * (p + 1)])
        kcf_ref[:, 128 * p:128 * (p + 1)] = kc
        qc_t = q_t(A_CQ, wqc_ref, p)
        q2 = jnp.concatenate([_half_rows(qc_t, 0), _half_rows(qc_t, 1)], axis=1)
        scores.append(_dot(kc.astype(BF16), q2))
    s_t = jnp.concatenate(scores, axis=1)

    sink = jnp.concatenate(
        [jnp.full((1, n), sink_ref[layer, h], F32) for h in range(SWA_Q_HEADS)]
        + [jnp.full((1, n * NA_HEADS), -jnp.inf, F32)], axis=1)
    mx = jnp.maximum(s_t.max(axis=0, keepdims=True), sink)
    p_t = jnp.exp(s_t - mx)
    inv = 1.0 / (p_t.sum(axis=0, keepdims=True) + jnp.exp(sink - mx))
    p_t = p_t.astype(BF16)

    nb = SWA_Q_HEADS * n
    o = _dot(v_t(A_BV, 0), p_t[:, :nb]) * inv[:, :nb]
    outs = [o[HEAD_DIM * (h // SWA_GROUP):HEAD_DIM * (h // SWA_GROUP + 1), n * h:n * (h + 1)]
            for h in range(SWA_Q_HEADS)]
    ob_ref[...] = jnp.concatenate(outs, axis=0).T.astype(BF16)
    outs = []
    for p in range(NA_HEADS // 2):
        cols = slice(nb + 2 * n * p, nb + 2 * n * (p + 1))
        o = _dot(v_t(A_CV, p), p_t[:, cols]) * inv[:, cols]
        outs += [o[:HEAD_DIM, :n], o[HEAD_DIM:, n:]]
    oc_ref[...] = jnp.concatenate(outs, axis=0).T.astype(BF16)


def _ctx_attention(sink, z, wqb, wkb, wqc, wkc, caches, l):
    blk = lambda w, j=0: pl.BlockSpec((CTX_LEN, w), lambda b: (b, j))
    cst = lambda w: pl.BlockSpec((1, w), lambda b: (0, 0))
    sds = lambda w: jax.ShapeDtypeStruct((T_CTX, w), BF16)
    cache_w = (128, 128, 512, 512)
    if caches is None:
        cache_blk = [pl.BlockSpec((None, DEPTH, CTX_LEN, w), lambda b: (b, 0, 0, 0)) for w in cache_w]
    else:
        cache_blk = [pl.BlockSpec((None, None, CTX_LEN, w), lambda b: (b, l, 0, 0)) for w in cache_w]
    cache_sds = [jax.ShapeDtypeStruct((N_CTX_SEQ, DEPTH, CTX_LEN, w), F32) for w in cache_w]
    carried = [] if caches is None else list(caches)
    n_in = 6
    return pl.pallas_call(
        functools.partial(_ctx_attn_kernel, layer=l),
        grid=(N_CTX_SEQ,),
        in_specs=[pl.BlockSpec(memory_space=pltpu.SMEM), blk(ATT_W, Z_ATT // ATT_W),
                  cst(512), cst(128), cst(512), cst(512)]
                 + [pl.BlockSpec(memory_space=pl.ANY)] * len(carried),
        out_specs=[blk(512), blk(512)] + cache_blk,
        out_shape=[sds(512), sds(512)] + cache_sds,
        input_output_aliases={n_in + j: 2 + j for j in range(len(carried))},
        compiler_params=_cparams(("arbitrary",)),
        name="ctx_attention",
    )(sink, z, wqb, wkb, wqc, wkc, *carried)


SWA_QBLK = 128
SWA_SPAN = SWA_QBLK + 2 * SWA_WINDOW


def _swa_kernel(sink_ref, qt_ref, k_ref, vt_ref, kc_ref, vc_ref, o_ref, kctx_scr, vctx_scr, *, layer):
    i = pl.program_id(1)
    nq = SWA_QBLK

    @pl.when(i == 0)
    def _():
        kctx_scr[...] = kc_ref[...].T.astype(BF16)
        vctx_scr[...] = vc_ref[...].astype(BF16)

    start = pl.multiple_of(jnp.clip(nq * (i - 1), 0, LAT_LEN - SWA_SPAN), nq)
    kwin = k_ref[pl.ds(start, SWA_SPAN), :]
    vwin_t = vt_ref[:, pl.ds(start, SWA_SPAN)]
    kctx = kctx_scr[...]
    s_lat, s_ctx = [], []
    for kvh in range(SWA_KV_HEADS):
        placed = []
        for h in range(SWA_GROUP * kvh, SWA_GROUP * (kvh + 1)):
            rows = qt_ref[HEAD_DIM * h:HEAD_DIM * (h + 1), :]
            z = jnp.zeros_like(rows)
            placed.append(jnp.concatenate([rows, z] if kvh == 0 else [z, rows], axis=0))
        q4 = jnp.concatenate(placed, axis=1)
        s_lat.append(_dot(kwin, q4))
        s_ctx.append(_dot(kctx, q4))
    s_lat = jnp.concatenate(s_lat, axis=1)
    s_ctx = jnp.concatenate(s_ctx, axis=1)

    kpos = start + lax.broadcasted_iota(jnp.int32, (SWA_SPAN, nq), 0)
    qpos = nq * i + lax.broadcasted_iota(jnp.int32, (SWA_SPAN, nq), 1)
    band = jnp.where(jnp.abs(kpos - qpos) <= SWA_WINDOW, 0.0, -jnp.inf)
    s_lat = s_lat + jnp.concatenate([band] * SWA_Q_HEADS, axis=1)
    sink = jnp.concatenate(
        [jnp.full((1, nq), sink_ref[layer, h], F32) for h in range(SWA_Q_HEADS)], axis=1)
    mx = jnp.maximum(jnp.maximum(s_lat.max(axis=0, keepdims=True),
                                 s_ctx.max(axis=0, keepdims=True)), sink)
    p_lat = jnp.exp(s_lat - mx)
    p_ctx = jnp.exp(s_ctx - mx)
    inv = 1.0 / (p_lat.sum(axis=0, keepdims=True) + p_ctx.sum(axis=0, keepdims=True)
                 + jnp.exp(sink - mx))
    o = (_dot(vwin_t, p_lat.astype(BF16)) + _dot(vctx_scr[...], p_ctx.astype(BF16))) * inv
    outs = [o[HEAD_DIM * (h // SWA_GROUP):HEAD_DIM * (h // SWA_GROUP + 1), nq * h:nq * (h + 1)]
            for h in range(SWA_Q_HEADS)]
    o_ref[...] = jnp.concatenate(outs, axis=0).T.astype(BF16)


def _swa_attention(sink, qb_t, kb, vb_t, cache_k, cache_v, l):
    nq = LAT_LEN // SWA_QBLK
    cache = pl.BlockSpec((None, None, 128, PAST_LEN), lambda b, i: (b, l, 0, 0))
    return pl.pallas_call(
        functools.partial(_swa_kernel, layer=l),
        grid=(N_LAT_SEQ, nq),
        in_specs=[pl.BlockSpec(memory_space=pltpu.SMEM),
                  pl.BlockSpec((512, SWA_QBLK), lambda b, i: (0, b * nq + i)),
                  pl.BlockSpec((LAT_LEN, 128), lambda b, i: (b, 0)),
                  pl.BlockSpec((128, LAT_LEN), lambda b, i: (0, b)),
                  cache, cache],
        out_specs=pl.BlockSpec((SWA_QBLK, 512), lambda b, i: (b * nq + i, 0)),
        out_shape=jax.ShapeDtypeStruct((T_LAT, 512), BF16),
        scratch_shapes=[pltpu.VMEM((PAST_LEN, 128), BF16), pltpu.VMEM((128, PAST_LEN), BF16)],
        compiler_params=_cparams(("arbitrary", "arbitrary")),
        name="swa_attention",
    )(sink, qb_t, kb, vb_t, cache_k, cache_v)


LAT_ROWS = LAT_LEN // GRID_W
NA_WIN_ROWS = min(NA_ROWS, LAT_ROWS)
NA_KEYS = NA_WIN_ROWS * GRID_W


def _na_kernel(q_ref, k_ref, v_ref, kc_ref, vc_ref, rp_ref, o_ref, kctx_scr, vctx_scr, bias_ref):
    r = pl.program_id(1)
    low = lax.broadcasted_iota(jnp.int32, (1, 128), 1) < HEAD_DIM

    @pl.when(r == 0)
    def _():
        kctx_scr[...] = kc_ref[...].astype(BF16)
        vctx_scr[...] = vc_ref[...].astype(BF16)
        cq = lax.broadcasted_iota(jnp.int32, (GRID_W, 128), 0)
        ck = lax.broadcasted_iota(jnp.int32, (GRID_W, 128), 1) % GRID_W
        cs = jnp.clip(cq - NA_COLS // 2, 0, GRID_W - NA_COLS)
        window = jnp.where((ck >= cs) & (ck < cs + NA_COLS), 0.0, -jnp.inf)
        for h in range(NA_HEADS):
            rows = [jnp.broadcast_to(rp_ref[h, dr:dr + 1, :], (GRID_W, 128))
                    for dr in range(2 * NA_ROWS - 1)]
            left = [pltpu.roll(x, 0, 1, stride=1, stride_axis=0) for x in rows[:-1]]
            right = [pltpu.roll(x, GRID_W, 1, stride=1, stride_axis=0) for x in rows[1:]]
            for j in range(2 * NA_ROWS - 2):
                bias_ref[h, j] = jnp.where(low, left[j], right[j]) + window

    rs = jnp.clip(r - NA_ROWS // 2, 0, LAT_ROWS - NA_WIN_ROWS)
    start = pl.multiple_of(rs * GRID_W, GRID_W)
    base = rs - r + NA_ROWS - 1
    npair = NA_HEADS // 2
    s_lat, s_ctx = [], []
    for p in range(npair):
        sl = slice(128 * p, 128 * (p + 1))
        q = q_ref[:, sl]
        zero = jnp.zeros_like(q)
        q2 = jnp.concatenate([jnp.where(low, q, zero), jnp.where(low, zero, q)], axis=0)
        bias = jnp.concatenate(
            [jnp.concatenate([bias_ref[2 * p + hh, base + 2 * w] for w in range(NA_WIN_ROWS // 2)],
                             axis=1) for hh in range(2)], axis=0)
        s_lat.append(_nt_dot(q2, k_ref[pl.ds(start, NA_KEYS), sl]) + bias)
        s_ctx.append(_dot(q2, kctx_scr[sl, :]))
    s_lat = jnp.concatenate(s_lat, axis=0)
    s_ctx = jnp.concatenate(s_ctx, axis=0)
    mx = jnp.maximum(s_lat.max(axis=-1, keepdims=True), s_ctx.max(axis=-1, keepdims=True))
    p_lat = jnp.exp(s_lat - mx)
    p_ctx = jnp.exp(s_ctx - mx)
    inv = 1.0 / (p_lat.sum(axis=-1, keepdims=True) + p_ctx.sum(axis=-1, keepdims=True))
    p_lat = p_lat.astype(BF16)
    p_ctx = p_ctx.astype(BF16)
    tiles = []
    for p in range(npair):
        sl = slice(128 * p, 128 * (p + 1))
        rows = slice(2 * GRID_W * p, 2 * GRID_W * (p + 1))
        o2 = (_dot(p_lat[rows], v_ref[pl.ds(start, NA_KEYS), sl])
              + _nt_dot(p_ctx[rows], vctx_scr[sl, :])) * inv[rows]
        tiles.append(jnp.where(low, o2[:GRID_W], o2[GRID_W:]))
    o_ref[...] = jnp.concatenate(tiles, axis=1).astype(BF16)


def _na_attention(qc, kc, vc, cache_k, cache_v, rp_cyc, l):
    kv = pl.BlockSpec((LAT_LEN, 512), lambda b, r: (b, 0))
    cache = pl.BlockSpec((None, None, 512, PAST_LEN), lambda b, r: (b, l, 0, 0))
    return pl.pallas_call(
        _na_kernel,
        grid=(N_LAT_SEQ, LAT_ROWS),
        in_specs=[pl.BlockSpec((GRID_W, 512), lambda b, r: (b * LAT_ROWS + r, 0)),
                  kv, kv, cache, cache,
                  pl.BlockSpec((None, NA_HEADS, 2 * NA_ROWS - 1, 128), lambda b, r: (l, 0, 0, 0))],
        out_specs=pl.BlockSpec((GRID_W, 512), lambda b, r: (b * LAT_ROWS + r, 0)),
        out_shape=jax.ShapeDtypeStruct((T_LAT, 512), BF16),
        scratch_shapes=[pltpu.VMEM((512, PAST_LEN), BF16), pltpu.VMEM((512, PAST_LEN), BF16),
                        pltpu.VMEM((NA_HEADS, 2 * NA_ROWS - 2, GRID_W, 128), F32)],
        compiler_params=_cparams(("arbitrary", "arbitrary")),
        name="na_attention",
    )(qc, kc, vc, cache_k, cache_v, rp_cyc)


def _na_table_rows(rpb):
    pad = jnp.zeros(rpb.shape[:-1] + (128 - (2 * NA_COLS - 1),), F32)
    return jnp.concatenate([rpb[..., NA_COLS - 1:], pad, rpb[..., :NA_COLS - 1]], axis=-1)


def _gla_constants():
    c = GLA_C
    t = np.arange(c)[:, None]
    u = np.arange(c)[None, :]
    blocks = []
    for k in range(GLA_LEVELS):
        b = 1 << k
        m = ((t >> k) | 1) * b - 1
        query = ((t >> k) & 1) == 1
        blocks.append(np.where(query, (u > m) & (u <= t), (u > t) & (u <= m)))
    blocks.append(u <= t)
    blocks.append(u > t)
    blocks.append(np.ones((16, c), bool))
    fwd = np.concatenate(blocks, axis=0).astype(np.float32)
    bwd_blocks = [blk[::-1, ::-1] for blk in blocks]
    bwd = np.concatenate(bwd_blocks, axis=0).astype(np.float32)
    seg = np.stack([fwd, bwd])
    seg = np.concatenate([seg, seg], axis=-1)

    s = np.arange(c)[None, :]
    x = t ^ s
    lev = np.where(x == 0, GLA_LEVELS, np.floor(np.log2(np.maximum(x, 1))).astype(np.int64))
    lev_f = np.where(s <= t, lev, -1)
    lev_b = np.where(s >= t, lev, -1)
    levmap = np.stack([np.tile(lev_f, (1, GLA_HEADS)), np.tile(lev_b, (1, GLA_HEADS))])
    return seg, levmap.astype(np.int32)


def _gla_state_init(d, cc, s0_ref, st_ref):
    c = GLA_C
    ctx_chunks = T_CTX // c
    is_lat = cc >= ctx_chunks
    per_seq = jnp.where(is_lat, LAT_LEN // c, CTX_LEN // c)
    pos = jnp.where(is_lat, cc - ctx_chunks, cc) % per_seq
    first = (pos == 0) if d == 0 else (pos == per_seq - 1)

    @pl.when(first)
    def _():
        for h in range(GLA_HEADS):
            blk = jnp.where(is_lat, s0_ref[h], 0.0)
            row = [jnp.zeros((GLA_DV, GLA_DK), F32)] * GLA_HEADS
            row[h] = blk
            st_ref[GLA_DV * h:GLA_DV * (h + 1), :] = jnp.concatenate(row, axis=1)


def _gla_chunk(q_ref, k_ref, v_ref, lr_ref, w2_ref, ba_ref, seg_ref, lev_ref, o_ref, sfin_ref, st_ref):
    c = GLA_C
    x = _dot(lr_ref[...], w2_ref[...]) + ba_ref[...]
    la = (jnp.minimum(x, 0.0) - jnp.log1p(jnp.exp(-jnp.abs(x)))) * (1.0 / GLA_TAU)
    la_hi = la.astype(BF16)
    la_lo = (la - la_hi.astype(F32)).astype(BF16)
    seg = _dot(seg_ref[...], jnp.concatenate([la_hi, la_lo], axis=0))

    q = q_ref[...].astype(F32) * (GLA_DK ** -0.5)
    k = k_ref[...].astype(F32)
    v = v_ref[...]
    lev = lev_ref[...]
    lane_head = lax.broadcasted_iota(jnp.int32, (1, GLA_HEADS * GLA_DK), 1) // GLA_DK
    head_sel = [jnp.where(lane_head == h, 1.0, 0.0).astype(BF16) for h in range(GLA_HEADS)]

    def pair_scores(qs, ks):
        ksb = ks.astype(BF16)
        kbd = jnp.concatenate([ksb * head_sel[h] for h in range(GLA_HEADS)], axis=0)
        return _nt_dot(qs.astype(BF16), kbd)

    attn = jnp.where(lev == GLA_LEVELS, pair_scores(q, k), 0.0)
    for lvl in range(GLA_LEVELS):
        e = jnp.exp(seg[c * lvl:c * (lvl + 1)])
        attn = jnp.where(lev == lvl, pair_scores(q * e, k * e), attn)
    attn = attn.astype(BF16)

    cum = seg[c * GLA_LEVELS:c * (GLA_LEVELS + 1)]
    rem = seg[c * (GLA_LEVELS + 1):c * (GLA_LEVELS + 2)]
    tot = seg[c * (GLA_LEVELS + 2):c * (GLA_LEVELS + 2) + 1]
    q_in = (q * jnp.exp(cum)).astype(BF16)
    k_in = (k * jnp.exp(rem)).astype(BF16)
    state = st_ref[...]
    o_inter = _nt_dot(q_in, state.astype(BF16))
    for h in range(GLA_HEADS):
        sl = slice(GLA_DV * h, GLA_DV * (h + 1))
        o_ref[:, sl] = o_inter[:, sl] + _dot(attn[:, c * h:c * (h + 1)], v[:, sl])

    upd = _tn_dot(v, k_in)
    row_head = lax.broadcasted_iota(jnp.int32, (GLA_HEADS * GLA_DV, 1), 0) // GLA_DV
    new_state = state * jnp.exp(tot) + jnp.where(row_head == lane_head, upd, 0.0)
    st_ref[...] = new_state
    for h in range(GLA_HEADS):
        sfin_ref[h] = new_state[GLA_DV * h:GLA_DV * (h + 1), GLA_DK * h:GLA_DK * (h + 1)]


def _gla_kernel(qf, kf, vf, lrf, qb, kb, vb, lrb, w2_ref, ba_ref, seg_ref, lev_ref, s0f, s0b,
                of_ref, ob_ref, sff_ref, sfb_ref, stf, stb):
    step = pl.program_id(0)
    last = pl.num_programs(0) - 1
    _gla_state_init(0, step, s0f, stf)
    _gla_state_init(1, last - step, s0b, stb)
    _gla_chunk(qf, kf, vf, lrf, w2_ref.at[0], ba_ref.at[0], seg_ref.at[0], lev_ref.at[0],
               of_ref, sff_ref, stf)
    _gla_chunk(qb, kb, vb, lrb, w2_ref.at[1], ba_ref.at[1], seg_ref.at[1], lev_ref.at[1],
               ob_ref, sfb_ref, stb)


def _gla(z, w2p, ba, seg, levmap, s0t, l):
    c = GLA_C
    nchunk = T_ALL // c
    ctx_chunks = T_CTX // c
    nseq = N_CTX_SEQ + N_LAT_SEQ
    fwd = lambda s: s
    bwd = lambda s: nchunk - 1 - s

    def seq_of(cc):
        return jnp.where(cc < ctx_chunks, cc // (CTX_LEN // c),
                         N_CTX_SEQ + (cc - ctx_chunks) // (LAT_LEN // c))

    def lat_of(cc):
        return jnp.clip((cc - ctx_chunks) // (LAT_LEN // c), 0, N_LAT_SEQ - 1)

    def chunk_specs(pos):
        return [pl.BlockSpec((c, 256), lambda s: (pos(s), Z_AQ // 256)),
                pl.BlockSpec((c, 256), lambda s: (pos(s), Z_AK // 256)),
                pl.BlockSpec((c, 512), lambda s: (pos(s), Z_AV // 512)),
                pl.BlockSpec((c, 128), lambda s: (pos(s), Z_LR // 128))]

    def s0_spec(d, pos):
        return pl.BlockSpec((None, None, None, GLA_HEADS, GLA_DV, GLA_DK),
                            lambda s: (lat_of(pos(s)), l, d, 0, 0, 0))

    def sfin_spec(pos):
        return pl.BlockSpec((None, GLA_HEADS, GLA_DV, GLA_DK), lambda s: (seq_of(pos(s)), 0, 0, 0))

    state_sds = jax.ShapeDtypeStruct((nseq, GLA_HEADS, GLA_DV, GLA_DK), F32)
    out_sds = jax.ShapeDtypeStruct((T_ALL, 512), F32)
    state_scr = pltpu.VMEM((GLA_HEADS * GLA_DV, GLA_HEADS * GLA_DK), F32)
    return pl.pallas_call(
        _gla_kernel,
        grid=(nchunk,),
        in_specs=chunk_specs(fwd) + chunk_specs(bwd) + [
            pl.BlockSpec((None, 2, 128, 256), lambda s: (l, 0, 0, 0)),
            pl.BlockSpec((None, 2, 1, 256), lambda s: (l, 0, 0, 0)),
            pl.BlockSpec((2, GLA_GROWS, 2 * c), lambda s: (0, 0, 0)),
            pl.BlockSpec((2, c, GLA_HEADS * c), lambda s: (0, 0, 0)),
            s0_spec(0, fwd), s0_spec(1, bwd),
        ],
        out_specs=[pl.BlockSpec((c, 512), lambda s: (fwd(s), 0)),
                   pl.BlockSpec((c, 512), lambda s: (bwd(s), 0)),
                   sfin_spec(fwd), sfin_spec(bwd)],
        out_shape=[out_sds, out_sds, state_sds, state_sds],
        scratch_shapes=[state_scr, state_scr],
        compiler_params=_cparams(("arbitrary",)),
        name="gla",
    )(z, z, z, z, z, z, z, z, w2p, ba, seg, levmap, s0t, s0t)


def _merge_kernel(x_ref, of_ref, obk_ref, ar_ref, gates_ref, bc_ref, bl_ref, cc_ref, cl_ref,
                  wpa_f32, wpb_f32, wpc_f32, wo_f32, gn_ref, mod_ref, y_ref,
                  wpa_ref, wpb_ref, wpc_ref, wo_ref):
    @pl.when(pl.program_id(0) == 0)
    def _():
        for src, dst in ((wpa_f32, wpa_ref), (wpb_f32, wpb_ref), (wpc_f32, wpc_ref), (wo_f32, wo_ref)):
            dst[...] = src[...].astype(BF16)

    ctx = pl.program_id(0) < T_CTX // x_ref.shape[0]
    o = of_ref[...] + obk_ref[...]
    heads = []
    for h in range(GLA_HEADS):
        oh = o[:, GLA_DV * h:GLA_DV * (h + 1)]
        ms = jnp.mean(oh * oh, axis=-1, keepdims=True)
        heads.append(oh * lax.rsqrt(ms + EPS) * gn_ref[...])
    oa = (jnp.concatenate(heads, axis=1) * _silu(ar_ref[...].astype(F32))).astype(BF16)
    ob = jnp.where(ctx, bc_ref[...], bl_ref[...])
    oc = jnp.where(ctx, cc_ref[...], cl_ref[...])
    d = D_MODEL
    gate = lambda j: _sigmoid(gates_ref[:, j * d:(j + 1) * d].astype(F32))
    merged = (gate(0) * _dot(oa, wpa_ref[...]) + gate(1) * _dot(ob, wpb_ref[...])
              + gate(2) * _dot(oc, wpc_ref[...]))
    out = _dot(merged.astype(BF16), wo_ref[...])
    y_ref[...] = x_ref[...] + mod_ref[:, 2 * d:3 * d] * out


def _merge(x, o_fwd, o_bwd, z, ob_ctx, ob_lat, oc_ctx, oc_lat, wpa, wpb, wpc, wo, gn, mod4, l):
    tm = 512
    ctx_blocks = T_CTX // tm
    ctx_idx = lambda i: (jnp.minimum(i, ctx_blocks - 1), 0)
    lat_idx = lambda i: (jnp.maximum(i - ctx_blocks, 0), 0)
    wspec = lambda k: pl.BlockSpec((None, k, D_MODEL), lambda i: (l, 0, 0))
    return pl.pallas_call(
        _merge_kernel,
        grid=(T_ALL // tm,),
        in_specs=[
            pl.BlockSpec((tm, D_MODEL), lambda i: (i, 0)),
            pl.BlockSpec((tm, 512), lambda i: (i, 0)),
            pl.BlockSpec((tm, 512), lambda i: (i, 0)),
            pl.BlockSpec((tm, 512), lambda i: (i, Z_AR // 512)),
            pl.BlockSpec((tm, 3 * D_MODEL), lambda i: (i, 0)),
            pl.BlockSpec((tm, 512), ctx_idx), pl.BlockSpec((tm, 512), lat_idx),
            pl.BlockSpec((tm, 512), ctx_idx), pl.BlockSpec((tm, 512), lat_idx),
            wspec(512), wspec(512), wspec(512), wspec(D_MODEL),
            pl.BlockSpec((None, 1, GLA_DV), lambda i: (l, 0, 0)),
            pl.BlockSpec((None, None, 1, 6 * D_MODEL), lambda i: (l, _group_of_rows(i, tm), 0, 0)),
        ],
        out_specs=pl.BlockSpec((tm, D_MODEL), lambda i: (i, 0)),
        out_shape=jax.ShapeDtypeStruct((T_ALL, D_MODEL), F32),
        scratch_shapes=[pltpu.VMEM((512, D_MODEL), BF16)] * 3 + [pltpu.VMEM((D_MODEL, D_MODEL), BF16)],
        compiler_params=_cparams(("arbitrary",)),
        name="merge",
    )(x, o_fwd, o_bwd, z, z, ob_ctx, ob_lat, oc_ctx, oc_lat, wpa, wpb, wpc, wo, gn, mod4)


def _mlp_kernel(x_ref, g_ref, mod_ref, w1_ref, w2_ref, y_ref, w1_scr, w2_scr, h_scr, acc_scr, *, nk):
    s = pl.program_id(0)
    d = D_MODEL
    th = w1_ref.shape[1]

    def normed():
        h = _norm_mod(x_ref[...], g_ref[...], mod_ref[:, 3 * d:4 * d], mod_ref[:, 4 * d:5 * d])
        return h.astype(BF16)

    def act(u):
        return jnp.square(jnp.maximum(u, 0.0)).astype(BF16)

    @pl.when(s == 0)
    def _():
        h_scr[...] = normed()

    @pl.when(s < nk)
    def _():
        col = pl.multiple_of(s * th, th)
        w1 = w1_ref[...].astype(BF16)
        w2 = w2_ref[...].astype(BF16)
        w1_scr[:, pl.ds(col, th)] = w1
        w2_scr[pl.ds(col, th), :] = w2
        part = _dot(act(_dot(h_scr[...], w1)), w2)

        @pl.when(s == 0)
        def _():
            acc_scr[...] = part

        @pl.when(s > 0)
        def _():
            acc_scr[...] += part

        @pl.when(s == nk - 1)
        def _():
            y_ref[...] = x_ref[...] + mod_ref[:, 5 * d:6 * d] * acc_scr[...]

    @pl.when(s >= nk)
    def _():
        out = _dot(act(_dot(normed(), w1_scr[...])), w2_scr[...])
        y_ref[...] = x_ref[...] + mod_ref[:, 5 * d:6 * d] * out


def _mlp(x, norm_w, mod4, w1, w2, l):
    tm, th = 512, 512
    nk = MLP_HIDDEN // th
    row = lambda s: jnp.maximum(s - (nk - 1), 0)
    chunk = lambda s: jnp.minimum(s, nk - 1)
    return pl.pallas_call(
        functools.partial(_mlp_kernel, nk=nk),
        grid=(nk - 1 + T_ALL // tm,),
        in_specs=[
            pl.BlockSpec((tm, D_MODEL), lambda s: (row(s), 0)),
            pl.BlockSpec((None, 1, D_MODEL), lambda s: (l, 0, 0)),
            pl.BlockSpec((None, None, 1, 6 * D_MODEL),
                         lambda s: (l, _group_of_rows(row(s), tm), 0, 0)),
            pl.BlockSpec((None, D_MODEL, th), lambda s: (l, 0, chunk(s))),
            pl.BlockSpec((None, th, D_MODEL), lambda s: (l, chunk(s), 0)),
        ],
        out_specs=pl.BlockSpec((tm, D_MODEL), lambda s: (row(s), 0)),
        out_shape=jax.ShapeDtypeStruct((T_ALL, D_MODEL), F32),
        scratch_shapes=[pltpu.VMEM((D_MODEL, MLP_HIDDEN), BF16), pltpu.VMEM((MLP_HIDDEN, D_MODEL), BF16),
                        pltpu.VMEM((tm, D_MODEL), BF16), pltpu.VMEM((tm, D_MODEL), F32)],
        compiler_params=_cparams(("arbitrary",)),
        name="mlp",
    )(x, norm_w, mod4, w1, w2)


def _rope_tables():
    t = jnp.arange(LAT_LEN)
    row = (t // GRID_W).astype(F32)
    col = (t % GRID_W).astype(F32)
    nf = HEAD_DIM // 4
    inv_freq = ROPE_BASE ** (-jnp.arange(nf, dtype=F32) / nf)
    ang_r = row[:, None] * inv_freq[None, :]
    ang_c = col[:, None] * inv_freq[None, :]
    cos = jnp.concatenate([jnp.cos(ang_r)] * 2 + [jnp.cos(ang_c)] * 2, axis=1)
    sin = jnp.concatenate([-jnp.sin(ang_r), jnp.sin(ang_r), -jnp.sin(ang_c), jnp.sin(ang_c)], axis=1)
    return jnp.tile(cos, (1, 2)), jnp.tile(sin, (1, 2))


def kernel(x_prompt, x_sample, state_gla, cache_swa_k, cache_swa_v, cache_na_k, cache_na_v, c,
           c_ctx, w_mod, b_mod, norm1, norm2, w_in, w_a2_f, b_a_f, w_a2_b, b_a_b, gla_onorm,
           qn_swa, kn_swa, sink_swa, qn_na, kn_na, rpb_na, w_pa, w_pb, w_pc, w_o, w_fc1, w_fc2):
    d = D_MODEL
    x = jnp.concatenate([x_prompt.reshape(T_CTX, d), x_sample.reshape(T_LAT, d)], axis=0)

    cond8 = jnp.zeros((8, d), F32).at[0].set(c_ctx).at[1:1 + N_LAT_SEQ].set(c)
    mod4 = _modulation(cond8, w_mod, b_mod).reshape(DEPTH, 8, 1, 6 * d)

    w_in_p = _w_in_layout(jnp.swapaxes(w_in, 1, 2))
    norm1r = norm1.reshape(DEPTH, 1, d)
    norm2r = norm2.reshape(DEPTH, 1, d)
    gnr = gla_onorm.reshape(DEPTH, 1, GLA_DV)
    w2p = jnp.zeros((DEPTH, 2, 128, GLA_HEADS * GLA_DK), F32)
    w2p = w2p.at[:, 0, 0:GLA_LOWRANK].set(w_a2_f).at[:, 1, GLA_LOWRANK:2 * GLA_LOWRANK].set(w_a2_b)
    w2p = w2p.astype(BF16)
    ba = jnp.stack([b_a_f, b_a_b], axis=1).reshape(DEPTH, 2, 1, GLA_HEADS * GLA_DK)
    seg_np, lev_np = _gla_constants()
    seg = jnp.asarray(seg_np, BF16)
    levmap = jnp.asarray(lev_np)
    s0t = jnp.swapaxes(state_gla, -1, -2)
    cos_t, sin_t = _rope_tables()
    rp_cyc = _na_table_rows(rpb_na)
    feat_major = lambda a: a.transpose(0, 1, 3, 4, 2).reshape(N_LAT_SEQ, DEPTH, -1, PAST_LEN)
    csk, csv, cnk, cnv = (feat_major(a) for a in (cache_swa_k, cache_swa_v, cache_na_k, cache_na_v))

    st_l = []
    caches = None
    for l in range(DEPTH):
        z = _inproj(x, norm1r, mod4, w_in_p, l)
        wqb = jnp.tile(qn_swa[l], 8)[None, :]
        wkb = jnp.tile(kn_swa[l], 2)[None, :]
        wqc = jnp.tile(qn_na[l], 8)[None, :]
        wkc = jnp.tile(kn_na[l], 8)[None, :]
        qb, kb, vb, qc, kc, vc = _attn_prep(z, wqb, wkb, wqc, wkc, cos_t, sin_t)
        o_fwd, o_bwd, sfin_f, sfin_b = _gla(z, w2p, ba, seg, levmap, s0t, l)
        ob_ctx, oc_ctx, *caches = _ctx_attention(sink_swa, z, wqb, wkb, wqc, wkc, caches, l)
        ob_lat = _swa_attention(sink_swa, qb, kb, vb, csk, csv, l)
        oc_lat = _na_attention(qc, kc, vc, cnk, cnv, rp_cyc, l)
        x = _merge(x, o_fwd, o_bwd, z, ob_ctx, ob_lat, oc_ctx, oc_lat, w_pa, w_pb, w_pc, w_o, gnr, mod4, l)
        x = _mlp(x, norm2r, mod4, w_fc1, w_fc2, l)

        sfin = jnp.stack([sfin_f[:N_CTX_SEQ], sfin_b[:N_CTX_SEQ]], axis=0)
        st_l.append(jnp.swapaxes(sfin, -1, -2))

    y_prompt = x[:T_CTX].reshape(N_CTX_SEQ, CTX_LEN, d)
    y_sample = x[T_CTX:].reshape(N_LAT_SEQ, LAT_LEN, d)
    new_state = jnp.stack(st_l, axis=0).transpose(2, 0, 1, 3, 4, 5)

    swa_k, swa_v, na_k, na_v = caches
    kv_shape = lambda heads: (N_CTX_SEQ, DEPTH, CTX_LEN, heads, HEAD_DIM)
    return (y_prompt, y_sample, new_state,
            swa_k.reshape(kv_shape(SWA_KV_HEADS)), swa_v.reshape(kv_shape(SWA_KV_HEADS)),
            na_k.reshape(kv_shape(NA_HEADS)), na_v.reshape(kv_shape(NA_HEADS)))
```

```python
import functools

import numpy as np
import jax
import jax.numpy as jnp
from jax import lax
from jax.experimental import pallas as pl
from jax.experimental.pallas import tpu as pltpu

F32 = jnp.float32
BF16 = jnp.bfloat16

D_MODEL = 1024
DEPTH = 4
N_CTX_SEQ = 16
CTX_LEN = 256
N_LAT_SEQ = 2
LAT_LEN = 1024
PAST_LEN = 512
T_CTX = N_CTX_SEQ * CTX_LEN
T_LAT = N_LAT_SEQ * LAT_LEN
T_ALL = T_CTX + T_LAT
GRID_W = 64
HEAD_DIM = 64
GLA_HEADS = 4
GLA_DK = 64
GLA_DV = 128
GLA_LOWRANK = 16
GLA_TAU = 16.0
SWA_Q_HEADS = 8
SWA_KV_HEADS = 2
SWA_GROUP = 4
SWA_WINDOW = 128
NA_HEADS = 8
NA_ROWS = 8
NA_COLS = 16
MLP_HIDDEN = 4 * D_MODEL
ROPE_BASE = 10000.0
EPS = 1e-6

Z_GATES = 0
Z_GLA = 3072
Z_AQ, Z_AK, Z_AV, Z_AR = 3072, 3328, 3584, 4096
Z_ATT = 4608
ATT_W = 2304
A_BQ, A_BK, A_BV, A_CQ, A_CK, A_CV = 0, 512, 640, 768, 1280, 1792
Z_LR = 6912
Z_W = 7168

GLA_C = 128
GLA_LEVELS = 7
GLA_GROWS = (GLA_LEVELS + 2) * GLA_C + 16

VMEM_LIMIT = 56 * 1024 * 1024


def _cparams(sem):
    return pltpu.CompilerParams(dimension_semantics=sem, vmem_limit_bytes=VMEM_LIMIT)


def _sigmoid(x):
    return 1.0 / (1.0 + jnp.exp(-x))


def _silu(x):
    return x * _sigmoid(x)


def _nt_dot(a, b):
    return lax.dot_general(a, b, (((1,), (1,)), ((), ())), preferred_element_type=F32)


def _tn_dot(a, b):
    return lax.dot_general(a, b, (((0,), (0,)), ((), ())), preferred_element_type=F32)


def _dot(a, b):
    return jnp.dot(a, b, preferred_element_type=F32)


def _mod_kernel(cond_ref, w_ref, b_ref, o_ref):
    s = _silu(cond_ref[...]).astype(BF16)
    o_ref[...] = _dot(s, w_ref[...].astype(BF16)) + b_ref[...]


def _modulation(cond8, w_mod, b_mod):
    tn = 1024
    return pl.pallas_call(
        _mod_kernel,
        grid=(DEPTH, 6 * D_MODEL // tn),
        in_specs=[
            pl.BlockSpec((8, D_MODEL), lambda l, j: (0, 0)),
            pl.BlockSpec((None, D_MODEL, tn), lambda l, j: (l, 0, j)),
            pl.BlockSpec((None, 1, tn), lambda l, j: (l, 0, j)),
        ],
        out_specs=pl.BlockSpec((None, 8, tn), lambda l, j: (l, 0, j)),
        out_shape=jax.ShapeDtypeStruct((DEPTH, 8, 6 * D_MODEL), F32),
        compiler_params=_cparams(("arbitrary", "arbitrary")),
        name="modulation",
    )(cond8, w_mod, b_mod.reshape(DEPTH, 1, 6 * D_MODEL))


def _group_of_rows(row_block, rows_per_block):
    first = row_block * rows_per_block
    return jnp.maximum(first - T_CTX + LAT_LEN, 0) // LAT_LEN


def _norm_mod(x, g, shift, scale):
    ms = jnp.mean(x * x, axis=-1, keepdims=True)
    y = x * lax.rsqrt(ms + EPS) * g
    return y * (1.0 + scale) + shift


W_IN_SEGMENTS = ((3872, 6944), (0, 1536), (1568, 3872), (1536, 1568))
W_IN_COLS = 6944


def _w_in_layout_kernel(w_ref, o_ref):
    dst = 0
    for lo, hi in W_IN_SEGMENTS:
        o_ref[dst:dst + hi - lo, :] = w_ref[lo:hi, :].astype(BF16)
        dst += hi - lo
    o_ref[dst:, :] = jnp.zeros((Z_W - dst, o_ref.shape[1]), BF16)


def _w_in_layout(w_in_t):
    tk = 256
    return pl.pallas_call(
        _w_in_layout_kernel,
        grid=(DEPTH, D_MODEL // tk),
        in_specs=[pl.BlockSpec((None, W_IN_COLS, tk), lambda l, r: (l, 0, r))],
        out_specs=pl.BlockSpec((None, Z_W, tk), lambda l, r: (l, 0, r)),
        out_shape=jax.ShapeDtypeStruct((DEPTH, Z_W, D_MODEL), BF16),
        compiler_params=_cparams(("arbitrary", "arbitrary")),
        name="w_in_layout",
    )(w_in_t)


def _inproj_kernel(x_ref, g_ref, mod_ref, w_ref, z_ref, h_scr):
    @pl.when(pl.program_id(1) == 0)
    def _():
        h = _norm_mod(x_ref[...], g_ref[...],
                      mod_ref[:, 0:D_MODEL], mod_ref[:, D_MODEL:2 * D_MODEL])
        h_scr[...] = h.astype(BF16)

    tn = z_ref.shape[1]
    row = pl.multiple_of(pl.program_id(1) * tn, tn)
    z_ref[...] = _nt_dot(h_scr[...], w_ref[pl.ds(row, tn), :]).astype(z_ref.dtype)


def _inproj(x, norm_w, mod4, w_in_p, l):
    tm, tn = 1024, 1792
    return pl.pallas_call(
        _inproj_kernel,
        grid=(T_ALL // tm, Z_W // tn),
        in_specs=[
            pl.BlockSpec((tm, D_MODEL), lambda i, j: (i, 0)),
            pl.BlockSpec((None, 1, D_MODEL), lambda i, j: (l, 0, 0)),
            pl.BlockSpec((None, None, 1, 6 * D_MODEL),
                         lambda i, j: (l, _group_of_rows(i, tm), 0, 0)),
            pl.BlockSpec((None, Z_W, D_MODEL), lambda i, j: (l, 0, 0),
                         pipeline_mode=pl.Buffered(1)),
        ],
        out_specs=pl.BlockSpec((tm, tn), lambda i, j: (i, j)),
        out_shape=jax.ShapeDtypeStruct((T_ALL, Z_W), BF16),
        scratch_shapes=[pltpu.VMEM((tm, D_MODEL), BF16)],
        compiler_params=_cparams(("arbitrary", "arbitrary")),
        name="inproj",
    )(x, norm_w, mod4, w_in_p)


def _pair_norm(x, w):
    sq = x * x
    low = lax.broadcasted_iota(jnp.int32, (1, 128), 1) < HEAD_DIM
    s_low = jnp.sum(jnp.where(low, sq, 0.0), axis=-1, keepdims=True)
    s_all = jnp.sum(sq, axis=-1, keepdims=True)
    ms = jnp.where(low, s_low, s_all - s_low) * (1.0 / HEAD_DIM)
    return x * lax.rsqrt(ms + EPS) * w


def _rope(x, cos, sin_signed):
    n = x.shape[-1]
    lane = lax.broadcasted_iota(jnp.int32, (1, n), 1)
    first = (lane % 32) < 16
    partner = jnp.where(first, pltpu.roll(x, n - 16, 1), pltpu.roll(x, 16, 1))
    return x * cos + partner * sin_signed


def _prep_kernel(z_ref, wqb_ref, wkb_ref, wqc_ref, wkc_ref, cos_ref, sin_ref,
                 qb_ref, kb_ref, vb_ref, qc_ref, kc_ref, vc_ref):
    qscale = HEAD_DIM ** -0.5

    def normed(col, width, w_ref):
        tiles = []
        for p in range(width // 128):
            x = z_ref[:, col + 128 * p: col + 128 * (p + 1)].astype(F32)
            tiles.append(_pair_norm(x, w_ref[:, 128 * p:128 * (p + 1)]))
        return tiles

    qb = normed(A_BQ, 512, wqb_ref)
    kb = normed(A_BK, 128, wkb_ref)
    qc = normed(A_CQ, 512, wqc_ref)
    kc = normed(A_CK, 512, wkc_ref)

    vb_ref[...] = z_ref[:, A_BV:A_BV + 128].astype(F32).T.astype(BF16)
    vc_ref[...] = z_ref[:, A_CV:A_CV + 512]
    for p in range(4):
        qc_ref[:, 128 * p:128 * (p + 1)] = (qc[p] * qscale).astype(BF16)
        kc_ref[:, 128 * p:128 * (p + 1)] = kc[p].astype(BF16)

    cos = cos_ref[...]
    sin = sin_ref[...]
    for p in range(4):
        qb_ref[128 * p:128 * (p + 1), :] = (_rope(qb[p], cos, sin) * qscale).T.astype(BF16)
    kb_ref[...] = _rope(kb[0], cos, sin).astype(BF16)


def _attn_prep(z, wqb, wkb, wqc, wkc, cos_t, sin_t):
    tm = 512
    lat_blocks = LAT_LEN // tm
    rope_idx = lambda i: (i % lat_blocks, 0)
    row = lambda w: pl.BlockSpec((tm, w), lambda i: (i, 0))
    col = lambda w: pl.BlockSpec((w, tm), lambda i: (0, i))
    cst = lambda w: pl.BlockSpec((1, w), lambda i: (0, 0))
    sds = lambda w: jax.ShapeDtypeStruct((T_LAT, w), BF16)
    sds_t = lambda w: jax.ShapeDtypeStruct((w, T_LAT), BF16)
    return pl.pallas_call(
        _prep_kernel,
        grid=(T_LAT // tm,),
        in_specs=[
            pl.BlockSpec((tm, ATT_W), lambda i: (T_CTX // tm + i, Z_ATT // ATT_W)),
            cst(512), cst(128), cst(512), cst(512),
            pl.BlockSpec((tm, 128), rope_idx),
            pl.BlockSpec((tm, 128), rope_idx),
        ],
        out_specs=[col(512), row(128), col(128), row(512), row(512), row(512)],
        out_shape=[sds_t(512), sds(128), sds_t(128), sds(512), sds(512), sds(512)],
        compiler_params=_cparams(("arbitrary",)),
        name="attn_prep",
    )(z, wqb, wkb, wqc, wkc, cos_t, sin_t)


def _half_rows(t, half):
    z = jnp.zeros((HEAD_DIM, t.shape[1]), t.dtype)
    return jnp.concatenate([t[:HEAD_DIM], z] if half == 0 else [z, t[HEAD_DIM:]], axis=0)


def _ctx_attn_kernel(sink_ref, z_ref, wqb_ref, wkb_ref, wqc_ref, wkc_ref, *refs, layer):
    ob_ref, oc_ref = refs[-6:-4]
    cache_refs = refs[-4:]
    if len(refs) == 6:
        for ref in cache_refs:
            ref[...] = jnp.zeros(ref.shape, F32)
        cache_refs = [ref.at[layer] for ref in cache_refs]
    kbf_ref, vbf_ref, kcf_ref, vcf_ref = cache_refs
    vbf_ref[...] = z_ref[:, A_BV:A_BV + 128].astype(F32)
    vcf_ref[...] = z_ref[:, A_CV:A_CV + 512].astype(F32)
    n = z_ref.shape[0]
    qscale = HEAD_DIM ** -0.5

    def pair(col, p):
        return z_ref[:, col + 128 * p: col + 128 * (p + 1)]

    def q_t(col, w_ref, p):
        y = _pair_norm(pair(col, p).astype(F32), w_ref[:, 128 * p:128 * (p + 1)]) * qscale
        return y.T.astype(BF16)

    def v_t(col, p):
        return pair(col, p).astype(F32).T.astype(BF16)

    kb = _pair_norm(pair(A_BK, 0).astype(F32), wkb_ref[...])
    kbf_ref[...] = kb
    kb = kb.astype(BF16)
    qb_t = [q_t(A_BQ, wqb_ref, p) for p in range(4)]
    scores = []
    for kvh in range(SWA_KV_HEADS):
        placed = []
        for h in range(SWA_GROUP * kvh, SWA_GROUP * (kvh + 1)):
            rows = qb_t[h // 2][HEAD_DIM * (h % 2):HEAD_DIM * (h % 2 + 1)]
            z = jnp.zeros_like(rows)
            placed.append(jnp.concatenate([rows, z] if kvh == 0 else [z, rows], axis=0))
        scores.append(_dot(kb, jnp.concatenate(placed, axis=1)))
    for p in range(NA_HEADS // 2):
        kc = _pair_norm(pair(A_CK, p).astype(F32), wkc_ref[:, 128 * p:128 * (p + 1)])
        kcf_ref[:, 128 * p:128 * (p + 1)] = kc
        qc_t = q_t(A_CQ, wqc_ref, p)
        q2 = jnp.concatenate([_half_rows(qc_t, 0), _half_rows(qc_t, 1)], axis=1)
        scores.append(_dot(kc.astype(BF16), q2))
    s_t = jnp.concatenate(scores, axis=1)

    sink = jnp.concatenate(
        [jnp.full((1, n), sink_ref[layer, h], F32) for h in range(SWA_Q_HEADS)]
        + [jnp.full((1, n * NA_HEADS), -jnp.inf, F32)], axis=1)
    mx = jnp.maximum(s_t.max(axis=0, keepdims=True), sink)
    p_t = jnp.exp(s_t - mx)
    inv = 1.0 / (p_t.sum(axis=0, keepdims=True) + jnp.exp(sink - mx))
    p_t = p_t.astype(BF16)

    nb = SWA_Q_HEADS * n
    o = _dot(v_t(A_BV, 0), p_t[:, :nb]) * inv[:, :nb]
    outs = [o[HEAD_DIM * (h // SWA_GROUP):HEAD_DIM * (h // SWA_GROUP + 1), n * h:n * (h + 1)]
            for h in range(SWA_Q_HEADS)]
    ob_ref[...] = jnp.concatenate(outs, axis=0).T.astype(BF16)
    outs = []
    for p in range(NA_HEADS // 2):
        cols = slice(nb + 2 * n * p, nb + 2 * n * (p + 1))
        o = _dot(v_t(A_CV, p), p_t[:, cols]) * inv[:, cols]
        outs += [o[:HEAD_DIM, :n], o[HEAD_DIM:, n:]]
    oc_ref[...] = jnp.concatenate(outs, axis=0).T.astype(BF16)


def _ctx_attention(sink, z, wqb, wkb, wqc, wkc, caches, l):
    blk = lambda w, j=0: pl.BlockSpec((CTX_LEN, w), lambda b: (b, j))
    cst = lambda w: pl.BlockSpec((1, w), lambda b: (0, 0))
    sds = lambda w: jax.ShapeDtypeStruct((T_CTX, w), BF16)
    cache_w = (128, 128, 512, 512)
    if caches is None:
        cache_blk = [pl.BlockSpec((None, DEPTH, CTX_LEN, w), lambda b: (b, 0, 0, 0)) for w in cache_w]
    else:
        cache_blk = [pl.BlockSpec((None, None, CTX_LEN, w), lambda b: (b, l, 0, 0)) for w in cache_w]
    cache_sds = [jax.ShapeDtypeStruct((N_CTX_SEQ, DEPTH, CTX_LEN, w), F32) for w in cache_w]
    carried = [] if caches is None else list(caches)
    n_in = 6
    return pl.pallas_call(
        functools.partial(_ctx_attn_kernel, layer=l),
        grid=(N_CTX_SEQ,),
        in_specs=[pl.BlockSpec(memory_space=pltpu.SMEM), blk(ATT_W, Z_ATT // ATT_W),
                  cst(512), cst(128), cst(512), cst(512)]
                 + [pl.BlockSpec(memory_space=pl.ANY)] * len(carried),
        out_specs=[blk(512), blk(512)] + cache_blk,
        out_shape=[sds(512), sds(512)] + cache_sds,
        input_output_aliases={n_in + j: 2 + j for j in range(len(carried))},
        compiler_params=_cparams(("arbitrary",)),
        name="ctx_attention",
    )(sink, z, wqb, wkb, wqc, wkc, *carried)


SWA_QBLK = 128
SWA_SPAN = SWA_QBLK + 2 * SWA_WINDOW


def _swa_kernel(sink_ref, qt_ref, k_ref, vt_ref, kc_ref, vc_ref, o_ref, kctx_scr, vctx_scr, *, layer):
    i = pl.program_id(1)
    nq = SWA_QBLK

    @pl.when(i == 0)
    def _():
        kctx_scr[...] = kc_ref[...].T.astype(BF16)
        vctx_scr[...] = vc_ref[...].astype(BF16)

    start = pl.multiple_of(jnp.clip(nq * (i - 1), 0, LAT_LEN - SWA_SPAN), nq)
    kwin = k_ref[pl.ds(start, SWA_SPAN), :]
    vwin_t = vt_ref[:, pl.ds(start, SWA_SPAN)]
    kctx = kctx_scr[...]
    s_lat, s_ctx = [], []
    for kvh in range(SWA_KV_HEADS):
        placed = []
        for h in range(SWA_GROUP * kvh, SWA_GROUP * (kvh + 1)):
            rows = qt_ref[HEAD_DIM * h:HEAD_DIM * (h + 1), :]
            z = jnp.zeros_like(rows)
            placed.append(jnp.concatenate([rows, z] if kvh == 0 else [z, rows], axis=0))
        q4 = jnp.concatenate(placed, axis=1)
        s_lat.append(_dot(kwin, q4))
        s_ctx.append(_dot(kctx, q4))
    s_lat = jnp.concatenate(s_lat, axis=1)
    s_ctx = jnp.concatenate(s_ctx, axis=1)

    kpos = start + lax.broadcasted_iota(jnp.int32, (SWA_SPAN, nq), 0)
    qpos = nq * i + lax.broadcasted_iota(jnp.int32, (SWA_SPAN, nq), 1)
    band = jnp.where(jnp.abs(kpos - qpos) <= SWA_WINDOW, 0.0, -jnp.inf)
    s_lat = s_lat + jnp.concatenate([band] * SWA_Q_HEADS, axis=1)
    sink = jnp.concatenate(
        [jnp.full((1, nq), sink_ref[layer, h], F32) for h in range(SWA_Q_HEADS)], axis=1)
    mx = jnp.maximum(jnp.maximum(s_lat.max(axis=0, keepdims=True),
                                 s_ctx.max(axis=0, keepdims=True)), sink)
    p_lat = jnp.exp(s_lat - mx)
    p_ctx = jnp.exp(s_ctx - mx)
    inv = 1.0 / (p_lat.sum(axis=0, keepdims=True) + p_ctx.sum(axis=0, keepdims=True)
                 + jnp.exp(sink - mx))
    o = (_dot(vwin_t, p_lat.astype(BF16)) + _dot(vctx_scr[...], p_ctx.astype(BF16))) * inv
    outs = [o[HEAD_DIM * (h // SWA_GROUP):HEAD_DIM * (h // SWA_GROUP + 1), nq * h:nq * (h + 1)]
            for h in range(SWA_Q_HEADS)]
    o_ref[...] = jnp.concatenate(outs, axis=0).T.astype(BF16)


def _swa_attention(sink, qb_t, kb, vb_t, cache_k, cache_v, l):
    nq = LAT_LEN // SWA_QBLK
    cache = pl.BlockSpec((None, None, 128, PAST_LEN), lambda b, i: (b, l, 0, 0))
    return pl.pallas_call(
        functools.partial(_swa_kernel, layer=l),
        grid=(N_LAT_SEQ, nq),
        in_specs=[pl.BlockSpec(memory_space=pltpu.SMEM),
                  pl.BlockSpec((512, SWA_QBLK), lambda b, i: (0, b * nq + i)),
                  pl.BlockSpec((LAT_LEN, 128), lambda b, i: (b, 0)),
                  pl.BlockSpec((128, LAT_LEN), lambda b, i: (0, b)),
                  cache, cache],
        out_specs=pl.BlockSpec((SWA_QBLK, 512), lambda b, i: (b * nq + i, 0)),
        out_shape=jax.ShapeDtypeStruct((T_LAT, 512), BF16),
        scratch_shapes=[pltpu.VMEM((PAST_LEN, 128), BF16), pltpu.VMEM((128, PAST_LEN), BF16)],
        compiler_params=_cparams(("arbitrary", "arbitrary")),
        name="swa_attention",
    )(sink, qb_t, kb, vb_t, cache_k, cache_v)


LAT_ROWS = LAT_LEN // GRID_W
NA_WIN_ROWS = min(NA_ROWS, LAT_ROWS)
NA_KEYS = NA_WIN_ROWS * GRID_W


def _na_kernel(q_ref, k_ref, v_ref, kc_ref, vc_ref, rp_ref, o_ref, kctx_scr, vctx_scr, bias_ref):
    r = pl.program_id(1)
    low = lax.broadcasted_iota(jnp.int32, (1, 128), 1) < HEAD_DIM

    @pl.when(r == 0)
    def _():
        kctx_scr[...] = kc_ref[...].astype(BF16)
        vctx_scr[...] = vc_ref[...].astype(BF16)
        cq = lax.broadcasted_iota(jnp.int32, (GRID_W, 128), 0)
        ck = lax.broadcasted_iota(jnp.int32, (GRID_W, 128), 1) % GRID_W
        cs = jnp.clip(cq - NA_COLS // 2, 0, GRID_W - NA_COLS)
        window = jnp.where((ck >= cs) & (ck < cs + NA_COLS), 0.0, -jnp.inf)
        for h in range(NA_HEADS):
            rows = [jnp.broadcast_to(rp_ref[h, dr:dr + 1, :], (GRID_W, 128))
                    for dr in range(2 * NA_ROWS - 1)]
            left = [pltpu.roll(x, 0, 1, stride=1, stride_axis=0) for x in rows[:-1]]
            right = [pltpu.roll(x, GRID_W, 1, stride=1, stride_axis=0) for x in rows[1:]]
            for j in range(2 * NA_ROWS - 2):
                bias_ref[h, j] = jnp.where(low, left[j], right[j]) + window

    rs = jnp.clip(r - NA_ROWS // 2, 0, LAT_ROWS - NA_WIN_ROWS)
    start = pl.multiple_of(rs * GRID_W, GRID_W)
    base = rs - r + NA_ROWS - 1
    npair = NA_HEADS // 2
    s_lat, s_ctx = [], []
    for p in range(npair):
        sl = slice(128 * p, 128 * (p + 1))
        q = q_ref[:, sl]
        zero = jnp.zeros_like(q)
        q2 = jnp.concatenate([jnp.where(low, q, zero), jnp.where(low, zero, q)], axis=0)
        bias = jnp.concatenate(
            [jnp.concatenate([bias_ref[2 * p + hh, base + 2 * w] for w in range(NA_WIN_ROWS // 2)],
                             axis=1) for hh in range(2)], axis=0)
        s_lat.append(_nt_dot(q2, k_ref[pl.ds(start, NA_KEYS), sl]) + bias)
        s_ctx.append(_dot(q2, kctx_scr[sl, :]))
    s_lat = jnp.concatenate(s_lat, axis=0)
    s_ctx = jnp.concatenate(s_ctx, axis=0)
    mx = jnp.maximum(s_lat.max(axis=-1, keepdims=True), s_ctx.max(axis=-1, keepdims=True))
    p_lat = jnp.exp(s_lat - mx)
    p_ctx = jnp.exp(s_ctx - mx)
    inv = 1.0 / (p_lat.sum(axis=-1, keepdims=True) + p_ctx.sum(axis=-1, keepdims=True))
    p_lat = p_lat.astype(BF16)
    p_ctx = p_ctx.astype(BF16)
    tiles = []
    for p in range(npair):
        sl = slice(128 * p, 128 * (p + 1))
        rows = slice(2 * GRID_W * p, 2 * GRID_W * (p + 1))
        o2 = (_dot(p_lat[rows], v_ref[pl.ds(start, NA_KEYS), sl])
              + _nt_dot(p_ctx[rows], vctx_scr[sl, :])) * inv[rows]
        tiles.append(jnp.where(low, o2[:GRID_W], o2[GRID_W:]))
    o_ref[...] = jnp.concatenate(tiles, axis=1).astype(BF16)


def _na_attention(qc, kc, vc, cache_k, cache_v, rp_cyc, l):
    kv = pl.BlockSpec((LAT_LEN, 512), lambda b, r: (b, 0))
    cache = pl.BlockSpec((None, None, 512, PAST_LEN), lambda b, r: (b, l, 0, 0))
    return pl.pallas_call(
        _na_kernel,
        grid=(N_LAT_SEQ, LAT_ROWS),
        in_specs=[pl.BlockSpec((GRID_W, 512), lambda b, r: (b * LAT_ROWS + r, 0)),
                  kv, kv, cache, cache,
                  pl.BlockSpec((None, NA_HEADS, 2 * NA_ROWS - 1, 128), lambda b, r: (l, 0, 0, 0))],
        out_specs=pl.BlockSpec((GRID_W, 512), lambda b, r: (b * LAT_ROWS + r, 0)),
        out_shape=jax.ShapeDtypeStruct((T_LAT, 512), BF16),
        scratch_shapes=[pltpu.VMEM((512, PAST_LEN), BF16), pltpu.VMEM((512, PAST_LEN), BF16),
                        pltpu.VMEM((NA_HEADS, 2 * NA_ROWS - 2, GRID_W, 128), F32)],
        compiler_params=_cparams(("arbitrary", "arbitrary")),
        name="na_attention",
    )(qc, kc, vc, cache_k, cache_v, rp_cyc)


def _na_table_rows(rpb):
    pad = jnp.zeros(rpb.shape[:-1] + (128 - (2 * NA_COLS - 1),), F32)
    return jnp.concatenate([rpb[..., NA_COLS - 1:], pad, rpb[..., :NA_COLS - 1]], axis=-1)


def _gla_constants():
    c = GLA_C
    t = np.arange(c)[:, None]
    u = np.arange(c)[None, :]
    blocks = []
    for k in range(GLA_LEVELS):
        b = 1 << k
        m = ((t >> k) | 1) * b - 1
        query = ((t >> k) & 1) == 1
        blocks.append(np.where(query, (u > m) & (u <= t), (u > t) & (u <= m)))
    blocks.append(u <= t)
    blocks.append(u > t)
    blocks.append(np.ones((16, c), bool))
    fwd = np.concatenate(blocks, axis=0).astype(np.float32)
    bwd_blocks = [blk[::-1, ::-1] for blk in blocks]
    bwd = np.concatenate(bwd_blocks, axis=0).astype(np.float32)
    seg = np.stack([fwd, bwd])
    seg = np.concatenate([seg, seg], axis=-1)

    s = np.arange(c)[None, :]
    x = t ^ s
    lev = np.where(x == 0, GLA_LEVELS, np.floor(np.log2(np.maximum(x, 1))).astype(np.int64))
    lev_f = np.where(s <= t, lev, -1)
    lev_b = np.where(s >= t, lev, -1)
    levmap = np.stack([np.tile(lev_f, (1, GLA_HEADS)), np.tile(lev_b, (1, GLA_HEADS))])
    return seg, levmap.astype(np.int32)


def _gla_state_init(d, cc, s0_ref, st_ref):
    c = GLA_C
    ctx_chunks = T_CTX // c
    is_lat = cc >= ctx_chunks
    per_seq = jnp.where(is_lat, LAT_LEN // c, CTX_LEN // c)
    pos = jnp.where(is_lat, cc - ctx_chunks, cc) % per_seq
    first = (pos == 0) if d == 0 else (pos == per_seq - 1)

    @pl.when(first)
    def _():
        for h in range(GLA_HEADS):
            blk = jnp.where(is_lat, s0_ref[h], 0.0)
            row = [jnp.zeros((GLA_DV, GLA_DK), F32)] * GLA_HEADS
            row[h] = blk
            st_ref[GLA_DV * h:GLA_DV * (h + 1), :] = jnp.concatenate(row, axis=1)


def _gla_chunks(chains):
    c = GLA_C
    lane_head = lax.broadcasted_iota(jnp.int32, (1, GLA_HEADS * GLA_DK), 1) // GLA_DK
    row_head = lax.broadcasted_iota(jnp.int32, (GLA_HEADS * GLA_DV, 1), 0) // GLA_DV
    head_sel = [jnp.where(lane_head == h, 1.0, 0.0).astype(BF16) for h in range(GLA_HEADS)]

    def pair_scores(qs, ks):
        kbd = jnp.concatenate([ks * head_sel[h] for h in range(GLA_HEADS)], axis=0)
        return _nt_dot(qs, kbd)

    segs = []
    for (_, _, _, lr_ref, w2_ref, ba_ref, seg_ref, *_) in chains:
        x = _dot(lr_ref[...], w2_ref[...]) + ba_ref[...]
        la = (jnp.minimum(x, 0.0) - jnp.log(1.0 + jnp.exp(-jnp.abs(x)))) * (1.0 / GLA_TAU)
        la_hi = la.astype(BF16)
        la_lo = (la - la_hi.astype(F32)).astype(BF16)
        segs.append(_dot(seg_ref[...], jnp.concatenate([la_hi, la_lo], axis=0)))

    def decay(seg, block):
        return jnp.exp(seg[c * block:c * (block + 1)]).astype(BF16)

    qs = [ch[0][...] * (GLA_DK ** -0.5) for ch in chains]
    ks = [ch[1][...] for ch in chains]
    attn = [jnp.where(ch[7][...] == GLA_LEVELS, pair_scores(q, k), 0.0)
            for ch, q, k in zip(chains, qs, ks)]
    for lvl in range(GLA_LEVELS):
        for n, ch in enumerate(chains):
            e = decay(segs[n], lvl)
            attn[n] = jnp.where(ch[7][...] == lvl, pair_scores(qs[n] * e, ks[n] * e), attn[n])

    for n, (_, _, v_ref, _, _, _, _, _, o_ref, sfin_ref, st_ref) in enumerate(chains):
        seg = segs[n]
        v = v_ref[...]
        a = attn[n].astype(BF16)
        tot = seg[c * (GLA_LEVELS + 2):c * (GLA_LEVELS + 2) + 1]
        q_in = qs[n] * decay(seg, GLA_LEVELS)
        k_in = ks[n] * decay(seg, GLA_LEVELS + 1)
        state = st_ref[...]
        o_inter = _nt_dot(q_in, state.astype(BF16))
        for h in range(GLA_HEADS):
            sl = slice(GLA_DV * h, GLA_DV * (h + 1))
            o_ref[:, sl] = o_inter[:, sl] + _dot(a[:, c * h:c * (h + 1)], v[:, sl])
        upd = _tn_dot(v, k_in)
        new_state = state * jnp.exp(tot) + jnp.where(row_head == lane_head, upd, 0.0)
        st_ref[...] = new_state
        for h in range(GLA_HEADS):
            sfin_ref[h] = new_state[GLA_DV * h:GLA_DV * (h + 1), GLA_DK * h:GLA_DK * (h + 1)]


def _gla_kernel(qf, kf, vf, lrf, qb, kb, vb, lrb, w2_ref, ba_ref, seg_ref, lev_ref, s0f, s0b,
                of_ref, ob_ref, sff_ref, sfb_ref, stf, stb):
    step = pl.program_id(0)
    last = pl.num_programs(0) - 1
    _gla_state_init(0, step, s0f, stf)
    _gla_state_init(1, last - step, s0b, stb)
    _gla_chunks([
        (qf, kf, vf, lrf, w2_ref.at[0], ba_ref.at[0], seg_ref.at[0], lev_ref.at[0], of_ref, sff_ref, stf),
        (qb, kb, vb, lrb, w2_ref.at[1], ba_ref.at[1], seg_ref.at[1], lev_ref.at[1], ob_ref, sfb_ref, stb),
    ])


def _gla(z, w2p, ba, seg, levmap, s0t, l):
    c = GLA_C
    nchunk = T_ALL // c
    ctx_chunks = T_CTX // c
    nseq = N_CTX_SEQ + N_LAT_SEQ
    fwd = lambda s: s
    bwd = lambda s: nchunk - 1 - s

    def seq_of(cc):
        return jnp.where(cc < ctx_chunks, cc // (CTX_LEN // c),
                         N_CTX_SEQ + (cc - ctx_chunks) // (LAT_LEN // c))

    def lat_of(cc):
        return jnp.clip((cc - ctx_chunks) // (LAT_LEN // c), 0, N_LAT_SEQ - 1)

    def chunk_specs(pos):
        return [pl.BlockSpec((c, 256), lambda s: (pos(s), Z_AQ // 256)),
                pl.BlockSpec((c, 256), lambda s: (pos(s), Z_AK // 256)),
                pl.BlockSpec((c, 512), lambda s: (pos(s), Z_AV // 512)),
                pl.BlockSpec((c, 128), lambda s: (pos(s), Z_LR // 128))]

    def s0_spec(d, pos):
        return pl.BlockSpec((None, None, None, GLA_HEADS, GLA_DV, GLA_DK),
                            lambda s: (lat_of(pos(s)), l, d, 0, 0, 0))

    def sfin_spec(pos):
        return pl.BlockSpec((None, GLA_HEADS, GLA_DV, GLA_DK), lambda s: (seq_of(pos(s)), 0, 0, 0))

    state_sds = jax.ShapeDtypeStruct((nseq, GLA_HEADS, GLA_DV, GLA_DK), F32)
    out_sds = jax.ShapeDtypeStruct((T_ALL, 512), F32)
    state_scr = pltpu.VMEM((GLA_HEADS * GLA_DV, GLA_HEADS * GLA_DK), F32)
    return pl.pallas_call(
        _gla_kernel,
        grid=(nchunk,),
        in_specs=chunk_specs(fwd) + chunk_specs(bwd) + [
            pl.BlockSpec((None, 2, 128, 256), lambda s: (l, 0, 0, 0)),
            pl.BlockSpec((None, 2, 1, 256), lambda s: (l, 0, 0, 0)),
            pl.BlockSpec((2, GLA_GROWS, 2 * c), lambda s: (0, 0, 0)),
            pl.BlockSpec((2, c, GLA_HEADS * c), lambda s: (0, 0, 0)),
            s0_spec(0, fwd), s0_spec(1, bwd),
        ],
        out_specs=[pl.BlockSpec((c, 512), lambda s: (fwd(s), 0)),
                   pl.BlockSpec((c, 512), lambda s: (bwd(s), 0)),
                   sfin_spec(fwd), sfin_spec(bwd)],
        out_shape=[out_sds, out_sds, state_sds, state_sds],
        scratch_shapes=[state_scr, state_scr],
        compiler_params=_cparams(("arbitrary",)),
        name="gla",
    )(z, z, z, z, z, z, z, z, w2p, ba, seg, levmap, s0t, s0t)


def _cast_kernel(*refs):
    n = len(refs) // 2
    for src, dst in zip(refs[:n], refs[n:]):
        dst[...] = src[...].astype(dst.dtype)


def _cast_bf16(*weights):
    spec = lambda w: pl.BlockSpec((None,) + w.shape[1:], lambda l: (l, 0, 0))
    return pl.pallas_call(
        _cast_kernel,
        grid=(DEPTH,),
        in_specs=[spec(w) for w in weights],
        out_specs=[spec(w) for w in weights],
        out_shape=[jax.ShapeDtypeStruct(w.shape, BF16) for w in weights],
        compiler_params=_cparams(("arbitrary",)),
        name="cast_bf16",
    )(*weights)


def _post_mixer_kernel(x_ref, of_ref, obk_ref, ar_ref, gates_ref, bc_ref, bl_ref, cc_ref, cl_ref,
                       wpa_ref, wpb_ref, wpc_ref, wo_ref, gn_ref, g2_ref, mod_ref, w1_ref, w2_ref,
                       y_ref, w1_scr, w2_scr, h_scr, x1_scr, acc_scr, *, nk):
    s = pl.program_id(0)
    d = D_MODEL
    tm = x_ref.shape[0]
    th = w1_ref.shape[1]
    ctx = jnp.maximum(s - (nk - 1), 0) < T_CTX // tm

    def merged_residual():
        o = of_ref[...] + obk_ref[...]
        heads = []
        for h in range(GLA_HEADS):
            oh = o[:, GLA_DV * h:GLA_DV * (h + 1)]
            ms = jnp.mean(oh * oh, axis=-1, keepdims=True)
            heads.append(oh * lax.rsqrt(ms + EPS) * gn_ref[...])
        oa = (jnp.concatenate(heads, axis=1) * _silu(ar_ref[...].astype(F32))).astype(BF16)
        ob = jnp.where(ctx, bc_ref[...], bl_ref[...])
        oc = jnp.where(ctx, cc_ref[...], cl_ref[...])
        gate = lambda j: _sigmoid(gates_ref[:, j * d:(j + 1) * d].astype(F32))
        merged = (gate(0) * _dot(oa, wpa_ref[...]) + gate(1) * _dot(ob, wpb_ref[...])
                  + gate(2) * _dot(oc, wpc_ref[...]))
        return x_ref[...] + mod_ref[:, 2 * d:3 * d] * _dot(merged.astype(BF16), wo_ref[...])

    def normed(x1):
        return _norm_mod(x1, g2_ref[...], mod_ref[:, 3 * d:4 * d], mod_ref[:, 4 * d:5 * d]).astype(BF16)

    def act(u):
        return jnp.square(jnp.maximum(u, 0.0)).astype(BF16)

    @pl.when(s == 0)
    def _():
        x1 = merged_residual()
        x1_scr[...] = x1
        h_scr[...] = normed(x1)

    @pl.when(s < nk)
    def _():
        col = pl.multiple_of(s * th, th)
        w1 = w1_ref[...].astype(BF16)
        w2 = w2_ref[...].astype(BF16)
        w1_scr[:, pl.ds(col, th)] = w1
        w2_scr[pl.ds(col, th), :] = w2
        part = _dot(act(_dot(h_scr[...], w1)), w2)

        @pl.when(s == 0)
        def _():
            acc_scr[...] = part

        @pl.when(s > 0)
        def _():
            acc_scr[...] += part

        @pl.when(s == nk - 1)
        def _():
            y_ref[...] = x1_scr[...] + mod_ref[:, 5 * d:6 * d] * acc_scr[...]

    @pl.when(s >= nk)
    def _():
        x1 = merged_residual()
        out = _dot(act(_dot(normed(x1), w1_scr[...])), w2_scr[...])
        y_ref[...] = x1 + mod_ref[:, 5 * d:6 * d] * out


def _post_mixer(x, o_fwd, o_bwd, z, ob_ctx, ob_lat, oc_ctx, oc_lat, wpa, wpb, wpc, wo, gn, norm2_w,
                mod4, w1, w2, l):
    tm, th = 256, 512
    nk = MLP_HIDDEN // th
    ctx_blocks = T_CTX // tm
    row = lambda s: jnp.maximum(s - (nk - 1), 0)
    chunk = lambda s: jnp.minimum(s, nk - 1)
    ctx_idx = lambda s: (jnp.minimum(row(s), ctx_blocks - 1), 0)
    lat_idx = lambda s: (jnp.maximum(row(s) - ctx_blocks, 0), 0)
    rows = lambda w, j=0: pl.BlockSpec((tm, w), lambda s: (row(s), j))
    resident = lambda k: pl.BlockSpec((None, k, D_MODEL), lambda s: (l, 0, 0),
                                      pipeline_mode=pl.Buffered(1))
    return pl.pallas_call(
        functools.partial(_post_mixer_kernel, nk=nk),
        grid=(nk - 1 + T_ALL // tm,),
        in_specs=[
            rows(D_MODEL), rows(512), rows(512), rows(512, Z_AR // 512), rows(3 * D_MODEL),
            pl.BlockSpec((tm, 512), ctx_idx), pl.BlockSpec((tm, 512), lat_idx),
            pl.BlockSpec((tm, 512), ctx_idx), pl.BlockSpec((tm, 512), lat_idx),
            resident(512), resident(512), resident(512), resident(D_MODEL),
            pl.BlockSpec((None, 1, GLA_DV), lambda s: (l, 0, 0)),
            pl.BlockSpec((None, 1, D_MODEL), lambda s: (l, 0, 0)),
            pl.BlockSpec((None, None, 1, 6 * D_MODEL),
                         lambda s: (l, _group_of_rows(row(s), tm), 0, 0)),
            pl.BlockSpec((None, D_MODEL, th), lambda s: (l, 0, chunk(s))),
            pl.BlockSpec((None, th, D_MODEL), lambda s: (l, chunk(s), 0)),
        ],
        out_specs=rows(D_MODEL),
        out_shape=jax.ShapeDtypeStruct((T_ALL, D_MODEL), F32),
        scratch_shapes=[pltpu.VMEM((D_MODEL, MLP_HIDDEN), BF16), pltpu.VMEM((MLP_HIDDEN, D_MODEL), BF16),
                        pltpu.VMEM((tm, D_MODEL), BF16), pltpu.VMEM((tm, D_MODEL), F32),
                        pltpu.VMEM((tm, D_MODEL), F32)],
        compiler_params=_cparams(("arbitrary",)),
        name="post_mixer",
    )(x, o_fwd, o_bwd, z, z, ob_ctx, ob_lat, oc_ctx, oc_lat, wpa, wpb, wpc, wo, gn, norm2_w, mod4, w1, w2)


def _rope_tables():
    t = jnp.arange(LAT_LEN)
    row = (t // GRID_W).astype(F32)
    col = (t % GRID_W).astype(F32)
    nf = HEAD_DIM // 4
    inv_freq = ROPE_BASE ** (-jnp.arange(nf, dtype=F32) / nf)
    ang_r = row[:, None] * inv_freq[None, :]
    ang_c = col[:, None] * inv_freq[None, :]
    cos = jnp.concatenate([jnp.cos(ang_r)] * 2 + [jnp.cos(ang_c)] * 2, axis=1)
    sin = jnp.concatenate([-jnp.sin(ang_r), jnp.sin(ang_r), -jnp.sin(ang_c), jnp.sin(ang_c)], axis=1)
    return jnp.tile(cos, (1, 2)), jnp.tile(sin, (1, 2))


def kernel(x_prompt, x_sample, state_gla, cache_swa_k, cache_swa_v, cache_na_k, cache_na_v, c,
           c_ctx, w_mod, b_mod, norm1, norm2, w_in, w_a2_f, b_a_f, w_a2_b, b_a_b, gla_onorm,
           qn_swa, kn_swa, sink_swa, qn_na, kn_na, rpb_na, w_pa, w_pb, w_pc, w_o, w_fc1, w_fc2):
    d = D_MODEL
    x = jnp.concatenate([x_prompt.reshape(T_CTX, d), x_sample.reshape(T_LAT, d)], axis=0)

    cond8 = jnp.zeros((8, d), F32).at[0].set(c_ctx).at[1:1 + N_LAT_SEQ].set(c)
    mod4 = _modulation(cond8, w_mod, b_mod).reshape(DEPTH, 8, 1, 6 * d)

    w_in_p = _w_in_layout(jnp.swapaxes(w_in, 1, 2))
    wpa, wpb, wpc, wo = _cast_bf16(w_pa, w_pb, w_pc, w_o)
    norm1r = norm1.reshape(DEPTH, 1, d)
    norm2r = norm2.reshape(DEPTH, 1, d)
    gnr = gla_onorm.reshape(DEPTH, 1, GLA_DV)
    w2p = jnp.zeros((DEPTH, 2, 128, GLA_HEADS * GLA_DK), F32)
    w2p = w2p.at[:, 0, 0:GLA_LOWRANK].set(w_a2_f).at[:, 1, GLA_LOWRANK:2 * GLA_LOWRANK].set(w_a2_b)
    w2p = w2p.astype(BF16)
    ba = jnp.stack([b_a_f, b_a_b], axis=1).reshape(DEPTH, 2, 1, GLA_HEADS * GLA_DK)
    seg_np, lev_np = _gla_constants()
    seg = jnp.asarray(seg_np, BF16)
    levmap = jnp.asarray(lev_np)
    s0t = jnp.swapaxes(state_gla, -1, -2)
    cos_t, sin_t = _rope_tables()
    rp_cyc = _na_table_rows(rpb_na)
    feat_major = lambda a: a.transpose(0, 1, 3, 4, 2).reshape(N_LAT_SEQ, DEPTH, -1, PAST_LEN)
    csk, csv, cnk, cnv = (feat_major(a) for a in (cache_swa_k, cache_swa_v, cache_na_k, cache_na_v))

    st_l = []
    caches = None
    for l in range(DEPTH):
        z = _inproj(x, norm1r, mod4, w_in_p, l)
        wqb = jnp.tile(qn_swa[l], 8)[None, :]
        wkb = jnp.tile(kn_swa[l], 2)[None, :]
        wqc = jnp.tile(qn_na[l], 8)[None, :]
        wkc = jnp.tile(kn_na[l], 8)[None, :]
        qb, kb, vb, qc, kc, vc = _attn_prep(z, wqb, wkb, wqc, wkc, cos_t, sin_t)
        o_fwd, o_bwd, sfin_f, sfin_b = _gla(z, w2p, ba, seg, levmap, s0t, l)
        ob_ctx, oc_ctx, *caches = _ctx_attention(sink_swa, z, wqb, wkb, wqc, wkc, caches, l)
        ob_lat = _swa_attention(sink_swa, qb, kb, vb, csk, csv, l)
        oc_lat = _na_attention(qc, kc, vc, cnk, cnv, rp_cyc, l)
        x = _post_mixer(x, o_fwd, o_bwd, z, ob_ctx, ob_lat, oc_ctx, oc_lat, wpa, wpb, wpc, wo, gnr,
                        norm2r, mod4, w_fc1, w_fc2, l)

        sfin = jnp.stack([sfin_f[:N_CTX_SEQ], sfin_b[:N_CTX_SEQ]], axis=0)
        st_l.append(jnp.swapaxes(sfin, -1, -2))

    y_prompt = x[:T_CTX].reshape(N_CTX_SEQ, CTX_LEN, d)
    y_sample = x[T_CTX:].reshape(N_LAT_SEQ, LAT_LEN, d)
    new_state = jnp.stack(st_l, axis=0).transpose(2, 0, 1, 3, 4, 5)

    swa_k, swa_v, na_k, na_v = caches
    kv_shape = lambda heads: (N_CTX_SEQ, DEPTH, CTX_LEN, heads, HEAD_DIM)
    return (y_prompt, y_sample, new_state,
            swa_k.reshape(kv_shape(SWA_KV_HEADS)), swa_v.reshape(kv_shape(SWA_KV_HEADS)),
            na_k.reshape(kv_shape(NA_HEADS)), na_v.reshape(kv_shape(NA_HEADS)))
```

```python
import functools

import numpy as np
import jax
import jax.numpy as jnp
from jax import lax
from jax.experimental import pallas as pl
from jax.experimental.pallas import tpu as pltpu

F32 = jnp.float32
BF16 = jnp.bfloat16

D_MODEL = 1024
DEPTH = 4
N_CTX_SEQ = 16
CTX_LEN = 256
N_LAT_SEQ = 2
LAT_LEN = 1024
PAST_LEN = 512
T_CTX = N_CTX_SEQ * CTX_LEN
T_LAT = N_LAT_SEQ * LAT_LEN
T_ALL = T_CTX + T_LAT
GRID_W = 64
HEAD_DIM = 64
GLA_HEADS = 4
GLA_DK = 64
GLA_DV = 128
GLA_LOWRANK = 16
GLA_TAU = 16.0
SWA_Q_HEADS = 8
SWA_KV_HEADS = 2
SWA_GROUP = 4
SWA_WINDOW = 128
NA_HEADS = 8
NA_ROWS = 8
NA_COLS = 16
MLP_HIDDEN = 4 * D_MODEL
ROPE_BASE = 10000.0
EPS = 1e-6
LOG2E = 1.4426950408889634
QK_SCALE = HEAD_DIM ** -0.5 * LOG2E

Z_GATES = 0
Z_GLA = 3072
Z_AQ, Z_AK, Z_AV, Z_AR = 3072, 3328, 3584, 4096
Z_ATT = 4608
ATT_W = 2304
A_BQ, A_BK, A_BV, A_CQ, A_CK, A_CV = 0, 512, 640, 768, 1280, 1792
Z_LR = 6912
Z_W = 7168

GLA_C = 128
GLA_LEVELS = 7
GLA_GROWS = (GLA_LEVELS + 2) * GLA_C + 16

VMEM_LIMIT = 56 * 1024 * 1024


def _cparams(sem):
    return pltpu.CompilerParams(dimension_semantics=sem, vmem_limit_bytes=VMEM_LIMIT)


def _sigmoid(x):
    return 1.0 / (1.0 + jnp.exp(-x))


def _silu(x):
    return x * _sigmoid(x)


def _nt_dot(a, b):
    return lax.dot_general(a, b, (((1,), (1,)), ((), ())), preferred_element_type=F32)


def _tn_dot(a, b):
    return lax.dot_general(a, b, (((0,), (0,)), ((), ())), preferred_element_type=F32)


def _dot(a, b):
    return jnp.dot(a, b, preferred_element_type=F32)


def _mod_kernel(cond_ref, w_ref, b_ref, o_ref):
    s = _silu(cond_ref[...]).astype(BF16)
    o_ref[...] = _dot(s, w_ref[...].astype(BF16)) + b_ref[...]


def _modulation(cond8, w_mod, b_mod):
    tn = 1024
    return pl.pallas_call(
        _mod_kernel,
        grid=(DEPTH, 6 * D_MODEL // tn),
        in_specs=[
            pl.BlockSpec((8, D_MODEL), lambda l, j: (0, 0)),
            pl.BlockSpec((None, D_MODEL, tn), lambda l, j: (l, 0, j)),
            pl.BlockSpec((None, 1, tn), lambda l, j: (l, 0, j)),
        ],
        out_specs=pl.BlockSpec((None, 8, tn), lambda l, j: (l, 0, j)),
        out_shape=jax.ShapeDtypeStruct((DEPTH, 8, 6 * D_MODEL), F32),
        compiler_params=_cparams(("arbitrary", "arbitrary")),
        name="modulation",
    )(cond8, w_mod, b_mod.reshape(DEPTH, 1, 6 * D_MODEL))


def _group_of_rows(row_block, rows_per_block):
    first = row_block * rows_per_block
    return jnp.maximum(first - T_CTX + LAT_LEN, 0) // LAT_LEN


def _norm_mod(x, g, shift, scale):
    ms = jnp.mean(x * x, axis=-1, keepdims=True)
    y = x * lax.rsqrt(ms + EPS) * g
    return y * (1.0 + scale) + shift


W_IN_SEGMENTS = ((3872, 6944), (0, 1536), (1568, 3872), (1536, 1568))
W_IN_COLS = 6944


def _w_in_layout_kernel(w_ref, o_ref):
    dst = 0
    for lo, hi in W_IN_SEGMENTS:
        o_ref[dst:dst + hi - lo, :] = w_ref[lo:hi, :].astype(BF16)
        dst += hi - lo
    o_ref[dst:, :] = jnp.zeros((Z_W - dst, o_ref.shape[1]), BF16)


def _w_in_layout(w_in_t):
    tk = 256
    return pl.pallas_call(
        _w_in_layout_kernel,
        grid=(DEPTH, D_MODEL // tk),
        in_specs=[pl.BlockSpec((None, W_IN_COLS, tk), lambda l, r: (l, 0, r))],
        out_specs=pl.BlockSpec((None, Z_W, tk), lambda l, r: (l, 0, r)),
        out_shape=jax.ShapeDtypeStruct((DEPTH, Z_W, D_MODEL), BF16),
        compiler_params=_cparams(("arbitrary", "arbitrary")),
        name="w_in_layout",
    )(w_in_t)


def _inproj_kernel(x_ref, g_ref, mod_ref, w_ref, z_ref):
    h = _norm_mod(x_ref[...], g_ref[...], mod_ref[:, 0:D_MODEL], mod_ref[:, D_MODEL:2 * D_MODEL])
    z_ref[...] = _nt_dot(h.astype(BF16), w_ref[...]).astype(z_ref.dtype)


def _inproj(x, norm_w, mod4, w_in_p, l):
    tm = 512
    return pl.pallas_call(
        _inproj_kernel,
        grid=(T_ALL // tm,),
        in_specs=[
            pl.BlockSpec((tm, D_MODEL), lambda i: (i, 0)),
            pl.BlockSpec((None, 1, D_MODEL), lambda i: (l, 0, 0)),
            pl.BlockSpec((None, None, 1, 6 * D_MODEL), lambda i: (l, _group_of_rows(i, tm), 0, 0)),
            pl.BlockSpec((None, Z_W, D_MODEL), lambda i: (l, 0, 0),
                         pipeline_mode=pl.Buffered(1)),
        ],
        out_specs=pl.BlockSpec((tm, Z_W), lambda i: (i, 0)),
        out_shape=jax.ShapeDtypeStruct((T_ALL, Z_W), BF16),
        compiler_params=_cparams(("arbitrary",)),
        name="inproj",
    )(x, norm_w, mod4, w_in_p)


def _head_norm_pairs(z_ref, groups):
    same_head = (lax.broadcasted_iota(jnp.int32, (128, 128), 0) // HEAD_DIM
                 == lax.broadcasted_iota(jnp.int32, (128, 128), 1) // HEAD_DIM)
    ones = jnp.where(same_head, 1.0, 0.0).astype(BF16)
    tiles = [[(z_ref[:, col + 128 * p: col + 128 * (p + 1)].astype(F32), w_ref[:, 128 * p:128 * (p + 1)])
              for p in range(width // 128)] for col, width, w_ref in groups]
    sums = [[_dot((x * x).astype(BF16), ones) for x, _ in group] for group in tiles]
    return [[x * lax.rsqrt(ss * (1.0 / HEAD_DIM) + EPS) * w for (x, w), ss in zip(group, gs)]
            for group, gs in zip(tiles, sums)]


def _rope(x, cos, sin_signed):
    n = x.shape[-1]
    lane = lax.broadcasted_iota(jnp.int32, (1, n), 1)
    first = (lane % 32) < 16
    partner = jnp.where(first, pltpu.roll(x, n - 16, 1), pltpu.roll(x, 16, 1))
    return x * cos + partner * sin_signed


def _prep_kernel(z_ref, wqb_ref, wkb_ref, wqc_ref, wkc_ref, cos_ref, sin_ref,
                 qb_ref, kb_ref, vb_ref, qc_ref, kc_ref, vc_ref):
    qscale = QK_SCALE

    qb, kb, qc, kc = _head_norm_pairs(z_ref, [(A_BQ, 512, wqb_ref), (A_BK, 128, wkb_ref),
                                              (A_CQ, 512, wqc_ref), (A_CK, 512, wkc_ref)])

    vb_ref[...] = z_ref[:, A_BV:A_BV + 128].astype(F32).T.astype(BF16)
    vc_ref[...] = z_ref[:, A_CV:A_CV + 512]
    for p in range(4):
        qc_ref[:, 128 * p:128 * (p + 1)] = (qc[p] * qscale).astype(BF16)
        kc_ref[:, 128 * p:128 * (p + 1)] = kc[p].astype(BF16)

    cos = cos_ref[...]
    sin = sin_ref[...]
    for p in range(4):
        qb_ref[128 * p:128 * (p + 1), :] = (_rope(qb[p], cos, sin) * qscale).T.astype(BF16)
    kb_ref[...] = _rope(kb[0], cos, sin).astype(BF16)


def _attn_prep(z, wqb, wkb, wqc, wkc, cos_t, sin_t):
    tm = 512
    lat_blocks = LAT_LEN // tm
    rope_idx = lambda i: (i % lat_blocks, 0)
    row = lambda w: pl.BlockSpec((tm, w), lambda i: (i, 0))
    col = lambda w: pl.BlockSpec((w, tm), lambda i: (0, i))
    cst = lambda w: pl.BlockSpec((1, w), lambda i: (0, 0))
    sds = lambda w: jax.ShapeDtypeStruct((T_LAT, w), BF16)
    sds_t = lambda w: jax.ShapeDtypeStruct((w, T_LAT), BF16)
    return pl.pallas_call(
        _prep_kernel,
        grid=(T_LAT // tm,),
        in_specs=[
            pl.BlockSpec((tm, ATT_W), lambda i: (T_CTX // tm + i, Z_ATT // ATT_W)),
            cst(512), cst(128), cst(512), cst(512),
            pl.BlockSpec((tm, 128), rope_idx),
            pl.BlockSpec((tm, 128), rope_idx),
        ],
        out_specs=[col(512), row(128), col(128), row(512), row(512), row(512)],
        out_shape=[sds_t(512), sds(128), sds_t(128), sds(512), sds(512), sds(512)],
        compiler_params=_cparams(("arbitrary",)),
        name="attn_prep",
    )(z, wqb, wkb, wqc, wkc, cos_t, sin_t)


def _half_rows(t, half):
    z = jnp.zeros((HEAD_DIM, t.shape[1]), t.dtype)
    return jnp.concatenate([t[:HEAD_DIM], z] if half == 0 else [z, t[HEAD_DIM:]], axis=0)


def _ctx_attn_kernel(sink_ref, z_ref, wqb_ref, wkb_ref, wqc_ref, wkc_ref, *refs, layer):
    ob_ref, oc_ref = refs[-6:-4]
    cache_refs = refs[-4:]
    if len(refs) == 6:
        for ref in cache_refs:
            ref[...] = jnp.zeros(ref.shape, F32)
        cache_refs = [ref.at[layer] for ref in cache_refs]
    kbf_ref, vbf_ref, kcf_ref, vcf_ref = cache_refs
    vbf_ref[...] = z_ref[:, A_BV:A_BV + 128].astype(F32)
    vcf_ref[...] = z_ref[:, A_CV:A_CV + 512].astype(F32)
    n = z_ref.shape[0]
    qscale = QK_SCALE

    def pair(col, p):
        return z_ref[:, col + 128 * p: col + 128 * (p + 1)]

    qb_n, kb_n, qc_n, kc_n = _head_norm_pairs(z_ref, [(A_BQ, 512, wqb_ref), (A_BK, 128, wkb_ref),
                                                      (A_CQ, 512, wqc_ref), (A_CK, 512, wkc_ref)])

    def q_t(tile):
        return (tile * qscale).T.astype(BF16)

    def v_t(col, p):
        return pair(col, p).astype(F32).T.astype(BF16)

    kb = kb_n[0]
    kbf_ref[...] = kb
    kb = kb.astype(BF16)
    qb_t = [q_t(tile) for tile in qb_n]
    scores = []
    for kvh in range(SWA_KV_HEADS):
        placed = []
        for h in range(SWA_GROUP * kvh, SWA_GROUP * (kvh + 1)):
            rows = qb_t[h // 2][HEAD_DIM * (h % 2):HEAD_DIM * (h % 2 + 1)]
            z = jnp.zeros_like(rows)
            placed.append(jnp.concatenate([rows, z] if kvh == 0 else [z, rows], axis=0))
        scores.append(_dot(kb, jnp.concatenate(placed, axis=1)))
    for p in range(NA_HEADS // 2):
        kc = kc_n[p]
        kcf_ref[:, 128 * p:128 * (p + 1)] = kc
        qc_t = q_t(qc_n[p])
        q2 = jnp.concatenate([_half_rows(qc_t, 0), _half_rows(qc_t, 1)], axis=1)
        scores.append(_dot(kc.astype(BF16), q2))
    s_t = jnp.concatenate(scores, axis=1)

    sink = jnp.concatenate(
        [jnp.full((1, n), sink_ref[layer, h] * LOG2E, F32) for h in range(SWA_Q_HEADS)]
        + [jnp.full((1, n * NA_HEADS), -jnp.inf, F32)], axis=1)
    mx = jnp.maximum(s_t.max(axis=0, keepdims=True), sink)
    p_t = jnp.exp2(s_t - mx).astype(BF16)
    ones = jnp.ones((8, n), BF16)
    inv = 1.0 / (_dot(ones, p_t)[0:1] + jnp.exp2(sink - mx))

    nb = SWA_Q_HEADS * n
    o = _dot(v_t(A_BV, 0), p_t[:, :nb]) * inv[:, :nb]
    outs = [o[HEAD_DIM * (h // SWA_GROUP):HEAD_DIM * (h // SWA_GROUP + 1), n * h:n * (h + 1)]
            for h in range(SWA_Q_HEADS)]
    ob_ref[...] = jnp.concatenate(outs, axis=0).T.astype(BF16)
    outs = []
    for p in range(NA_HEADS // 2):
        cols = slice(nb + 2 * n * p, nb + 2 * n * (p + 1))
        o = _dot(v_t(A_CV, p), p_t[:, cols]) * inv[:, cols]
        outs += [o[:HEAD_DIM, :n], o[HEAD_DIM:, n:]]
    oc_ref[...] = jnp.concatenate(outs, axis=0).T.astype(BF16)


def _ctx_attention(sink, z, wqb, wkb, wqc, wkc, caches, l):
    blk = lambda w, j=0: pl.BlockSpec((CTX_LEN, w), lambda b: (b, j))
    cst = lambda w: pl.BlockSpec((1, w), lambda b: (0, 0))
    sds = lambda w: jax.ShapeDtypeStruct((T_CTX, w), BF16)
    cache_w = (128, 128, 512, 512)
    if caches is None:
        cache_blk = [pl.BlockSpec((None, DEPTH, CTX_LEN, w), lambda b: (b, 0, 0, 0)) for w in cache_w]
    else:
        cache_blk = [pl.BlockSpec((None, None, CTX_LEN, w), lambda b: (b, l, 0, 0)) for w in cache_w]
    cache_sds = [jax.ShapeDtypeStruct((N_CTX_SEQ, DEPTH, CTX_LEN, w), F32) for w in cache_w]
    carried = [] if caches is None else list(caches)
    n_in = 6
    return pl.pallas_call(
        functools.partial(_ctx_attn_kernel, layer=l),
        grid=(N_CTX_SEQ,),
        in_specs=[pl.BlockSpec(memory_space=pltpu.SMEM), blk(ATT_W, Z_ATT // ATT_W),
                  cst(512), cst(128), cst(512), cst(512)]
                 + [pl.BlockSpec(memory_space=pl.ANY)] * len(carried),
        out_specs=[blk(512), blk(512)] + cache_blk,
        out_shape=[sds(512), sds(512)] + cache_sds,
        input_output_aliases={n_in + j: 2 + j for j in range(len(carried))},
        compiler_params=_cparams(("arbitrary",)),
        name="ctx_attention",
    )(sink, z, wqb, wkb, wqc, wkc, *carried)


SWA_QBLK = 128
SWA_SPAN = SWA_QBLK + 2 * SWA_WINDOW


def _swa_kernel(sink_ref, qt_ref, k_ref, vt_ref, kc_ref, vc_ref, o_ref, kctx_scr, vctx_scr, *, layer):
    i = pl.program_id(1)
    nq = SWA_QBLK

    @pl.when(i == 0)
    def _():
        kctx_scr[...] = kc_ref[...].T.astype(BF16)
        vctx_scr[...] = vc_ref[...].astype(BF16)

    start = pl.multiple_of(jnp.clip(nq * (i - 1), 0, LAT_LEN - SWA_SPAN), nq)
    kwin = k_ref[pl.ds(start, SWA_SPAN), :]
    vwin_t = vt_ref[:, pl.ds(start, SWA_SPAN)]
    kctx = kctx_scr[...]
    s_lat, s_ctx = [], []
    for kvh in range(SWA_KV_HEADS):
        placed = []
        for h in range(SWA_GROUP * kvh, SWA_GROUP * (kvh + 1)):
            rows = qt_ref[HEAD_DIM * h:HEAD_DIM * (h + 1), :]
            z = jnp.zeros_like(rows)
            placed.append(jnp.concatenate([rows, z] if kvh == 0 else [z, rows], axis=0))
        q4 = jnp.concatenate(placed, axis=1)
        s_lat.append(_dot(kwin, q4))
        s_ctx.append(_dot(kctx, q4))
    s_lat = jnp.concatenate(s_lat, axis=1)
    s_ctx = jnp.concatenate(s_ctx, axis=1)

    kpos = start + lax.broadcasted_iota(jnp.int32, (SWA_SPAN, nq), 0)
    qpos = nq * i + lax.broadcasted_iota(jnp.int32, (SWA_SPAN, nq), 1)
    band = jnp.where(jnp.abs(kpos - qpos) <= SWA_WINDOW, 0.0, -jnp.inf)
    s_lat = s_lat + jnp.concatenate([band] * SWA_Q_HEADS, axis=1)
    sink = jnp.concatenate(
        [jnp.full((1, nq), sink_ref[layer, h] * LOG2E, F32) for h in range(SWA_Q_HEADS)], axis=1)
    mx = jnp.maximum(jnp.maximum(s_lat.max(axis=0, keepdims=True),
                                 s_ctx.max(axis=0, keepdims=True)), sink)
    p_lat = jnp.exp2(s_lat - mx).astype(BF16)
    p_ctx = jnp.exp2(s_ctx - mx).astype(BF16)
    key_sum = lambda p: _dot(jnp.ones((8, p.shape[0]), BF16), p)[0:1]
    inv = 1.0 / (key_sum(p_lat) + key_sum(p_ctx) + jnp.exp2(sink - mx))
    o = (_dot(vwin_t, p_lat) + _dot(vctx_scr[...], p_ctx)) * inv
    outs = [o[HEAD_DIM * (h // SWA_GROUP):HEAD_DIM * (h // SWA_GROUP + 1), nq * h:nq * (h + 1)]
            for h in range(SWA_Q_HEADS)]
    o_ref[...] = jnp.concatenate(outs, axis=0).T.astype(BF16)


def _swa_attention(sink, qb_t, kb, vb_t, cache_k, cache_v, l):
    nq = LAT_LEN // SWA_QBLK
    cache = pl.BlockSpec((None, None, 128, PAST_LEN), lambda b, i: (b, l, 0, 0))
    return pl.pallas_call(
        functools.partial(_swa_kernel, layer=l),
        grid=(N_LAT_SEQ, nq),
        in_specs=[pl.BlockSpec(memory_space=pltpu.SMEM),
                  pl.BlockSpec((512, SWA_QBLK), lambda b, i: (0, b * nq + i)),
                  pl.BlockSpec((LAT_LEN, 128), lambda b, i: (b, 0)),
                  pl.BlockSpec((128, LAT_LEN), lambda b, i: (0, b)),
                  cache, cache],
        out_specs=pl.BlockSpec((SWA_QBLK, 512), lambda b, i: (b * nq + i, 0)),
        out_shape=jax.ShapeDtypeStruct((T_LAT, 512), BF16),
        scratch_shapes=[pltpu.VMEM((PAST_LEN, 128), BF16), pltpu.VMEM((128, PAST_LEN), BF16)],
        compiler_params=_cparams(("arbitrary", "arbitrary")),
        name="swa_attention",
    )(sink, qb_t, kb, vb_t, cache_k, cache_v)


LAT_ROWS = LAT_LEN // GRID_W
NA_WIN_ROWS = min(NA_ROWS, LAT_ROWS)
NA_KEYS = NA_WIN_ROWS * GRID_W


def _na_kernel(q_ref, k_ref, v_ref, kc_ref, vc_ref, rp_ref, o_ref, kctx_scr, vctx_scr, bias_ref):
    r = pl.program_id(1)
    low = lax.broadcasted_iota(jnp.int32, (1, 128), 1) < HEAD_DIM

    @pl.when(r == 0)
    def _():
        kctx_scr[...] = kc_ref[...].astype(BF16)
        vctx_scr[...] = vc_ref[...].astype(BF16)
        cq = lax.broadcasted_iota(jnp.int32, (GRID_W, 128), 0)
        ck = lax.broadcasted_iota(jnp.int32, (GRID_W, 128), 1) % GRID_W
        cs = jnp.clip(cq - NA_COLS // 2, 0, GRID_W - NA_COLS)
        window = jnp.where((ck >= cs) & (ck < cs + NA_COLS), 0.0, -jnp.inf)
        for h in range(NA_HEADS):
            rows = [jnp.broadcast_to(rp_ref[h, dr:dr + 1, :], (GRID_W, 128))
                    for dr in range(2 * NA_ROWS - 1)]
            left = [pltpu.roll(x, 0, 1, stride=1, stride_axis=0) for x in rows[:-1]]
            right = [pltpu.roll(x, GRID_W, 1, stride=1, stride_axis=0) for x in rows[1:]]
            for j in range(2 * NA_ROWS - 2):
                bias_ref[h, j] = jnp.where(low, left[j], right[j]) * LOG2E + window

    rs = jnp.clip(r - NA_ROWS // 2, 0, LAT_ROWS - NA_WIN_ROWS)
    start = pl.multiple_of(rs * GRID_W, GRID_W)
    base = rs - r + NA_ROWS - 1
    npair = NA_HEADS // 2
    s_lat, s_ctx = [], []
    for p in range(npair):
        sl = slice(128 * p, 128 * (p + 1))
        q = q_ref[:, sl]
        zero = jnp.zeros_like(q)
        q2 = jnp.concatenate([jnp.where(low, q, zero), jnp.where(low, zero, q)], axis=0)
        bias = jnp.concatenate(
            [jnp.concatenate([bias_ref[2 * p + hh, base + 2 * w] for w in range(NA_WIN_ROWS // 2)],
                             axis=1) for hh in range(2)], axis=0)
        s_lat.append(_nt_dot(q2, k_ref[pl.ds(start, NA_KEYS), sl]) + bias)
        s_ctx.append(_dot(q2, kctx_scr[sl, :]))
    s_lat = jnp.concatenate(s_lat, axis=0)
    s_ctx = jnp.concatenate(s_ctx, axis=0)
    mx = jnp.maximum(s_lat.max(axis=-1, keepdims=True), s_ctx.max(axis=-1, keepdims=True))
    p_lat = jnp.exp2(s_lat - mx)
    p_ctx = jnp.exp2(s_ctx - mx)
    inv = 1.0 / (p_lat.sum(axis=-1, keepdims=True) + p_ctx.sum(axis=-1, keepdims=True))
    p_lat = p_lat.astype(BF16)
    p_ctx = p_ctx.astype(BF16)
    tiles = []
    for p in range(npair):
        sl = slice(128 * p, 128 * (p + 1))
        rows = slice(2 * GRID_W * p, 2 * GRID_W * (p + 1))
        o2 = (_dot(p_lat[rows], v_ref[pl.ds(start, NA_KEYS), sl])
              + _nt_dot(p_ctx[rows], vctx_scr[sl, :])) * inv[rows]
        tiles.append(jnp.where(low, o2[:GRID_W], o2[GRID_W:]))
    o_ref[...] = jnp.concatenate(tiles, axis=1).astype(BF16)


def _na_attention(qc, kc, vc, cache_k, cache_v, rp_cyc, l):
    kv = pl.BlockSpec((LAT_LEN, 512), lambda b, r: (b, 0))
    cache = pl.BlockSpec((None, None, 512, PAST_LEN), lambda b, r: (b, l, 0, 0))
    return pl.pallas_call(
        _na_kernel,
        grid=(N_LAT_SEQ, LAT_ROWS),
        in_specs=[pl.BlockSpec((GRID_W, 512), lambda b, r: (b * LAT_ROWS + r, 0)),
                  kv, kv, cache, cache,
                  pl.BlockSpec((None, NA_HEADS, 2 * NA_ROWS - 1, 128), lambda b, r: (l, 0, 0, 0))],
        out_specs=pl.BlockSpec((GRID_W, 512), lambda b, r: (b * LAT_ROWS + r, 0)),
        out_shape=jax.ShapeDtypeStruct((T_LAT, 512), BF16),
        scratch_shapes=[pltpu.VMEM((512, PAST_LEN), BF16), pltpu.VMEM((512, PAST_LEN), BF16),
                        pltpu.VMEM((NA_HEADS, 2 * NA_ROWS - 2, GRID_W, 128), F32)],
        compiler_params=_cparams(("arbitrary", "arbitrary")),
        name="na_attention",
    )(qc, kc, vc, cache_k, cache_v, rp_cyc)


def _na_table_rows(rpb):
    pad = jnp.zeros(rpb.shape[:-1] + (128 - (2 * NA_COLS - 1),), F32)
    return jnp.concatenate([rpb[..., NA_COLS - 1:], pad, rpb[..., :NA_COLS - 1]], axis=-1)


def _gla_constants():
    c = GLA_C
    t = np.arange(c)[:, None]
    u = np.arange(c)[None, :]
    blocks = []
    for k in range(GLA_LEVELS):
        b = 1 << k
        m = ((t >> k) | 1) * b - 1
        query = ((t >> k) & 1) == 1
        blocks.append(np.where(query, (u > m) & (u <= t), (u > t) & (u <= m)))
    blocks.append(u <= t)
    blocks.append(u > t)
    blocks.append(np.ones((16, c), bool))
    fwd = np.concatenate(blocks, axis=0).astype(np.float32)
    bwd_blocks = [blk[::-1, ::-1] for blk in blocks]
    bwd = np.concatenate(bwd_blocks, axis=0).astype(np.float32)
    seg = np.stack([fwd, bwd])
    seg = np.concatenate([seg, seg], axis=-1)

    s = np.arange(c)[None, :]
    x = t ^ s
    lev = np.where(x == 0, GLA_LEVELS, np.floor(np.log2(np.maximum(x, 1))).astype(np.int64))
    lev_f = np.where(s <= t, lev, -1)
    lev_b = np.where(s >= t, lev, -1)
    levmap = np.stack([np.tile(lev_f, (1, GLA_HEADS)), np.tile(lev_b, (1, GLA_HEADS))])
    return seg, levmap.astype(np.int32)


def _gla_state_init(d, cc, s0_ref, st_ref):
    c = GLA_C
    ctx_chunks = T_CTX // c
    is_lat = cc >= ctx_chunks
    per_seq = jnp.where(is_lat, LAT_LEN // c, CTX_LEN // c)
    pos = jnp.where(is_lat, cc - ctx_chunks, cc) % per_seq
    first = (pos == 0) if d == 0 else (pos == per_seq - 1)

    @pl.when(first)
    def _():
        for h in range(GLA_HEADS):
            blk = jnp.where(is_lat, s0_ref[h], 0.0)
            row = [jnp.zeros((GLA_DV, GLA_DK), F32)] * GLA_HEADS
            row[h] = blk
            st_ref[GLA_DV * h:GLA_DV * (h + 1), :] = jnp.concatenate(row, axis=1)


def _gla_chunks(chains):
    c = GLA_C
    lane_head = lax.broadcasted_iota(jnp.int32, (1, GLA_HEADS * GLA_DK), 1) // GLA_DK
    row_head = lax.broadcasted_iota(jnp.int32, (GLA_HEADS * GLA_DV, 1), 0) // GLA_DV
    head_sel = [jnp.where(lane_head == h, 1.0, 0.0).astype(BF16) for h in range(GLA_HEADS)]

    def operands(q, k, e=None):
        if e is not None:
            q, k = q * e, k * e
        return q, jnp.concatenate([k * head_sel[h] for h in range(GLA_HEADS)], axis=0)

    segs = []
    for (_, _, _, lr_ref, w2_ref, ba_ref, seg_ref, *_) in chains:
        x = _dot(lr_ref[...], w2_ref[...]) + ba_ref[...]
        la = (jnp.minimum(x, 0.0) - jnp.log(1.0 + jnp.exp(-jnp.abs(x)))) * (1.0 / GLA_TAU)
        la_hi = la.astype(BF16)
        la_lo = (la - la_hi.astype(F32)).astype(BF16)
        segs.append(_dot(seg_ref[...], jnp.concatenate([la_hi, la_lo], axis=0)))

    def decay(seg, block):
        return jnp.exp(seg[c * block:c * (block + 1)]).astype(BF16)

    qs = [ch[0][...] * (GLA_DK ** -0.5) for ch in chains]
    ks = [ch[1][...] for ch in chains]
    order = [GLA_LEVELS] + list(range(GLA_LEVELS))
    prepare = lambda lvl: [operands(qs[n], ks[n], None if lvl == GLA_LEVELS else decay(segs[n], lvl))
                           for n in range(len(chains))]
    attn = [None] * len(chains)
    ops = prepare(order[0])
    for i, lvl in enumerate(order):
        scores = [_nt_dot(q, kbd) for q, kbd in ops]
        if i + 1 < len(order):
            ops = prepare(order[i + 1])
        for n, ch in enumerate(chains):
            attn[n] = jnp.where(ch[7][...] == lvl, scores[n], 0.0 if attn[n] is None else attn[n])

    for n, (_, _, v_ref, _, _, _, _, _, o_ref, sfin_ref, st_ref) in enumerate(chains):
        seg = segs[n]
        v = v_ref[...]
        a = attn[n].astype(BF16)
        tot = seg[c * (GLA_LEVELS + 2):c * (GLA_LEVELS + 2) + 1]
        q_in = qs[n] * decay(seg, GLA_LEVELS)
        k_in = ks[n] * decay(seg, GLA_LEVELS + 1)
        state = st_ref[...]
        o_inter = _nt_dot(q_in, state.astype(BF16))
        for h in range(GLA_HEADS):
            sl = slice(GLA_DV * h, GLA_DV * (h + 1))
            o_ref[:, sl] = o_inter[:, sl] + _dot(a[:, c * h:c * (h + 1)], v[:, sl])
        upd = _tn_dot(v, k_in)
        new_state = state * jnp.exp(tot) + jnp.where(row_head == lane_head, upd, 0.0)
        st_ref[...] = new_state
        for h in range(GLA_HEADS):
            sfin_ref[h] = new_state[GLA_DV * h:GLA_DV * (h + 1), GLA_DK * h:GLA_DK * (h + 1)]


def _gla_kernel(qf, kf, vf, lrf, qb, kb, vb, lrb, w2_ref, ba_ref, seg_ref, lev_ref, s0f, s0b,
                of_ref, ob_ref, sff_ref, sfb_ref, stf, stb):
    step = pl.program_id(0)
    last = pl.num_programs(0) - 1
    _gla_state_init(0, step, s0f, stf)
    _gla_state_init(1, last - step, s0b, stb)
    _gla_chunks([
        (qf, kf, vf, lrf, w2_ref.at[0], ba_ref.at[0], seg_ref.at[0], lev_ref.at[0], of_ref, sff_ref, stf),
        (qb, kb, vb, lrb, w2_ref.at[1], ba_ref.at[1], seg_ref.at[1], lev_ref.at[1], ob_ref, sfb_ref, stb),
    ])


def _gla(z, w2p, ba, seg, levmap, s0t, l):
    c = GLA_C
    nchunk = T_ALL // c
    ctx_chunks = T_CTX // c
    nseq = N_CTX_SEQ + N_LAT_SEQ
    fwd = lambda s: s
    bwd = lambda s: nchunk - 1 - s

    def seq_of(cc):
        return jnp.where(cc < ctx_chunks, cc // (CTX_LEN // c),
                         N_CTX_SEQ + (cc - ctx_chunks) // (LAT_LEN // c))

    def lat_of(cc):
        return jnp.clip((cc - ctx_chunks) // (LAT_LEN // c), 0, N_LAT_SEQ - 1)

    def chunk_specs(pos):
        return [pl.BlockSpec((c, 256), lambda s: (pos(s), Z_AQ // 256)),
                pl.BlockSpec((c, 256), lambda s: (pos(s), Z_AK // 256)),
                pl.BlockSpec((c, 512), lambda s: (pos(s), Z_AV // 512)),
                pl.BlockSpec((c, 128), lambda s: (pos(s), Z_LR // 128))]

    def s0_spec(d, pos):
        return pl.BlockSpec((None, None, None, GLA_HEADS, GLA_DV, GLA_DK),
                            lambda s: (lat_of(pos(s)), l, d, 0, 0, 0))

    def sfin_spec(pos):
        return pl.BlockSpec((None, GLA_HEADS, GLA_DV, GLA_DK), lambda s: (seq_of(pos(s)), 0, 0, 0))

    state_sds = jax.ShapeDtypeStruct((nseq, GLA_HEADS, GLA_DV, GLA_DK), F32)
    out_sds = jax.ShapeDtypeStruct((T_ALL, 512), F32)
    state_scr = pltpu.VMEM((GLA_HEADS * GLA_DV, GLA_HEADS * GLA_DK), F32)
    return pl.pallas_call(
        _gla_kernel,
        grid=(nchunk,),
        in_specs=chunk_specs(fwd) + chunk_specs(bwd) + [
            pl.BlockSpec((None, 2, 128, 256), lambda s: (l, 0, 0, 0)),
            pl.BlockSpec((None, 2, 1, 256), lambda s: (l, 0, 0, 0)),
            pl.BlockSpec((2, GLA_GROWS, 2 * c), lambda s: (0, 0, 0)),
            pl.BlockSpec((2, c, GLA_HEADS * c), lambda s: (0, 0, 0)),
            s0_spec(0, fwd), s0_spec(1, bwd),
        ],
        out_specs=[pl.BlockSpec((c, 512), lambda s: (fwd(s), 0)),
                   pl.BlockSpec((c, 512), lambda s: (bwd(s), 0)),
                   sfin_spec(fwd), sfin_spec(bwd)],
        out_shape=[out_sds, out_sds, state_sds, state_sds],
        scratch_shapes=[state_scr, state_scr],
        compiler_params=_cparams(("arbitrary",)),
        name="gla",
    )(z, z, z, z, z, z, z, z, w2p, ba, seg, levmap, s0t, s0t)


def _cast_kernel(*refs):
    n = len(refs) // 2
    for src, dst in zip(refs[:n], refs[n:]):
        dst[...] = src[...].astype(dst.dtype)


def _cast_bf16(*weights):
    spec = lambda w: pl.BlockSpec((None,) + w.shape[1:], lambda l: (l, 0, 0))
    return pl.pallas_call(
        _cast_kernel,
        grid=(DEPTH,),
        in_specs=[spec(w) for w in weights],
        out_specs=[spec(w) for w in weights],
        out_shape=[jax.ShapeDtypeStruct(w.shape, BF16) for w in weights],
        compiler_params=_cparams(("arbitrary",)),
        name="cast_bf16",
    )(*weights)


def _post_mixer_kernel(x_ref, of_ref, obk_ref, ar_ref, gates_ref, bc_ref, bl_ref, cc_ref, cl_ref,
                       wpa_ref, wpb_ref, wpc_ref, wo_ref, gn_ref, g2_ref, mod_ref, w1_ref, w2_ref,
                       y_ref, w1_scr, w2_scr, h_scr, x1_scr, acc_scr, *, nk):
    s = pl.program_id(0)
    d = D_MODEL
    tm = x_ref.shape[0]
    th = w1_ref.shape[1]
    ctx = jnp.maximum(s - (nk - 1), 0) < T_CTX // tm

    def merged_residual():
        o = of_ref[...] + obk_ref[...]
        heads = []
        for h in range(GLA_HEADS):
            oh = o[:, GLA_DV * h:GLA_DV * (h + 1)]
            ms = jnp.mean(oh * oh, axis=-1, keepdims=True)
            heads.append(oh * lax.rsqrt(ms + EPS) * gn_ref[...])
        oa = (jnp.concatenate(heads, axis=1) * _silu(ar_ref[...].astype(F32))).astype(BF16)
        ob = jnp.where(ctx, bc_ref[...], bl_ref[...])
        oc = jnp.where(ctx, cc_ref[...], cl_ref[...])
        gate = lambda j: _sigmoid(gates_ref[:, j * d:(j + 1) * d].astype(F32))
        merged = (gate(0) * _dot(oa, wpa_ref[...]) + gate(1) * _dot(ob, wpb_ref[...])
                  + gate(2) * _dot(oc, wpc_ref[...]))
        return x_ref[...] + mod_ref[:, 2 * d:3 * d] * _dot(merged.astype(BF16), wo_ref[...])

    def normed(x1):
        return _norm_mod(x1, g2_ref[...], mod_ref[:, 3 * d:4 * d], mod_ref[:, 4 * d:5 * d]).astype(BF16)

    def act(u):
        return jnp.square(jnp.maximum(u, 0.0)).astype(BF16)

    @pl.when(s == 0)
    def _():
        x1 = merged_residual()
        x1_scr[...] = x1
        h_scr[...] = normed(x1)

    @pl.when(s < nk)
    def _():
        col = pl.multiple_of(s * th, th)
        w1 = w1_ref[...].astype(BF16)
        w2 = w2_ref[...].astype(BF16)
        w1_scr[:, pl.ds(col, th)] = w1
        w2_scr[pl.ds(col, th), :] = w2
        part = _dot(act(_dot(h_scr[...], w1)), w2)

        @pl.when(s == 0)
        def _():
            acc_scr[...] = part

        @pl.when(s > 0)
        def _():
            acc_scr[...] += part

        @pl.when(s == nk - 1)
        def _():
            y_ref[...] = x1_scr[...] + mod_ref[:, 5 * d:6 * d] * acc_scr[...]

    @pl.when(s >= nk)
    def _():
        x1 = merged_residual()
        out = _dot(act(_dot(normed(x1), w1_scr[...])), w2_scr[...])
        y_ref[...] = x1 + mod_ref[:, 5 * d:6 * d] * out


def _post_mixer(x, o_fwd, o_bwd, z, ob_ctx, ob_lat, oc_ctx, oc_lat, wpa, wpb, wpc, wo, gn, norm2_w,
                mod4, w1, w2, l):
    tm, th = 256, 512
    nk = MLP_HIDDEN // th
    ctx_blocks = T_CTX // tm
    row = lambda s: jnp.maximum(s - (nk - 1), 0)
    chunk = lambda s: jnp.minimum(s, nk - 1)
    ctx_idx = lambda s: (jnp.minimum(row(s), ctx_blocks - 1), 0)
    lat_idx = lambda s: (jnp.maximum(row(s) - ctx_blocks, 0), 0)
    rows = lambda w, j=0: pl.BlockSpec((tm, w), lambda s: (row(s), j))
    resident = lambda k: pl.BlockSpec((None, k, D_MODEL), lambda s: (l, 0, 0),
                                      pipeline_mode=pl.Buffered(1))
    return pl.pallas_call(
        functools.partial(_post_mixer_kernel, nk=nk),
        grid=(nk - 1 + T_ALL // tm,),
        in_specs=[
            rows(D_MODEL), rows(512), rows(512), rows(512, Z_AR // 512), rows(3 * D_MODEL),
            pl.BlockSpec((tm, 512), ctx_idx), pl.BlockSpec((tm, 512), lat_idx),
            pl.BlockSpec((tm, 512), ctx_idx), pl.BlockSpec((tm, 512), lat_idx),
            resident(512), resident(512), resident(512), resident(D_MODEL),
            pl.BlockSpec((None, 1, GLA_DV), lambda s: (l, 0, 0)),
            pl.BlockSpec((None, 1, D_MODEL), lambda s: (l, 0, 0)),
            pl.BlockSpec((None, None, 1, 6 * D_MODEL),
                         lambda s: (l, _group_of_rows(row(s), tm), 0, 0)),
            pl.BlockSpec((None, D_MODEL, th), lambda s: (l, 0, chunk(s))),
            pl.BlockSpec((None, th, D_MODEL), lambda s: (l, chunk(s), 0)),
        ],
        out_specs=rows(D_MODEL),
        out_shape=jax.ShapeDtypeStruct((T_ALL, D_MODEL), F32),
        scratch_shapes=[pltpu.VMEM((D_MODEL, MLP_HIDDEN), BF16), pltpu.VMEM((MLP_HIDDEN, D_MODEL), BF16),
                        pltpu.VMEM((tm, D_MODEL), BF16), pltpu.VMEM((tm, D_MODEL), F32),
                        pltpu.VMEM((tm, D_MODEL), F32)],
        compiler_params=_cparams(("arbitrary",)),
        name="post_mixer",
    )(x, o_fwd, o_bwd, z, z, ob_ctx, ob_lat, oc_ctx, oc_lat, wpa, wpb, wpc, wo, gn, norm2_w, mod4, w1, w2)


def _rope_tables():
    t = jnp.arange(LAT_LEN)
    row = (t // GRID_W).astype(F32)
    col = (t % GRID_W).astype(F32)
    nf = HEAD_DIM // 4
    inv_freq = ROPE_BASE ** (-jnp.arange(nf, dtype=F32) / nf)
    ang_r = row[:, None] * inv_freq[None, :]
    ang_c = col[:, None] * inv_freq[None, :]
    cos = jnp.concatenate([jnp.cos(ang_r)] * 2 + [jnp.cos(ang_c)] * 2, axis=1)
    sin = jnp.concatenate([-jnp.sin(ang_r), jnp.sin(ang_r), -jnp.sin(ang_c), jnp.sin(ang_c)], axis=1)
    return jnp.tile(cos, (1, 2)), jnp.tile(sin, (1, 2))


def kernel(x_prompt, x_sample, state_gla, cache_swa_k, cache_swa_v, cache_na_k, cache_na_v, c,
           c_ctx, w_mod, b_mod, norm1, norm2, w_in, w_a2_f, b_a_f, w_a2_b, b_a_b, gla_onorm,
           qn_swa, kn_swa, sink_swa, qn_na, kn_na, rpb_na, w_pa, w_pb, w_pc, w_o, w_fc1, w_fc2):
    d = D_MODEL
    x = jnp.concatenate([x_prompt.reshape(T_CTX, d), x_sample.reshape(T_LAT, d)], axis=0)

    cond8 = jnp.zeros((8, d), F32).at[0].set(c_ctx).at[1:1 + N_LAT_SEQ].set(c)
    mod4 = _modulation(cond8, w_mod, b_mod).reshape(DEPTH, 8, 1, 6 * d)

    w_in_p = _w_in_layout(jnp.swapaxes(w_in, 1, 2))
    wpa, wpb, wpc, wo = _cast_bf16(w_pa, w_pb, w_pc, w_o)
    norm1r = norm1.reshape(DEPTH, 1, d)
    norm2r = norm2.reshape(DEPTH, 1, d)
    gnr = gla_onorm.reshape(DEPTH, 1, GLA_DV)
    w2p = jnp.zeros((DEPTH, 2, 128, GLA_HEADS * GLA_DK), F32)
    w2p = w2p.at[:, 0, 0:GLA_LOWRANK].set(w_a2_f).at[:, 1, GLA_LOWRANK:2 * GLA_LOWRANK].set(w_a2_b)
    w2p = w2p.astype(BF16)
    ba = jnp.stack([b_a_f, b_a_b], axis=1).reshape(DEPTH, 2, 1, GLA_HEADS * GLA_DK)
    seg_np, lev_np = _gla_constants()
    seg = jnp.asarray(seg_np, BF16)
    levmap = jnp.asarray(lev_np)
    s0t = jnp.swapaxes(state_gla, -1, -2)
    cos_t, sin_t = _rope_tables()
    rp_cyc = _na_table_rows(rpb_na)
    feat_major = lambda a: a.transpose(0, 1, 3, 4, 2).reshape(N_LAT_SEQ, DEPTH, -1, PAST_LEN)
    csk, csv, cnk, cnv = (feat_major(a) for a in (cache_swa_k, cache_swa_v, cache_na_k, cache_na_v))

    st_l = []
    caches = None
    for l in range(DEPTH):
        z = _inproj(x, norm1r, mod4, w_in_p, l)
        wqb = jnp.tile(qn_swa[l], 8)[None, :]
        wkb = jnp.tile(kn_swa[l], 2)[None, :]
        wqc = jnp.tile(qn_na[l], 8)[None, :]
        wkc = jnp.tile(kn_na[l], 8)[None, :]
        qb, kb, vb, qc, kc, vc = _attn_prep(z, wqb, wkb, wqc, wkc, cos_t, sin_t)
        o_fwd, o_bwd, sfin_f, sfin_b = _gla(z, w2p, ba, seg, levmap, s0t, l)
        ob_ctx, oc_ctx, *caches = _ctx_attention(sink_swa, z, wqb, wkb, wqc, wkc, caches, l)
        ob_lat = _swa_attention(sink_swa, qb, kb, vb, csk, csv, l)
        oc_lat = _na_attention(qc, kc, vc, cnk, cnv, rp_cyc, l)
        x = _post_mixer(x, o_fwd, o_bwd, z, ob_ctx, ob_lat, oc_ctx, oc_lat, wpa, wpb, wpc, wo, gnr,
                        norm2r, mod4, w_fc1, w_fc2, l)

        sfin = jnp.stack([sfin_f[:N_CTX_SEQ], sfin_b[:N_CTX_SEQ]], axis=0)
        st_l.append(jnp.swapaxes(sfin, -1, -2))

    y_prompt = x[:T_CTX].reshape(N_CTX_SEQ, CTX_LEN, d)
    y_sample = x[T_CTX:].reshape(N_LAT_SEQ, LAT_LEN, d)
    new_state = jnp.stack(st_l, axis=0).transpose(2, 0, 1, 3, 4, 5)

    swa_k, swa_v, na_k, na_v = caches
    kv_shape = lambda heads: (N_CTX_SEQ, DEPTH, CTX_LEN, heads, HEAD_DIM)
    return (y_prompt, y_sample, new_state,
            swa_k.reshape(kv_shape(SWA_KV_HEADS)), swa_v.reshape(kv_shape(SWA_KV_HEADS)),
            na_k.reshape(kv_shape(NA_HEADS)), na_v.reshape(kv_shape(NA_HEADS)))
```

```python
import functools

import numpy as np
import jax
import jax.numpy as jnp
from jax import lax
from jax.experimental import pallas as pl
from jax.experimental.pallas import tpu as pltpu

F32 = jnp.float32
BF16 = jnp.bfloat16

D_MODEL = 1024
DEPTH = 4
N_CTX_SEQ = 16
CTX_LEN = 256
N_LAT_SEQ = 2
LAT_LEN = 1024
PAST_LEN = 512
T_CTX = N_CTX_SEQ * CTX_LEN
T_LAT = N_LAT_SEQ * LAT_LEN
T_ALL = T_CTX + T_LAT
GRID_W = 64
HEAD_DIM = 64
GLA_HEADS = 4
GLA_DK = 64
GLA_DV = 128
GLA_LOWRANK = 16
GLA_TAU = 16.0
SWA_Q_HEADS = 8
SWA_KV_HEADS = 2
SWA_GROUP = 4
SWA_WINDOW = 128
NA_HEADS = 8
NA_ROWS = 8
NA_COLS = 16
MLP_HIDDEN = 4 * D_MODEL
ROPE_BASE = 10000.0
EPS = 1e-6
LOG2E = 1.4426950408889634
QK_SCALE = HEAD_DIM ** -0.5 * LOG2E

Z_GATES = 0
Z_GLA = 3072
Z_AQ, Z_AK, Z_AV, Z_AR = 3072, 3328, 3584, 4096
Z_ATT = 4608
ATT_W = 2304
A_BQ, A_BK, A_BV, A_CQ, A_CK, A_CV = 0, 512, 640, 768, 1280, 1792
Z_LR = 6912
Z_W = 7168

GLA_C = 128
GLA_LEVELS = 7
GLA_GROWS = (GLA_LEVELS + 2) * GLA_C + 16

VMEM_LIMIT = 56 * 1024 * 1024


def _cparams(sem):
    return pltpu.CompilerParams(dimension_semantics=sem, vmem_limit_bytes=VMEM_LIMIT)


def _sigmoid(x):
    return 1.0 / (1.0 + jnp.exp(-x))


def _silu(x):
    return x * _sigmoid(x)


def _nt_dot(a, b):
    return lax.dot_general(a, b, (((1,), (1,)), ((), ())), preferred_element_type=F32)


def _tn_dot(a, b):
    return lax.dot_general(a, b, (((0,), (0,)), ((), ())), preferred_element_type=F32)


def _dot(a, b):
    return jnp.dot(a, b, preferred_element_type=F32)


def _mod_kernel(cond_ref, w_ref, b_ref, o_ref):
    s = _silu(cond_ref[...]).astype(BF16)
    o_ref[...] = _dot(s, w_ref[...].astype(BF16)) + b_ref[...]


def _modulation(cond8, w_mod, b_mod):
    tn = 1024
    return pl.pallas_call(
        _mod_kernel,
        grid=(DEPTH, 6 * D_MODEL // tn),
        in_specs=[
            pl.BlockSpec((8, D_MODEL), lambda l, j: (0, 0)),
            pl.BlockSpec((None, D_MODEL, tn), lambda l, j: (l, 0, j)),
            pl.BlockSpec((None, 1, tn), lambda l, j: (l, 0, j)),
        ],
        out_specs=pl.BlockSpec((None, 8, tn), lambda l, j: (l, 0, j)),
        out_shape=jax.ShapeDtypeStruct((DEPTH, 8, 6 * D_MODEL), F32),
        compiler_params=_cparams(("arbitrary", "arbitrary")),
        name="modulation",
    )(cond8, w_mod, b_mod.reshape(DEPTH, 1, 6 * D_MODEL))


def _group_of_rows(row_block, rows_per_block):
    first = row_block * rows_per_block
    return jnp.maximum(first - T_CTX + LAT_LEN, 0) // LAT_LEN


def _norm_mod(x, g, shift, scale):
    ms = jnp.mean(x * x, axis=-1, keepdims=True)
    y = x * lax.rsqrt(ms + EPS) * g
    return y * (1.0 + scale) + shift


W_IN_SEGMENTS = ((3872, 6944), (0, 1536), (1568, 3872), (1536, 1568))
W_IN_COLS = 6944


def _w_in_layout_kernel(w_ref, o_ref):
    dst = 0
    for lo, hi in W_IN_SEGMENTS:
        o_ref[dst:dst + hi - lo, :] = w_ref[lo:hi, :].astype(BF16)
        dst += hi - lo
    o_ref[dst:, :] = jnp.zeros((Z_W - dst, o_ref.shape[1]), BF16)


def _w_in_layout(w_in_t):
    tk = 256
    return pl.pallas_call(
        _w_in_layout_kernel,
        grid=(DEPTH, D_MODEL // tk),
        in_specs=[pl.BlockSpec((None, W_IN_COLS, tk), lambda l, r: (l, 0, r))],
        out_specs=pl.BlockSpec((None, Z_W, tk), lambda l, r: (l, 0, r)),
        out_shape=jax.ShapeDtypeStruct((DEPTH, Z_W, D_MODEL), BF16),
        compiler_params=_cparams(("arbitrary", "arbitrary")),
        name="w_in_layout",
    )(w_in_t)


def _token_rows(x_refs, ctx):
    if len(x_refs) == 1:
        return x_refs[0][...]
    return jnp.where(ctx, x_refs[0][...], x_refs[1][...])


def _token_specs(x_parts, tm, row):
    if len(x_parts) == 1:
        return [pl.BlockSpec((tm, D_MODEL), lambda s: (row(s), 0))]
    ctx_blocks = T_CTX // tm
    return [pl.BlockSpec((tm, D_MODEL), lambda s: (jnp.minimum(row(s), ctx_blocks - 1), 0)),
            pl.BlockSpec((tm, D_MODEL), lambda s: (jnp.maximum(row(s) - ctx_blocks, 0), 0))]


def _inproj_kernel(*refs):
    *x_refs, g_ref, mod_ref, w_ref, z_ref = refs
    ctx = pl.program_id(0) < T_CTX // z_ref.shape[0]
    h = _norm_mod(_token_rows(x_refs, ctx), g_ref[...],
                  mod_ref[:, 0:D_MODEL], mod_ref[:, D_MODEL:2 * D_MODEL])
    z_ref[...] = _nt_dot(h.astype(BF16), w_ref[...]).astype(z_ref.dtype)


def _inproj(x_parts, norm_w, mod4, w_in_p, l):
    tm = 512
    return pl.pallas_call(
        _inproj_kernel,
        grid=(T_ALL // tm,),
        in_specs=_token_specs(x_parts, tm, lambda i: i) + [
            pl.BlockSpec((None, 1, D_MODEL), lambda i: (l, 0, 0)),
            pl.BlockSpec((None, None, 1, 6 * D_MODEL), lambda i: (l, _group_of_rows(i, tm), 0, 0)),
            pl.BlockSpec((None, Z_W, D_MODEL), lambda i: (l, 0, 0),
                         pipeline_mode=pl.Buffered(1)),
        ],
        out_specs=pl.BlockSpec((tm, Z_W), lambda i: (i, 0)),
        out_shape=jax.ShapeDtypeStruct((T_ALL, Z_W), BF16),
        compiler_params=_cparams(("arbitrary",)),
        name="inproj",
    )(*x_parts, norm_w, mod4, w_in_p)


def _head_norm_pairs(z_ref, groups):
    same_head = (lax.broadcasted_iota(jnp.int32, (128, 128), 0) // HEAD_DIM
                 == lax.broadcasted_iota(jnp.int32, (128, 128), 1) // HEAD_DIM)
    ones = jnp.where(same_head, 1.0, 0.0).astype(BF16)
    tiles = [[(z_ref[:, col + 128 * p: col + 128 * (p + 1)].astype(F32), w_ref[:, 128 * p:128 * (p + 1)])
              for p in range(width // 128)] for col, width, w_ref in groups]
    sums = [[_dot((x * x).astype(BF16), ones) for x, _ in group] for group in tiles]
    return [[x * lax.rsqrt(ss * (1.0 / HEAD_DIM) + EPS) * w for (x, w), ss in zip(group, gs)]
            for group, gs in zip(tiles, sums)]


def _rope(x, cos, sin_signed):
    n = x.shape[-1]
    lane = lax.broadcasted_iota(jnp.int32, (1, n), 1)
    first = (lane % 32) < 16
    partner = jnp.where(first, pltpu.roll(x, n - 16, 1), pltpu.roll(x, 16, 1))
    return x * cos + partner * sin_signed


def _prep_kernel(z_ref, wqb_ref, wkb_ref, wqc_ref, wkc_ref, cos_ref, sin_ref,
                 qb_ref, kb_ref, vb_ref, qc_ref, kc_ref, vc_ref):
    qscale = QK_SCALE

    qb, kb, qc, kc = _head_norm_pairs(z_ref, [(A_BQ, 512, wqb_ref), (A_BK, 128, wkb_ref),
                                              (A_CQ, 512, wqc_ref), (A_CK, 512, wkc_ref)])

    vb_ref[...] = z_ref[:, A_BV:A_BV + 128].astype(F32).T.astype(BF16)
    vc_ref[...] = z_ref[:, A_CV:A_CV + 512]
    for p in range(4):
        qc_ref[:, 128 * p:128 * (p + 1)] = (qc[p] * qscale).astype(BF16)
        kc_ref[:, 128 * p:128 * (p + 1)] = kc[p].astype(BF16)

    cos = cos_ref[...]
    sin = sin_ref[...]
    for p in range(4):
        qb_ref[128 * p:128 * (p + 1), :] = (_rope(qb[p], cos, sin) * qscale).T.astype(BF16)
    kb_ref[...] = _rope(kb[0], cos, sin).astype(BF16)


def _attn_prep(z, wqb, wkb, wqc, wkc, cos_t, sin_t):
    tm = 512
    lat_blocks = LAT_LEN // tm
    rope_idx = lambda i: (i % lat_blocks, 0)
    row = lambda w: pl.BlockSpec((tm, w), lambda i: (i, 0))
    col = lambda w: pl.BlockSpec((w, tm), lambda i: (0, i))
    cst = lambda w: pl.BlockSpec((1, w), lambda i: (0, 0))
    sds = lambda w: jax.ShapeDtypeStruct((T_LAT, w), BF16)
    sds_t = lambda w: jax.ShapeDtypeStruct((w, T_LAT), BF16)
    return pl.pallas_call(
        _prep_kernel,
        grid=(T_LAT // tm,),
        in_specs=[
            pl.BlockSpec((tm, ATT_W), lambda i: (T_CTX // tm + i, Z_ATT // ATT_W)),
            cst(512), cst(128), cst(512), cst(512),
            pl.BlockSpec((tm, 128), rope_idx),
            pl.BlockSpec((tm, 128), rope_idx),
        ],
        out_specs=[col(512), row(128), col(128), row(512), row(512), row(512)],
        out_shape=[sds_t(512), sds(128), sds_t(128), sds(512), sds(512), sds(512)],
        compiler_params=_cparams(("arbitrary",)),
        name="attn_prep",
    )(z, wqb, wkb, wqc, wkc, cos_t, sin_t)


def _half_rows(t, half):
    z = jnp.zeros((HEAD_DIM, t.shape[1]), t.dtype)
    return jnp.concatenate([t[:HEAD_DIM], z] if half == 0 else [z, t[HEAD_DIM:]], axis=0)


def _ctx_attn_kernel(sink_ref, z_ref, wqb_ref, wkb_ref, wqc_ref, wkc_ref, *refs, layer):
    ob_ref, oc_ref = refs[-6:-4]
    cache_refs = refs[-4:]
    if len(refs) == 6:
        for ref in cache_refs:
            ref[...] = jnp.zeros(ref.shape, F32)
        cache_refs = [ref.at[layer] for ref in cache_refs]
    kbf_ref, vbf_ref, kcf_ref, vcf_ref = cache_refs
    vbf_ref[...] = z_ref[:, A_BV:A_BV + 128].astype(F32)
    vcf_ref[...] = z_ref[:, A_CV:A_CV + 512].astype(F32)
    n = z_ref.shape[0]
    qscale = QK_SCALE

    def pair(col, p):
        return z_ref[:, col + 128 * p: col + 128 * (p + 1)]

    qb_n, kb_n, qc_n, kc_n = _head_norm_pairs(z_ref, [(A_BQ, 512, wqb_ref), (A_BK, 128, wkb_ref),
                                                      (A_CQ, 512, wqc_ref), (A_CK, 512, wkc_ref)])

    def q_t(tile):
        return (tile * qscale).T.astype(BF16)

    def v_t(col, p):
        return pair(col, p).astype(F32).T.astype(BF16)

    kb = kb_n[0]
    kbf_ref[...] = kb
    kb = kb.astype(BF16)
    qb_t = [q_t(tile) for tile in qb_n]
    scores = []
    for kvh in range(SWA_KV_HEADS):
        placed = []
        for h in range(SWA_GROUP * kvh, SWA_GROUP * (kvh + 1)):
            rows = qb_t[h // 2][HEAD_DIM * (h % 2):HEAD_DIM * (h % 2 + 1)]
            z = jnp.zeros_like(rows)
            placed.append(jnp.concatenate([rows, z] if kvh == 0 else [z, rows], axis=0))
        scores.append(_dot(kb, jnp.concatenate(placed, axis=1)))
    for p in range(NA_HEADS // 2):
        kc = kc_n[p]
        kcf_ref[:, 128 * p:128 * (p + 1)] = kc
        qc_t = q_t(qc_n[p])
        q2 = jnp.concatenate([_half_rows(qc_t, 0), _half_rows(qc_t, 1)], axis=1)
        scores.append(_dot(kc.astype(BF16), q2))
    s_t = jnp.concatenate(scores, axis=1)

    sink = jnp.concatenate(
        [jnp.full((1, n), sink_ref[layer, h] * LOG2E, F32) for h in range(SWA_Q_HEADS)]
        + [jnp.full((1, n * NA_HEADS), -jnp.inf, F32)], axis=1)
    mx = jnp.maximum(s_t.max(axis=0, keepdims=True), sink)
    p_t = jnp.exp2(s_t - mx)
    inv = 1.0 / (p_t.sum(axis=0, keepdims=True) + jnp.exp2(sink - mx))
    p_t = p_t.astype(BF16)

    nb = SWA_Q_HEADS * n
    o = _dot(v_t(A_BV, 0), p_t[:, :nb]) * inv[:, :nb]
    outs = [o[HEAD_DIM * (h // SWA_GROUP):HEAD_DIM * (h // SWA_GROUP + 1), n * h:n * (h + 1)]
            for h in range(SWA_Q_HEADS)]
    ob_ref[...] = jnp.concatenate(outs, axis=0).T.astype(BF16)
    outs = []
    for p in range(NA_HEADS // 2):
        cols = slice(nb + 2 * n * p, nb + 2 * n * (p + 1))
        o = _dot(v_t(A_CV, p), p_t[:, cols]) * inv[:, cols]
        outs += [o[:HEAD_DIM, :n], o[HEAD_DIM:, n:]]
    oc_ref[...] = jnp.concatenate(outs, axis=0).T.astype(BF16)


def _ctx_attention(sink, z, wqb, wkb, wqc, wkc, caches, l):
    blk = lambda w, j=0: pl.BlockSpec((CTX_LEN, w), lambda b: (b, j))
    cst = lambda w: pl.BlockSpec((1, w), lambda b: (0, 0))
    sds = lambda w: jax.ShapeDtypeStruct((T_CTX, w), BF16)
    cache_w = (128, 128, 512, 512)
    if caches is None:
        cache_blk = [pl.BlockSpec((None, DEPTH, CTX_LEN, w), lambda b: (b, 0, 0, 0)) for w in cache_w]
    else:
        cache_blk = [pl.BlockSpec((None, None, CTX_LEN, w), lambda b: (b, l, 0, 0)) for w in cache_w]
    cache_sds = [jax.ShapeDtypeStruct((N_CTX_SEQ, DEPTH, CTX_LEN, w), F32) for w in cache_w]
    carried = [] if caches is None else list(caches)
    n_in = 6
    return pl.pallas_call(
        functools.partial(_ctx_attn_kernel, layer=l),
        grid=(N_CTX_SEQ,),
        in_specs=[pl.BlockSpec(memory_space=pltpu.SMEM), blk(ATT_W, Z_ATT // ATT_W),
                  cst(512), cst(128), cst(512), cst(512)]
                 + [pl.BlockSpec(memory_space=pl.ANY)] * len(carried),
        out_specs=[blk(512), blk(512)] + cache_blk,
        out_shape=[sds(512), sds(512)] + cache_sds,
        input_output_aliases={n_in + j: 2 + j for j in range(len(carried))},
        compiler_params=_cparams(("arbitrary",)),
        name="ctx_attention",
    )(sink, z, wqb, wkb, wqc, wkc, *carried)


SWA_QBLK = 128
SWA_SPAN = SWA_QBLK + 2 * SWA_WINDOW


def _swa_kernel(sink_ref, qt_ref, k_ref, vt_ref, kc_ref, vc_ref, o_ref, kctx_scr, vctx_scr, *, layer):
    i = pl.program_id(1)
    nq = SWA_QBLK

    @pl.when(i == 0)
    def _():
        kctx_scr[...] = kc_ref[...].T.astype(BF16)
        vctx_scr[...] = vc_ref[...].astype(BF16)

    start = pl.multiple_of(jnp.clip(nq * (i - 1), 0, LAT_LEN - SWA_SPAN), nq)
    kwin = k_ref[pl.ds(start, SWA_SPAN), :]
    vwin_t = vt_ref[:, pl.ds(start, SWA_SPAN)]
    kctx = kctx_scr[...]
    s_lat, s_ctx = [], []
    for kvh in range(SWA_KV_HEADS):
        placed = []
        for h in range(SWA_GROUP * kvh, SWA_GROUP * (kvh + 1)):
            rows = qt_ref[HEAD_DIM * h:HEAD_DIM * (h + 1), :]
            z = jnp.zeros_like(rows)
            placed.append(jnp.concatenate([rows, z] if kvh == 0 else [z, rows], axis=0))
        q4 = jnp.concatenate(placed, axis=1)
        s_lat.append(_dot(kwin, q4))
        s_ctx.append(_dot(kctx, q4))
    s_lat = jnp.concatenate(s_lat, axis=1)
    s_ctx = jnp.concatenate(s_ctx, axis=1)

    kpos = start + lax.broadcasted_iota(jnp.int32, (SWA_SPAN, nq), 0)
    qpos = nq * i + lax.broadcasted_iota(jnp.int32, (SWA_SPAN, nq), 1)
    band = jnp.where(jnp.abs(kpos - qpos) <= SWA_WINDOW, 0.0, -jnp.inf)
    s_lat = s_lat + jnp.concatenate([band] * SWA_Q_HEADS, axis=1)
    sink = jnp.concatenate(
        [jnp.full((1, nq), sink_ref[layer, h] * LOG2E, F32) for h in range(SWA_Q_HEADS)], axis=1)
    mx = jnp.maximum(jnp.maximum(s_lat.max(axis=0, keepdims=True),
                                 s_ctx.max(axis=0, keepdims=True)), sink)
    p_lat = jnp.exp2(s_lat - mx)
    p_ctx = jnp.exp2(s_ctx - mx)
    inv = 1.0 / (p_lat.sum(axis=0, keepdims=True) + p_ctx.sum(axis=0, keepdims=True)
                 + jnp.exp2(sink - mx))
    o = (_dot(vwin_t, p_lat.astype(BF16)) + _dot(vctx_scr[...], p_ctx.astype(BF16))) * inv
    outs = [o[HEAD_DIM * (h // SWA_GROUP):HEAD_DIM * (h // SWA_GROUP + 1), nq * h:nq * (h + 1)]
            for h in range(SWA_Q_HEADS)]
    o_ref[...] = jnp.concatenate(outs, axis=0).T.astype(BF16)


def _swa_attention(sink, qb_t, kb, vb_t, cache_k, cache_v, l):
    nq = LAT_LEN // SWA_QBLK
    cache = pl.BlockSpec((None, None, 128, PAST_LEN), lambda b, i: (b, l, 0, 0))
    return pl.pallas_call(
        functools.partial(_swa_kernel, layer=l),
        grid=(N_LAT_SEQ, nq),
        in_specs=[pl.BlockSpec(memory_space=pltpu.SMEM),
                  pl.BlockSpec((512, SWA_QBLK), lambda b, i: (0, b * nq + i)),
                  pl.BlockSpec((LAT_LEN, 128), lambda b, i: (b, 0)),
                  pl.BlockSpec((128, LAT_LEN), lambda b, i: (0, b)),
                  cache, cache],
        out_specs=pl.BlockSpec((SWA_QBLK, 512), lambda b, i: (b * nq + i, 0)),
        out_shape=jax.ShapeDtypeStruct((T_LAT, 512), BF16),
        scratch_shapes=[pltpu.VMEM((PAST_LEN, 128), BF16), pltpu.VMEM((128, PAST_LEN), BF16)],
        compiler_params=_cparams(("arbitrary", "arbitrary")),
        name="swa_attention",
    )(sink, qb_t, kb, vb_t, cache_k, cache_v)


LAT_ROWS = LAT_LEN // GRID_W
NA_WIN_ROWS = min(NA_ROWS, LAT_ROWS)
NA_KEYS = NA_WIN_ROWS * GRID_W


def _na_kernel(q_ref, k_ref, v_ref, kc_ref, vc_ref, rp_ref, o_ref, kctx_scr, vctx_scr, bias_ref):
    r = pl.program_id(1)
    low = lax.broadcasted_iota(jnp.int32, (1, 128), 1) < HEAD_DIM

    @pl.when(r == 0)
    def _():
        kctx_scr[...] = kc_ref[...].astype(BF16)
        vctx_scr[...] = vc_ref[...].astype(BF16)
        cq = lax.broadcasted_iota(jnp.int32, (GRID_W, 128), 0)
        ck = lax.broadcasted_iota(jnp.int32, (GRID_W, 128), 1) % GRID_W
        cs = jnp.clip(cq - NA_COLS // 2, 0, GRID_W - NA_COLS)
        window = jnp.where((ck >= cs) & (ck < cs + NA_COLS), 0.0, -jnp.inf)
        for h in range(NA_HEADS):
            rows = [jnp.broadcast_to(rp_ref[h, dr:dr + 1, :], (GRID_W, 128))
                    for dr in range(2 * NA_ROWS - 1)]
            left = [pltpu.roll(x, 0, 1, stride=1, stride_axis=0) for x in rows[:-1]]
            right = [pltpu.roll(x, GRID_W, 1, stride=1, stride_axis=0) for x in rows[1:]]
            for j in range(2 * NA_ROWS - 2):
                bias_ref[h, j] = jnp.where(low, left[j], right[j]) * LOG2E + window

    rs = jnp.clip(r - NA_ROWS // 2, 0, LAT_ROWS - NA_WIN_ROWS)
    start = pl.multiple_of(rs * GRID_W, GRID_W)
    base = rs - r + NA_ROWS - 1
    npair = NA_HEADS // 2
    s_lat, s_ctx = [], []
    for p in range(npair):
        sl = slice(128 * p, 128 * (p + 1))
        q = q_ref[:, sl]
        zero = jnp.zeros_like(q)
        q2 = jnp.concatenate([jnp.where(low, q, zero), jnp.where(low, zero, q)], axis=0)
        bias = jnp.concatenate(
            [jnp.concatenate([bias_ref[2 * p + hh, base + 2 * w] for w in range(NA_WIN_ROWS // 2)],
                             axis=1) for hh in range(2)], axis=0)
        s_lat.append(_nt_dot(q2, k_ref[pl.ds(start, NA_KEYS), sl]) + bias)
        s_ctx.append(_dot(q2, kctx_scr[sl, :]))
    s_lat = jnp.concatenate(s_lat, axis=0)
    s_ctx = jnp.concatenate(s_ctx, axis=0)
    mx = jnp.maximum(s_lat.max(axis=-1, keepdims=True), s_ctx.max(axis=-1, keepdims=True))
    p_lat = jnp.exp2(s_lat - mx)
    p_ctx = jnp.exp2(s_ctx - mx)
    inv = 1.0 / (p_lat.sum(axis=-1, keepdims=True) + p_ctx.sum(axis=-1, keepdims=True))
    p_lat = p_lat.astype(BF16)
    p_ctx = p_ctx.astype(BF16)
    tiles = []
    for p in range(npair):
        sl = slice(128 * p, 128 * (p + 1))
        rows = slice(2 * GRID_W * p, 2 * GRID_W * (p + 1))
        o2 = (_dot(p_lat[rows], v_ref[pl.ds(start, NA_KEYS), sl])
              + _nt_dot(p_ctx[rows], vctx_scr[sl, :])) * inv[rows]
        tiles.append(jnp.where(low, o2[:GRID_W], o2[GRID_W:]))
    o_ref[...] = jnp.concatenate(tiles, axis=1).astype(BF16)


def _na_attention(qc, kc, vc, cache_k, cache_v, rp_cyc, l):
    kv = pl.BlockSpec((LAT_LEN, 512), lambda b, r: (b, 0))
    cache = pl.BlockSpec((None, None, 512, PAST_LEN), lambda b, r: (b, l, 0, 0))
    return pl.pallas_call(
        _na_kernel,
        grid=(N_LAT_SEQ, LAT_ROWS),
        in_specs=[pl.BlockSpec((GRID_W, 512), lambda b, r: (b * LAT_ROWS + r, 0)),
                  kv, kv, cache, cache,
                  pl.BlockSpec((None, NA_HEADS, 2 * NA_ROWS - 1, 128), lambda b, r: (l, 0, 0, 0))],
        out_specs=pl.BlockSpec((GRID_W, 512), lambda b, r: (b * LAT_ROWS + r, 0)),
        out_shape=jax.ShapeDtypeStruct((T_LAT, 512), BF16),
        scratch_shapes=[pltpu.VMEM((512, PAST_LEN), BF16), pltpu.VMEM((512, PAST_LEN), BF16),
                        pltpu.VMEM((NA_HEADS, 2 * NA_ROWS - 2, GRID_W, 128), F32)],
        compiler_params=_cparams(("arbitrary", "arbitrary")),
        name="na_attention",
    )(qc, kc, vc, cache_k, cache_v, rp_cyc)


def _na_table_rows(rpb):
    pad = jnp.zeros(rpb.shape[:-1] + (128 - (2 * NA_COLS - 1),), F32)
    return jnp.concatenate([rpb[..., NA_COLS - 1:], pad, rpb[..., :NA_COLS - 1]], axis=-1)


def _gla_constants():
    c = GLA_C
    t = np.arange(c)[:, None]
    u = np.arange(c)[None, :]
    blocks = []
    for k in range(GLA_LEVELS):
        b = 1 << k
        m = ((t >> k) | 1) * b - 1
        query = ((t >> k) & 1) == 1
        blocks.append(np.where(query, (u > m) & (u <= t), (u > t) & (u <= m)))
    blocks.append(u <= t)
    blocks.append(u > t)
    blocks.append(np.ones((16, c), bool))
    fwd = np.concatenate(blocks, axis=0).astype(np.float32)
    bwd_blocks = [blk[::-1, ::-1] for blk in blocks]
    bwd = np.concatenate(bwd_blocks, axis=0).astype(np.float32)
    seg = np.stack([fwd, bwd])
    seg = np.concatenate([seg, seg], axis=-1)

    s = np.arange(c)[None, :]
    x = t ^ s
    lev = np.where(x == 0, GLA_LEVELS, np.floor(np.log2(np.maximum(x, 1))).astype(np.int64))
    lev_f = np.where(s <= t, lev, -1)
    lev_b = np.where(s >= t, lev, -1)
    levmap = np.stack([np.tile(lev_f, (1, GLA_HEADS)), np.tile(lev_b, (1, GLA_HEADS))])
    return seg, levmap.astype(np.int32)


def _gla_state_init(d, cc, s0_ref, st_ref):
    c = GLA_C
    ctx_chunks = T_CTX // c
    is_lat = cc >= ctx_chunks
    per_seq = jnp.where(is_lat, LAT_LEN // c, CTX_LEN // c)
    pos = jnp.where(is_lat, cc - ctx_chunks, cc) % per_seq
    first = (pos == 0) if d == 0 else (pos == per_seq - 1)

    @pl.when(first)
    def _():
        for h in range(GLA_HEADS):
            blk = jnp.where(is_lat, s0_ref[h], 0.0)
            row = [jnp.zeros((GLA_DV, GLA_DK), F32)] * GLA_HEADS
            row[h] = blk
            st_ref[GLA_DV * h:GLA_DV * (h + 1), :] = jnp.concatenate(row, axis=1)


def _gla_chunks(chains):
    c = GLA_C
    lane_head = lax.broadcasted_iota(jnp.int32, (1, GLA_HEADS * GLA_DK), 1) // GLA_DK
    row_head = lax.broadcasted_iota(jnp.int32, (GLA_HEADS * GLA_DV, 1), 0) // GLA_DV
    head_sel = [jnp.where(lane_head == h, 1.0, 0.0).astype(BF16) for h in range(GLA_HEADS)]

    def operands(q, k, e=None):
        if e is not None:
            q, k = q * e, k * e
        return q, jnp.concatenate([k * head_sel[h] for h in range(GLA_HEADS)], axis=0)

    segs = []
    for (_, _, _, lr_ref, w2_ref, ba_ref, seg_ref, *_) in chains:
        x = _dot(lr_ref[...], w2_ref[...]) + ba_ref[...]
        la = (jnp.minimum(x, 0.0) - jnp.log(1.0 + jnp.exp(-jnp.abs(x)))) * (1.0 / GLA_TAU)
        la_hi = la.astype(BF16)
        la_lo = (la - la_hi.astype(F32)).astype(BF16)
        segs.append(_dot(seg_ref[...], jnp.concatenate([la_hi, la_lo], axis=0)))

    def decay(seg, block):
        return jnp.exp(seg[c * block:c * (block + 1)]).astype(BF16)

    qs = [ch[0][...] * (GLA_DK ** -0.5) for ch in chains]
    ks = [ch[1][...] for ch in chains]
    order = [GLA_LEVELS] + list(range(GLA_LEVELS))
    prepare = lambda lvl: [operands(qs[n], ks[n], None if lvl == GLA_LEVELS else decay(segs[n], lvl))
                           for n in range(len(chains))]
    attn = [None] * len(chains)
    ops = prepare(order[0])
    for i, lvl in enumerate(order):
        scores = [_nt_dot(q, kbd) for q, kbd in ops]
        if i + 1 < len(order):
            ops = prepare(order[i + 1])
        for n, ch in enumerate(chains):
            attn[n] = jnp.where(ch[7][...] == lvl, scores[n], 0.0 if attn[n] is None else attn[n])

    for n, (_, _, v_ref, _, _, _, _, _, o_ref, sfin_ref, st_ref) in enumerate(chains):
        seg = segs[n]
        v = v_ref[...]
        a = attn[n].astype(BF16)
        tot = seg[c * (GLA_LEVELS + 2):c * (GLA_LEVELS + 2) + 1]
        q_in = qs[n] * decay(seg, GLA_LEVELS)
        k_in = ks[n] * decay(seg, GLA_LEVELS + 1)
        state = st_ref[...]
        o_inter = _nt_dot(q_in, state.astype(BF16))
        for h in range(GLA_HEADS):
            sl = slice(GLA_DV * h, GLA_DV * (h + 1))
            o_ref[:, sl] = o_inter[:, sl] + _dot(a[:, c * h:c * (h + 1)], v[:, sl])
        upd = _tn_dot(v, k_in)
        new_state = state * jnp.exp(tot) + jnp.where(row_head == lane_head, upd, 0.0)
        st_ref[...] = new_state
        for h in range(GLA_HEADS):
            sfin_ref[h] = new_state[GLA_DV * h:GLA_DV * (h + 1), GLA_DK * h:GLA_DK * (h + 1)].T


def _gla_kernel(qf, kf, vf, lrf, qb, kb, vb, lrb, w2_ref, ba_ref, seg_ref, lev_ref, s0f, s0b,
                of_ref, ob_ref, sff_ref, sfb_ref, stf, stb):
    step = pl.program_id(0)
    last = pl.num_programs(0) - 1
    _gla_state_init(0, step, s0f, stf)
    _gla_state_init(1, last - step, s0b, stb)
    _gla_chunks([
        (qf, kf, vf, lrf, w2_ref.at[0], ba_ref.at[0], seg_ref.at[0], lev_ref.at[0], of_ref, sff_ref, stf),
        (qb, kb, vb, lrb, w2_ref.at[1], ba_ref.at[1], seg_ref.at[1], lev_ref.at[1], ob_ref, sfb_ref, stb),
    ])


def _gla(z, w2p, ba, seg, levmap, s0t, l):
    c = GLA_C
    nchunk = T_ALL // c
    ctx_chunks = T_CTX // c
    nseq = N_CTX_SEQ + N_LAT_SEQ
    fwd = lambda s: s
    bwd = lambda s: nchunk - 1 - s

    def seq_of(cc):
        return jnp.where(cc < ctx_chunks, cc // (CTX_LEN // c),
                         N_CTX_SEQ + (cc - ctx_chunks) // (LAT_LEN // c))

    def lat_of(cc):
        return jnp.clip((cc - ctx_chunks) // (LAT_LEN // c), 0, N_LAT_SEQ - 1)

    def chunk_specs(pos):
        return [pl.BlockSpec((c, 256), lambda s: (pos(s), Z_AQ // 256)),
                pl.BlockSpec((c, 256), lambda s: (pos(s), Z_AK // 256)),
                pl.BlockSpec((c, 512), lambda s: (pos(s), Z_AV // 512)),
                pl.BlockSpec((c, 128), lambda s: (pos(s), Z_LR // 128))]

    def s0_spec(d, pos):
        return pl.BlockSpec((None, None, None, GLA_HEADS, GLA_DV, GLA_DK),
                            lambda s: (lat_of(pos(s)), l, d, 0, 0, 0))

    def sfin_spec(pos):
        return pl.BlockSpec((None, GLA_HEADS, GLA_DK, GLA_DV), lambda s: (seq_of(pos(s)), 0, 0, 0))

    state_sds = jax.ShapeDtypeStruct((nseq, GLA_HEADS, GLA_DK, GLA_DV), F32)
    out_sds = jax.ShapeDtypeStruct((T_ALL, 512), F32)
    state_scr = pltpu.VMEM((GLA_HEADS * GLA_DV, GLA_HEADS * GLA_DK), F32)
    return pl.pallas_call(
        _gla_kernel,
        grid=(nchunk,),
        in_specs=chunk_specs(fwd) + chunk_specs(bwd) + [
            pl.BlockSpec((None, 2, 128, 256), lambda s: (l, 0, 0, 0)),
            pl.BlockSpec((None, 2, 1, 256), lambda s: (l, 0, 0, 0)),
            pl.BlockSpec((2, GLA_GROWS, 2 * c), lambda s: (0, 0, 0)),
            pl.BlockSpec((2, c, GLA_HEADS * c), lambda s: (0, 0, 0)),
            s0_spec(0, fwd), s0_spec(1, bwd),
        ],
        out_specs=[pl.BlockSpec((c, 512), lambda s: (fwd(s), 0)),
                   pl.BlockSpec((c, 512), lambda s: (bwd(s), 0)),
                   sfin_spec(fwd), sfin_spec(bwd)],
        out_shape=[out_sds, out_sds, state_sds, state_sds],
        scratch_shapes=[state_scr, state_scr],
        compiler_params=_cparams(("arbitrary",)),
        name="gla",
    )(z, z, z, z, z, z, z, z, w2p, ba, seg, levmap, s0t, s0t)


def _cast_kernel(*refs):
    n = len(refs) // 2
    for src, dst in zip(refs[:n], refs[n:]):
        dst[...] = src[...].astype(dst.dtype)


def _cast_bf16(*weights):
    spec = lambda w: pl.BlockSpec((None,) + w.shape[1:], lambda l: (l, 0, 0))
    return pl.pallas_call(
        _cast_kernel,
        grid=(DEPTH,),
        in_specs=[spec(w) for w in weights],
        out_specs=[spec(w) for w in weights],
        out_shape=[jax.ShapeDtypeStruct(w.shape, BF16) for w in weights],
        compiler_params=_cparams(("arbitrary",)),
        name="cast_bf16",
    )(*weights)


def _post_mixer_kernel(*refs, nk, n_x, n_y):
    x_refs, refs = refs[:n_x], refs[n_x:]
    (of_ref, obk_ref, ar_ref, gates_ref, bc_ref, bl_ref, cc_ref, cl_ref, wpa_ref, wpb_ref, wpc_ref,
     wo_ref, gn_ref, g2_ref, mod_ref, w1_ref, w2_ref) = refs[:17]
    y_refs = refs[17:17 + n_y]
    w1_scr, w2_scr, h_scr, x1_scr, acc_scr = refs[17 + n_y:]
    s = pl.program_id(0)
    d = D_MODEL
    tm = of_ref.shape[0]
    th = w1_ref.shape[1]
    ctx = jnp.maximum(s - (nk - 1), 0) < T_CTX // tm

    def put_y(val):
        if n_y == 1:
            y_refs[0][...] = val
        else:
            @pl.when(ctx)
            def _():
                y_refs[0][...] = val

            @pl.when(jnp.logical_not(ctx))
            def _():
                y_refs[1][...] = val

    def merged_residual():
        o = of_ref[...] + obk_ref[...]
        heads = []
        for h in range(GLA_HEADS):
            oh = o[:, GLA_DV * h:GLA_DV * (h + 1)]
            ms = jnp.mean(oh * oh, axis=-1, keepdims=True)
            heads.append(oh * lax.rsqrt(ms + EPS) * gn_ref[...])
        oa = (jnp.concatenate(heads, axis=1) * _silu(ar_ref[...].astype(F32))).astype(BF16)
        ob = jnp.where(ctx, bc_ref[...], bl_ref[...])
        oc = jnp.where(ctx, cc_ref[...], cl_ref[...])
        gate = lambda j: _sigmoid(gates_ref[:, j * d:(j + 1) * d].astype(F32))
        merged = (gate(0) * _dot(oa, wpa_ref[...]) + gate(1) * _dot(ob, wpb_ref[...])
                  + gate(2) * _dot(oc, wpc_ref[...]))
        return (_token_rows(x_refs, ctx)
                + mod_ref[:, 2 * d:3 * d] * _dot(merged.astype(BF16), wo_ref[...]))

    def normed(x1):
        return _norm_mod(x1, g2_ref[...], mod_ref[:, 3 * d:4 * d], mod_ref[:, 4 * d:5 * d]).astype(BF16)

    def act(u):
        return jnp.square(jnp.maximum(u, 0.0)).astype(BF16)

    @pl.when(s == 0)
    def _():
        x1 = merged_residual()
        x1_scr[...] = x1
        h_scr[...] = normed(x1)

    @pl.when(s < nk)
    def _():
        col = pl.multiple_of(s * th, th)
        w1 = w1_ref[...].astype(BF16)
        w2 = w2_ref[...].astype(BF16)
        w1_scr[:, pl.ds(col, th)] = w1
        w2_scr[pl.ds(col, th), :] = w2
        part = _dot(act(_dot(h_scr[...], w1)), w2)

        @pl.when(s == 0)
        def _():
            acc_scr[...] = part

        @pl.when(s > 0)
        def _():
            acc_scr[...] += part

        @pl.when(s == nk - 1)
        def _():
            put_y(x1_scr[...] + mod_ref[:, 5 * d:6 * d] * acc_scr[...])

    @pl.when(s >= nk)
    def _():
        x1 = merged_residual()
        out = _dot(act(_dot(normed(x1), w1_scr[...])), w2_scr[...])
        put_y(x1 + mod_ref[:, 5 * d:6 * d] * out)


def _post_mixer(x_parts, o_fwd, o_bwd, z, ob_ctx, ob_lat, oc_ctx, oc_lat, wpa, wpb, wpc, wo, gn, norm2_w,
                mod4, w1, w2, l, split_out):
    tm, th = 256, 512
    nk = MLP_HIDDEN // th
    ctx_blocks = T_CTX // tm
    row = lambda s: jnp.maximum(s - (nk - 1), 0)
    chunk = lambda s: jnp.minimum(s, nk - 1)
    ctx_idx = lambda s: (jnp.minimum(row(s), ctx_blocks - 1), 0)
    lat_idx = lambda s: (jnp.maximum(row(s) - ctx_blocks, 0), 0)
    rows = lambda w, j=0: pl.BlockSpec((tm, w), lambda s: (row(s), j))
    resident = lambda k: pl.BlockSpec((None, k, D_MODEL), lambda s: (l, 0, 0),
                                      pipeline_mode=pl.Buffered(1))
    if split_out:
        out_specs = [pl.BlockSpec((tm, D_MODEL), ctx_idx), pl.BlockSpec((tm, D_MODEL), lat_idx)]
        out_shape = [jax.ShapeDtypeStruct((T_CTX, D_MODEL), F32), jax.ShapeDtypeStruct((T_LAT, D_MODEL), F32)]
    else:
        out_specs = [rows(D_MODEL)]
        out_shape = [jax.ShapeDtypeStruct((T_ALL, D_MODEL), F32)]
    return pl.pallas_call(
        functools.partial(_post_mixer_kernel, nk=nk, n_x=len(x_parts), n_y=len(out_specs)),
        grid=(nk - 1 + T_ALL // tm,),
        in_specs=_token_specs(x_parts, tm, row) + [
            rows(512), rows(512), rows(512, Z_AR // 512), rows(3 * D_MODEL),
            pl.BlockSpec((tm, 512), ctx_idx), pl.BlockSpec((tm, 512), lat_idx),
            pl.BlockSpec((tm, 512), ctx_idx), pl.BlockSpec((tm, 512), lat_idx),
            resident(512), resident(512), resident(512), resident(D_MODEL),
            pl.BlockSpec((None, 1, GLA_DV), lambda s: (l, 0, 0)),
            pl.BlockSpec((None, 1, D_MODEL), lambda s: (l, 0, 0)),
            pl.BlockSpec((None, None, 1, 6 * D_MODEL),
                         lambda s: (l, _group_of_rows(row(s), tm), 0, 0)),
            pl.BlockSpec((None, D_MODEL, th), lambda s: (l, 0, chunk(s))),
            pl.BlockSpec((None, th, D_MODEL), lambda s: (l, chunk(s), 0)),
        ],
        out_specs=out_specs,
        out_shape=out_shape,
        scratch_shapes=[pltpu.VMEM((D_MODEL, MLP_HIDDEN), BF16), pltpu.VMEM((MLP_HIDDEN, D_MODEL), BF16),
                        pltpu.VMEM((tm, D_MODEL), BF16), pltpu.VMEM((tm, D_MODEL), F32),
                        pltpu.VMEM((tm, D_MODEL), F32)],
        compiler_params=_cparams(("arbitrary",)),
        name="post_mixer",
    )(*x_parts, o_fwd, o_bwd, z, z, ob_ctx, ob_lat, oc_ctx, oc_lat, wpa, wpb, wpc, wo, gn, norm2_w, mod4,
      w1, w2)


def _rope_tables():
    t = jnp.arange(LAT_LEN)
    row = (t // GRID_W).astype(F32)
    col = (t % GRID_W).astype(F32)
    nf = HEAD_DIM // 4
    inv_freq = ROPE_BASE ** (-jnp.arange(nf, dtype=F32) / nf)
    ang_r = row[:, None] * inv_freq[None, :]
    ang_c = col[:, None] * inv_freq[None, :]
    cos = jnp.concatenate([jnp.cos(ang_r)] * 2 + [jnp.cos(ang_c)] * 2, axis=1)
    sin = jnp.concatenate([-jnp.sin(ang_r), jnp.sin(ang_r), -jnp.sin(ang_c), jnp.sin(ang_c)], axis=1)
    return jnp.tile(cos, (1, 2)), jnp.tile(sin, (1, 2))


def kernel(x_prompt, x_sample, state_gla, cache_swa_k, cache_swa_v, cache_na_k, cache_na_v, c,
           c_ctx, w_mod, b_mod, norm1, norm2, w_in, w_a2_f, b_a_f, w_a2_b, b_a_b, gla_onorm,
           qn_swa, kn_swa, sink_swa, qn_na, kn_na, rpb_na, w_pa, w_pb, w_pc, w_o, w_fc1, w_fc2):
    d = D_MODEL
    x_parts = (x_prompt.reshape(T_CTX, d), x_sample.reshape(T_LAT, d))

    cond8 = jnp.zeros((8, d), F32).at[0].set(c_ctx).at[1:1 + N_LAT_SEQ].set(c)
    mod4 = _modulation(cond8, w_mod, b_mod).reshape(DEPTH, 8, 1, 6 * d)

    w_in_p = _w_in_layout(jnp.swapaxes(w_in, 1, 2))
    wpa, wpb, wpc, wo = _cast_bf16(w_pa, w_pb, w_pc, w_o)
    norm1r = norm1.reshape(DEPTH, 1, d)
    norm2r = norm2.reshape(DEPTH, 1, d)
    gnr = gla_onorm.reshape(DEPTH, 1, GLA_DV)
    w2p = jnp.zeros((DEPTH, 2, 128, GLA_HEADS * GLA_DK), F32)
    w2p = w2p.at[:, 0, 0:GLA_LOWRANK].set(w_a2_f).at[:, 1, GLA_LOWRANK:2 * GLA_LOWRANK].set(w_a2_b)
    w2p = w2p.astype(BF16)
    ba = jnp.stack([b_a_f, b_a_b], axis=1).reshape(DEPTH, 2, 1, GLA_HEADS * GLA_DK)
    seg_np, lev_np = _gla_constants()
    seg = jnp.asarray(seg_np, BF16)
    levmap = jnp.asarray(lev_np)
    s0t = jnp.swapaxes(state_gla, -1, -2)
    cos_t, sin_t = _rope_tables()
    rp_cyc = _na_table_rows(rpb_na)
    feat_major = lambda a: a.transpose(0, 1, 3, 4, 2).reshape(N_LAT_SEQ, DEPTH, -1, PAST_LEN)
    csk, csv, cnk, cnv = (feat_major(a) for a in (cache_swa_k, cache_swa_v, cache_na_k, cache_na_v))

    st_l = []
    caches = None
    for l in range(DEPTH):
        z = _inproj(x_parts, norm1r, mod4, w_in_p, l)
        wqb = jnp.tile(qn_swa[l], 8)[None, :]
        wkb = jnp.tile(kn_swa[l], 2)[None, :]
        wqc = jnp.tile(qn_na[l], 8)[None, :]
        wkc = jnp.tile(kn_na[l], 8)[None, :]
        qb, kb, vb, qc, kc, vc = _attn_prep(z, wqb, wkb, wqc, wkc, cos_t, sin_t)
        o_fwd, o_bwd, sfin_f, sfin_b = _gla(z, w2p, ba, seg, levmap, s0t, l)
        ob_ctx, oc_ctx, *caches = _ctx_attention(sink_swa, z, wqb, wkb, wqc, wkc, caches, l)
        ob_lat = _swa_attention(sink_swa, qb, kb, vb, csk, csv, l)
        oc_lat = _na_attention(qc, kc, vc, cnk, cnv, rp_cyc, l)
        x_parts = _post_mixer(x_parts, o_fwd, o_bwd, z, ob_ctx, ob_lat, oc_ctx, oc_lat, wpa, wpb, wpc, wo,
                              gnr, norm2r, mod4, w_fc1, w_fc2, l, split_out=(l == DEPTH - 1))

        st_l += [sfin_f[:N_CTX_SEQ], sfin_b[:N_CTX_SEQ]]

    y_prompt = x_parts[0].reshape(N_CTX_SEQ, CTX_LEN, d)
    y_sample = x_parts[1].reshape(N_LAT_SEQ, LAT_LEN, d)
    new_state = jnp.stack(st_l, axis=1).reshape(
        N_CTX_SEQ, DEPTH, 2, GLA_HEADS, GLA_DK, GLA_DV)

    swa_k, swa_v, na_k, na_v = caches
    kv_shape = lambda heads: (N_CTX_SEQ, DEPTH, CTX_LEN, heads, HEAD_DIM)
    return (y_prompt, y_sample, new_state,
            swa_k.reshape(kv_shape(SWA_KV_HEADS)), swa_v.reshape(kv_shape(SWA_KV_HEADS)),
            na_k.reshape(kv_shape(NA_HEADS)), na_v.reshape(kv_shape(NA_HEADS)))
```

```python
import functools

import numpy as np
import jax
import jax.numpy as jnp
from jax import lax
from jax.experimental import pallas as pl
from jax.experimental.pallas import tpu as pltpu

F32 = jnp.float32
BF16 = jnp.bfloat16

D_MODEL = 1024
DEPTH = 4
N_CTX_SEQ = 16
CTX_LEN = 256
N_LAT_SEQ = 2
LAT_LEN = 1024
PAST_LEN = 512
T_CTX = N_CTX_SEQ * CTX_LEN
T_LAT = N_LAT_SEQ * LAT_LEN
T_ALL = T_CTX + T_LAT
GRID_W = 64
HEAD_DIM = 64
GLA_HEADS = 4
GLA_DK = 64
GLA_DV = 128
GLA_LOWRANK = 16
GLA_TAU = 16.0
SWA_Q_HEADS = 8
SWA_KV_HEADS = 2
SWA_GROUP = 4
SWA_WINDOW = 128
NA_HEADS = 8
NA_ROWS = 8
NA_COLS = 16
MLP_HIDDEN = 4 * D_MODEL
ROPE_BASE = 10000.0
EPS = 1e-6
LOG2E = 1.4426950408889634
QK_SCALE = HEAD_DIM ** -0.5 * LOG2E

Z_GATES = 0
Z_GLA = 3072
Z_AQ, Z_AK, Z_AV, Z_AR = 3072, 3328, 3584, 4096
Z_ATT = 4608
ATT_W = 2304
A_BQ, A_BK, A_BV, A_CQ, A_CK, A_CV = 0, 512, 640, 768, 1280, 1792
Z_LR = 6912
Z_W = 7168

GLA_C = 128
GLA_LEVELS = 7
GLA_GROWS = (GLA_LEVELS + 2) * GLA_C + 16

VMEM_LIMIT = 56 * 1024 * 1024


def _cparams(sem):
    return pltpu.CompilerParams(dimension_semantics=sem, vmem_limit_bytes=VMEM_LIMIT)


def _sigmoid(x):
    return 1.0 / (1.0 + jnp.exp(-x))


def _silu(x):
    return x * _sigmoid(x)


def _nt_dot(a, b):
    return lax.dot_general(a, b, (((1,), (1,)), ((), ())), preferred_element_type=F32)


def _tn_dot(a, b):
    return lax.dot_general(a, b, (((0,), (0,)), ((), ())), preferred_element_type=F32)


def _dot(a, b):
    return jnp.dot(a, b, preferred_element_type=F32)


def _mod_kernel(cond_ref, w_ref, b_ref, o_ref):
    s = _silu(cond_ref[...]).astype(BF16)
    o_ref[...] = _dot(s, w_ref[...].astype(BF16)) + b_ref[...]


def _modulation(cond8, w_mod, b_mod):
    tn = 1024
    return pl.pallas_call(
        _mod_kernel,
        grid=(DEPTH, 6 * D_MODEL // tn),
        in_specs=[
            pl.BlockSpec((8, D_MODEL), lambda l, j: (0, 0)),
            pl.BlockSpec((None, D_MODEL, tn), lambda l, j: (l, 0, j)),
            pl.BlockSpec((None, 1, tn), lambda l, j: (l, 0, j)),
        ],
        out_specs=pl.BlockSpec((None, 8, tn), lambda l, j: (l, 0, j)),
        out_shape=jax.ShapeDtypeStruct((DEPTH, 8, 6 * D_MODEL), F32),
        compiler_params=_cparams(("arbitrary", "arbitrary")),
        name="modulation",
    )(cond8, w_mod, b_mod.reshape(DEPTH, 1, 6 * D_MODEL))


def _group_of_rows(row_block, rows_per_block):
    first = row_block * rows_per_block
    return jnp.maximum(first - T_CTX + LAT_LEN, 0) // LAT_LEN


def _norm_mod(x, g, shift, scale):
    ms = jnp.mean(x * x, axis=-1, keepdims=True)
    y = x * lax.rsqrt(ms + EPS) * g
    return y * (1.0 + scale) + shift


W_IN_SEGMENTS = ((3872, 6944), (0, 1536), (1568, 3872), (1536, 1568))
W_IN_COLS = 6944


def _w_in_layout_kernel(w_ref, o_ref):
    dst = 0
    for lo, hi in W_IN_SEGMENTS:
        o_ref[dst:dst + hi - lo, :] = w_ref[lo:hi, :].astype(BF16)
        dst += hi - lo
    o_ref[dst:, :] = jnp.zeros((Z_W - dst, o_ref.shape[1]), BF16)


def _w_in_layout(w_in_t):
    tk = 256
    return pl.pallas_call(
        _w_in_layout_kernel,
        grid=(DEPTH, D_MODEL // tk),
        in_specs=[pl.BlockSpec((None, W_IN_COLS, tk), lambda l, r: (l, 0, r))],
        out_specs=pl.BlockSpec((None, Z_W, tk), lambda l, r: (l, 0, r)),
        out_shape=jax.ShapeDtypeStruct((DEPTH, Z_W, D_MODEL), BF16),
        compiler_params=_cparams(("arbitrary", "arbitrary")),
        name="w_in_layout",
    )(w_in_t)


def _token_rows(x_refs, ctx):
    if len(x_refs) == 1:
        return x_refs[0][...]
    return jnp.where(ctx, x_refs[0][...], x_refs[1][...])


def _token_specs(x_parts, tm, row):
    if len(x_parts) == 1:
        return [pl.BlockSpec((tm, D_MODEL), lambda s: (row(s), 0))]
    ctx_blocks = T_CTX // tm
    return [pl.BlockSpec((tm, D_MODEL), lambda s: (jnp.minimum(row(s), ctx_blocks - 1), 0)),
            pl.BlockSpec((tm, D_MODEL), lambda s: (jnp.maximum(row(s) - ctx_blocks, 0), 0))]


def _inproj_kernel(*refs):
    *x_refs, g_ref, mod_ref, w_ref, z_ref = refs
    ctx = pl.program_id(0) < T_CTX // z_ref.shape[0]
    h = _norm_mod(_token_rows(x_refs, ctx), g_ref[...],
                  mod_ref[:, 0:D_MODEL], mod_ref[:, D_MODEL:2 * D_MODEL])
    z_ref[...] = _nt_dot(h.astype(BF16), w_ref[...]).astype(z_ref.dtype)


def _inproj(x_parts, norm_w, mod4, w_in_p, l):
    tm = 512
    return pl.pallas_call(
        _inproj_kernel,
        grid=(T_ALL // tm,),
        in_specs=_token_specs(x_parts, tm, lambda i: i) + [
            pl.BlockSpec((None, 1, D_MODEL), lambda i: (l, 0, 0)),
            pl.BlockSpec((None, None, 1, 6 * D_MODEL), lambda i: (l, _group_of_rows(i, tm), 0, 0)),
            pl.BlockSpec((None, Z_W, D_MODEL), lambda i: (l, 0, 0),
                         pipeline_mode=pl.Buffered(1)),
        ],
        out_specs=pl.BlockSpec((tm, Z_W), lambda i: (i, 0)),
        out_shape=jax.ShapeDtypeStruct((T_ALL, Z_W), BF16),
        compiler_params=_cparams(("arbitrary",)),
        name="inproj",
    )(*x_parts, norm_w, mod4, w_in_p)


def _head_norm_pairs(z_ref, groups):
    same_head = (lax.broadcasted_iota(jnp.int32, (128, 128), 0) // HEAD_DIM
                 == lax.broadcasted_iota(jnp.int32, (128, 128), 1) // HEAD_DIM)
    ones = jnp.where(same_head, 1.0, 0.0).astype(BF16)
    tiles = [[(z_ref[:, col + 128 * p: col + 128 * (p + 1)].astype(F32), w_ref[:, 128 * p:128 * (p + 1)])
              for p in range(width // 128)] for col, width, w_ref in groups]
    sums = [[_dot((x * x).astype(BF16), ones) for x, _ in group] for group in tiles]
    return [[x * lax.rsqrt(ss * (1.0 / HEAD_DIM) + EPS) * w for (x, w), ss in zip(group, gs)]
            for group, gs in zip(tiles, sums)]


def _rope(x, cos, sin_signed):
    n = x.shape[-1]
    lane = lax.broadcasted_iota(jnp.int32, (1, n), 1)
    first = (lane % 32) < 16
    partner = jnp.where(first, pltpu.roll(x, n - 16, 1), pltpu.roll(x, 16, 1))
    return x * cos + partner * sin_signed


def _prep_kernel(z_ref, wqb_ref, wkb_ref, wqc_ref, wkc_ref, cos_ref, sin_ref,
                 qb_ref, kb_ref, vb_ref, qc_ref, kc_ref, vc_ref):
    qscale = QK_SCALE

    qb, kb, qc, kc = _head_norm_pairs(z_ref, [(A_BQ, 512, wqb_ref), (A_BK, 128, wkb_ref),
                                              (A_CQ, 512, wqc_ref), (A_CK, 512, wkc_ref)])

    vb_ref[...] = z_ref[:, A_BV:A_BV + 128].astype(F32).T.astype(BF16)
    vc_ref[...] = z_ref[:, A_CV:A_CV + 512]
    for p in range(4):
        qc_ref[:, 128 * p:128 * (p + 1)] = (qc[p] * qscale).astype(BF16)
        kc_ref[:, 128 * p:128 * (p + 1)] = kc[p].astype(BF16)

    cos = cos_ref[...]
    sin = sin_ref[...]
    for p in range(4):
        qb_ref[128 * p:128 * (p + 1), :] = (_rope(qb[p], cos, sin) * qscale).T.astype(BF16)
    kb_ref[...] = _rope(kb[0], cos, sin).astype(BF16)


def _attn_prep(z, wqb, wkb, wqc, wkc, cos_t, sin_t):
    tm = 512
    lat_blocks = LAT_LEN // tm
    rope_idx = lambda i: (i % lat_blocks, 0)
    row = lambda w: pl.BlockSpec((tm, w), lambda i: (i, 0))
    col = lambda w: pl.BlockSpec((w, tm), lambda i: (0, i))
    cst = lambda w: pl.BlockSpec((1, w), lambda i: (0, 0))
    sds = lambda w: jax.ShapeDtypeStruct((T_LAT, w), BF16)
    sds_t = lambda w: jax.ShapeDtypeStruct((w, T_LAT), BF16)
    return pl.pallas_call(
        _prep_kernel,
        grid=(T_LAT // tm,),
        in_specs=[
            pl.BlockSpec((tm, ATT_W), lambda i: (T_CTX // tm + i, Z_ATT // ATT_W)),
            cst(512), cst(128), cst(512), cst(512),
            pl.BlockSpec((tm, 128), rope_idx),
            pl.BlockSpec((tm, 128), rope_idx),
        ],
        out_specs=[col(512), row(128), col(128), row(512), row(512), row(512)],
        out_shape=[sds_t(512), sds(128), sds_t(128), sds(512), sds(512), sds(512)],
        compiler_params=_cparams(("arbitrary",)),
        name="attn_prep",
    )(z, wqb, wkb, wqc, wkc, cos_t, sin_t)


def _half_rows(t, half):
    z = jnp.zeros((HEAD_DIM, t.shape[1]), t.dtype)
    return jnp.concatenate([t[:HEAD_DIM], z] if half == 0 else [z, t[HEAD_DIM:]], axis=0)


def _ctx_attn_kernel(sink_ref, z_ref, wqb_ref, wkb_ref, wqc_ref, wkc_ref, *refs, layer):
    ob_ref, oc_ref = refs[-6:-4]
    cache_refs = refs[-4:]
    if len(refs) == 6:
        for ref in cache_refs:
            ref[...] = jnp.zeros(ref.shape, F32)
        cache_refs = [ref.at[layer] for ref in cache_refs]
    kbf_ref, vbf_ref, kcf_ref, vcf_ref = cache_refs
    vbf_ref[...] = z_ref[:, A_BV:A_BV + 128].astype(F32)
    vcf_ref[...] = z_ref[:, A_CV:A_CV + 512].astype(F32)
    n = z_ref.shape[0]
    qscale = QK_SCALE

    def pair(col, p):
        return z_ref[:, col + 128 * p: col + 128 * (p + 1)]

    qb_n, kb_n, qc_n, kc_n = _head_norm_pairs(z_ref, [(A_BQ, 512, wqb_ref), (A_BK, 128, wkb_ref),
                                                      (A_CQ, 512, wqc_ref), (A_CK, 512, wkc_ref)])

    def q_t(tile):
        return (tile * qscale).T.astype(BF16)

    def v_t(col, p):
        return pair(col, p).astype(F32).T.astype(BF16)

    kb = kb_n[0]
    kbf_ref[...] = kb
    kb = kb.astype(BF16)
    qb_t = [q_t(tile) for tile in qb_n]
    scores = []
    for kvh in range(SWA_KV_HEADS):
        placed = []
        for h in range(SWA_GROUP * kvh, SWA_GROUP * (kvh + 1)):
            rows = qb_t[h // 2][HEAD_DIM * (h % 2):HEAD_DIM * (h % 2 + 1)]
            z = jnp.zeros_like(rows)
            placed.append(jnp.concatenate([rows, z] if kvh == 0 else [z, rows], axis=0))
        scores.append(_dot(kb, jnp.concatenate(placed, axis=1)))
    for p in range(NA_HEADS // 2):
        kc = kc_n[p]
        kcf_ref[:, 128 * p:128 * (p + 1)] = kc
        qc_t = q_t(qc_n[p])
        q2 = jnp.concatenate([_half_rows(qc_t, 0), _half_rows(qc_t, 1)], axis=1)
        scores.append(_dot(kc.astype(BF16), q2))
    s_t = jnp.concatenate(scores, axis=1)

    sink = jnp.concatenate(
        [jnp.full((1, n), sink_ref[layer, h] * LOG2E, F32) for h in range(SWA_Q_HEADS)]
        + [jnp.full((1, n * NA_HEADS), -jnp.inf, F32)], axis=1)
    mx = jnp.maximum(s_t.max(axis=0, keepdims=True), sink)
    p_t = jnp.exp2(s_t - mx)
    inv = 1.0 / (p_t.sum(axis=0, keepdims=True) + jnp.exp2(sink - mx))
    p_t = p_t.astype(BF16)

    nb = SWA_Q_HEADS * n
    o = _dot(v_t(A_BV, 0), p_t[:, :nb]) * inv[:, :nb]
    outs = [o[HEAD_DIM * (h // SWA_GROUP):HEAD_DIM * (h // SWA_GROUP + 1), n * h:n * (h + 1)]
            for h in range(SWA_Q_HEADS)]
    ob_ref[...] = jnp.concatenate(outs, axis=0).T.astype(BF16)
    outs = []
    for p in range(NA_HEADS // 2):
        cols = slice(nb + 2 * n * p, nb + 2 * n * (p + 1))
        o = _dot(v_t(A_CV, p), p_t[:, cols]) * inv[:, cols]
        outs += [o[:HEAD_DIM, :n], o[HEAD_DIM:, n:]]
    oc_ref[...] = jnp.concatenate(outs, axis=0).T.astype(BF16)


def _ctx_attention(sink, z, wqb, wkb, wqc, wkc, caches, l):
    blk = lambda w, j=0: pl.BlockSpec((CTX_LEN, w), lambda b: (b, j))
    cst = lambda w: pl.BlockSpec((1, w), lambda b: (0, 0))
    sds = lambda w: jax.ShapeDtypeStruct((T_CTX, w), BF16)
    cache_w = (128, 128, 512, 512)
    if caches is None:
        cache_blk = [pl.BlockSpec((None, DEPTH, CTX_LEN, w), lambda b: (b, 0, 0, 0)) for w in cache_w]
    else:
        cache_blk = [pl.BlockSpec((None, None, CTX_LEN, w), lambda b: (b, l, 0, 0)) for w in cache_w]
    cache_sds = [jax.ShapeDtypeStruct((N_CTX_SEQ, DEPTH, CTX_LEN, w), F32) for w in cache_w]
    carried = [] if caches is None else list(caches)
    n_in = 6
    return pl.pallas_call(
        functools.partial(_ctx_attn_kernel, layer=l),
        grid=(N_CTX_SEQ,),
        in_specs=[pl.BlockSpec(memory_space=pltpu.SMEM), blk(ATT_W, Z_ATT // ATT_W),
                  cst(512), cst(128), cst(512), cst(512)]
                 + [pl.BlockSpec(memory_space=pl.ANY)] * len(carried),
        out_specs=[blk(512), blk(512)] + cache_blk,
        out_shape=[sds(512), sds(512)] + cache_sds,
        input_output_aliases={n_in + j: 2 + j for j in range(len(carried))},
        compiler_params=_cparams(("arbitrary",)),
        name="ctx_attention",
    )(sink, z, wqb, wkb, wqc, wkc, *carried)


SWA_QBLK = 128
SWA_SPAN = SWA_QBLK + 2 * SWA_WINDOW


def _swa_kernel(sink_ref, qt_ref, k_ref, vt_ref, kc_ref, vc_ref, o_ref, kctx_scr, vctx_scr, *, layer):
    i = pl.program_id(1)
    nq = SWA_QBLK

    @pl.when(i == 0)
    def _():
        kctx_scr[...] = kc_ref[...].T.astype(BF16)
        vctx_scr[...] = vc_ref[...].astype(BF16)

    start = pl.multiple_of(jnp.clip(nq * (i - 1), 0, LAT_LEN - SWA_SPAN), nq)
    kwin = k_ref[pl.ds(start, SWA_SPAN), :]
    vwin_t = vt_ref[:, pl.ds(start, SWA_SPAN)]
    kctx = kctx_scr[...]
    s_lat, s_ctx = [], []
    for kvh in range(SWA_KV_HEADS):
        placed = []
        for h in range(SWA_GROUP * kvh, SWA_GROUP * (kvh + 1)):
            rows = qt_ref[HEAD_DIM * h:HEAD_DIM * (h + 1), :]
            z = jnp.zeros_like(rows)
            placed.append(jnp.concatenate([rows, z] if kvh == 0 else [z, rows], axis=0))
        q4 = jnp.concatenate(placed, axis=1)
        s_lat.append(_dot(kwin, q4))
        s_ctx.append(_dot(kctx, q4))
    s_lat = jnp.concatenate(s_lat, axis=1)
    s_ctx = jnp.concatenate(s_ctx, axis=1)

    kpos = start + lax.broadcasted_iota(jnp.int32, (SWA_SPAN, nq), 0)
    qpos = nq * i + lax.broadcasted_iota(jnp.int32, (SWA_SPAN, nq), 1)
    band = jnp.where(jnp.abs(kpos - qpos) <= SWA_WINDOW, 0.0, -jnp.inf)
    s_lat = s_lat + jnp.concatenate([band] * SWA_Q_HEADS, axis=1)
    sink = jnp.concatenate(
        [jnp.full((1, nq), sink_ref[layer, h] * LOG2E, F32) for h in range(SWA_Q_HEADS)], axis=1)
    mx = jnp.maximum(jnp.maximum(s_lat.max(axis=0, keepdims=True),
                                 s_ctx.max(axis=0, keepdims=True)), sink)
    p_lat = jnp.exp2(s_lat - mx)
    p_ctx = jnp.exp2(s_ctx - mx)
    inv = 1.0 / (p_lat.sum(axis=0, keepdims=True) + p_ctx.sum(axis=0, keepdims=True)
                 + jnp.exp2(sink - mx))
    o = (_dot(vwin_t, p_lat.astype(BF16)) + _dot(vctx_scr[...], p_ctx.astype(BF16))) * inv
    outs = [o[HEAD_DIM * (h // SWA_GROUP):HEAD_DIM * (h // SWA_GROUP + 1), nq * h:nq * (h + 1)]
            for h in range(SWA_Q_HEADS)]
    o_ref[...] = jnp.concatenate(outs, axis=0).T.astype(BF16)


def _swa_attention(sink, qb_t, kb, vb_t, cache_k, cache_v, l):
    nq = LAT_LEN // SWA_QBLK
    cache = pl.BlockSpec((None, None, 128, PAST_LEN), lambda b, i: (b, l, 0, 0))
    return pl.pallas_call(
        functools.partial(_swa_kernel, layer=l),
        grid=(N_LAT_SEQ, nq),
        in_specs=[pl.BlockSpec(memory_space=pltpu.SMEM),
                  pl.BlockSpec((512, SWA_QBLK), lambda b, i: (0, b * nq + i)),
                  pl.BlockSpec((LAT_LEN, 128), lambda b, i: (b, 0)),
                  pl.BlockSpec((128, LAT_LEN), lambda b, i: (0, b)),
                  cache, cache],
        out_specs=pl.BlockSpec((SWA_QBLK, 512), lambda b, i: (b * nq + i, 0)),
        out_shape=jax.ShapeDtypeStruct((T_LAT, 512), BF16),
        scratch_shapes=[pltpu.VMEM((PAST_LEN, 128), BF16), pltpu.VMEM((128, PAST_LEN), BF16)],
        compiler_params=_cparams(("arbitrary", "arbitrary")),
        name="swa_attention",
    )(sink, qb_t, kb, vb_t, cache_k, cache_v)


LAT_ROWS = LAT_LEN // GRID_W
NA_WIN_ROWS = min(NA_ROWS, LAT_ROWS)
NA_KEYS = NA_WIN_ROWS * GRID_W


def _na_kernel(q_ref, k_ref, v_ref, kc_ref, vc_ref, rp_ref, o_ref, kctx_scr, vctx_scr, bias_ref):
    r = pl.program_id(1)
    low = lax.broadcasted_iota(jnp.int32, (1, 128), 1) < HEAD_DIM

    @pl.when(r == 0)
    def _():
        kctx_scr[...] = kc_ref[...].astype(BF16)
        vctx_scr[...] = vc_ref[...].astype(BF16)
        cq = lax.broadcasted_iota(jnp.int32, (GRID_W, 128), 0)
        ck = lax.broadcasted_iota(jnp.int32, (GRID_W, 128), 1) % GRID_W
        cs = jnp.clip(cq - NA_COLS // 2, 0, GRID_W - NA_COLS)
        window = jnp.where((ck >= cs) & (ck < cs + NA_COLS), 0.0, -jnp.inf)
        for h in range(NA_HEADS):
            rows = [jnp.broadcast_to(rp_ref[h, dr:dr + 1, :], (GRID_W, 128))
                    for dr in range(2 * NA_ROWS - 1)]
            left = [pltpu.roll(x, 0, 1, stride=1, stride_axis=0) for x in rows[:-1]]
            right = [pltpu.roll(x, GRID_W, 1, stride=1, stride_axis=0) for x in rows[1:]]
            for j in range(2 * NA_ROWS - 2):
                bias_ref[h, j] = jnp.where(low, left[j], right[j]) * LOG2E + window

    rs = jnp.clip(r - NA_ROWS // 2, 0, LAT_ROWS - NA_WIN_ROWS)
    start = pl.multiple_of(rs * GRID_W, GRID_W)
    base = rs - r + NA_ROWS - 1
    npair = NA_HEADS // 2
    s_lat, s_ctx = [], []
    for p in range(npair):
        sl = slice(128 * p, 128 * (p + 1))
        q = q_ref[:, sl]
        zero = jnp.zeros_like(q)
        q2 = jnp.concatenate([jnp.where(low, q, zero), jnp.where(low, zero, q)], axis=0)
        bias = jnp.concatenate(
            [jnp.concatenate([bias_ref[2 * p + hh, base + 2 * w] for w in range(NA_WIN_ROWS // 2)],
                             axis=1) for hh in range(2)], axis=0)
        s_lat.append(_nt_dot(q2, k_ref[pl.ds(start, NA_KEYS), sl]) + bias)
        s_ctx.append(_dot(q2, kctx_scr[sl, :]))
    s_lat = jnp.concatenate(s_lat, axis=0)
    s_ctx = jnp.concatenate(s_ctx, axis=0)
    mx = jnp.maximum(s_lat.max(axis=-1, keepdims=True), s_ctx.max(axis=-1, keepdims=True))
    p_lat = jnp.exp2(s_lat - mx)
    p_ctx = jnp.exp2(s_ctx - mx)
    inv = 1.0 / (p_lat.sum(axis=-1, keepdims=True) + p_ctx.sum(axis=-1, keepdims=True))
    p_lat = p_lat.astype(BF16)
    p_ctx = p_ctx.astype(BF16)
    tiles = []
    for p in range(npair):
        sl = slice(128 * p, 128 * (p + 1))
        rows = slice(2 * GRID_W * p, 2 * GRID_W * (p + 1))
        o2 = (_dot(p_lat[rows], v_ref[pl.ds(start, NA_KEYS), sl])
              + _nt_dot(p_ctx[rows], vctx_scr[sl, :])) * inv[rows]
        tiles.append(jnp.where(low, o2[:GRID_W], o2[GRID_W:]))
    o_ref[...] = jnp.concatenate(tiles, axis=1).astype(BF16)


def _na_attention(qc, kc, vc, cache_k, cache_v, rp_cyc, l):
    kv = pl.BlockSpec((LAT_LEN, 512), lambda b, r: (b, 0))
    cache = pl.BlockSpec((None, None, 512, PAST_LEN), lambda b, r: (b, l, 0, 0))
    return pl.pallas_call(
        _na_kernel,
        grid=(N_LAT_SEQ, LAT_ROWS),
        in_specs=[pl.BlockSpec((GRID_W, 512), lambda b, r: (b * LAT_ROWS + r, 0)),
                  kv, kv, cache, cache,
                  pl.BlockSpec((None, NA_HEADS, 2 * NA_ROWS - 1, 128), lambda b, r: (l, 0, 0, 0))],
        out_specs=pl.BlockSpec((GRID_W, 512), lambda b, r: (b * LAT_ROWS + r, 0)),
        out_shape=jax.ShapeDtypeStruct((T_LAT, 512), BF16),
        scratch_shapes=[pltpu.VMEM((512, PAST_LEN), BF16), pltpu.VMEM((512, PAST_LEN), BF16),
                        pltpu.VMEM((NA_HEADS, 2 * NA_ROWS - 2, GRID_W, 128), F32)],
        compiler_params=_cparams(("arbitrary", "arbitrary")),
        name="na_attention",
    )(qc, kc, vc, cache_k, cache_v, rp_cyc)


def _na_table_rows(rpb):
    pad = jnp.zeros(rpb.shape[:-1] + (128 - (2 * NA_COLS - 1),), F32)
    return jnp.concatenate([rpb[..., NA_COLS - 1:], pad, rpb[..., :NA_COLS - 1]], axis=-1)


def _gla_constants():
    c = GLA_C
    t = np.arange(c)[:, None]
    u = np.arange(c)[None, :]
    blocks = []
    for k in range(GLA_LEVELS):
        b = 1 << k
        m = ((t >> k) | 1) * b - 1
        query = ((t >> k) & 1) == 1
        blocks.append(np.where(query, (u > m) & (u <= t), (u > t) & (u <= m)))
    blocks.append(u <= t)
    blocks.append(u > t)
    blocks.append(np.ones((16, c), bool))
    fwd = np.concatenate(blocks, axis=0).astype(np.float32)
    bwd_blocks = [blk[::-1, ::-1] for blk in blocks]
    bwd = np.concatenate(bwd_blocks, axis=0).astype(np.float32)
    seg = np.stack([fwd, bwd])
    seg = np.concatenate([seg, seg], axis=-1)

    s = np.arange(c)[None, :]
    x = t ^ s
    lev = np.where(x == 0, GLA_LEVELS, np.floor(np.log2(np.maximum(x, 1))).astype(np.int64))
    lev_f = np.where(s <= t, lev, -1)
    lev_b = np.where(s >= t, lev, -1)
    levmap = np.stack([np.tile(lev_f, (1, GLA_HEADS)), np.tile(lev_b, (1, GLA_HEADS))])
    return seg, levmap.astype(np.int32)


def _gla_chunks(chains):
    c = GLA_C
    lane_head = lax.broadcasted_iota(jnp.int32, (1, GLA_HEADS * GLA_DK), 1) // GLA_DK
    row_head = lax.broadcasted_iota(jnp.int32, (GLA_HEADS * GLA_DV, 1), 0) // GLA_DV
    head_sel = [jnp.where(lane_head == h, 1.0, 0.0).astype(BF16) for h in range(GLA_HEADS)]

    def operands(q, k, e=None):
        if e is not None:
            q, k = q * e, k * e
        return q, jnp.concatenate([k * head_sel[h] for h in range(GLA_HEADS)], axis=0)

    segs = []
    for (_, _, _, lr_ref, w2_ref, ba_ref, seg_ref, *_) in chains:
        x = _dot(lr_ref[...], w2_ref[...]) + ba_ref[...]
        la = (jnp.minimum(x, 0.0) - jnp.log(1.0 + jnp.exp(-jnp.abs(x)))) * (1.0 / GLA_TAU)
        la_hi = la.astype(BF16)
        la_lo = (la - la_hi.astype(F32)).astype(BF16)
        segs.append(_dot(seg_ref[...], jnp.concatenate([la_hi, la_lo], axis=0)))

    def decay(seg, block):
        return jnp.exp(seg[c * block:c * (block + 1)]).astype(BF16)

    qs = [ch[0][...] * (GLA_DK ** -0.5) for ch in chains]
    ks = [ch[1][...] for ch in chains]
    order = [GLA_LEVELS] + list(range(GLA_LEVELS))
    prepare = lambda lvl: [operands(qs[n], ks[n], None if lvl == GLA_LEVELS else decay(segs[n], lvl))
                           for n in range(len(chains))]
    attn = [None] * len(chains)
    ops = prepare(order[0])
    for i, lvl in enumerate(order):
        scores = [_nt_dot(q, kbd) for q, kbd in ops]
        if i + 1 < len(order):
            ops = prepare(order[i + 1])
        for n, ch in enumerate(chains):
            attn[n] = jnp.where(ch[7][...] == lvl, scores[n], 0.0 if attn[n] is None else attn[n])

    new_states = []
    for n, (_, _, v_ref, _, _, _, _, _, o_ref, st_ref) in enumerate(chains):
        seg = segs[n]
        v = v_ref[...]
        a = attn[n].astype(BF16)
        tot = seg[c * (GLA_LEVELS + 2):c * (GLA_LEVELS + 2) + 1]
        q_in = qs[n] * decay(seg, GLA_LEVELS)
        k_in = ks[n] * decay(seg, GLA_LEVELS + 1)
        state = st_ref[...]
        o_inter = _nt_dot(q_in, state.astype(BF16))
        for h in range(GLA_HEADS):
            sl = slice(GLA_DV * h, GLA_DV * (h + 1))
            o_ref[:, sl] = o_inter[:, sl] + _dot(a[:, c * h:c * (h + 1)], v[:, sl])
        upd = _tn_dot(v, k_in)
        new_state = state * jnp.exp(tot) + jnp.where(row_head == lane_head, upd, 0.0)
        st_ref[...] = new_state
        new_states.append(new_state)
    return new_states


GLA_GROUPS = 3
GLA_GROUP_ROWS = T_ALL // GLA_GROUPS
GLA_CTX_GROUPS = T_CTX // GLA_GROUP_ROWS
GLA_STEPS = GLA_GROUP_ROWS // GLA_C
GLA_CTX_CHUNKS = CTX_LEN // GLA_C
GLA_LAT_CHUNKS = LAT_LEN // GLA_C


def _gla_kernel(qf, kf, vf, lrf, qb, kb, vb, lrb, w2_ref, ba_ref, seg_ref, lev_ref, s0f, s0b,
                of_ref, ob_ref, sff_ref, sfb_ref, states):
    step = pl.program_id(0)

    @pl.when(step % GLA_CTX_CHUNKS == 0)
    def _():
        for d in range(2):
            for g in range(GLA_CTX_GROUPS):
                states[d, g] = jnp.zeros(states.shape[2:], F32)

    @pl.when(step % GLA_LAT_CHUNKS == 0)
    def _():
        for d, s0 in enumerate((s0f, s0b)):
            for h in range(GLA_HEADS):
                row = [jnp.zeros((GLA_DV, GLA_DK), F32)] * GLA_HEADS
                row[h] = s0[h]
                states[d, GLA_GROUPS - 1, GLA_DV * h:GLA_DV * (h + 1), :] = jnp.concatenate(row, axis=1)

    chains = []
    for g in range(GLA_GROUPS):
        chains.append((qf.at[g], kf.at[g], vf.at[g], lrf.at[g], w2_ref.at[0], ba_ref.at[0],
                       seg_ref.at[0], lev_ref.at[0], of_ref.at[g], states.at[0, g]))
        chains.append((qb.at[g], kb.at[g], vb.at[g], lrb.at[g], w2_ref.at[1], ba_ref.at[1],
                       seg_ref.at[1], lev_ref.at[1], ob_ref.at[g], states.at[1, g]))
    new_states = _gla_chunks(chains)

    @pl.when(step % GLA_CTX_CHUNKS == GLA_CTX_CHUNKS - 1)
    def _():
        for g in range(GLA_CTX_GROUPS):
            for d, out in enumerate((sff_ref, sfb_ref)):
                st = new_states[2 * g + d]
                for h in range(GLA_HEADS):
                    out[g, h] = st[GLA_DV * h:GLA_DV * (h + 1), GLA_DK * h:GLA_DK * (h + 1)].T


def _gla(z, w2p, ba, seg, levmap, s0t, l):
    c = GLA_C
    z3 = z.reshape(GLA_GROUPS, GLA_GROUP_ROWS, Z_W)
    fwd = lambda s: s
    bwd = lambda s: GLA_STEPS - 1 - s
    seqs_per_group = GLA_GROUP_ROWS // CTX_LEN

    def chunk_specs(pos):
        return [pl.BlockSpec((GLA_GROUPS, c, 256), lambda s: (0, pos(s), Z_AQ // 256)),
                pl.BlockSpec((GLA_GROUPS, c, 256), lambda s: (0, pos(s), Z_AK // 256)),
                pl.BlockSpec((GLA_GROUPS, c, 512), lambda s: (0, pos(s), Z_AV // 512)),
                pl.BlockSpec((GLA_GROUPS, c, 128), lambda s: (0, pos(s), Z_LR // 128))]

    def s0_spec(d, pos):
        return pl.BlockSpec((None, None, None, GLA_HEADS, GLA_DV, GLA_DK),
                            lambda s: (pos(s) // GLA_LAT_CHUNKS, l, d, 0, 0, 0))

    def out_spec(pos):
        return pl.BlockSpec((GLA_GROUPS, c, 512), lambda s: (0, pos(s), 0))

    def sfin_spec(pos):
        return pl.BlockSpec((GLA_CTX_GROUPS, None, GLA_HEADS, GLA_DK, GLA_DV),
                            lambda s: (0, pos(s) // GLA_CTX_CHUNKS, 0, 0, 0))

    state_sds = jax.ShapeDtypeStruct((GLA_CTX_GROUPS, seqs_per_group, GLA_HEADS, GLA_DK, GLA_DV), F32)
    out_sds = jax.ShapeDtypeStruct((GLA_GROUPS, GLA_GROUP_ROWS, 512), F32)
    o_f, o_b, sf, sb = pl.pallas_call(
        _gla_kernel,
        grid=(GLA_STEPS,),
        in_specs=chunk_specs(fwd) + chunk_specs(bwd) + [
            pl.BlockSpec((None, 2, 128, 256), lambda s: (l, 0, 0, 0)),
            pl.BlockSpec((None, 2, 1, 256), lambda s: (l, 0, 0, 0)),
            pl.BlockSpec((2, GLA_GROWS, 2 * c), lambda s: (0, 0, 0)),
            pl.BlockSpec((2, c, GLA_HEADS * c), lambda s: (0, 0, 0)),
            s0_spec(0, fwd), s0_spec(1, bwd),
        ],
        out_specs=[out_spec(fwd), out_spec(bwd), sfin_spec(fwd), sfin_spec(bwd)],
        out_shape=[out_sds, out_sds, state_sds, state_sds],
        scratch_shapes=[pltpu.VMEM((2, GLA_GROUPS, GLA_HEADS * GLA_DV, GLA_HEADS * GLA_DK), F32)],
        compiler_params=_cparams(("arbitrary",)),
        name="gla",
    )(z3, z3, z3, z3, z3, z3, z3, z3, w2p, ba, seg, levmap, s0t, s0t)
    flat = lambda a: a.reshape((-1,) + a.shape[2:])
    return flat(o_f), flat(o_b), flat(sf), flat(sb)


def _cast_kernel(*refs):
    n = len(refs) // 2
    for src, dst in zip(refs[:n], refs[n:]):
        dst[...] = src[...].astype(dst.dtype)


def _cast_bf16(*weights):
    spec = lambda w: pl.BlockSpec((None,) + w.shape[1:], lambda l: (l, 0, 0))
    return pl.pallas_call(
        _cast_kernel,
        grid=(DEPTH,),
        in_specs=[spec(w) for w in weights],
        out_specs=[spec(w) for w in weights],
        out_shape=[jax.ShapeDtypeStruct(w.shape, BF16) for w in weights],
        compiler_params=_cparams(("arbitrary",)),
        name="cast_bf16",
    )(*weights)


def _post_mixer_kernel(*refs, nk, n_x, n_y):
    x_refs, refs = refs[:n_x], refs[n_x:]
    (of_ref, obk_ref, ar_ref, gates_ref, bc_ref, bl_ref, cc_ref, cl_ref, wpa_ref, wpb_ref, wpc_ref,
     wo_ref, gn_ref, g2_ref, mod_ref, w1_ref, w2_ref) = refs[:17]
    y_refs = refs[17:17 + n_y]
    w1_scr, w2_scr, h_scr, x1_scr, acc_scr = refs[17 + n_y:]
    s = pl.program_id(0)
    d = D_MODEL
    tm = of_ref.shape[0]
    th = w1_ref.shape[1]
    ctx = jnp.maximum(s - (nk - 1), 0) < T_CTX // tm

    def put_y(val):
        if n_y == 1:
            y_refs[0][...] = val
        else:
            @pl.when(ctx)
            def _():
                y_refs[0][...] = val

            @pl.when(jnp.logical_not(ctx))
            def _():
                y_refs[1][...] = val

    def merged_residual():
        o = of_ref[...] + obk_ref[...]
        heads = []
        for h in range(GLA_HEADS):
            oh = o[:, GLA_DV * h:GLA_DV * (h + 1)]
            ms = jnp.mean(oh * oh, axis=-1, keepdims=True)
            heads.append(oh * lax.rsqrt(ms + EPS) * gn_ref[...])
        oa = (jnp.concatenate(heads, axis=1) * _silu(ar_ref[...].astype(F32))).astype(BF16)
        ob = jnp.where(ctx, bc_ref[...], bl_ref[...])
        oc = jnp.where(ctx, cc_ref[...], cl_ref[...])
        gate = lambda j: _sigmoid(gates_ref[:, j * d:(j + 1) * d].astype(F32))
        merged = (gate(0) * _dot(oa, wpa_ref[...]) + gate(1) * _dot(ob, wpb_ref[...])
                  + gate(2) * _dot(oc, wpc_ref[...]))
        return (_token_rows(x_refs, ctx)
                + mod_ref[:, 2 * d:3 * d] * _dot(merged.astype(BF16), wo_ref[...]))

    def normed(x1):
        return _norm_mod(x1, g2_ref[...], mod_ref[:, 3 * d:4 * d], mod_ref[:, 4 * d:5 * d]).astype(BF16)

    def act(u):
        return jnp.square(jnp.maximum(u, 0.0)).astype(BF16)

    @pl.when(s == 0)
    def _():
        x1 = merged_residual()
        x1_scr[...] = x1
        h_scr[...] = normed(x1)

    @pl.when(s < nk)
    def _():
        col = pl.multiple_of(s * th, th)
        w1 = w1_ref[...].astype(BF16)
        w2 = w2_ref[...].astype(BF16)
        w1_scr[:, pl.ds(col, th)] = w1
        w2_scr[pl.ds(col, th), :] = w2
        part = _dot(act(_dot(h_scr[...], w1)), w2)

        @pl.when(s == 0)
        def _():
            acc_scr[...] = part

        @pl.when(s > 0)
        def _():
            acc_scr[...] += part

        @pl.when(s == nk - 1)
        def _():
            put_y(x1_scr[...] + mod_ref[:, 5 * d:6 * d] * acc_scr[...])

    @pl.when(s >= nk)
    def _():
        x1 = merged_residual()
        out = _dot(act(_dot(normed(x1), w1_scr[...])), w2_scr[...])
        put_y(x1 + mod_ref[:, 5 * d:6 * d] * out)


def _post_mixer(x_parts, o_fwd, o_bwd, z, ob_ctx, ob_lat, oc_ctx, oc_lat, wpa, wpb, wpc, wo, gn, norm2_w,
                mod4, w1, w2, l, split_out):
    tm, th = 256, 512
    nk = MLP_HIDDEN // th
    ctx_blocks = T_CTX // tm
    row = lambda s: jnp.maximum(s - (nk - 1), 0)
    chunk = lambda s: jnp.minimum(s, nk - 1)
    ctx_idx = lambda s: (jnp.minimum(row(s), ctx_blocks - 1), 0)
    lat_idx = lambda s: (jnp.maximum(row(s) - ctx_blocks, 0), 0)
    rows = lambda w, j=0: pl.BlockSpec((tm, w), lambda s: (row(s), j))
    resident = lambda k: pl.BlockSpec((None, k, D_MODEL), lambda s: (l, 0, 0),
                                      pipeline_mode=pl.Buffered(1))
    if split_out:
        out_specs = [pl.BlockSpec((tm, D_MODEL), ctx_idx), pl.BlockSpec((tm, D_MODEL), lat_idx)]
        out_shape = [jax.ShapeDtypeStruct((T_CTX, D_MODEL), F32), jax.ShapeDtypeStruct((T_LAT, D_MODEL), F32)]
    else:
        out_specs = [rows(D_MODEL)]
        out_shape = [jax.ShapeDtypeStruct((T_ALL, D_MODEL), F32)]
    return pl.pallas_call(
        functools.partial(_post_mixer_kernel, nk=nk, n_x=len(x_parts), n_y=len(out_specs)),
        grid=(nk - 1 + T_ALL // tm,),
        in_specs=_token_specs(x_parts, tm, row) + [
            rows(512), rows(512), rows(512, Z_AR // 512), rows(3 * D_MODEL),
            pl.BlockSpec((tm, 512), ctx_idx), pl.BlockSpec((tm, 512), lat_idx),
            pl.BlockSpec((tm, 512), ctx_idx), pl.BlockSpec((tm, 512), lat_idx),
            resident(512), resident(512), resident(512), resident(D_MODEL),
            pl.BlockSpec((None, 1, GLA_DV), lambda s: (l, 0, 0)),
            pl.BlockSpec((None, 1, D_MODEL), lambda s: (l, 0, 0)),
            pl.BlockSpec((None, None, 1, 6 * D_MODEL),
                         lambda s: (l, _group_of_rows(row(s), tm), 0, 0)),
            pl.BlockSpec((None, D_MODEL, th), lambda s: (l, 0, chunk(s))),
            pl.BlockSpec((None, th, D_MODEL), lambda s: (l, chunk(s), 0)),
        ],
        out_specs=out_specs,
        out_shape=out_shape,
        scratch_shapes=[pltpu.VMEM((D_MODEL, MLP_HIDDEN), BF16), pltpu.VMEM((MLP_HIDDEN, D_MODEL), BF16),
                        pltpu.VMEM((tm, D_MODEL), BF16), pltpu.VMEM((tm, D_MODEL), F32),
                        pltpu.VMEM((tm, D_MODEL), F32)],
        compiler_params=_cparams(("arbitrary",)),
        name="post_mixer",
    )(*x_parts, o_fwd, o_bwd, z, z, ob_ctx, ob_lat, oc_ctx, oc_lat, wpa, wpb, wpc, wo, gn, norm2_w, mod4,
      w1, w2)


def _rope_tables():
    t = jnp.arange(LAT_LEN)
    row = (t // GRID_W).astype(F32)
    col = (t % GRID_W).astype(F32)
    nf = HEAD_DIM // 4
    inv_freq = ROPE_BASE ** (-jnp.arange(nf, dtype=F32) / nf)
    ang_r = row[:, None] * inv_freq[None, :]
    ang_c = col[:, None] * inv_freq[None, :]
    cos = jnp.concatenate([jnp.cos(ang_r)] * 2 + [jnp.cos(ang_c)] * 2, axis=1)
    sin = jnp.concatenate([-jnp.sin(ang_r), jnp.sin(ang_r), -jnp.sin(ang_c), jnp.sin(ang_c)], axis=1)
    return jnp.tile(cos, (1, 2)), jnp.tile(sin, (1, 2))


def kernel(x_prompt, x_sample, state_gla, cache_swa_k, cache_swa_v, cache_na_k, cache_na_v, c,
           c_ctx, w_mod, b_mod, norm1, norm2, w_in, w_a2_f, b_a_f, w_a2_b, b_a_b, gla_onorm,
           qn_swa, kn_swa, sink_swa, qn_na, kn_na, rpb_na, w_pa, w_pb, w_pc, w_o, w_fc1, w_fc2):
    d = D_MODEL
    x_parts = (x_prompt.reshape(T_CTX, d), x_sample.reshape(T_LAT, d))

    cond8 = jnp.zeros((8, d), F32).at[0].set(c_ctx).at[1:1 + N_LAT_SEQ].set(c)
    mod4 = _modulation(cond8, w_mod, b_mod).reshape(DEPTH, 8, 1, 6 * d)

    w_in_p = _w_in_layout(jnp.swapaxes(w_in, 1, 2))
    wpa, wpb, wpc, wo = _cast_bf16(w_pa, w_pb, w_pc, w_o)
    norm1r = norm1.reshape(DEPTH, 1, d)
    norm2r = norm2.reshape(DEPTH, 1, d)
    gnr = gla_onorm.reshape(DEPTH, 1, GLA_DV)
    w2p = jnp.zeros((DEPTH, 2, 128, GLA_HEADS * GLA_DK), F32)
    w2p = w2p.at[:, 0, 0:GLA_LOWRANK].set(w_a2_f).at[:, 1, GLA_LOWRANK:2 * GLA_LOWRANK].set(w_a2_b)
    w2p = w2p.astype(BF16)
    ba = jnp.stack([b_a_f, b_a_b], axis=1).reshape(DEPTH, 2, 1, GLA_HEADS * GLA_DK)
    seg_np, lev_np = _gla_constants()
    seg = jnp.asarray(seg_np, BF16)
    levmap = jnp.asarray(lev_np)
    s0t = jnp.swapaxes(state_gla, -1, -2)
    cos_t, sin_t = _rope_tables()
    rp_cyc = _na_table_rows(rpb_na)
    feat_major = lambda a: a.transpose(0, 1, 3, 4, 2).reshape(N_LAT_SEQ, DEPTH, -1, PAST_LEN)
    csk, csv, cnk, cnv = (feat_major(a) for a in (cache_swa_k, cache_swa_v, cache_na_k, cache_na_v))

    st_l = []
    caches = None
    for l in range(DEPTH):
        z = _inproj(x_parts, norm1r, mod4, w_in_p, l)
        wqb = jnp.tile(qn_swa[l], 8)[None, :]
        wkb = jnp.tile(kn_swa[l], 2)[None, :]
        wqc = jnp.tile(qn_na[l], 8)[None, :]
        wkc = jnp.tile(kn_na[l], 8)[None, :]
        qb, kb, vb, qc, kc, vc = _attn_prep(z, wqb, wkb, wqc, wkc, cos_t, sin_t)
        o_fwd, o_bwd, sfin_f, sfin_b = _gla(z, w2p, ba, seg, levmap, s0t, l)
        ob_ctx, oc_ctx, *caches = _ctx_attention(sink_swa, z, wqb, wkb, wqc, wkc, caches, l)
        ob_lat = _swa_attention(sink_swa, qb, kb, vb, csk, csv, l)
        oc_lat = _na_attention(qc, kc, vc, cnk, cnv, rp_cyc, l)
        x_parts = _post_mixer(x_parts, o_fwd, o_bwd, z, ob_ctx, ob_lat, oc_ctx, oc_lat, wpa, wpb, wpc, wo,
                              gnr, norm2r, mod4, w_fc1, w_fc2, l, split_out=(l == DEPTH - 1))

        st_l += [sfin_f, sfin_b]

    y_prompt = x_parts[0].reshape(N_CTX_SEQ, CTX_LEN, d)
    y_sample = x_parts[1].reshape(N_LAT_SEQ, LAT_LEN, d)
    new_state = jnp.stack(st_l, axis=1).reshape(
        N_CTX_SEQ, DEPTH, 2, GLA_HEADS, GLA_DK, GLA_DV)

    swa_k, swa_v, na_k, na_v = caches
    kv_shape = lambda heads: (N_CTX_SEQ, DEPTH, CTX_LEN, heads, HEAD_DIM)
    return (y_prompt, y_sample, new_state,
            swa_k.reshape(kv_shape(SWA_KV_HEADS)), swa_v.reshape(kv_shape(SWA_KV_HEADS)),
            na_k.reshape(kv_shape(NA_HEADS)), na_v.reshape(kv_shape(NA_HEADS)))
```

```python
import functools

import numpy as np
import jax
import jax.numpy as jnp
from jax import lax
from jax.experimental import pallas as pl
from jax.experimental.pallas import tpu as pltpu

F32 = jnp.float32
BF16 = jnp.bfloat16

D_MODEL = 1024
DEPTH = 4
N_CTX_SEQ = 16
CTX_LEN = 256
N_LAT_SEQ = 2
LAT_LEN = 1024
PAST_LEN = 512
T_CTX = N_CTX_SEQ * CTX_LEN
T_LAT = N_LAT_SEQ * LAT_LEN
T_ALL = T_CTX + T_LAT
GRID_W = 64
HEAD_DIM = 64
GLA_HEADS = 4
GLA_DK = 64
GLA_DV = 128
GLA_LOWRANK = 16
GLA_TAU = 16.0
SWA_Q_HEADS = 8
SWA_KV_HEADS = 2
SWA_GROUP = 4
SWA_WINDOW = 128
NA_HEADS = 8
NA_ROWS = 8
NA_COLS = 16
MLP_HIDDEN = 4 * D_MODEL
ROPE_BASE = 10000.0
EPS = 1e-6
LOG2E = 1.4426950408889634
QK_SCALE = HEAD_DIM ** -0.5 * LOG2E

Z_GATES = 0
Z_GLA = 3072
Z_AQ, Z_AK, Z_AV, Z_AR = 3072, 3328, 3584, 4096
Z_ATT = 4608
ATT_W = 2304
A_BQ, A_BK, A_BV, A_CQ, A_CK, A_CV = 0, 512, 640, 768, 1280, 1792
Z_LR = 6912
Z_W = 7168

GLA_C = 128
GLA_LEVELS = 7
GLA_GROWS = (GLA_LEVELS + 2) * GLA_C + 16

VMEM_LIMIT = 56 * 1024 * 1024


def _cparams(sem):
    return pltpu.CompilerParams(dimension_semantics=sem, vmem_limit_bytes=VMEM_LIMIT)


def _sigmoid(x):
    return 1.0 / (1.0 + jnp.exp(-x))


def _silu(x):
    return x * _sigmoid(x)


def _nt_dot(a, b):
    return lax.dot_general(a, b, (((1,), (1,)), ((), ())), preferred_element_type=F32)


def _tn_dot(a, b):
    return lax.dot_general(a, b, (((0,), (0,)), ((), ())), preferred_element_type=F32)


def _dot(a, b):
    return jnp.dot(a, b, preferred_element_type=F32)


def _mod_kernel(cond_ref, w_ref, b_ref, o_ref):
    s = _silu(cond_ref[...]).astype(BF16)
    o_ref[...] = _dot(s, w_ref[...].astype(BF16)) + b_ref[...]


def _modulation(cond8, w_mod, b_mod):
    tn = 1024
    return pl.pallas_call(
        _mod_kernel,
        grid=(DEPTH, 6 * D_MODEL // tn),
        in_specs=[
            pl.BlockSpec((8, D_MODEL), lambda l, j: (0, 0)),
            pl.BlockSpec((None, D_MODEL, tn), lambda l, j: (l, 0, j)),
            pl.BlockSpec((None, 1, tn), lambda l, j: (l, 0, j)),
        ],
        out_specs=pl.BlockSpec((None, 8, tn), lambda l, j: (l, 0, j)),
        out_shape=jax.ShapeDtypeStruct((DEPTH, 8, 6 * D_MODEL), F32),
        compiler_params=_cparams(("arbitrary", "arbitrary")),
        name="modulation",
    )(cond8, w_mod, b_mod.reshape(DEPTH, 1, 6 * D_MODEL))


def _group_of_rows(row_block, rows_per_block):
    first = row_block * rows_per_block
    return jnp.maximum(first - T_CTX + LAT_LEN, 0) // LAT_LEN


def _norm_mod(x, g, shift, scale):
    ms = jnp.mean(x * x, axis=-1, keepdims=True)
    y = x * lax.rsqrt(ms + EPS) * g
    return y * (1.0 + scale) + shift


W_IN_SEGMENTS = ((3872, 6944), (0, 1536), (1568, 3872), (1536, 1568))
W_IN_COLS = 6944


def _w_in_layout_kernel(w_ref, o_ref):
    dst = 0
    for lo, hi in W_IN_SEGMENTS:
        o_ref[dst:dst + hi - lo, :] = w_ref[lo:hi, :].astype(BF16)
        dst += hi - lo
    o_ref[dst:, :] = jnp.zeros((Z_W - dst, o_ref.shape[1]), BF16)


def _w_in_layout(w_in_t):
    tk = 256
    return pl.pallas_call(
        _w_in_layout_kernel,
        grid=(DEPTH, D_MODEL // tk),
        in_specs=[pl.BlockSpec((None, W_IN_COLS, tk), lambda l, r: (l, 0, r))],
        out_specs=pl.BlockSpec((None, Z_W, tk), lambda l, r: (l, 0, r)),
        out_shape=jax.ShapeDtypeStruct((DEPTH, Z_W, D_MODEL), BF16),
        compiler_params=_cparams(("arbitrary", "arbitrary")),
        name="w_in_layout",
    )(w_in_t)


def _token_rows(x_refs, ctx):
    if len(x_refs) == 1:
        return x_refs[0][...]
    return jnp.where(ctx, x_refs[0][...], x_refs[1][...])


def _token_specs(x_parts, tm, row):
    if len(x_parts) == 1:
        return [pl.BlockSpec((tm, D_MODEL), lambda s: (row(s), 0))]
    ctx_blocks = T_CTX // tm
    return [pl.BlockSpec((tm, D_MODEL), lambda s: (jnp.minimum(row(s), ctx_blocks - 1), 0)),
            pl.BlockSpec((tm, D_MODEL), lambda s: (jnp.maximum(row(s) - ctx_blocks, 0), 0))]


def _inproj_kernel(*refs):
    *x_refs, g_ref, mod_ref, w_ref, z_ref = refs
    ctx = pl.program_id(0) < T_CTX // z_ref.shape[0]
    h = _norm_mod(_token_rows(x_refs, ctx), g_ref[...],
                  mod_ref[:, 0:D_MODEL], mod_ref[:, D_MODEL:2 * D_MODEL])
    z_ref[...] = _nt_dot(h.astype(BF16), w_ref[...]).astype(z_ref.dtype)


def _inproj(x_parts, norm_w, mod4, w_in_p, l):
    tm = 512
    return pl.pallas_call(
        _inproj_kernel,
        grid=(T_ALL // tm,),
        in_specs=_token_specs(x_parts, tm, lambda i: i) + [
            pl.BlockSpec((None, 1, D_MODEL), lambda i: (l, 0, 0)),
            pl.BlockSpec((None, None, 1, 6 * D_MODEL), lambda i: (l, _group_of_rows(i, tm), 0, 0)),
            pl.BlockSpec((None, Z_W, D_MODEL), lambda i: (l, 0, 0),
                         pipeline_mode=pl.Buffered(1)),
        ],
        out_specs=pl.BlockSpec((tm, Z_W), lambda i: (i, 0)),
        out_shape=jax.ShapeDtypeStruct((T_ALL, Z_W), BF16),
        compiler_params=_cparams(("arbitrary",)),
        name="inproj",
    )(*x_parts, norm_w, mod4, w_in_p)


def _head_norm_pairs(z_ref, groups):
    same_head = (lax.broadcasted_iota(jnp.int32, (128, 128), 0) // HEAD_DIM
                 == lax.broadcasted_iota(jnp.int32, (128, 128), 1) // HEAD_DIM)
    ones = jnp.where(same_head, 1.0, 0.0).astype(BF16)
    tiles = [[(z_ref[:, col + 128 * p: col + 128 * (p + 1)].astype(F32), w_ref[:, 128 * p:128 * (p + 1)])
              for p in range(width // 128)] for col, width, w_ref in groups]
    sums = [[_dot((x * x).astype(BF16), ones) for x, _ in group] for group in tiles]
    return [[x * lax.rsqrt(ss * (1.0 / HEAD_DIM) + EPS) * w for (x, w), ss in zip(group, gs)]
            for group, gs in zip(tiles, sums)]


def _rope(x, cos, sin_signed):
    n = x.shape[-1]
    lane = lax.broadcasted_iota(jnp.int32, (1, n), 1)
    first = (lane % 32) < 16
    partner = jnp.where(first, pltpu.roll(x, n - 16, 1), pltpu.roll(x, 16, 1))
    return x * cos + partner * sin_signed


def _prep_kernel(z_ref, wqb_ref, wkb_ref, wqc_ref, wkc_ref, cos_ref, sin_ref,
                 qb_ref, kb_ref, vb_ref, qc_ref, kc_ref, vc_ref):
    qscale = QK_SCALE

    qb, kb, qc, kc = _head_norm_pairs(z_ref, [(A_BQ, 512, wqb_ref), (A_BK, 128, wkb_ref),
                                              (A_CQ, 512, wqc_ref), (A_CK, 512, wkc_ref)])

    vb_ref[...] = z_ref[:, A_BV:A_BV + 128].astype(F32).T.astype(BF16)
    vc_ref[...] = z_ref[:, A_CV:A_CV + 512]
    for p in range(4):
        qc_ref[:, 128 * p:128 * (p + 1)] = (qc[p] * qscale).astype(BF16)
        kc_ref[:, 128 * p:128 * (p + 1)] = kc[p].astype(BF16)

    cos = cos_ref[...]
    sin = sin_ref[...]
    for p in range(4):
        qb_ref[128 * p:128 * (p + 1), :] = (_rope(qb[p], cos, sin) * qscale).T.astype(BF16)
    kb_ref[...] = _rope(kb[0], cos, sin).astype(BF16)


def _attn_prep(z, wqb, wkb, wqc, wkc, cos_t, sin_t):
    tm = 512
    lat_blocks = LAT_LEN // tm
    rope_idx = lambda i: (i % lat_blocks, 0)
    row = lambda w: pl.BlockSpec((tm, w), lambda i: (i, 0))
    col = lambda w: pl.BlockSpec((w, tm), lambda i: (0, i))
    cst = lambda w: pl.BlockSpec((1, w), lambda i: (0, 0))
    sds = lambda w: jax.ShapeDtypeStruct((T_LAT, w), BF16)
    sds_t = lambda w: jax.ShapeDtypeStruct((w, T_LAT), BF16)
    return pl.pallas_call(
        _prep_kernel,
        grid=(T_LAT // tm,),
        in_specs=[
            pl.BlockSpec((tm, ATT_W), lambda i: (T_CTX // tm + i, Z_ATT // ATT_W)),
            cst(512), cst(128), cst(512), cst(512),
            pl.BlockSpec((tm, 128), rope_idx),
            pl.BlockSpec((tm, 128), rope_idx),
        ],
        out_specs=[col(512), row(128), col(128), row(512), row(512), row(512)],
        out_shape=[sds_t(512), sds(128), sds_t(128), sds(512), sds(512), sds(512)],
        compiler_params=_cparams(("arbitrary",)),
        name="attn_prep",
    )(z, wqb, wkb, wqc, wkc, cos_t, sin_t)


def _half_rows(t, half):
    z = jnp.zeros((HEAD_DIM, t.shape[1]), t.dtype)
    return jnp.concatenate([t[:HEAD_DIM], z] if half == 0 else [z, t[HEAD_DIM:]], axis=0)


def _ctx_attn_kernel(sink_ref, z_ref, wqb_ref, wkb_ref, wqc_ref, wkc_ref, *refs, layer):
    ob_ref, oc_ref = refs[-6:-4]
    cache_refs = refs[-4:]
    if len(refs) == 6:
        for ref in cache_refs:
            ref[...] = jnp.zeros(ref.shape, F32)
        cache_refs = [ref.at[layer] for ref in cache_refs]
    kbf_ref, vbf_ref, kcf_ref, vcf_ref = cache_refs
    vbf_ref[...] = z_ref[:, A_BV:A_BV + 128].astype(F32)
    vcf_ref[...] = z_ref[:, A_CV:A_CV + 512].astype(F32)
    n = z_ref.shape[0]
    qscale = QK_SCALE

    def pair(col, p):
        return z_ref[:, col + 128 * p: col + 128 * (p + 1)]

    qb_n, kb_n, qc_n, kc_n = _head_norm_pairs(z_ref, [(A_BQ, 512, wqb_ref), (A_BK, 128, wkb_ref),
                                                      (A_CQ, 512, wqc_ref), (A_CK, 512, wkc_ref)])

    def q_t(tile):
        return (tile * qscale).T.astype(BF16)

    def v_t(col, p):
        return pair(col, p).astype(F32).T.astype(BF16)

    kb = kb_n[0]
    kbf_ref[...] = kb
    kb = kb.astype(BF16)
    qb_t = [q_t(tile) for tile in qb_n]
    scores = []
    for kvh in range(SWA_KV_HEADS):
        placed = []
        for h in range(SWA_GROUP * kvh, SWA_GROUP * (kvh + 1)):
            rows = qb_t[h // 2][HEAD_DIM * (h % 2):HEAD_DIM * (h % 2 + 1)]
            z = jnp.zeros_like(rows)
            placed.append(jnp.concatenate([rows, z] if kvh == 0 else [z, rows], axis=0))
        scores.append(_dot(kb, jnp.concatenate(placed, axis=1)))
    for p in range(NA_HEADS // 2):
        kc = kc_n[p]
        kcf_ref[:, 128 * p:128 * (p + 1)] = kc
        qc_t = q_t(qc_n[p])
        q2 = jnp.concatenate([_half_rows(qc_t, 0), _half_rows(qc_t, 1)], axis=1)
        scores.append(_dot(kc.astype(BF16), q2))
    s_t = jnp.concatenate(scores, axis=1)

    sink = jnp.concatenate(
        [jnp.full((1, n), sink_ref[layer, h] * LOG2E, F32) for h in range(SWA_Q_HEADS)]
        + [jnp.full((1, n * NA_HEADS), -jnp.inf, F32)], axis=1)
    mx = jnp.maximum(s_t.max(axis=0, keepdims=True), sink)
    p_t = jnp.exp2(s_t - mx)
    inv = 1.0 / (p_t.sum(axis=0, keepdims=True) + jnp.exp2(sink - mx))
    p_t = p_t.astype(BF16)

    nb = SWA_Q_HEADS * n
    o = _dot(v_t(A_BV, 0), p_t[:, :nb]) * inv[:, :nb]
    outs = [o[HEAD_DIM * (h // SWA_GROUP):HEAD_DIM * (h // SWA_GROUP + 1), n * h:n * (h + 1)]
            for h in range(SWA_Q_HEADS)]
    ob_ref[...] = jnp.concatenate(outs, axis=0).T.astype(BF16)
    outs = []
    for p in range(NA_HEADS // 2):
        cols = slice(nb + 2 * n * p, nb + 2 * n * (p + 1))
        o = _dot(v_t(A_CV, p), p_t[:, cols]) * inv[:, cols]
        outs += [o[:HEAD_DIM, :n], o[HEAD_DIM:, n:]]
    oc_ref[...] = jnp.concatenate(outs, axis=0).T.astype(BF16)


def _ctx_attention(sink, z, wqb, wkb, wqc, wkc, caches, l):
    blk = lambda w, j=0: pl.BlockSpec((CTX_LEN, w), lambda b: (b, j))
    cst = lambda w: pl.BlockSpec((1, w), lambda b: (0, 0))
    sds = lambda w: jax.ShapeDtypeStruct((T_CTX, w), BF16)
    cache_w = (128, 128, 512, 512)
    if caches is None:
        cache_blk = [pl.BlockSpec((None, DEPTH, CTX_LEN, w), lambda b: (b, 0, 0, 0)) for w in cache_w]
    else:
        cache_blk = [pl.BlockSpec((None, None, CTX_LEN, w), lambda b: (b, l, 0, 0)) for w in cache_w]
    cache_sds = [jax.ShapeDtypeStruct((N_CTX_SEQ, DEPTH, CTX_LEN, w), F32) for w in cache_w]
    carried = [] if caches is None else list(caches)
    n_in = 6
    return pl.pallas_call(
        functools.partial(_ctx_attn_kernel, layer=l),
        grid=(N_CTX_SEQ,),
        in_specs=[pl.BlockSpec(memory_space=pltpu.SMEM), blk(ATT_W, Z_ATT // ATT_W),
                  cst(512), cst(128), cst(512), cst(512)]
                 + [pl.BlockSpec(memory_space=pl.ANY)] * len(carried),
        out_specs=[blk(512), blk(512)] + cache_blk,
        out_shape=[sds(512), sds(512)] + cache_sds,
        input_output_aliases={n_in + j: 2 + j for j in range(len(carried))},
        compiler_params=_cparams(("arbitrary",)),
        name="ctx_attention",
    )(sink, z, wqb, wkb, wqc, wkc, *carried)


SWA_QBLK = 128
SWA_SPAN = SWA_QBLK + 2 * SWA_WINDOW
SWA_BLOCKS_PER_STEP = 4


def _swa_kernel(sink_ref, qt_ref, k_ref, vt_ref, kc_ref, vc_ref, o_ref, kctx_scr, vctx_scr, *, layer):
    step = pl.program_id(1)
    nq = SWA_QBLK

    @pl.when(step == 0)
    def _():
        kctx_scr[...] = kc_ref[...].T.astype(BF16)
        vctx_scr[...] = vc_ref[...].astype(BF16)

    kctx = kctx_scr[...]
    starts, s_lat, s_ctx = [], [], []
    for blk in range(SWA_BLOCKS_PER_STEP):
        i = step * SWA_BLOCKS_PER_STEP + blk
        start = pl.multiple_of(jnp.clip(nq * (i - 1), 0, LAT_LEN - SWA_SPAN), nq)
        starts.append(start)
        kwin = k_ref[pl.ds(start, SWA_SPAN), :]
        kpos = start + lax.broadcasted_iota(jnp.int32, (SWA_SPAN, nq), 0)
        qpos = nq * i + lax.broadcasted_iota(jnp.int32, (SWA_SPAN, nq), 1)
        band = jnp.where(jnp.abs(kpos - qpos) <= SWA_WINDOW, 0.0, -jnp.inf)
        for kvh in range(SWA_KV_HEADS):
            placed = []
            for h in range(SWA_GROUP * kvh, SWA_GROUP * (kvh + 1)):
                rows = qt_ref[HEAD_DIM * h:HEAD_DIM * (h + 1), nq * blk:nq * (blk + 1)]
                z = jnp.zeros_like(rows)
                placed.append(jnp.concatenate([rows, z] if kvh == 0 else [z, rows], axis=0))
            q4 = jnp.concatenate(placed, axis=1)
            s_lat.append(_dot(kwin, q4) + jnp.concatenate([band] * SWA_GROUP, axis=1))
            s_ctx.append(_dot(kctx, q4))
    s_lat = jnp.concatenate(s_lat, axis=1)
    s_ctx = jnp.concatenate(s_ctx, axis=1)

    sink = jnp.concatenate(
        [jnp.full((1, nq), sink_ref[layer, h] * LOG2E, F32) for h in range(SWA_Q_HEADS)]
        * SWA_BLOCKS_PER_STEP, axis=1)
    mx = jnp.maximum(jnp.maximum(s_lat.max(axis=0, keepdims=True),
                                 s_ctx.max(axis=0, keepdims=True)), sink)
    p_lat = jnp.exp2(s_lat - mx)
    p_ctx = jnp.exp2(s_ctx - mx)
    inv = 1.0 / (p_lat.sum(axis=0, keepdims=True) + p_ctx.sum(axis=0, keepdims=True)
                 + jnp.exp2(sink - mx))
    p_lat = p_lat.astype(BF16)
    o_ctx = _dot(vctx_scr[...], p_ctx.astype(BF16))
    width = SWA_Q_HEADS * nq
    for blk, start in enumerate(starts):
        cols = slice(width * blk, width * (blk + 1))
        vwin_t = vt_ref[:, pl.ds(start, SWA_SPAN)]
        o = (_dot(vwin_t, p_lat[:, cols]) + o_ctx[:, cols]) * inv[:, cols]
        outs = [o[HEAD_DIM * (h // SWA_GROUP):HEAD_DIM * (h // SWA_GROUP + 1), nq * h:nq * (h + 1)]
                for h in range(SWA_Q_HEADS)]
        o_ref[nq * blk:nq * (blk + 1), :] = jnp.concatenate(outs, axis=0).T.astype(BF16)


def _swa_attention(sink, qb_t, kb, vb_t, cache_k, cache_v, l):
    rows = SWA_QBLK * SWA_BLOCKS_PER_STEP
    steps = LAT_LEN // rows
    cache = pl.BlockSpec((None, None, 128, PAST_LEN), lambda b, i: (b, l, 0, 0))
    return pl.pallas_call(
        functools.partial(_swa_kernel, layer=l),
        grid=(N_LAT_SEQ, steps),
        in_specs=[pl.BlockSpec(memory_space=pltpu.SMEM),
                  pl.BlockSpec((512, rows), lambda b, i: (0, b * steps + i)),
                  pl.BlockSpec((LAT_LEN, 128), lambda b, i: (b, 0)),
                  pl.BlockSpec((128, LAT_LEN), lambda b, i: (0, b)),
                  cache, cache],
        out_specs=pl.BlockSpec((rows, 512), lambda b, i: (b * steps + i, 0)),
        out_shape=jax.ShapeDtypeStruct((T_LAT, 512), BF16),
        scratch_shapes=[pltpu.VMEM((PAST_LEN, 128), BF16), pltpu.VMEM((128, PAST_LEN), BF16)],
        compiler_params=_cparams(("arbitrary", "arbitrary")),
        name="swa_attention",
    )(sink, qb_t, kb, vb_t, cache_k, cache_v)


LAT_ROWS = LAT_LEN // GRID_W
NA_WIN_ROWS = min(NA_ROWS, LAT_ROWS)
NA_KEYS = NA_WIN_ROWS * GRID_W
NA_ROWS_PER_STEP = 4


def _na_kernel(q_ref, k_ref, v_ref, kc_ref, vc_ref, rp_ref, o_ref, kctx_scr, vctx_scr, bias_ref):
    step = pl.program_id(1)
    low = lax.broadcasted_iota(jnp.int32, (1, 128), 1) < HEAD_DIM

    @pl.when(step == 0)
    def _():
        kctx_scr[...] = kc_ref[...].astype(BF16)
        vctx_scr[...] = vc_ref[...].astype(BF16)
        cq = lax.broadcasted_iota(jnp.int32, (GRID_W, 128), 0)
        ck = lax.broadcasted_iota(jnp.int32, (GRID_W, 128), 1) % GRID_W
        cs = jnp.clip(cq - NA_COLS // 2, 0, GRID_W - NA_COLS)
        window = jnp.where((ck >= cs) & (ck < cs + NA_COLS), 0.0, -jnp.inf)
        for h in range(NA_HEADS):
            rows = [jnp.broadcast_to(rp_ref[h, dr:dr + 1, :], (GRID_W, 128))
                    for dr in range(2 * NA_ROWS - 1)]
            left = [pltpu.roll(x, 0, 1, stride=1, stride_axis=0) for x in rows[:-1]]
            right = [pltpu.roll(x, GRID_W, 1, stride=1, stride_axis=0) for x in rows[1:]]
            for j in range(2 * NA_ROWS - 2):
                bias_ref[h, j] = jnp.where(low, left[j], right[j]) * LOG2E + window

    npair = NA_HEADS // 2
    windows = []
    s_lat, s_ctx = [], []
    for rr in range(NA_ROWS_PER_STEP):
        r = step * NA_ROWS_PER_STEP + rr
        rs = jnp.clip(r - NA_ROWS // 2, 0, LAT_ROWS - NA_WIN_ROWS)
        start = pl.multiple_of(rs * GRID_W, GRID_W)
        base = rs - r + NA_ROWS - 1
        windows.append(start)
        for p in range(npair):
            sl = slice(128 * p, 128 * (p + 1))
            q = q_ref[GRID_W * rr:GRID_W * (rr + 1), sl]
            zero = jnp.zeros_like(q)
            q2 = jnp.concatenate([jnp.where(low, q, zero), jnp.where(low, zero, q)], axis=0)
            bias = jnp.concatenate(
                [jnp.concatenate([bias_ref[2 * p + hh, base + 2 * w] for w in range(NA_WIN_ROWS // 2)],
                                 axis=1) for hh in range(2)], axis=0)
            s_lat.append(_nt_dot(q2, k_ref[pl.ds(start, NA_KEYS), sl]) + bias)
            s_ctx.append(_dot(q2, kctx_scr[sl, :]))
    s_lat = jnp.concatenate(s_lat, axis=0)
    s_ctx = jnp.concatenate(s_ctx, axis=0)
    mx = jnp.maximum(s_lat.max(axis=-1, keepdims=True), s_ctx.max(axis=-1, keepdims=True))
    p_lat = jnp.exp2(s_lat - mx)
    p_ctx = jnp.exp2(s_ctx - mx)
    inv = 1.0 / (p_lat.sum(axis=-1, keepdims=True) + p_ctx.sum(axis=-1, keepdims=True))
    p_lat = p_lat.astype(BF16)
    p_ctx = p_ctx.astype(BF16)
    for rr, start in enumerate(windows):
        tiles = []
        for p in range(npair):
            sl = slice(128 * p, 128 * (p + 1))
            first = 2 * GRID_W * (npair * rr + p)
            rows = slice(first, first + 2 * GRID_W)
            o2 = (_dot(p_lat[rows], v_ref[pl.ds(start, NA_KEYS), sl])
                  + _nt_dot(p_ctx[rows], vctx_scr[sl, :])) * inv[rows]
            tiles.append(jnp.where(low, o2[:GRID_W], o2[GRID_W:]))
        o_ref[GRID_W * rr:GRID_W * (rr + 1), :] = jnp.concatenate(tiles, axis=1).astype(BF16)


def _na_attention(qc, kc, vc, cache_k, cache_v, rp_cyc, l):
    steps = LAT_ROWS // NA_ROWS_PER_STEP
    rows = NA_ROWS_PER_STEP * GRID_W
    kv = pl.BlockSpec((LAT_LEN, 512), lambda b, r: (b, 0))
    cache = pl.BlockSpec((None, None, 512, PAST_LEN), lambda b, r: (b, l, 0, 0))
    return pl.pallas_call(
        _na_kernel,
        grid=(N_LAT_SEQ, steps),
        in_specs=[pl.BlockSpec((rows, 512), lambda b, r: (b * steps + r, 0)),
                  kv, kv, cache, cache,
                  pl.BlockSpec((None, NA_HEADS, 2 * NA_ROWS - 1, 128), lambda b, r: (l, 0, 0, 0))],
        out_specs=pl.BlockSpec((rows, 512), lambda b, r: (b * steps + r, 0)),
        out_shape=jax.ShapeDtypeStruct((T_LAT, 512), BF16),
        scratch_shapes=[pltpu.VMEM((512, PAST_LEN), BF16), pltpu.VMEM((512, PAST_LEN), BF16),
                        pltpu.VMEM((NA_HEADS, 2 * NA_ROWS - 2, GRID_W, 128), F32)],
        compiler_params=_cparams(("arbitrary", "arbitrary")),
        name="na_attention",
    )(qc, kc, vc, cache_k, cache_v, rp_cyc)


def _na_table_rows(rpb):
    pad = jnp.zeros(rpb.shape[:-1] + (128 - (2 * NA_COLS - 1),), F32)
    return jnp.concatenate([rpb[..., NA_COLS - 1:], pad, rpb[..., :NA_COLS - 1]], axis=-1)


def _gla_constants():
    c = GLA_C
    t = np.arange(c)[:, None]
    u = np.arange(c)[None, :]
    blocks = []
    for k in range(GLA_LEVELS):
        b = 1 << k
        m = ((t >> k) | 1) * b - 1
        query = ((t >> k) & 1) == 1
        blocks.append(np.where(query, (u > m) & (u <= t), (u > t) & (u <= m)))
    blocks.append(u <= t)
    blocks.append(u > t)
    blocks.append(np.ones((16, c), bool))
    fwd = np.concatenate(blocks, axis=0).astype(np.float32)
    bwd_blocks = [blk[::-1, ::-1] for blk in blocks]
    bwd = np.concatenate(bwd_blocks, axis=0).astype(np.float32)
    seg = np.stack([fwd, bwd])
    seg = np.concatenate([seg, seg], axis=-1)

    s = np.arange(c)[None, :]
    x = t ^ s
    lev = np.where(x == 0, GLA_LEVELS, np.floor(np.log2(np.maximum(x, 1))).astype(np.int64))
    lev_f = np.where(s <= t, lev, -1)
    lev_b = np.where(s >= t, lev, -1)
    levmap = np.stack([np.tile(lev_f, (1, GLA_HEADS)), np.tile(lev_b, (1, GLA_HEADS))])
    return seg, levmap.astype(np.int32)


def _gla_chunks(chains):
    c = GLA_C
    lane_head = lax.broadcasted_iota(jnp.int32, (1, GLA_HEADS * GLA_DK), 1) // GLA_DK
    row_head = lax.broadcasted_iota(jnp.int32, (GLA_HEADS * GLA_DV, 1), 0) // GLA_DV
    head_sel = [jnp.where(lane_head == h, 1.0, 0.0).astype(BF16) for h in range(GLA_HEADS)]

    def operands(q, k, e=None):
        if e is not None:
            q, k = q * e, k * e
        return q, jnp.concatenate([k * head_sel[h] for h in range(GLA_HEADS)], axis=0)

    segs = []
    for (_, _, _, lr_ref, w2_ref, ba_ref, seg_ref, *_) in chains:
        x = _dot(lr_ref[...], w2_ref[...]) + ba_ref[...]
        la = (jnp.minimum(x, 0.0) - jnp.log(1.0 + jnp.exp(-jnp.abs(x)))) * (1.0 / GLA_TAU)
        la_hi = la.astype(BF16)
        la_lo = (la - la_hi.astype(F32)).astype(BF16)
        segs.append(_dot(seg_ref[...], jnp.concatenate([la_hi, la_lo], axis=0)))

    def decay(seg, block):
        return jnp.exp(seg[c * block:c * (block + 1)]).astype(BF16)

    qs = [ch[0][...] * (GLA_DK ** -0.5) for ch in chains]
    ks = [ch[1][...] for ch in chains]
    order = [GLA_LEVELS] + list(range(GLA_LEVELS))
    prepare = lambda lvl: [operands(qs[n], ks[n], None if lvl == GLA_LEVELS else decay(segs[n], lvl))
                           for n in range(len(chains))]
    attn = [None] * len(chains)
    ops = prepare(order[0])
    for i, lvl in enumerate(order):
        scores = [_nt_dot(q, kbd) for q, kbd in ops]
        if i + 1 < len(order):
            ops = prepare(order[i + 1])
        for n, ch in enumerate(chains):
            attn[n] = jnp.where(ch[7][...] == lvl, scores[n], 0.0 if attn[n] is None else attn[n])

    new_states = []
    for n, (_, _, v_ref, _, _, _, _, _, o_ref, st_ref) in enumerate(chains):
        seg = segs[n]
        v = v_ref[...]
        a = attn[n].astype(BF16)
        tot = seg[c * (GLA_LEVELS + 2):c * (GLA_LEVELS + 2) + 1]
        q_in = qs[n] * decay(seg, GLA_LEVELS)
        k_in = ks[n] * decay(seg, GLA_LEVELS + 1)
        state = st_ref[...]
        o_inter = _nt_dot(q_in, state.astype(BF16))
        for h in range(GLA_HEADS):
            sl = slice(GLA_DV * h, GLA_DV * (h + 1))
            o_ref[:, sl] = o_inter[:, sl] + _dot(a[:, c * h:c * (h + 1)], v[:, sl])
        upd = _tn_dot(v, k_in)
        new_state = state * jnp.exp(tot) + jnp.where(row_head == lane_head, upd, 0.0)
        st_ref[...] = new_state
        new_states.append(new_state)
    return new_states


GLA_GROUPS = 3
GLA_GROUP_ROWS = T_ALL // GLA_GROUPS
GLA_CTX_GROUPS = T_CTX // GLA_GROUP_ROWS
GLA_STEPS = GLA_GROUP_ROWS // GLA_C
GLA_CTX_CHUNKS = CTX_LEN // GLA_C
GLA_LAT_CHUNKS = LAT_LEN // GLA_C


def _gla_kernel(qf, kf, vf, lrf, qb, kb, vb, lrb, w2_ref, ba_ref, seg_ref, lev_ref, s0f, s0b,
                of_ref, ob_ref, sff_ref, sfb_ref, states):
    step = pl.program_id(0)

    @pl.when(step % GLA_CTX_CHUNKS == 0)
    def _():
        for d in range(2):
            for g in range(GLA_CTX_GROUPS):
                states[d, g] = jnp.zeros(states.shape[2:], F32)

    @pl.when(step % GLA_LAT_CHUNKS == 0)
    def _():
        for d, s0 in enumerate((s0f, s0b)):
            for h in range(GLA_HEADS):
                row = [jnp.zeros((GLA_DV, GLA_DK), F32)] * GLA_HEADS
                row[h] = s0[h]
                states[d, GLA_GROUPS - 1, GLA_DV * h:GLA_DV * (h + 1), :] = jnp.concatenate(row, axis=1)

    chains = []
    for g in range(GLA_GROUPS):
        chains.append((qf.at[g], kf.at[g], vf.at[g], lrf.at[g], w2_ref.at[0], ba_ref.at[0],
                       seg_ref.at[0], lev_ref.at[0], of_ref.at[g], states.at[0, g]))
        chains.append((qb.at[g], kb.at[g], vb.at[g], lrb.at[g], w2_ref.at[1], ba_ref.at[1],
                       seg_ref.at[1], lev_ref.at[1], ob_ref.at[g], states.at[1, g]))
    new_states = _gla_chunks(chains)

    @pl.when(step % GLA_CTX_CHUNKS == GLA_CTX_CHUNKS - 1)
    def _():
        for g in range(GLA_CTX_GROUPS):
            for d, out in enumerate((sff_ref, sfb_ref)):
                st = new_states[2 * g + d]
                for h in range(GLA_HEADS):
                    out[g, h] = st[GLA_DV * h:GLA_DV * (h + 1), GLA_DK * h:GLA_DK * (h + 1)].T


def _gla(z, w2p, ba, seg, levmap, s0t, l):
    c = GLA_C
    z3 = z.reshape(GLA_GROUPS, GLA_GROUP_ROWS, Z_W)
    fwd = lambda s: s
    bwd = lambda s: GLA_STEPS - 1 - s
    seqs_per_group = GLA_GROUP_ROWS // CTX_LEN

    def chunk_specs(pos):
        return [pl.BlockSpec((GLA_GROUPS, c, 256), lambda s: (0, pos(s), Z_AQ // 256)),
                pl.BlockSpec((GLA_GROUPS, c, 256), lambda s: (0, pos(s), Z_AK // 256)),
                pl.BlockSpec((GLA_GROUPS, c, 512), lambda s: (0, pos(s), Z_AV // 512)),
                pl.BlockSpec((GLA_GROUPS, c, 128), lambda s: (0, pos(s), Z_LR // 128))]

    def s0_spec(d, pos):
        return pl.BlockSpec((None, None, None, GLA_HEADS, GLA_DV, GLA_DK),
                            lambda s: (pos(s) // GLA_LAT_CHUNKS, l, d, 0, 0, 0))

    def out_spec(pos):
        return pl.BlockSpec((GLA_GROUPS, c, 512), lambda s: (0, pos(s), 0))

    def sfin_spec(pos):
        return pl.BlockSpec((GLA_CTX_GROUPS, None, GLA_HEADS, GLA_DK, GLA_DV),
                            lambda s: (0, pos(s) // GLA_CTX_CHUNKS, 0, 0, 0))

    state_sds = jax.ShapeDtypeStruct((GLA_CTX_GROUPS, seqs_per_group, GLA_HEADS, GLA_DK, GLA_DV), F32)
    out_sds = jax.ShapeDtypeStruct((GLA_GROUPS, GLA_GROUP_ROWS, 512), F32)
    o_f, o_b, sf, sb = pl.pallas_call(
        _gla_kernel,
        grid=(GLA_STEPS,),
        in_specs=chunk_specs(fwd) + chunk_specs(bwd) + [
            pl.BlockSpec((None, 2, 128, 256), lambda s: (l, 0, 0, 0)),
            pl.BlockSpec((None, 2, 1, 256), lambda s: (l, 0, 0, 0)),
            pl.BlockSpec((2, GLA_GROWS, 2 * c), lambda s: (0, 0, 0)),
            pl.BlockSpec((2, c, GLA_HEADS * c), lambda s: (0, 0, 0)),
            s0_spec(0, fwd), s0_spec(1, bwd),
        ],
        out_specs=[out_spec(fwd), out_spec(bwd), sfin_spec(fwd), sfin_spec(bwd)],
        out_shape=[out_sds, out_sds, state_sds, state_sds],
        scratch_shapes=[pltpu.VMEM((2, GLA_GROUPS, GLA_HEADS * GLA_DV, GLA_HEADS * GLA_DK), F32)],
        compiler_params=_cparams(("arbitrary",)),
        name="gla",
    )(z3, z3, z3, z3, z3, z3, z3, z3, w2p, ba, seg, levmap, s0t, s0t)
    flat = lambda a: a.reshape((-1,) + a.shape[2:])
    return flat(o_f), flat(o_b), flat(sf), flat(sb)


def _cast_kernel(*refs):
    n = len(refs) // 2
    for src, dst in zip(refs[:n], refs[n:]):
        dst[...] = src[...].astype(dst.dtype)


def _cast_bf16(*weights):
    spec = lambda w: pl.BlockSpec((None,) + w.shape[1:], lambda l: (l, 0, 0))
    return pl.pallas_call(
        _cast_kernel,
        grid=(DEPTH,),
        in_specs=[spec(w) for w in weights],
        out_specs=[spec(w) for w in weights],
        out_shape=[jax.ShapeDtypeStruct(w.shape, BF16) for w in weights],
        compiler_params=_cparams(("arbitrary",)),
        name="cast_bf16",
    )(*weights)


def _post_mixer_kernel(*refs, nk, n_x, n_y):
    x_refs, refs = refs[:n_x], refs[n_x:]
    (of_ref, obk_ref, ar_ref, gates_ref, bc_ref, bl_ref, cc_ref, cl_ref, wpa_ref, wpb_ref, wpc_ref,
     wo_ref, gn_ref, g2_ref, mod_ref, w1_ref, w2_ref) = refs[:17]
    y_refs = refs[17:17 + n_y]
    w1_scr, w2_scr, h_scr, x1_scr, acc_scr = refs[17 + n_y:]
    s = pl.program_id(0)
    d = D_MODEL
    tm = of_ref.shape[0]
    th = w1_ref.shape[1]
    ctx = jnp.maximum(s - (nk - 1), 0) < T_CTX // tm

    def put_y(val):
        if n_y == 1:
            y_refs[0][...] = val
        else:
            @pl.when(ctx)
            def _():
                y_refs[0][...] = val

            @pl.when(jnp.logical_not(ctx))
            def _():
                y_refs[1][...] = val

    def merged_residual():
        o = of_ref[...] + obk_ref[...]
        heads = []
        for h in range(GLA_HEADS):
            oh = o[:, GLA_DV * h:GLA_DV * (h + 1)]
            ms = jnp.mean(oh * oh, axis=-1, keepdims=True)
            heads.append(oh * lax.rsqrt(ms + EPS) * gn_ref[...])
        oa = (jnp.concatenate(heads, axis=1) * _silu(ar_ref[...].astype(F32))).astype(BF16)
        ob = jnp.where(ctx, bc_ref[...], bl_ref[...])
        oc = jnp.where(ctx, cc_ref[...], cl_ref[...])
        gate = lambda j: _sigmoid(gates_ref[:, j * d:(j + 1) * d].astype(F32))
        merged = (gate(0) * _dot(oa, wpa_ref[...]) + gate(1) * _dot(ob, wpb_ref[...])
                  + gate(2) * _dot(oc, wpc_ref[...]))
        return (_token_rows(x_refs, ctx)
                + mod_ref[:, 2 * d:3 * d] * _dot(merged.astype(BF16), wo_ref[...]))

    def normed(x1):
        return _norm_mod(x1, g2_ref[...], mod_ref[:, 3 * d:4 * d], mod_ref[:, 4 * d:5 * d]).astype(BF16)

    def act(u):
        return jnp.square(jnp.maximum(u, 0.0)).astype(BF16)

    @pl.when(s == 0)
    def _():
        x1 = merged_residual()
        x1_scr[...] = x1
        h_scr[...] = normed(x1)

    @pl.when(s < nk)
    def _():
        col = pl.multiple_of(s * th, th)
        w1 = w1_ref[...].astype(BF16)
        w2 = w2_ref[...].astype(BF16)
        w1_scr[:, pl.ds(col, th)] = w1
        w2_scr[pl.ds(col, th), :] = w2
        part = _dot(act(_dot(h_scr[...], w1)), w2)

        @pl.when(s == 0)
        def _():
            acc_scr[...] = part

        @pl.when(s > 0)
        def _():
            acc_scr[...] += part

        @pl.when(s == nk - 1)
        def _():
            put_y(x1_scr[...] + mod_ref[:, 5 * d:6 * d] * acc_scr[...])

    @pl.when(s >= nk)
    def _():
        x1 = merged_residual()
        out = _dot(act(_dot(normed(x1), w1_scr[...])), w2_scr[...])
        put_y(x1 + mod_ref[:, 5 * d:6 * d] * out)


def _post_mixer(x_parts, o_fwd, o_bwd, z, ob_ctx, ob_lat, oc_ctx, oc_lat, wpa, wpb, wpc, wo, gn, norm2_w,
                mod4, w1, w2, l, split_out):
    tm, th = 256, 512
    nk = MLP_HIDDEN // th
    ctx_blocks = T_CTX // tm
    row = lambda s: jnp.maximum(s - (nk - 1), 0)
    chunk = lambda s: jnp.minimum(s, nk - 1)
    ctx_idx = lambda s: (jnp.minimum(row(s), ctx_blocks - 1), 0)
    lat_idx = lambda s: (jnp.maximum(row(s) - ctx_blocks, 0), 0)
    rows = lambda w, j=0: pl.BlockSpec((tm, w), lambda s: (row(s), j))
    resident = lambda k: pl.BlockSpec((None, k, D_MODEL), lambda s: (l, 0, 0),
                                      pipeline_mode=pl.Buffered(1))
    if split_out:
        out_specs = [pl.BlockSpec((tm, D_MODEL), ctx_idx), pl.BlockSpec((tm, D_MODEL), lat_idx)]
        out_shape = [jax.ShapeDtypeStruct((T_CTX, D_MODEL), F32), jax.ShapeDtypeStruct((T_LAT, D_MODEL), F32)]
    else:
        out_specs = [rows(D_MODEL)]
        out_shape = [jax.ShapeDtypeStruct((T_ALL, D_MODEL), F32)]
    return pl.pallas_call(
        functools.partial(_post_mixer_kernel, nk=nk, n_x=len(x_parts), n_y=len(out_specs)),
        grid=(nk - 1 + T_ALL // tm,),
        in_specs=_token_specs(x_parts, tm, row) + [
            rows(512), rows(512), rows(512, Z_AR // 512), rows(3 * D_MODEL),
            pl.BlockSpec((tm, 512), ctx_idx), pl.BlockSpec((tm, 512), lat_idx),
            pl.BlockSpec((tm, 512), ctx_idx), pl.BlockSpec((tm, 512), lat_idx),
            resident(512), resident(512), resident(512), resident(D_MODEL),
            pl.BlockSpec((None, 1, GLA_DV), lambda s: (l, 0, 0)),
            pl.BlockSpec((None, 1, D_MODEL), lambda s: (l, 0, 0)),
            pl.BlockSpec((None, None, 1, 6 * D_MODEL),
                         lambda s: (l, _group_of_rows(row(s), tm), 0, 0)),
            pl.BlockSpec((None, D_MODEL, th), lambda s: (l, 0, chunk(s))),
            pl.BlockSpec((None, th, D_MODEL), lambda s: (l, chunk(s), 0)),
        ],
        out_specs=out_specs,
        out_shape=out_shape,
        scratch_shapes=[pltpu.VMEM((D_MODEL, MLP_HIDDEN), BF16), pltpu.VMEM((MLP_HIDDEN, D_MODEL), BF16),
                        pltpu.VMEM((tm, D_MODEL), BF16), pltpu.VMEM((tm, D_MODEL), F32),
                        pltpu.VMEM((tm, D_MODEL), F32)],
        compiler_params=_cparams(("arbitrary",)),
        name="post_mixer",
    )(*x_parts, o_fwd, o_bwd, z, z, ob_ctx, ob_lat, oc_ctx, oc_lat, wpa, wpb, wpc, wo, gn, norm2_w, mod4,
      w1, w2)


def _rope_tables():
    t = jnp.arange(LAT_LEN)
    row = (t // GRID_W).astype(F32)
    col = (t % GRID_W).astype(F32)
    nf = HEAD_DIM // 4
    inv_freq = ROPE_BASE ** (-jnp.arange(nf, dtype=F32) / nf)
    ang_r = row[:, None] * inv_freq[None, :]
    ang_c = col[:, None] * inv_freq[None, :]
    cos = jnp.concatenate([jnp.cos(ang_r)] * 2 + [jnp.cos(ang_c)] * 2, axis=1)
    sin = jnp.concatenate([-jnp.sin(ang_r), jnp.sin(ang_r), -jnp.sin(ang_c), jnp.sin(ang_c)], axis=1)
    return jnp.tile(cos, (1, 2)), jnp.tile(sin, (1, 2))


def kernel(x_prompt, x_sample, state_gla, cache_swa_k, cache_swa_v, cache_na_k, cache_na_v, c,
           c_ctx, w_mod, b_mod, norm1, norm2, w_in, w_a2_f, b_a_f, w_a2_b, b_a_b, gla_onorm,
           qn_swa, kn_swa, sink_swa, qn_na, kn_na, rpb_na, w_pa, w_pb, w_pc, w_o, w_fc1, w_fc2):
    d = D_MODEL
    x_parts = (x_prompt.reshape(T_CTX, d), x_sample.reshape(T_LAT, d))

    cond8 = jnp.zeros((8, d), F32).at[0].set(c_ctx).at[1:1 + N_LAT_SEQ].set(c)
    mod4 = _modulation(cond8, w_mod, b_mod).reshape(DEPTH, 8, 1, 6 * d)

    w_in_p = _w_in_layout(jnp.swapaxes(w_in, 1, 2))
    wpa, wpb, wpc, wo = _cast_bf16(w_pa, w_pb, w_pc, w_o)
    norm1r = norm1.reshape(DEPTH, 1, d)
    norm2r = norm2.reshape(DEPTH, 1, d)
    gnr = gla_onorm.reshape(DEPTH, 1, GLA_DV)
    w2p = jnp.zeros((DEPTH, 2, 128, GLA_HEADS * GLA_DK), F32)
    w2p = w2p.at[:, 0, 0:GLA_LOWRANK].set(w_a2_f).at[:, 1, GLA_LOWRANK:2 * GLA_LOWRANK].set(w_a2_b)
    w2p = w2p.astype(BF16)
    ba = jnp.stack([b_a_f, b_a_b], axis=1).reshape(DEPTH, 2, 1, GLA_HEADS * GLA_DK)
    seg_np, lev_np = _gla_constants()
    seg = jnp.asarray(seg_np, BF16)
    levmap = jnp.asarray(lev_np)
    s0t = jnp.swapaxes(state_gla, -1, -2)
    cos_t, sin_t = _rope_tables()
    rp_cyc = _na_table_rows(rpb_na)
    feat_major = lambda a: a.transpose(0, 1, 3, 4, 2).reshape(N_LAT_SEQ, DEPTH, -1, PAST_LEN)
    csk, csv, cnk, cnv = (feat_major(a) for a in (cache_swa_k, cache_swa_v, cache_na_k, cache_na_v))

    st_l = []
    caches = None
    for l in range(DEPTH):
        z = _inproj(x_parts, norm1r, mod4, w_in_p, l)
        wqb = jnp.tile(qn_swa[l], 8)[None, :]
        wkb = jnp.tile(kn_swa[l], 2)[None, :]
        wqc = jnp.tile(qn_na[l], 8)[None, :]
        wkc = jnp.tile(kn_na[l], 8)[None, :]
        qb, kb, vb, qc, kc, vc = _attn_prep(z, wqb, wkb, wqc, wkc, cos_t, sin_t)
        o_fwd, o_bwd, sfin_f, sfin_b = _gla(z, w2p, ba, seg, levmap, s0t, l)
        ob_ctx, oc_ctx, *caches = _ctx_attention(sink_swa, z, wqb, wkb, wqc, wkc, caches, l)
        ob_lat = _swa_attention(sink_swa, qb, kb, vb, csk, csv, l)
        oc_lat = _na_attention(qc, kc, vc, cnk, cnv, rp_cyc, l)
        x_parts = _post_mixer(x_parts, o_fwd, o_bwd, z, ob_ctx, ob_lat, oc_ctx, oc_lat, wpa, wpb, wpc, wo,
                              gnr, norm2r, mod4, w_fc1, w_fc2, l, split_out=(l == DEPTH - 1))

        st_l += [sfin_f, sfin_b]

    y_prompt = x_parts[0].reshape(N_CTX_SEQ, CTX_LEN, d)
    y_sample = x_parts[1].reshape(N_LAT_SEQ, LAT_LEN, d)
    new_state = jnp.stack(st_l, axis=1).reshape(
        N_CTX_SEQ, DEPTH, 2, GLA_HEADS, GLA_DK, GLA_DV)

    swa_k, swa_v, na_k, na_v = caches
    kv_shape = lambda heads: (N_CTX_SEQ, DEPTH, CTX_LEN, heads, HEAD_DIM)
    return (y_prompt, y_sample, new_state,
            swa_k.reshape(kv_shape(SWA_KV_HEADS)), swa_v.reshape(kv_shape(SWA_KV_HEADS)),
            na_k.reshape(kv_shape(NA_HEADS)), na_v.reshape(kv_shape(NA_HEADS)))
```

```python
import functools

import numpy as np
import jax
import jax.numpy as jnp
from jax import lax
from jax.experimental import pallas as pl
from jax.experimental.pallas import tpu as pltpu

F32 = jnp.float32
BF16 = jnp.bfloat16

D_MODEL = 1024
DEPTH = 4
N_CTX_SEQ = 16
CTX_LEN = 256
N_LAT_SEQ = 2
LAT_LEN = 1024
PAST_LEN = 512
T_CTX = N_CTX_SEQ * CTX_LEN
T_LAT = N_LAT_SEQ * LAT_LEN
T_ALL = T_CTX + T_LAT
GRID_W = 64
HEAD_DIM = 64
GLA_HEADS = 4
GLA_DK = 64
GLA_DV = 128
GLA_LOWRANK = 16
GLA_TAU = 16.0
SWA_Q_HEADS = 8
SWA_KV_HEADS = 2
SWA_GROUP = 4
SWA_WINDOW = 128
NA_HEADS = 8
NA_ROWS = 8
NA_COLS = 16
MLP_HIDDEN = 4 * D_MODEL
ROPE_BASE = 10000.0
EPS = 1e-6
LOG2E = 1.4426950408889634
QK_SCALE = HEAD_DIM ** -0.5 * LOG2E

Z_GATES = 0
Z_GLA = 3072
Z_AQ, Z_AK, Z_AV, Z_AR = 3072, 3328, 3584, 4096
Z_ATT = 4608
ATT_W = 2304
A_BQ, A_BK, A_BV, A_CQ, A_CK, A_CV = 0, 512, 640, 768, 1280, 1792
Z_LR = 6912
Z_W = 7168

GLA_C = 128
GLA_LEVELS = 7
GLA_GROWS = (GLA_LEVELS + 2) * GLA_C + 16

VMEM_LIMIT = 56 * 1024 * 1024


def _cparams(sem):
    return pltpu.CompilerParams(dimension_semantics=sem, vmem_limit_bytes=VMEM_LIMIT)


def _sigmoid(x):
    return 1.0 / (1.0 + jnp.exp(-x))


def _silu(x):
    return x * _sigmoid(x)


def _nt_dot(a, b):
    return lax.dot_general(a, b, (((1,), (1,)), ((), ())), preferred_element_type=F32)


def _tn_dot(a, b):
    return lax.dot_general(a, b, (((0,), (0,)), ((), ())), preferred_element_type=F32)


def _dot(a, b):
    return jnp.dot(a, b, preferred_element_type=F32)


def _mod_kernel(cond_ref, w_ref, b_ref, o_ref):
    s = _silu(cond_ref[...]).astype(BF16)
    o_ref[...] = _dot(s, w_ref[...].astype(BF16)) + b_ref[...]


def _modulation(cond8, w_mod, b_mod):
    tn = 1024
    return pl.pallas_call(
        _mod_kernel,
        grid=(DEPTH, 6 * D_MODEL // tn),
        in_specs=[
            pl.BlockSpec((8, D_MODEL), lambda l, j: (0, 0)),
            pl.BlockSpec((None, D_MODEL, tn), lambda l, j: (l, 0, j)),
            pl.BlockSpec((None, 1, tn), lambda l, j: (l, 0, j)),
        ],
        out_specs=pl.BlockSpec((None, 8, tn), lambda l, j: (l, 0, j)),
        out_shape=jax.ShapeDtypeStruct((DEPTH, 8, 6 * D_MODEL), F32),
        compiler_params=_cparams(("arbitrary", "arbitrary")),
        name="modulation",
    )(cond8, w_mod, b_mod.reshape(DEPTH, 1, 6 * D_MODEL))


def _group_of_rows(row_block, rows_per_block):
    first = row_block * rows_per_block
    return jnp.maximum(first - T_CTX + LAT_LEN, 0) // LAT_LEN


def _norm_mod(x, g, shift, scale):
    ms = jnp.mean(x * x, axis=-1, keepdims=True)
    y = x * lax.rsqrt(ms + EPS) * g
    return y * (1.0 + scale) + shift


W_IN_SEGMENTS = ((3872, 6944), (0, 1536), (1568, 3872), (1536, 1568))
W_IN_COLS = 6944


def _w_in_layout_kernel(w_ref, o_ref):
    dst = 0
    for lo, hi in W_IN_SEGMENTS:
        o_ref[dst:dst + hi - lo, :] = w_ref[lo:hi, :].astype(BF16)
        dst += hi - lo
    o_ref[dst:, :] = jnp.zeros((Z_W - dst, o_ref.shape[1]), BF16)


def _w_in_layout(w_in_t):
    tk = 256
    return pl.pallas_call(
        _w_in_layout_kernel,
        grid=(DEPTH, D_MODEL // tk),
        in_specs=[pl.BlockSpec((None, W_IN_COLS, tk), lambda l, r: (l, 0, r))],
        out_specs=pl.BlockSpec((None, Z_W, tk), lambda l, r: (l, 0, r)),
        out_shape=jax.ShapeDtypeStruct((DEPTH, Z_W, D_MODEL), BF16),
        compiler_params=_cparams(("arbitrary", "arbitrary")),
        name="w_in_layout",
    )(w_in_t)


def _token_rows(x_refs, ctx):
    if len(x_refs) == 1:
        return x_refs[0][...]
    return jnp.where(ctx, x_refs[0][...], x_refs[1][...])


def _token_specs(x_parts, tm, row):
    if len(x_parts) == 1:
        return [pl.BlockSpec((tm, D_MODEL), lambda s: (row(s), 0))]
    ctx_blocks = T_CTX // tm
    return [pl.BlockSpec((tm, D_MODEL), lambda s: (jnp.minimum(row(s), ctx_blocks - 1), 0)),
            pl.BlockSpec((tm, D_MODEL), lambda s: (jnp.maximum(row(s) - ctx_blocks, 0), 0))]


def _inproj_kernel(*refs):
    *x_refs, g_ref, mod_ref, w_ref, z_ref = refs
    ctx = pl.program_id(0) < T_CTX // z_ref.shape[0]
    h = _norm_mod(_token_rows(x_refs, ctx), g_ref[...],
                  mod_ref[:, 0:D_MODEL], mod_ref[:, D_MODEL:2 * D_MODEL])
    z_ref[...] = _nt_dot(h.astype(BF16), w_ref[...]).astype(z_ref.dtype)


def _inproj(x_parts, norm_w, mod4, w_in_p, l):
    tm = 512
    return pl.pallas_call(
        _inproj_kernel,
        grid=(T_ALL // tm,),
        in_specs=_token_specs(x_parts, tm, lambda i: i) + [
            pl.BlockSpec((None, 1, D_MODEL), lambda i: (l, 0, 0)),
            pl.BlockSpec((None, None, 1, 6 * D_MODEL), lambda i: (l, _group_of_rows(i, tm), 0, 0)),
            pl.BlockSpec((None, Z_W, D_MODEL), lambda i: (l, 0, 0),
                         pipeline_mode=pl.Buffered(1)),
        ],
        out_specs=pl.BlockSpec((tm, Z_W), lambda i: (i, 0)),
        out_shape=jax.ShapeDtypeStruct((T_ALL, Z_W), BF16),
        compiler_params=_cparams(("arbitrary",)),
        name="inproj",
    )(*x_parts, norm_w, mod4, w_in_p)


def _head_norm_pairs(z_ref, groups):
    same_head = (lax.broadcasted_iota(jnp.int32, (128, 128), 0) // HEAD_DIM
                 == lax.broadcasted_iota(jnp.int32, (128, 128), 1) // HEAD_DIM)
    ones = jnp.where(same_head, 1.0, 0.0).astype(BF16)
    tiles = [[(z_ref[:, col + 128 * p: col + 128 * (p + 1)].astype(F32), w_ref[:, 128 * p:128 * (p + 1)])
              for p in range(width // 128)] for col, width, w_ref in groups]
    sums = [[_dot((x * x).astype(BF16), ones) for x, _ in group] for group in tiles]
    return [[x * lax.rsqrt(ss * (1.0 / HEAD_DIM) + EPS) * w for (x, w), ss in zip(group, gs)]
            for group, gs in zip(tiles, sums)]


def _rope(x, cos, sin_signed):
    n = x.shape[-1]
    lane = lax.broadcasted_iota(jnp.int32, (1, n), 1)
    first = (lane % 32) < 16
    partner = jnp.where(first, pltpu.roll(x, n - 16, 1), pltpu.roll(x, 16, 1))
    return x * cos + partner * sin_signed


def _prep_kernel(z_ref, wqb_ref, wkb_ref, wqc_ref, wkc_ref, cos_ref, sin_ref,
                 qb_ref, kb_ref, vb_ref, qc_ref, kc_ref, vc_ref):
    qscale = QK_SCALE

    qb, kb, qc, kc = _head_norm_pairs(z_ref, [(A_BQ, 512, wqb_ref), (A_BK, 128, wkb_ref),
                                              (A_CQ, 512, wqc_ref), (A_CK, 512, wkc_ref)])

    vb_ref[...] = z_ref[:, A_BV:A_BV + 128].astype(F32).T.astype(BF16)
    vc_ref[...] = z_ref[:, A_CV:A_CV + 512]
    for p in range(4):
        qc_ref[:, 128 * p:128 * (p + 1)] = (qc[p] * qscale).astype(BF16)
        kc_ref[:, 128 * p:128 * (p + 1)] = kc[p].astype(BF16)

    cos = cos_ref[...]
    sin = sin_ref[...]
    for p in range(4):
        qb_ref[128 * p:128 * (p + 1), :] = (_rope(qb[p], cos, sin) * qscale).T.astype(BF16)
    kb_ref[...] = _rope(kb[0], cos, sin).astype(BF16)


def _attn_prep(z, wqb, wkb, wqc, wkc, cos_t, sin_t):
    tm = 512
    lat_blocks = LAT_LEN // tm
    rope_idx = lambda i: (i % lat_blocks, 0)
    row = lambda w: pl.BlockSpec((tm, w), lambda i: (i, 0))
    col = lambda w: pl.BlockSpec((w, tm), lambda i: (0, i))
    cst = lambda w: pl.BlockSpec((1, w), lambda i: (0, 0))
    sds = lambda w: jax.ShapeDtypeStruct((T_LAT, w), BF16)
    sds_t = lambda w: jax.ShapeDtypeStruct((w, T_LAT), BF16)
    return pl.pallas_call(
        _prep_kernel,
        grid=(T_LAT // tm,),
        in_specs=[
            pl.BlockSpec((tm, ATT_W), lambda i: (T_CTX // tm + i, Z_ATT // ATT_W)),
            cst(512), cst(128), cst(512), cst(512),
            pl.BlockSpec((tm, 128), rope_idx),
            pl.BlockSpec((tm, 128), rope_idx),
        ],
        out_specs=[col(512), row(128), col(128), row(512), row(512), row(512)],
        out_shape=[sds_t(512), sds(128), sds_t(128), sds(512), sds(512), sds(512)],
        compiler_params=_cparams(("arbitrary",)),
        name="attn_prep",
    )(z, wqb, wkb, wqc, wkc, cos_t, sin_t)


def _half_rows(t, half):
    z = jnp.zeros((HEAD_DIM, t.shape[1]), t.dtype)
    return jnp.concatenate([t[:HEAD_DIM], z] if half == 0 else [z, t[HEAD_DIM:]], axis=0)


CTX_SEQS_PER_STEP = 2


def _ctx_attn_kernel(sink_ref, z_ref, wqb_ref, wkb_ref, wqc_ref, wkc_ref, *refs, layer):
    ob_ref, oc_ref = refs[-6:-4]
    cache_refs = refs[-4:]
    first_layer = len(refs) == 6
    if first_layer:
        for ref in cache_refs:
            ref[...] = jnp.zeros(ref.shape, F32)
    kbf_ref, vbf_ref, kcf_ref, vcf_ref = cache_refs
    n = CTX_LEN
    nseq = CTX_SEQS_PER_STEP
    qscale = QK_SCALE

    def put(ref, seq, val, cols=slice(None)):
        if first_layer:
            ref[seq, layer, :, cols] = val
        else:
            ref[seq, :, cols] = val

    def seq_rows(x, seq):
        return x[n * seq:n * (seq + 1)]

    def seq_cols(x, seq):
        return x[:, n * seq:n * (seq + 1)]

    def pair(col, p):
        return z_ref[:, col + 128 * p: col + 128 * (p + 1)]

    qb_n, kb_n, qc_n, kc_n = _head_norm_pairs(z_ref, [(A_BQ, 512, wqb_ref), (A_BK, 128, wkb_ref),
                                                      (A_CQ, 512, wqc_ref), (A_CK, 512, wkc_ref)])

    def q_t(tile):
        return (tile * qscale).T.astype(BF16)

    def v_t(col, p):
        return pair(col, p).astype(F32).T.astype(BF16)

    for seq in range(nseq):
        put(vbf_ref, seq, seq_rows(z_ref[:, A_BV:A_BV + 128], seq).astype(F32))
        put(vcf_ref, seq, seq_rows(z_ref[:, A_CV:A_CV + 512], seq).astype(F32))
        put(kbf_ref, seq, seq_rows(kb_n[0], seq))
        for p in range(NA_HEADS // 2):
            put(kcf_ref, seq, seq_rows(kc_n[p], seq), slice(128 * p, 128 * (p + 1)))

    kb = kb_n[0].astype(BF16)
    kc = [t.astype(BF16) for t in kc_n]
    qb_t = [q_t(tile) for tile in qb_n]
    qc_t = [q_t(tile) for tile in qc_n]
    scores = []
    for seq in range(nseq):
        for kvh in range(SWA_KV_HEADS):
            placed = []
            for h in range(SWA_GROUP * kvh, SWA_GROUP * (kvh + 1)):
                rows = seq_cols(qb_t[h // 2][HEAD_DIM * (h % 2):HEAD_DIM * (h % 2 + 1)], seq)
                z = jnp.zeros_like(rows)
                placed.append(jnp.concatenate([rows, z] if kvh == 0 else [z, rows], axis=0))
            scores.append(_dot(seq_rows(kb, seq), jnp.concatenate(placed, axis=1)))
        for p in range(NA_HEADS // 2):
            q = seq_cols(qc_t[p], seq)
            q2 = jnp.concatenate([_half_rows(q, 0), _half_rows(q, 1)], axis=1)
            scores.append(_dot(seq_rows(kc[p], seq), q2))
    s_t = jnp.concatenate(scores, axis=1)

    sink = jnp.concatenate(
        ([jnp.full((1, n), sink_ref[layer, h] * LOG2E, F32) for h in range(SWA_Q_HEADS)]
         + [jnp.full((1, n * NA_HEADS), -jnp.inf, F32)]) * nseq, axis=1)
    mx = jnp.maximum(s_t.max(axis=0, keepdims=True), sink)
    p_t = jnp.exp2(s_t - mx)
    inv = 1.0 / (p_t.sum(axis=0, keepdims=True) + jnp.exp2(sink - mx))
    p_t = p_t.astype(BF16)

    nb = SWA_Q_HEADS * n
    per_seq = (SWA_Q_HEADS + NA_HEADS) * n
    vb_t = v_t(A_BV, 0)
    vc_t = [v_t(A_CV, p) for p in range(NA_HEADS // 2)]
    for seq in range(nseq):
        c0 = per_seq * seq
        o = _dot(seq_cols(vb_t, seq), p_t[:, c0:c0 + nb]) * inv[:, c0:c0 + nb]
        outs = [o[HEAD_DIM * (h // SWA_GROUP):HEAD_DIM * (h // SWA_GROUP + 1), n * h:n * (h + 1)]
                for h in range(SWA_Q_HEADS)]
        ob_ref[n * seq:n * (seq + 1), :] = jnp.concatenate(outs, axis=0).T.astype(BF16)
        outs = []
        for p in range(NA_HEADS // 2):
            cols = slice(c0 + nb + 2 * n * p, c0 + nb + 2 * n * (p + 1))
            o = _dot(seq_cols(vc_t[p], seq), p_t[:, cols]) * inv[:, cols]
            outs += [o[:HEAD_DIM, :n], o[HEAD_DIM:, n:]]
        oc_ref[n * seq:n * (seq + 1), :] = jnp.concatenate(outs, axis=0).T.astype(BF16)


def _ctx_attention(sink, z, wqb, wkb, wqc, wkc, caches, l):
    nseq = CTX_SEQS_PER_STEP
    blk = lambda w, j=0: pl.BlockSpec((nseq * CTX_LEN, w), lambda b: (b, j))
    cst = lambda w: pl.BlockSpec((1, w), lambda b: (0, 0))
    sds = lambda w: jax.ShapeDtypeStruct((T_CTX, w), BF16)
    cache_w = (128, 128, 512, 512)
    if caches is None:
        cache_blk = [pl.BlockSpec((nseq, DEPTH, CTX_LEN, w), lambda b: (b, 0, 0, 0)) for w in cache_w]
    else:
        cache_blk = [pl.BlockSpec((nseq, None, CTX_LEN, w), lambda b: (b, l, 0, 0)) for w in cache_w]
    cache_sds = [jax.ShapeDtypeStruct((N_CTX_SEQ, DEPTH, CTX_LEN, w), F32) for w in cache_w]
    carried = [] if caches is None else list(caches)
    n_in = 6
    return pl.pallas_call(
        functools.partial(_ctx_attn_kernel, layer=l),
        grid=(N_CTX_SEQ // nseq,),
        in_specs=[pl.BlockSpec(memory_space=pltpu.SMEM), blk(ATT_W, Z_ATT // ATT_W),
                  cst(512), cst(128), cst(512), cst(512)]
                 + [pl.BlockSpec(memory_space=pl.ANY)] * len(carried),
        out_specs=[blk(512), blk(512)] + cache_blk,
        out_shape=[sds(512), sds(512)] + cache_sds,
        input_output_aliases={n_in + j: 2 + j for j in range(len(carried))},
        compiler_params=_cparams(("arbitrary",)),
        name="ctx_attention",
    )(sink, z, wqb, wkb, wqc, wkc, *carried)


SWA_QBLK = 128
SWA_SPAN = SWA_QBLK + 2 * SWA_WINDOW
SWA_BLOCKS_PER_STEP = 4


def _swa_kernel(sink_ref, qt_ref, k_ref, vt_ref, kc_ref, vc_ref, o_ref, kctx_scr, vctx_scr, *, layer):
    step = pl.program_id(1)
    nq = SWA_QBLK

    @pl.when(step == 0)
    def _():
        kctx_scr[...] = kc_ref[...].T.astype(BF16)
        vctx_scr[...] = vc_ref[...].astype(BF16)

    kctx = kctx_scr[...]
    starts, s_lat, s_ctx = [], [], []
    for blk in range(SWA_BLOCKS_PER_STEP):
        i = step * SWA_BLOCKS_PER_STEP + blk
        start = pl.multiple_of(jnp.clip(nq * (i - 1), 0, LAT_LEN - SWA_SPAN), nq)
        starts.append(start)
        kwin = k_ref[pl.ds(start, SWA_SPAN), :]
        kpos = start + lax.broadcasted_iota(jnp.int32, (SWA_SPAN, nq), 0)
        qpos = nq * i + lax.broadcasted_iota(jnp.int32, (SWA_SPAN, nq), 1)
        band = jnp.where(jnp.abs(kpos - qpos) <= SWA_WINDOW, 0.0, -jnp.inf)
        for kvh in range(SWA_KV_HEADS):
            placed = []
            for h in range(SWA_GROUP * kvh, SWA_GROUP * (kvh + 1)):
                rows = qt_ref[HEAD_DIM * h:HEAD_DIM * (h + 1), nq * blk:nq * (blk + 1)]
                z = jnp.zeros_like(rows)
                placed.append(jnp.concatenate([rows, z] if kvh == 0 else [z, rows], axis=0))
            q4 = jnp.concatenate(placed, axis=1)
            s_lat.append(_dot(kwin, q4) + jnp.concatenate([band] * SWA_GROUP, axis=1))
            s_ctx.append(_dot(kctx, q4))
    s_lat = jnp.concatenate(s_lat, axis=1)
    s_ctx = jnp.concatenate(s_ctx, axis=1)

    sink = jnp.concatenate(
        [jnp.full((1, nq), sink_ref[layer, h] * LOG2E, F32) for h in range(SWA_Q_HEADS)]
        * SWA_BLOCKS_PER_STEP, axis=1)
    mx = jnp.maximum(jnp.maximum(s_lat.max(axis=0, keepdims=True),
                                 s_ctx.max(axis=0, keepdims=True)), sink)
    p_lat = jnp.exp2(s_lat - mx)
    p_ctx = jnp.exp2(s_ctx - mx)
    inv = 1.0 / (p_lat.sum(axis=0, keepdims=True) + p_ctx.sum(axis=0, keepdims=True)
                 + jnp.exp2(sink - mx))
    p_lat = p_lat.astype(BF16)
    o_ctx = _dot(vctx_scr[...], p_ctx.astype(BF16))
    width = SWA_Q_HEADS * nq
    for blk, start in enumerate(starts):
        cols = slice(width * blk, width * (blk + 1))
        vwin_t = vt_ref[:, pl.ds(start, SWA_SPAN)]
        o = (_dot(vwin_t, p_lat[:, cols]) + o_ctx[:, cols]) * inv[:, cols]
        outs = [o[HEAD_DIM * (h // SWA_GROUP):HEAD_DIM * (h // SWA_GROUP + 1), nq * h:nq * (h + 1)]
                for h in range(SWA_Q_HEADS)]
        o_ref[nq * blk:nq * (blk + 1), :] = jnp.concatenate(outs, axis=0).T.astype(BF16)


def _swa_attention(sink, qb_t, kb, vb_t, cache_k, cache_v, l):
    rows = SWA_QBLK * SWA_BLOCKS_PER_STEP
    steps = LAT_LEN // rows
    cache = pl.BlockSpec((None, None, 128, PAST_LEN), lambda b, i: (b, l, 0, 0))
    return pl.pallas_call(
        functools.partial(_swa_kernel, layer=l),
        grid=(N_LAT_SEQ, steps),
        in_specs=[pl.BlockSpec(memory_space=pltpu.SMEM),
                  pl.BlockSpec((512, rows), lambda b, i: (0, b * steps + i)),
                  pl.BlockSpec((LAT_LEN, 128), lambda b, i: (b, 0)),
                  pl.BlockSpec((128, LAT_LEN), lambda b, i: (0, b)),
                  cache, cache],
        out_specs=pl.BlockSpec((rows, 512), lambda b, i: (b * steps + i, 0)),
        out_shape=jax.ShapeDtypeStruct((T_LAT, 512), BF16),
        scratch_shapes=[pltpu.VMEM((PAST_LEN, 128), BF16), pltpu.VMEM((128, PAST_LEN), BF16)],
        compiler_params=_cparams(("arbitrary", "arbitrary")),
        name="swa_attention",
    )(sink, qb_t, kb, vb_t, cache_k, cache_v)


LAT_ROWS = LAT_LEN // GRID_W
NA_WIN_ROWS = min(NA_ROWS, LAT_ROWS)
NA_KEYS = NA_WIN_ROWS * GRID_W
NA_ROWS_PER_STEP = 4


def _na_kernel(q_ref, k_ref, v_ref, kc_ref, vc_ref, rp_ref, o_ref, kctx_scr, vctx_scr, bias_ref):
    step = pl.program_id(1)
    low = lax.broadcasted_iota(jnp.int32, (1, 128), 1) < HEAD_DIM

    @pl.when(step == 0)
    def _():
        kctx_scr[...] = kc_ref[...].astype(BF16)
        vctx_scr[...] = vc_ref[...].astype(BF16)
        cq = lax.broadcasted_iota(jnp.int32, (GRID_W, 128), 0)
        ck = lax.broadcasted_iota(jnp.int32, (GRID_W, 128), 1) % GRID_W
        cs = jnp.clip(cq - NA_COLS // 2, 0, GRID_W - NA_COLS)
        window = jnp.where((ck >= cs) & (ck < cs + NA_COLS), 0.0, -jnp.inf)
        for h in range(NA_HEADS):
            rows = [jnp.broadcast_to(rp_ref[h, dr:dr + 1, :], (GRID_W, 128))
                    for dr in range(2 * NA_ROWS - 1)]
            left = [pltpu.roll(x, 0, 1, stride=1, stride_axis=0) for x in rows[:-1]]
            right = [pltpu.roll(x, GRID_W, 1, stride=1, stride_axis=0) for x in rows[1:]]
            for j in range(2 * NA_ROWS - 2):
                bias_ref[h, j] = jnp.where(low, left[j], right[j]) * LOG2E + window

    npair = NA_HEADS // 2
    windows = []
    s_lat, s_ctx = [], []
    for rr in range(NA_ROWS_PER_STEP):
        r = step * NA_ROWS_PER_STEP + rr
        rs = jnp.clip(r - NA_ROWS // 2, 0, LAT_ROWS - NA_WIN_ROWS)
        start = pl.multiple_of(rs * GRID_W, GRID_W)
        base = rs - r + NA_ROWS - 1
        windows.append(start)
        for p in range(npair):
            sl = slice(128 * p, 128 * (p + 1))
            q = q_ref[GRID_W * rr:GRID_W * (rr + 1), sl]
            zero = jnp.zeros_like(q)
            q2 = jnp.concatenate([jnp.where(low, q, zero), jnp.where(low, zero, q)], axis=0)
            bias = jnp.concatenate(
                [jnp.concatenate([bias_ref[2 * p + hh, base + 2 * w] for w in range(NA_WIN_ROWS // 2)],
                                 axis=1) for hh in range(2)], axis=0)
            s_lat.append(_nt_dot(q2, k_ref[pl.ds(start, NA_KEYS), sl]) + bias)
            s_ctx.append(_dot(q2, kctx_scr[sl, :]))
    s_lat = jnp.concatenate(s_lat, axis=0)
    s_ctx = jnp.concatenate(s_ctx, axis=0)
    mx = jnp.maximum(s_lat.max(axis=-1, keepdims=True), s_ctx.max(axis=-1, keepdims=True))
    p_lat = jnp.exp2(s_lat - mx)
    p_ctx = jnp.exp2(s_ctx - mx)
    inv = 1.0 / (p_lat.sum(axis=-1, keepdims=True) + p_ctx.sum(axis=-1, keepdims=True))
    p_lat = p_lat.astype(BF16)
    p_ctx = p_ctx.astype(BF16)
    for rr, start in enumerate(windows):
        tiles = []
        for p in range(npair):
            sl = slice(128 * p, 128 * (p + 1))
            first = 2 * GRID_W * (npair * rr + p)
            rows = slice(first, first + 2 * GRID_W)
            o2 = (_dot(p_lat[rows], v_ref[pl.ds(start, NA_KEYS), sl])
                  + _nt_dot(p_ctx[rows], vctx_scr[sl, :])) * inv[rows]
            tiles.append(jnp.where(low, o2[:GRID_W], o2[GRID_W:]))
        o_ref[GRID_W * rr:GRID_W * (rr + 1), :] = jnp.concatenate(tiles, axis=1).astype(BF16)


def _na_attention(qc, kc, vc, cache_k, cache_v, rp_cyc, l):
    steps = LAT_ROWS // NA_ROWS_PER_STEP
    rows = NA_ROWS_PER_STEP * GRID_W
    kv = pl.BlockSpec((LAT_LEN, 512), lambda b, r: (b, 0))
    cache = pl.BlockSpec((None, None, 512, PAST_LEN), lambda b, r: (b, l, 0, 0))
    return pl.pallas_call(
        _na_kernel,
        grid=(N_LAT_SEQ, steps),
        in_specs=[pl.BlockSpec((rows, 512), lambda b, r: (b * steps + r, 0)),
                  kv, kv, cache, cache,
                  pl.BlockSpec((None, NA_HEADS, 2 * NA_ROWS - 1, 128), lambda b, r: (l, 0, 0, 0))],
        out_specs=pl.BlockSpec((rows, 512), lambda b, r: (b * steps + r, 0)),
        out_shape=jax.ShapeDtypeStruct((T_LAT, 512), BF16),
        scratch_shapes=[pltpu.VMEM((512, PAST_LEN), BF16), pltpu.VMEM((512, PAST_LEN), BF16),
                        pltpu.VMEM((NA_HEADS, 2 * NA_ROWS - 2, GRID_W, 128), F32)],
        compiler_params=_cparams(("arbitrary", "arbitrary")),
        name="na_attention",
    )(qc, kc, vc, cache_k, cache_v, rp_cyc)


def _na_table_rows(rpb):
    pad = jnp.zeros(rpb.shape[:-1] + (128 - (2 * NA_COLS - 1),), F32)
    return jnp.concatenate([rpb[..., NA_COLS - 1:], pad, rpb[..., :NA_COLS - 1]], axis=-1)


def _gla_constants():
    c = GLA_C
    t = np.arange(c)[:, None]
    u = np.arange(c)[None, :]
    blocks = []
    for k in range(GLA_LEVELS):
        b = 1 << k
        m = ((t >> k) | 1) * b - 1
        query = ((t >> k) & 1) == 1
        blocks.append(np.where(query, (u > m) & (u <= t), (u > t) & (u <= m)))
    blocks.append(u <= t)
    blocks.append(u > t)
    blocks.append(np.ones((16, c), bool))
    fwd = np.concatenate(blocks, axis=0).astype(np.float32)
    bwd_blocks = [blk[::-1, ::-1] for blk in blocks]
    bwd = np.concatenate(bwd_blocks, axis=0).astype(np.float32)
    seg = np.stack([fwd, bwd])
    seg = np.concatenate([seg, seg], axis=-1)

    s = np.arange(c)[None, :]
    x = t ^ s
    lev = np.where(x == 0, GLA_LEVELS, np.floor(np.log2(np.maximum(x, 1))).astype(np.int64))
    lev_f = np.where(s <= t, lev, -1)
    lev_b = np.where(s >= t, lev, -1)
    levmap = np.stack([np.tile(lev_f, (1, GLA_HEADS)), np.tile(lev_b, (1, GLA_HEADS))])
    return seg, levmap.astype(np.int32)


def _gla_chunks(chains):
    c = GLA_C
    lane_head = lax.broadcasted_iota(jnp.int32, (1, GLA_HEADS * GLA_DK), 1) // GLA_DK
    row_head = lax.broadcasted_iota(jnp.int32, (GLA_HEADS * GLA_DV, 1), 0) // GLA_DV
    head_sel = [jnp.where(lane_head == h, 1.0, 0.0).astype(BF16) for h in range(GLA_HEADS)]

    def operands(q, k, e=None):
        if e is not None:
            q, k = q * e, k * e
        return q, jnp.concatenate([k * head_sel[h] for h in range(GLA_HEADS)], axis=0)

    segs = []
    for (_, _, _, lr_ref, w2_ref, ba_ref, seg_ref, *_) in chains:
        x = _dot(lr_ref[...], w2_ref[...]) + ba_ref[...]
        la = (jnp.minimum(x, 0.0) - jnp.log(1.0 + jnp.exp(-jnp.abs(x)))) * (1.0 / GLA_TAU)
        la_hi = la.astype(BF16)
        la_lo = (la - la_hi.astype(F32)).astype(BF16)
        segs.append(_dot(seg_ref[...], jnp.concatenate([la_hi, la_lo], axis=0)))

    def decay(seg, block):
        return jnp.exp(seg[c * block:c * (block + 1)]).astype(BF16)

    qs = [ch[0][...] * (GLA_DK ** -0.5) for ch in chains]
    ks = [ch[1][...] for ch in chains]
    order = [GLA_LEVELS] + list(range(GLA_LEVELS))
    prepare = lambda lvl: [operands(qs[n], ks[n], None if lvl == GLA_LEVELS else decay(segs[n], lvl))
                           for n in range(len(chains))]
    attn = [None] * len(chains)
    ops = prepare(order[0])
    for i, lvl in enumerate(order):
        scores = [_nt_dot(q, kbd) for q, kbd in ops]
        if i + 1 < len(order):
            ops = prepare(order[i + 1])
        for n, ch in enumerate(chains):
            attn[n] = jnp.where(ch[7][...] == lvl, scores[n], 0.0 if attn[n] is None else attn[n])

    new_states = []
    for n, (_, _, v_ref, _, _, _, _, _, o_ref, st_ref) in enumerate(chains):
        seg = segs[n]
        v = v_ref[...]
        a = attn[n].astype(BF16)
        tot = seg[c * (GLA_LEVELS + 2):c * (GLA_LEVELS + 2) + 1]
        q_in = qs[n] * decay(seg, GLA_LEVELS)
        k_in = ks[n] * decay(seg, GLA_LEVELS + 1)
        state = st_ref[...]
        o_inter = _nt_dot(q_in, state.astype(BF16))
        for h in range(GLA_HEADS):
            sl = slice(GLA_DV * h, GLA_DV * (h + 1))
            o_ref[:, sl] = o_inter[:, sl] + _dot(a[:, c * h:c * (h + 1)], v[:, sl])
        upd = _tn_dot(v, k_in)
        new_state = state * jnp.exp(tot) + jnp.where(row_head == lane_head, upd, 0.0)
        st_ref[...] = new_state
        new_states.append(new_state)
    return new_states


GLA_GROUPS = 3
GLA_GROUP_ROWS = T_ALL // GLA_GROUPS
GLA_CTX_GROUPS = T_CTX // GLA_GROUP_ROWS
GLA_STEPS = GLA_GROUP_ROWS // GLA_C
GLA_CTX_CHUNKS = CTX_LEN // GLA_C
GLA_LAT_CHUNKS = LAT_LEN // GLA_C


def _gla_kernel(qf, kf, vf, lrf, qb, kb, vb, lrb, w2_ref, ba_ref, seg_ref, lev_ref, s0f, s0b,
                of_ref, ob_ref, sff_ref, sfb_ref, states):
    step = pl.program_id(0)

    @pl.when(step % GLA_CTX_CHUNKS == 0)
    def _():
        for d in range(2):
            for g in range(GLA_CTX_GROUPS):
                states[d, g] = jnp.zeros(states.shape[2:], F32)

    @pl.when(step % GLA_LAT_CHUNKS == 0)
    def _():
        for d, s0 in enumerate((s0f, s0b)):
            for h in range(GLA_HEADS):
                row = [jnp.zeros((GLA_DV, GLA_DK), F32)] * GLA_HEADS
                row[h] = s0[h]
                states[d, GLA_GROUPS - 1, GLA_DV * h:GLA_DV * (h + 1), :] = jnp.concatenate(row, axis=1)

    chains = []
    for g in range(GLA_GROUPS):
        chains.append((qf.at[g], kf.at[g], vf.at[g], lrf.at[g], w2_ref.at[0], ba_ref.at[0],
                       seg_ref.at[0], lev_ref.at[0], of_ref.at[g], states.at[0, g]))
        chains.append((qb.at[g], kb.at[g], vb.at[g], lrb.at[g], w2_ref.at[1], ba_ref.at[1],
                       seg_ref.at[1], lev_ref.at[1], ob_ref.at[g], states.at[1, g]))
    new_states = _gla_chunks(chains)

    @pl.when(step % GLA_CTX_CHUNKS == GLA_CTX_CHUNKS - 1)
    def _():
        for g in range(GLA_CTX_GROUPS):
            for d, out in enumerate((sff_ref, sfb_ref)):
                st = new_states[2 * g + d]
                for h in range(GLA_HEADS):
                    out[g, h] = st[GLA_DV * h:GLA_DV * (h + 1), GLA_DK * h:GLA_DK * (h + 1)].T


def _gla(z, w2p, ba, seg, levmap, s0t, l):
    c = GLA_C
    z3 = z.reshape(GLA_GROUPS, GLA_GROUP_ROWS, Z_W)
    fwd = lambda s: s
    bwd = lambda s: GLA_STEPS - 1 - s
    seqs_per_group = GLA_GROUP_ROWS // CTX_LEN

    def chunk_specs(pos):
        return [pl.BlockSpec((GLA_GROUPS, c, 256), lambda s: (0, pos(s), Z_AQ // 256)),
                pl.BlockSpec((GLA_GROUPS, c, 256), lambda s: (0, pos(s), Z_AK // 256)),
                pl.BlockSpec((GLA_GROUPS, c, 512), lambda s: (0, pos(s), Z_AV // 512)),
                pl.BlockSpec((GLA_GROUPS, c, 128), lambda s: (0, pos(s), Z_LR // 128))]

    def s0_spec(d, pos):
        return pl.BlockSpec((None, None, None, GLA_HEADS, GLA_DV, GLA_DK),
                            lambda s: (pos(s) // GLA_LAT_CHUNKS, l, d, 0, 0, 0))

    def out_spec(pos):
        return pl.BlockSpec((GLA_GROUPS, c, 512), lambda s: (0, pos(s), 0))

    def sfin_spec(pos):
        return pl.BlockSpec((GLA_CTX_GROUPS, None, GLA_HEADS, GLA_DK, GLA_DV),
                            lambda s: (0, pos(s) // GLA_CTX_CHUNKS, 0, 0, 0))

    state_sds = jax.ShapeDtypeStruct((GLA_CTX_GROUPS, seqs_per_group, GLA_HEADS, GLA_DK, GLA_DV), F32)
    out_sds = jax.ShapeDtypeStruct((GLA_GROUPS, GLA_GROUP_ROWS, 512), F32)
    o_f, o_b, sf, sb = pl.pallas_call(
        _gla_kernel,
        grid=(GLA_STEPS,),
        in_specs=chunk_specs(fwd) + chunk_specs(bwd) + [
            pl.BlockSpec((None, 2, 128, 256), lambda s: (l, 0, 0, 0)),
            pl.BlockSpec((None, 2, 1, 256), lambda s: (l, 0, 0, 0)),
            pl.BlockSpec((2, GLA_GROWS, 2 * c), lambda s: (0, 0, 0)),
            pl.BlockSpec((2, c, GLA_HEADS * c), lambda s: (0, 0, 0)),
            s0_spec(0, fwd), s0_spec(1, bwd),
        ],
        out_specs=[out_spec(fwd), out_spec(bwd), sfin_spec(fwd), sfin_spec(bwd)],
        out_shape=[out_sds, out_sds, state_sds, state_sds],
        scratch_shapes=[pltpu.VMEM((2, GLA_GROUPS, GLA_HEADS * GLA_DV, GLA_HEADS * GLA_DK), F32)],
        compiler_params=_cparams(("arbitrary",)),
        name="gla",
    )(z3, z3, z3, z3, z3, z3, z3, z3, w2p, ba, seg, levmap, s0t, s0t)
    flat = lambda a: a.reshape((-1,) + a.shape[2:])
    return flat(o_f), flat(o_b), flat(sf), flat(sb)


def _cast_kernel(*refs):
    n = len(refs) // 2
    for src, dst in zip(refs[:n], refs[n:]):
        dst[...] = src[...].astype(dst.dtype)


def _cast_bf16(*weights):
    spec = lambda w: pl.BlockSpec((None,) + w.shape[1:], lambda l: (l, 0, 0))
    return pl.pallas_call(
        _cast_kernel,
        grid=(DEPTH,),
        in_specs=[spec(w) for w in weights],
        out_specs=[spec(w) for w in weights],
        out_shape=[jax.ShapeDtypeStruct(w.shape, BF16) for w in weights],
        compiler_params=_cparams(("arbitrary",)),
        name="cast_bf16",
    )(*weights)


def _post_mixer_kernel(*refs, nk, n_x, n_y):
    x_refs, refs = refs[:n_x], refs[n_x:]
    (of_ref, obk_ref, ar_ref, gates_ref, bc_ref, bl_ref, cc_ref, cl_ref, wpa_ref, wpb_ref, wpc_ref,
     wo_ref, gn_ref, g2_ref, mod_ref, w1_ref, w2_ref) = refs[:17]
    y_refs = refs[17:17 + n_y]
    w1_scr, w2_scr, h_scr, x1_scr, acc_scr = refs[17 + n_y:]
    s = pl.program_id(0)
    d = D_MODEL
    tm = of_ref.shape[0]
    th = w1_ref.shape[1]
    ctx = jnp.maximum(s - (nk - 1), 0) < T_CTX // tm

    def put_y(val):
        if n_y == 1:
            y_refs[0][...] = val
        else:
            @pl.when(ctx)
            def _():
                y_refs[0][...] = val

            @pl.when(jnp.logical_not(ctx))
            def _():
                y_refs[1][...] = val

    def merged_residual():
        o = of_ref[...] + obk_ref[...]
        heads = []
        for h in range(GLA_HEADS):
            oh = o[:, GLA_DV * h:GLA_DV * (h + 1)]
            ms = jnp.mean(oh * oh, axis=-1, keepdims=True)
            heads.append(oh * lax.rsqrt(ms + EPS) * gn_ref[...])
        oa = (jnp.concatenate(heads, axis=1) * _silu(ar_ref[...].astype(F32))).astype(BF16)
        ob = jnp.where(ctx, bc_ref[...], bl_ref[...])
        oc = jnp.where(ctx, cc_ref[...], cl_ref[...])
        gate = lambda j: _sigmoid(gates_ref[:, j * d:(j + 1) * d].astype(F32))
        merged = (gate(0) * _dot(oa, wpa_ref[...]) + gate(1) * _dot(ob, wpb_ref[...])
                  + gate(2) * _dot(oc, wpc_ref[...]))
        return (_token_rows(x_refs, ctx)
                + mod_ref[:, 2 * d:3 * d] * _dot(merged.astype(BF16), wo_ref[...]))

    def normed(x1):
        return _norm_mod(x1, g2_ref[...], mod_ref[:, 3 * d:4 * d], mod_ref[:, 4 * d:5 * d]).astype(BF16)

    def act(u):
        return jnp.square(jnp.maximum(u, 0.0)).astype(BF16)

    @pl.when(s == 0)
    def _():
        x1 = merged_residual()
        x1_scr[...] = x1
        h_scr[...] = normed(x1)

    @pl.when(s < nk)
    def _():
        col = pl.multiple_of(s * th, th)
        w1 = w1_ref[...].astype(BF16)
        w2 = w2_ref[...].astype(BF16)
        w1_scr[:, pl.ds(col, th)] = w1
        w2_scr[pl.ds(col, th), :] = w2
        part = _dot(act(_dot(h_scr[...], w1)), w2)

        @pl.when(s == 0)
        def _():
            acc_scr[...] = part

        @pl.when(s > 0)
        def _():
            acc_scr[...] += part

        @pl.when(s == nk - 1)
        def _():
            put_y(x1_scr[...] + mod_ref[:, 5 * d:6 * d] * acc_scr[...])

    @pl.when(s >= nk)
    def _():
        x1 = merged_residual()
        out = _dot(act(_dot(normed(x1), w1_scr[...])), w2_scr[...])
        put_y(x1 + mod_ref[:, 5 * d:6 * d] * out)


def _post_mixer(x_parts, o_fwd, o_bwd, z, ob_ctx, ob_lat, oc_ctx, oc_lat, wpa, wpb, wpc, wo, gn, norm2_w,
                mod4, w1, w2, l, split_out):
    tm, th = 256, 512
    nk = MLP_HIDDEN // th
    ctx_blocks = T_CTX // tm
    row = lambda s: jnp.maximum(s - (nk - 1), 0)
    chunk = lambda s: jnp.minimum(s, nk - 1)
    ctx_idx = lambda s: (jnp.minimum(row(s), ctx_blocks - 1), 0)
    lat_idx = lambda s: (jnp.maximum(row(s) - ctx_blocks, 0), 0)
    rows = lambda w, j=0: pl.BlockSpec((tm, w), lambda s: (row(s), j))
    resident = lambda k: pl.BlockSpec((None, k, D_MODEL), lambda s: (l, 0, 0),
                                      pipeline_mode=pl.Buffered(1))
    if split_out:
        out_specs = [pl.BlockSpec((tm, D_MODEL), ctx_idx), pl.BlockSpec((tm, D_MODEL), lat_idx)]
        out_shape = [jax.ShapeDtypeStruct((T_CTX, D_MODEL), F32), jax.ShapeDtypeStruct((T_LAT, D_MODEL), F32)]
    else:
        out_specs = [rows(D_MODEL)]
        out_shape = [jax.ShapeDtypeStruct((T_ALL, D_MODEL), F32)]
    return pl.pallas_call(
        functools.partial(_post_mixer_kernel, nk=nk, n_x=len(x_parts), n_y=len(out_specs)),
        grid=(nk - 1 + T_ALL // tm,),
        in_specs=_token_specs(x_parts, tm, row) + [
            rows(512), rows(512), rows(512, Z_AR // 512), rows(3 * D_MODEL),
            pl.BlockSpec((tm, 512), ctx_idx), pl.BlockSpec((tm, 512), lat_idx),
            pl.BlockSpec((tm, 512), ctx_idx), pl.BlockSpec((tm, 512), lat_idx),
            resident(512), resident(512), resident(512), resident(D_MODEL),
            pl.BlockSpec((None, 1, GLA_DV), lambda s: (l, 0, 0)),
            pl.BlockSpec((None, 1, D_MODEL), lambda s: (l, 0, 0)),
            pl.BlockSpec((None, None, 1, 6 * D_MODEL),
                         lambda s: (l, _group_of_rows(row(s), tm), 0, 0)),
            pl.BlockSpec((None, D_MODEL, th), lambda s: (l, 0, chunk(s))),
            pl.BlockSpec((None, th, D_MODEL), lambda s: (l, chunk(s), 0)),
        ],
        out_specs=out_specs,
        out_shape=out_shape,
        scratch_shapes=[pltpu.VMEM((D_MODEL, MLP_HIDDEN), BF16), pltpu.VMEM((MLP_HIDDEN, D_MODEL), BF16),
                        pltpu.VMEM((tm, D_MODEL), BF16), pltpu.VMEM((tm, D_MODEL), F32),
                        pltpu.VMEM((tm, D_MODEL), F32)],
        compiler_params=_cparams(("arbitrary",)),
        name="post_mixer",
    )(*x_parts, o_fwd, o_bwd, z, z, ob_ctx, ob_lat, oc_ctx, oc_lat, wpa, wpb, wpc, wo, gn, norm2_w, mod4,
      w1, w2)


def _rope_tables():
    t = jnp.arange(LAT_LEN)
    row = (t // GRID_W).astype(F32)
    col = (t % GRID_W).astype(F32)
    nf = HEAD_DIM // 4
    inv_freq = ROPE_BASE ** (-jnp.arange(nf, dtype=F32) / nf)
    ang_r = row[:, None] * inv_freq[None, :]
    ang_c = col[:, None] * inv_freq[None, :]
    cos = jnp.concatenate([jnp.cos(ang_r)] * 2 + [jnp.cos(ang_c)] * 2, axis=1)
    sin = jnp.concatenate([-jnp.sin(ang_r), jnp.sin(ang_r), -jnp.sin(ang_c), jnp.sin(ang_c)], axis=1)
    return jnp.tile(cos, (1, 2)), jnp.tile(sin, (1, 2))


def kernel(x_prompt, x_sample, state_gla, cache_swa_k, cache_swa_v, cache_na_k, cache_na_v, c,
           c_ctx, w_mod, b_mod, norm1, norm2, w_in, w_a2_f, b_a_f, w_a2_b, b_a_b, gla_onorm,
           qn_swa, kn_swa, sink_swa, qn_na, kn_na, rpb_na, w_pa, w_pb, w_pc, w_o, w_fc1, w_fc2):
    d = D_MODEL
    x_parts = (x_prompt.reshape(T_CTX, d), x_sample.reshape(T_LAT, d))

    cond8 = jnp.zeros((8, d), F32).at[0].set(c_ctx).at[1:1 + N_LAT_SEQ].set(c)
    mod4 = _modulation(cond8, w_mod, b_mod).reshape(DEPTH, 8, 1, 6 * d)

    w_in_p = _w_in_layout(jnp.swapaxes(w_in, 1, 2))
    wpa, wpb, wpc, wo = _cast_bf16(w_pa, w_pb, w_pc, w_o)
    norm1r = norm1.reshape(DEPTH, 1, d)
    norm2r = norm2.reshape(DEPTH, 1, d)
    gnr = gla_onorm.reshape(DEPTH, 1, GLA_DV)
    w2p = jnp.zeros((DEPTH, 2, 128, GLA_HEADS * GLA_DK), F32)
    w2p = w2p.at[:, 0, 0:GLA_LOWRANK].set(w_a2_f).at[:, 1, GLA_LOWRANK:2 * GLA_LOWRANK].set(w_a2_b)
    w2p = w2p.astype(BF16)
    ba = jnp.stack([b_a_f, b_a_b], axis=1).reshape(DEPTH, 2, 1, GLA_HEADS * GLA_DK)
    seg_np, lev_np = _gla_constants()
    seg = jnp.asarray(seg_np, BF16)
    levmap = jnp.asarray(lev_np)
    s0t = jnp.swapaxes(state_gla, -1, -2)
    cos_t, sin_t = _rope_tables()
    rp_cyc = _na_table_rows(rpb_na)
    feat_major = lambda a: a.transpose(0, 1, 3, 4, 2).reshape(N_LAT_SEQ, DEPTH, -1, PAST_LEN)
    csk, csv, cnk, cnv = (feat_major(a) for a in (cache_swa_k, cache_swa_v, cache_na_k, cache_na_v))

    st_l = []
    caches = None
    for l in range(DEPTH):
        z = _inproj(x_parts, norm1r, mod4, w_in_p, l)
        wqb = jnp.tile(qn_swa[l], 8)[None, :]
        wkb = jnp.tile(kn_swa[l], 2)[None, :]
        wqc = jnp.tile(qn_na[l], 8)[None, :]
        wkc = jnp.tile(kn_na[l], 8)[None, :]
        qb, kb, vb, qc, kc, vc = _attn_prep(z, wqb, wkb, wqc, wkc, cos_t, sin_t)
        o_fwd, o_bwd, sfin_f, sfin_b = _gla(z, w2p, ba, seg, levmap, s0t, l)
        ob_ctx, oc_ctx, *caches = _ctx_attention(sink_swa, z, wqb, wkb, wqc, wkc, caches, l)
        ob_lat = _swa_attention(sink_swa, qb, kb, vb, csk, csv, l)
        oc_lat = _na_attention(qc, kc, vc, cnk, cnv, rp_cyc, l)
        x_parts = _post_mixer(x_parts, o_fwd, o_bwd, z, ob_ctx, ob_lat, oc_ctx, oc_lat, wpa, wpb, wpc, wo,
                              gnr, norm2r, mod4, w_fc1, w_fc2, l, split_out=(l == DEPTH - 1))

        st_l += [sfin_f, sfin_b]

    y_prompt = x_parts[0].reshape(N_CTX_SEQ, CTX_LEN, d)
    y_sample = x_parts[1].reshape(N_LAT_SEQ, LAT_LEN, d)
    new_state = jnp.stack(st_l, axis=1).reshape(
        N_CTX_SEQ, DEPTH, 2, GLA_HEADS, GLA_DK, GLA_DV)

    swa_k, swa_v, na_k, na_v = caches
    kv_shape = lambda heads: (N_CTX_SEQ, DEPTH, CTX_LEN, heads, HEAD_DIM)
    return (y_prompt, y_sample, new_state,
            swa_k.reshape(kv_shape(SWA_KV_HEADS)), swa_v.reshape(kv_shape(SWA_KV_HEADS)),
            na_k.reshape(kv_shape(NA_HEADS)), na_v.reshape(kv_shape(NA_HEADS)))
```

```python
import functools

import numpy as np
import jax
import jax.numpy as jnp
from jax import lax
from jax.experimental import pallas as pl
from jax.experimental.pallas import tpu as pltpu

F32 = jnp.float32
BF16 = jnp.bfloat16

D_MODEL = 1024
DEPTH = 4
N_CTX_SEQ = 16
CTX_LEN = 256
N_LAT_SEQ = 2
LAT_LEN = 1024
PAST_LEN = 512
T_CTX = N_CTX_SEQ * CTX_LEN
T_LAT = N_LAT_SEQ * LAT_LEN
T_ALL = T_CTX + T_LAT
GRID_W = 64
HEAD_DIM = 64
GLA_HEADS = 4
GLA_DK = 64
GLA_DV = 128
GLA_LOWRANK = 16
GLA_TAU = 16.0
SWA_Q_HEADS = 8
SWA_KV_HEADS = 2
SWA_GROUP = 4
SWA_WINDOW = 128
NA_HEADS = 8
NA_ROWS = 8
NA_COLS = 16
MLP_HIDDEN = 4 * D_MODEL
ROPE_BASE = 10000.0
EPS = 1e-6
LOG2E = 1.4426950408889634
QK_SCALE = HEAD_DIM ** -0.5 * LOG2E

Z_GATES = 0
Z_GLA = 3072
Z_AQ, Z_AK, Z_AV, Z_AR = 3072, 3328, 3584, 4096
Z_ATT = 4608
ATT_W = 2304
A_BQ, A_BK, A_BV, A_CQ, A_CK, A_CV = 0, 512, 640, 768, 1280, 1792
Z_LR = 6912
Z_W = 7168

GLA_C = 128
GLA_LEVELS = 7
GLA_GROWS = (GLA_LEVELS + 2) * GLA_C + 16

VMEM_LIMIT = 56 * 1024 * 1024


def _cparams(sem):
    return pltpu.CompilerParams(dimension_semantics=sem, vmem_limit_bytes=VMEM_LIMIT)


def _sigmoid(x):
    return 1.0 / (1.0 + jnp.exp(-x))


def _silu(x):
    return x * _sigmoid(x)


def _nt_dot(a, b):
    return lax.dot_general(a, b, (((1,), (1,)), ((), ())), preferred_element_type=F32)


def _tn_dot(a, b):
    return lax.dot_general(a, b, (((0,), (0,)), ((), ())), preferred_element_type=F32)


def _dot(a, b):
    return jnp.dot(a, b, preferred_element_type=F32)


def _mod_kernel(cond_ref, w_ref, b_ref, o_ref):
    s = _silu(cond_ref[...]).astype(BF16)
    o_ref[...] = _dot(s, w_ref[...].astype(BF16)) + b_ref[...]


def _modulation(cond8, w_mod, b_mod):
    tn = 1024
    return pl.pallas_call(
        _mod_kernel,
        grid=(DEPTH, 6 * D_MODEL // tn),
        in_specs=[
            pl.BlockSpec((8, D_MODEL), lambda l, j: (0, 0)),
            pl.BlockSpec((None, D_MODEL, tn), lambda l, j: (l, 0, j)),
            pl.BlockSpec((None, 1, tn), lambda l, j: (l, 0, j)),
        ],
        out_specs=pl.BlockSpec((None, 8, tn), lambda l, j: (l, 0, j)),
        out_shape=jax.ShapeDtypeStruct((DEPTH, 8, 6 * D_MODEL), F32),
        compiler_params=_cparams(("arbitrary", "arbitrary")),
        name="modulation",
    )(cond8, w_mod, b_mod.reshape(DEPTH, 1, 6 * D_MODEL))


def _group_of_rows(row_block, rows_per_block):
    first = row_block * rows_per_block
    return jnp.maximum(first - T_CTX + LAT_LEN, 0) // LAT_LEN


def _norm_mod(x, g, shift, scale):
    ms = jnp.mean(x * x, axis=-1, keepdims=True)
    y = x * lax.rsqrt(ms + EPS) * g
    return y * (1.0 + scale) + shift


W_IN_SEGMENTS = ((3872, 6944), (0, 1536), (1568, 3872), (1536, 1568))
W_IN_COLS = 6944


def _w_in_layout_kernel(w_ref, o_ref):
    dst = 0
    for lo, hi in W_IN_SEGMENTS:
        o_ref[dst:dst + hi - lo, :] = w_ref[lo:hi, :].astype(BF16)
        dst += hi - lo
    o_ref[dst:, :] = jnp.zeros((Z_W - dst, o_ref.shape[1]), BF16)


def _w_in_layout(w_in_t):
    tk = 256
    return pl.pallas_call(
        _w_in_layout_kernel,
        grid=(DEPTH, D_MODEL // tk),
        in_specs=[pl.BlockSpec((None, W_IN_COLS, tk), lambda l, r: (l, 0, r))],
        out_specs=pl.BlockSpec((None, Z_W, tk), lambda l, r: (l, 0, r)),
        out_shape=jax.ShapeDtypeStruct((DEPTH, Z_W, D_MODEL), BF16),
        compiler_params=_cparams(("arbitrary", "arbitrary")),
        name="w_in_layout",
    )(w_in_t)


def _token_rows(x_refs, ctx):
    if len(x_refs) == 1:
        return x_refs[0][...]
    return jnp.where(ctx, x_refs[0][...], x_refs[1][...])


def _token_specs(x_parts, tm, row):
    if len(x_parts) == 1:
        return [pl.BlockSpec((tm, D_MODEL), lambda s: (row(s), 0))]
    ctx_blocks = T_CTX // tm
    return [pl.BlockSpec((tm, D_MODEL), lambda s: (jnp.minimum(row(s), ctx_blocks - 1), 0)),
            pl.BlockSpec((tm, D_MODEL), lambda s: (jnp.maximum(row(s) - ctx_blocks, 0), 0))]


def _inproj_kernel(*refs):
    *x_refs, g_ref, mod_ref, w_ref, z_ref = refs
    ctx = pl.program_id(0) < T_CTX // z_ref.shape[0]
    h = _norm_mod(_token_rows(x_refs, ctx), g_ref[...],
                  mod_ref[:, 0:D_MODEL], mod_ref[:, D_MODEL:2 * D_MODEL])
    z_ref[...] = _nt_dot(h.astype(BF16), w_ref[...]).astype(z_ref.dtype)


def _inproj(x_parts, norm_w, mod4, w_in_p, l):
    tm = 512
    return pl.pallas_call(
        _inproj_kernel,
        grid=(T_ALL // tm,),
        in_specs=_token_specs(x_parts, tm, lambda i: i) + [
            pl.BlockSpec((None, 1, D_MODEL), lambda i: (l, 0, 0)),
            pl.BlockSpec((None, None, 1, 6 * D_MODEL), lambda i: (l, _group_of_rows(i, tm), 0, 0)),
            pl.BlockSpec((None, Z_W, D_MODEL), lambda i: (l, 0, 0),
                         pipeline_mode=pl.Buffered(1)),
        ],
        out_specs=pl.BlockSpec((tm, Z_W), lambda i: (i, 0)),
        out_shape=jax.ShapeDtypeStruct((T_ALL, Z_W), BF16),
        compiler_params=_cparams(("arbitrary",)),
        name="inproj",
    )(*x_parts, norm_w, mod4, w_in_p)


def _head_norm_pairs(z_ref, groups):
    same_head = (lax.broadcasted_iota(jnp.int32, (128, 128), 0) // HEAD_DIM
                 == lax.broadcasted_iota(jnp.int32, (128, 128), 1) // HEAD_DIM)
    ones = jnp.where(same_head, 1.0, 0.0).astype(BF16)
    tiles = [[(z_ref[:, col + 128 * p: col + 128 * (p + 1)].astype(F32), w_ref[:, 128 * p:128 * (p + 1)])
              for p in range(width // 128)] for col, width, w_ref in groups]
    sums = [[_dot((x * x).astype(BF16), ones) for x, _ in group] for group in tiles]
    return [[x * lax.rsqrt(ss * (1.0 / HEAD_DIM) + EPS) * w for (x, w), ss in zip(group, gs)]
            for group, gs in zip(tiles, sums)]


def _rope(x, cos, sin_signed):
    n = x.shape[-1]
    lane = lax.broadcasted_iota(jnp.int32, (1, n), 1)
    first = (lane % 32) < 16
    partner = jnp.where(first, pltpu.roll(x, n - 16, 1), pltpu.roll(x, 16, 1))
    return x * cos + partner * sin_signed


def _prep_kernel(z_ref, wqb_ref, wkb_ref, wqc_ref, wkc_ref, cos_ref, sin_ref,
                 qb_ref, kb_ref, vb_ref, qc_ref, kc_ref, vc_ref):
    qscale = QK_SCALE

    qb, kb, qc, kc = _head_norm_pairs(z_ref, [(A_BQ, 512, wqb_ref), (A_BK, 128, wkb_ref),
                                              (A_CQ, 512, wqc_ref), (A_CK, 512, wkc_ref)])

    vb_ref[...] = z_ref[:, A_BV:A_BV + 128].astype(F32).T.astype(BF16)
    vc_ref[...] = z_ref[:, A_CV:A_CV + 512]
    for p in range(4):
        qc_ref[:, 128 * p:128 * (p + 1)] = (qc[p] * qscale).astype(BF16)
        kc_ref[:, 128 * p:128 * (p + 1)] = kc[p].astype(BF16)

    cos = cos_ref[...]
    sin = sin_ref[...]
    for p in range(4):
        qb_ref[128 * p:128 * (p + 1), :] = (_rope(qb[p], cos, sin) * qscale).T.astype(BF16)
    kb_ref[...] = _rope(kb[0], cos, sin).astype(BF16)


def _attn_prep(z, wqb, wkb, wqc, wkc, cos_t, sin_t):
    tm = 1024
    lat_blocks = LAT_LEN // tm
    rope_idx = lambda i: (i % lat_blocks, 0)
    row = lambda w: pl.BlockSpec((tm, w), lambda i: (i, 0))
    col = lambda w: pl.BlockSpec((w, tm), lambda i: (0, i))
    cst = lambda w: pl.BlockSpec((1, w), lambda i: (0, 0))
    sds = lambda w: jax.ShapeDtypeStruct((T_LAT, w), BF16)
    sds_t = lambda w: jax.ShapeDtypeStruct((w, T_LAT), BF16)
    return pl.pallas_call(
        _prep_kernel,
        grid=(T_LAT // tm,),
        in_specs=[
            pl.BlockSpec((tm, ATT_W), lambda i: (T_CTX // tm + i, Z_ATT // ATT_W)),
            cst(512), cst(128), cst(512), cst(512),
            pl.BlockSpec((tm, 128), rope_idx),
            pl.BlockSpec((tm, 128), rope_idx),
        ],
        out_specs=[col(512), row(128), col(128), row(512), row(512), row(512)],
        out_shape=[sds_t(512), sds(128), sds_t(128), sds(512), sds(512), sds(512)],
        compiler_params=_cparams(("arbitrary",)),
        name="attn_prep",
    )(z, wqb, wkb, wqc, wkc, cos_t, sin_t)


def _half_rows(t, half):
    z = jnp.zeros((HEAD_DIM, t.shape[1]), t.dtype)
    return jnp.concatenate([t[:HEAD_DIM], z] if half == 0 else [z, t[HEAD_DIM:]], axis=0)


CTX_SEQS_PER_STEP = 2


def _ctx_attn_kernel(sink_ref, z_ref, wqb_ref, wkb_ref, wqc_ref, wkc_ref, *refs, layer):
    ob_ref, oc_ref = refs[-6:-4]
    cache_refs = refs[-4:]
    first_layer = len(refs) == 6
    if first_layer:
        for ref in cache_refs:
            ref[...] = jnp.zeros(ref.shape, F32)
    kbf_ref, vbf_ref, kcf_ref, vcf_ref = cache_refs
    n = CTX_LEN
    nseq = CTX_SEQS_PER_STEP
    qscale = QK_SCALE

    def put(ref, seq, val, cols=slice(None)):
        if first_layer:
            ref[seq, layer, :, cols] = val
        else:
            ref[seq, :, cols] = val

    def seq_rows(x, seq):
        return x[n * seq:n * (seq + 1)]

    def seq_cols(x, seq):
        return x[:, n * seq:n * (seq + 1)]

    def pair(col, p):
        return z_ref[:, col + 128 * p: col + 128 * (p + 1)]

    qb_n, kb_n, qc_n, kc_n = _head_norm_pairs(z_ref, [(A_BQ, 512, wqb_ref), (A_BK, 128, wkb_ref),
                                                      (A_CQ, 512, wqc_ref), (A_CK, 512, wkc_ref)])

    def q_t(tile):
        return (tile * qscale).T.astype(BF16)

    def v_t(col, p):
        return pair(col, p).astype(F32).T.astype(BF16)

    for seq in range(nseq):
        put(vbf_ref, seq, seq_rows(z_ref[:, A_BV:A_BV + 128], seq).astype(F32))
        put(vcf_ref, seq, seq_rows(z_ref[:, A_CV:A_CV + 512], seq).astype(F32))
        put(kbf_ref, seq, seq_rows(kb_n[0], seq))
        for p in range(NA_HEADS // 2):
            put(kcf_ref, seq, seq_rows(kc_n[p], seq), slice(128 * p, 128 * (p + 1)))

    kb = kb_n[0].astype(BF16)
    kc = [t.astype(BF16) for t in kc_n]
    qb_t = [q_t(tile) for tile in qb_n]
    qc_t = [q_t(tile) for tile in qc_n]
    scores = []
    for seq in range(nseq):
        for kvh in range(SWA_KV_HEADS):
            placed = []
            for h in range(SWA_GROUP * kvh, SWA_GROUP * (kvh + 1)):
                rows = seq_cols(qb_t[h // 2][HEAD_DIM * (h % 2):HEAD_DIM * (h % 2 + 1)], seq)
                z = jnp.zeros_like(rows)
                placed.append(jnp.concatenate([rows, z] if kvh == 0 else [z, rows], axis=0))
            scores.append(_dot(seq_rows(kb, seq), jnp.concatenate(placed, axis=1)))
        for p in range(NA_HEADS // 2):
            q = seq_cols(qc_t[p], seq)
            q2 = jnp.concatenate([_half_rows(q, 0), _half_rows(q, 1)], axis=1)
            scores.append(_dot(seq_rows(kc[p], seq), q2))
    s_t = jnp.concatenate(scores, axis=1)

    sink = jnp.concatenate(
        ([jnp.full((1, n), sink_ref[layer, h] * LOG2E, F32) for h in range(SWA_Q_HEADS)]
         + [jnp.full((1, n * NA_HEADS), -jnp.inf, F32)]) * nseq, axis=1)
    mx = jnp.maximum(s_t.max(axis=0, keepdims=True), sink)
    p_t = jnp.exp2(s_t - mx)
    inv = 1.0 / (p_t.sum(axis=0, keepdims=True) + jnp.exp2(sink - mx))
    p_t = p_t.astype(BF16)

    nb = SWA_Q_HEADS * n
    per_seq = (SWA_Q_HEADS + NA_HEADS) * n
    vb_t = v_t(A_BV, 0)
    vc_t = [v_t(A_CV, p) for p in range(NA_HEADS // 2)]
    for seq in range(nseq):
        c0 = per_seq * seq
        o = _dot(seq_cols(vb_t, seq), p_t[:, c0:c0 + nb]) * inv[:, c0:c0 + nb]
        outs = [o[HEAD_DIM * (h // SWA_GROUP):HEAD_DIM * (h // SWA_GROUP + 1), n * h:n * (h + 1)]
                for h in range(SWA_Q_HEADS)]
        ob_ref[n * seq:n * (seq + 1), :] = jnp.concatenate(outs, axis=0).T.astype(BF16)
        outs = []
        for p in range(NA_HEADS // 2):
            cols = slice(c0 + nb + 2 * n * p, c0 + nb + 2 * n * (p + 1))
            o = _dot(seq_cols(vc_t[p], seq), p_t[:, cols]) * inv[:, cols]
            outs += [o[:HEAD_DIM, :n], o[HEAD_DIM:, n:]]
        oc_ref[n * seq:n * (seq + 1), :] = jnp.concatenate(outs, axis=0).T.astype(BF16)


def _ctx_attention(sink, z, wqb, wkb, wqc, wkc, caches, l):
    nseq = CTX_SEQS_PER_STEP
    blk = lambda w, j=0: pl.BlockSpec((nseq * CTX_LEN, w), lambda b: (b, j))
    cst = lambda w: pl.BlockSpec((1, w), lambda b: (0, 0))
    sds = lambda w: jax.ShapeDtypeStruct((T_CTX, w), BF16)
    cache_w = (128, 128, 512, 512)
    if caches is None:
        cache_blk = [pl.BlockSpec((nseq, DEPTH, CTX_LEN, w), lambda b: (b, 0, 0, 0)) for w in cache_w]
    else:
        cache_blk = [pl.BlockSpec((nseq, None, CTX_LEN, w), lambda b: (b, l, 0, 0)) for w in cache_w]
    cache_sds = [jax.ShapeDtypeStruct((N_CTX_SEQ, DEPTH, CTX_LEN, w), F32) for w in cache_w]
    carried = [] if caches is None else list(caches)
    n_in = 6
    return pl.pallas_call(
        functools.partial(_ctx_attn_kernel, layer=l),
        grid=(N_CTX_SEQ // nseq,),
        in_specs=[pl.BlockSpec(memory_space=pltpu.SMEM), blk(ATT_W, Z_ATT // ATT_W),
                  cst(512), cst(128), cst(512), cst(512)]
                 + [pl.BlockSpec(memory_space=pl.ANY)] * len(carried),
        out_specs=[blk(512), blk(512)] + cache_blk,
        out_shape=[sds(512), sds(512)] + cache_sds,
        input_output_aliases={n_in + j: 2 + j for j in range(len(carried))},
        compiler_params=_cparams(("arbitrary",)),
        name="ctx_attention",
    )(sink, z, wqb, wkb, wqc, wkc, *carried)


SWA_QBLK = 128
SWA_SPAN = SWA_QBLK + 2 * SWA_WINDOW
SWA_BLOCKS_PER_STEP = 4


def _swa_kernel(sink_ref, qt_ref, k_ref, vt_ref, kc_ref, vc_ref, o_ref, kctx_scr, vctx_scr, *, layer):
    step = pl.program_id(1)
    nq = SWA_QBLK

    @pl.when(step == 0)
    def _():
        kctx_scr[...] = kc_ref[...].T.astype(BF16)
        vctx_scr[...] = vc_ref[...].astype(BF16)

    kctx = kctx_scr[...]
    starts, s_lat, s_ctx = [], [], []
    for blk in range(SWA_BLOCKS_PER_STEP):
        i = step * SWA_BLOCKS_PER_STEP + blk
        start = pl.multiple_of(jnp.clip(nq * (i - 1), 0, LAT_LEN - SWA_SPAN), nq)
        starts.append(start)
        kwin = k_ref[pl.ds(start, SWA_SPAN), :]
        kpos = start + lax.broadcasted_iota(jnp.int32, (SWA_SPAN, nq), 0)
        qpos = nq * i + lax.broadcasted_iota(jnp.int32, (SWA_SPAN, nq), 1)
        band = jnp.where(jnp.abs(kpos - qpos) <= SWA_WINDOW, 0.0, -jnp.inf)
        for kvh in range(SWA_KV_HEADS):
            placed = []
            for h in range(SWA_GROUP * kvh, SWA_GROUP * (kvh + 1)):
                rows = qt_ref[HEAD_DIM * h:HEAD_DIM * (h + 1), nq * blk:nq * (blk + 1)]
                z = jnp.zeros_like(rows)
                placed.append(jnp.concatenate([rows, z] if kvh == 0 else [z, rows], axis=0))
            q4 = jnp.concatenate(placed, axis=1)
            s_lat.append(_dot(kwin, q4) + jnp.concatenate([band] * SWA_GROUP, axis=1))
            s_ctx.append(_dot(kctx, q4))
    s_lat = jnp.concatenate(s_lat, axis=1)
    s_ctx = jnp.concatenate(s_ctx, axis=1)

    sink = jnp.concatenate(
        [jnp.full((1, nq), sink_ref[layer, h] * LOG2E, F32) for h in range(SWA_Q_HEADS)]
        * SWA_BLOCKS_PER_STEP, axis=1)
    mx = jnp.maximum(jnp.maximum(s_lat.max(axis=0, keepdims=True),
                                 s_ctx.max(axis=0, keepdims=True)), sink)
    p_lat = jnp.exp2(s_lat - mx)
    p_ctx = jnp.exp2(s_ctx - mx)
    inv = 1.0 / (p_lat.sum(axis=0, keepdims=True) + p_ctx.sum(axis=0, keepdims=True)
                 + jnp.exp2(sink - mx))
    p_lat = p_lat.astype(BF16)
    o_ctx = _dot(vctx_scr[...], p_ctx.astype(BF16))
    width = SWA_Q_HEADS * nq
    for blk, start in enumerate(starts):
        cols = slice(width * blk, width * (blk + 1))
        vwin_t = vt_ref[:, pl.ds(start, SWA_SPAN)]
        o = (_dot(vwin_t, p_lat[:, cols]) + o_ctx[:, cols]) * inv[:, cols]
        outs = [o[HEAD_DIM * (h // SWA_GROUP):HEAD_DIM * (h // SWA_GROUP + 1), nq * h:nq * (h + 1)]
                for h in range(SWA_Q_HEADS)]
        o_ref[nq * blk:nq * (blk + 1), :] = jnp.concatenate(outs, axis=0).T.astype(BF16)


def _swa_attention(sink, qb_t, kb, vb_t, cache_k, cache_v, l):
    rows = SWA_QBLK * SWA_BLOCKS_PER_STEP
    steps = LAT_LEN // rows
    cache = pl.BlockSpec((None, None, 128, PAST_LEN), lambda b, i: (b, l, 0, 0))
    return pl.pallas_call(
        functools.partial(_swa_kernel, layer=l),
        grid=(N_LAT_SEQ, steps),
        in_specs=[pl.BlockSpec(memory_space=pltpu.SMEM),
                  pl.BlockSpec((512, rows), lambda b, i: (0, b * steps + i)),
                  pl.BlockSpec((LAT_LEN, 128), lambda b, i: (b, 0)),
                  pl.BlockSpec((128, LAT_LEN), lambda b, i: (0, b)),
                  cache, cache],
        out_specs=pl.BlockSpec((rows, 512), lambda b, i: (b * steps + i, 0)),
        out_shape=jax.ShapeDtypeStruct((T_LAT, 512), BF16),
        scratch_shapes=[pltpu.VMEM((PAST_LEN, 128), BF16), pltpu.VMEM((128, PAST_LEN), BF16)],
        compiler_params=_cparams(("arbitrary", "arbitrary")),
        name="swa_attention",
    )(sink, qb_t, kb, vb_t, cache_k, cache_v)


LAT_ROWS = LAT_LEN // GRID_W
NA_WIN_ROWS = min(NA_ROWS, LAT_ROWS)
NA_KEYS = NA_WIN_ROWS * GRID_W
NA_ROWS_PER_STEP = 8


def _na_kernel(q_ref, k_ref, v_ref, kc_ref, vc_ref, rp_ref, o_ref, kctx_scr, vctx_scr, bias_ref):
    step = pl.program_id(1)
    low = lax.broadcasted_iota(jnp.int32, (1, 128), 1) < HEAD_DIM

    @pl.when(step == 0)
    def _():
        kctx_scr[...] = kc_ref[...].astype(BF16)
        vctx_scr[...] = vc_ref[...].astype(BF16)
        cq = lax.broadcasted_iota(jnp.int32, (GRID_W, 128), 0)
        ck = lax.broadcasted_iota(jnp.int32, (GRID_W, 128), 1) % GRID_W
        cs = jnp.clip(cq - NA_COLS // 2, 0, GRID_W - NA_COLS)
        window = jnp.where((ck >= cs) & (ck < cs + NA_COLS), 0.0, -jnp.inf)
        for h in range(NA_HEADS):
            rows = [jnp.broadcast_to(rp_ref[h, dr:dr + 1, :], (GRID_W, 128))
                    for dr in range(2 * NA_ROWS - 1)]
            left = [pltpu.roll(x, 0, 1, stride=1, stride_axis=0) for x in rows[:-1]]
            right = [pltpu.roll(x, GRID_W, 1, stride=1, stride_axis=0) for x in rows[1:]]
            for j in range(2 * NA_ROWS - 2):
                bias_ref[h, j] = jnp.where(low, left[j], right[j]) * LOG2E + window

    npair = NA_HEADS // 2
    windows = []
    s_lat, s_ctx = [], []
    for rr in range(NA_ROWS_PER_STEP):
        r = step * NA_ROWS_PER_STEP + rr
        rs = jnp.clip(r - NA_ROWS // 2, 0, LAT_ROWS - NA_WIN_ROWS)
        start = pl.multiple_of(rs * GRID_W, GRID_W)
        base = rs - r + NA_ROWS - 1
        windows.append(start)
        for p in range(npair):
            sl = slice(128 * p, 128 * (p + 1))
            q = q_ref[GRID_W * rr:GRID_W * (rr + 1), sl]
            zero = jnp.zeros_like(q)
            q2 = jnp.concatenate([jnp.where(low, q, zero), jnp.where(low, zero, q)], axis=0)
            bias = jnp.concatenate(
                [jnp.concatenate([bias_ref[2 * p + hh, base + 2 * w] for w in range(NA_WIN_ROWS // 2)],
                                 axis=1) for hh in range(2)], axis=0)
            s_lat.append(_nt_dot(q2, k_ref[pl.ds(start, NA_KEYS), sl]) + bias)
            s_ctx.append(_dot(q2, kctx_scr[sl, :]))
    s_lat = jnp.concatenate(s_lat, axis=0)
    s_ctx = jnp.concatenate(s_ctx, axis=0)
    mx = jnp.maximum(s_lat.max(axis=-1, keepdims=True), s_ctx.max(axis=-1, keepdims=True))
    p_lat = jnp.exp2(s_lat - mx)
    p_ctx = jnp.exp2(s_ctx - mx)
    inv = 1.0 / (p_lat.sum(axis=-1, keepdims=True) + p_ctx.sum(axis=-1, keepdims=True))
    p_lat = p_lat.astype(BF16)
    p_ctx = p_ctx.astype(BF16)
    for rr, start in enumerate(windows):
        tiles = []
        for p in range(npair):
            sl = slice(128 * p, 128 * (p + 1))
            first = 2 * GRID_W * (npair * rr + p)
            rows = slice(first, first + 2 * GRID_W)
            o2 = (_dot(p_lat[rows], v_ref[pl.ds(start, NA_KEYS), sl])
                  + _nt_dot(p_ctx[rows], vctx_scr[sl, :])) * inv[rows]
            tiles.append(jnp.where(low, o2[:GRID_W], o2[GRID_W:]))
        o_ref[GRID_W * rr:GRID_W * (rr + 1), :] = jnp.concatenate(tiles, axis=1).astype(BF16)


def _na_attention(qc, kc, vc, cache_k, cache_v, rp_cyc, l):
    steps = LAT_ROWS // NA_ROWS_PER_STEP
    rows = NA_ROWS_PER_STEP * GRID_W
    kv = pl.BlockSpec((LAT_LEN, 512), lambda b, r: (b, 0))
    cache = pl.BlockSpec((None, None, 512, PAST_LEN), lambda b, r: (b, l, 0, 0))
    return pl.pallas_call(
        _na_kernel,
        grid=(N_LAT_SEQ, steps),
        in_specs=[pl.BlockSpec((rows, 512), lambda b, r: (b * steps + r, 0)),
                  kv, kv, cache, cache,
                  pl.BlockSpec((None, NA_HEADS, 2 * NA_ROWS - 1, 128), lambda b, r: (l, 0, 0, 0))],
        out_specs=pl.BlockSpec((rows, 512), lambda b, r: (b * steps + r, 0)),
        out_shape=jax.ShapeDtypeStruct((T_LAT, 512), BF16),
        scratch_shapes=[pltpu.VMEM((512, PAST_LEN), BF16), pltpu.VMEM((512, PAST_LEN), BF16),
                        pltpu.VMEM((NA_HEADS, 2 * NA_ROWS - 2, GRID_W, 128), F32)],
        compiler_params=_cparams(("arbitrary", "arbitrary")),
        name="na_attention",
    )(qc, kc, vc, cache_k, cache_v, rp_cyc)


def _na_table_rows(rpb):
    pad = jnp.zeros(rpb.shape[:-1] + (128 - (2 * NA_COLS - 1),), F32)
    return jnp.concatenate([rpb[..., NA_COLS - 1:], pad, rpb[..., :NA_COLS - 1]], axis=-1)


def _gla_constants():
    c = GLA_C
    t = np.arange(c)[:, None]
    u = np.arange(c)[None, :]
    blocks = []
    for k in range(GLA_LEVELS):
        b = 1 << k
        m = ((t >> k) | 1) * b - 1
        query = ((t >> k) & 1) == 1
        blocks.append(np.where(query, (u > m) & (u <= t), (u > t) & (u <= m)))
    blocks.append(u <= t)
    blocks.append(u > t)
    blocks.append(np.ones((16, c), bool))
    fwd = np.concatenate(blocks, axis=0).astype(np.float32)
    bwd_blocks = [blk[::-1, ::-1] for blk in blocks]
    bwd = np.concatenate(bwd_blocks, axis=0).astype(np.float32)
    seg = np.stack([fwd, bwd])
    seg = np.concatenate([seg, seg], axis=-1)

    s = np.arange(c)[None, :]
    x = t ^ s
    lev = np.where(x == 0, GLA_LEVELS, np.floor(np.log2(np.maximum(x, 1))).astype(np.int64))
    lev_f = np.where(s <= t, lev, -1)
    lev_b = np.where(s >= t, lev, -1)
    levmap = np.stack([np.tile(lev_f, (1, GLA_HEADS)), np.tile(lev_b, (1, GLA_HEADS))])
    return seg, levmap.astype(np.int32)


def _gla_chunks(chains):
    c = GLA_C
    lane_head = lax.broadcasted_iota(jnp.int32, (1, GLA_HEADS * GLA_DK), 1) // GLA_DK
    row_head = lax.broadcasted_iota(jnp.int32, (GLA_HEADS * GLA_DV, 1), 0) // GLA_DV
    head_sel = [jnp.where(lane_head == h, 1.0, 0.0).astype(BF16) for h in range(GLA_HEADS)]

    def operands(q, k, e=None):
        if e is not None:
            q, k = q * e, k * e
        return q, jnp.concatenate([k * head_sel[h] for h in range(GLA_HEADS)], axis=0)

    segs = []
    for (_, _, _, lr_ref, w2_ref, ba_ref, seg_ref, *_) in chains:
        x = _dot(lr_ref[...], w2_ref[...]) + ba_ref[...]
        la = (jnp.minimum(x, 0.0) - jnp.log(1.0 + jnp.exp(-jnp.abs(x)))) * (1.0 / GLA_TAU)
        la_hi = la.astype(BF16)
        la_lo = (la - la_hi.astype(F32)).astype(BF16)
        segs.append(_dot(seg_ref[...], jnp.concatenate([la_hi, la_lo], axis=0)))

    def decay(seg, block):
        return jnp.exp(seg[c * block:c * (block + 1)]).astype(BF16)

    qs = [ch[0][...] * (GLA_DK ** -0.5) for ch in chains]
    ks = [ch[1][...] for ch in chains]
    order = [GLA_LEVELS] + list(range(GLA_LEVELS))
    prepare = lambda lvl: [operands(qs[n], ks[n], None if lvl == GLA_LEVELS else decay(segs[n], lvl))
                           for n in range(len(chains))]
    attn = [None] * len(chains)
    ops = prepare(order[0])
    for i, lvl in enumerate(order):
        scores = [_nt_dot(q, kbd) for q, kbd in ops]
        if i + 1 < len(order):
            ops = prepare(order[i + 1])
        for n, ch in enumerate(chains):
            attn[n] = jnp.where(ch[7][...] == lvl, scores[n], 0.0 if attn[n] is None else attn[n])

    new_states = []
    for n, (_, _, v_ref, _, _, _, _, _, o_ref, st_ref) in enumerate(chains):
        seg = segs[n]
        v = v_ref[...]
        a = attn[n].astype(BF16)
        tot = seg[c * (GLA_LEVELS + 2):c * (GLA_LEVELS + 2) + 1]
        q_in = qs[n] * decay(seg, GLA_LEVELS)
        k_in = ks[n] * decay(seg, GLA_LEVELS + 1)
        state = st_ref[...]
        o_inter = _nt_dot(q_in, state.astype(BF16))
        for h in range(GLA_HEADS):
            sl = slice(GLA_DV * h, GLA_DV * (h + 1))
            o_ref[:, sl] = o_inter[:, sl] + _dot(a[:, c * h:c * (h + 1)], v[:, sl])
        upd = _tn_dot(v, k_in)
        new_state = state * jnp.exp(tot) + jnp.where(row_head == lane_head, upd, 0.0)
        st_ref[...] = new_state
        new_states.append(new_state)
    return new_states


GLA_GROUPS = 3
GLA_GROUP_ROWS = T_ALL // GLA_GROUPS
GLA_CTX_GROUPS = T_CTX // GLA_GROUP_ROWS
GLA_STEPS = GLA_GROUP_ROWS // GLA_C
GLA_CTX_CHUNKS = CTX_LEN // GLA_C
GLA_LAT_CHUNKS = LAT_LEN // GLA_C


def _gla_kernel(qf, kf, vf, lrf, qb, kb, vb, lrb, w2_ref, ba_ref, seg_ref, lev_ref, s0f, s0b,
                of_ref, ob_ref, sff_ref, sfb_ref, states):
    step = pl.program_id(0)

    @pl.when(step % GLA_CTX_CHUNKS == 0)
    def _():
        for d in range(2):
            for g in range(GLA_CTX_GROUPS):
                states[d, g] = jnp.zeros(states.shape[2:], F32)

    @pl.when(step % GLA_LAT_CHUNKS == 0)
    def _():
        for d, s0 in enumerate((s0f, s0b)):
            for h in range(GLA_HEADS):
                row = [jnp.zeros((GLA_DV, GLA_DK), F32)] * GLA_HEADS
                row[h] = s0[h]
                states[d, GLA_GROUPS - 1, GLA_DV * h:GLA_DV * (h + 1), :] = jnp.concatenate(row, axis=1)

    chains = []
    for g in range(GLA_GROUPS):
        chains.append((qf.at[g], kf.at[g], vf.at[g], lrf.at[g], w2_ref.at[0], ba_ref.at[0],
                       seg_ref.at[0], lev_ref.at[0], of_ref.at[g], states.at[0, g]))
        chains.append((qb.at[g], kb.at[g], vb.at[g], lrb.at[g], w2_ref.at[1], ba_ref.at[1],
                       seg_ref.at[1], lev_ref.at[1], ob_ref.at[g], states.at[1, g]))
    new_states = _gla_chunks(chains)

    @pl.when(step % GLA_CTX_CHUNKS == GLA_CTX_CHUNKS - 1)
    def _():
        for g in range(GLA_CTX_GROUPS):
            for d, out in enumerate((sff_ref, sfb_ref)):
                st = new_states[2 * g + d]
                for h in range(GLA_HEADS):
                    out[g, h] = st[GLA_DV * h:GLA_DV * (h + 1), GLA_DK * h:GLA_DK * (h + 1)].T


def _gla(z, w2p, ba, seg, levmap, s0t, l):
    c = GLA_C
    z3 = z.reshape(GLA_GROUPS, GLA_GROUP_ROWS, Z_W)
    fwd = lambda s: s
    bwd = lambda s: GLA_STEPS - 1 - s
    seqs_per_group = GLA_GROUP_ROWS // CTX_LEN

    def chunk_specs(pos):
        return [pl.BlockSpec((GLA_GROUPS, c, 256), lambda s: (0, pos(s), Z_AQ // 256)),
                pl.BlockSpec((GLA_GROUPS, c, 256), lambda s: (0, pos(s), Z_AK // 256)),
                pl.BlockSpec((GLA_GROUPS, c, 512), lambda s: (0, pos(s), Z_AV // 512)),
                pl.BlockSpec((GLA_GROUPS, c, 128), lambda s: (0, pos(s), Z_LR // 128))]

    def s0_spec(d, pos):
        return pl.BlockSpec((None, None, None, GLA_HEADS, GLA_DV, GLA_DK),
                            lambda s: (pos(s) // GLA_LAT_CHUNKS, l, d, 0, 0, 0))

    def out_spec(pos):
        return pl.BlockSpec((GLA_GROUPS, c, 512), lambda s: (0, pos(s), 0))

    def sfin_spec(pos):
        return pl.BlockSpec((GLA_CTX_GROUPS, None, GLA_HEADS, GLA_DK, GLA_DV),
                            lambda s: (0, pos(s) // GLA_CTX_CHUNKS, 0, 0, 0))

    state_sds = jax.ShapeDtypeStruct((GLA_CTX_GROUPS, seqs_per_group, GLA_HEADS, GLA_DK, GLA_DV), F32)
    out_sds = jax.ShapeDtypeStruct((GLA_GROUPS, GLA_GROUP_ROWS, 512), F32)
    o_f, o_b, sf, sb = pl.pallas_call(
        _gla_kernel,
        grid=(GLA_STEPS,),
        in_specs=chunk_specs(fwd) + chunk_specs(bwd) + [
            pl.BlockSpec((None, 2, 128, 256), lambda s: (l, 0, 0, 0)),
            pl.BlockSpec((None, 2, 1, 256), lambda s: (l, 0, 0, 0)),
            pl.BlockSpec((2, GLA_GROWS, 2 * c), lambda s: (0, 0, 0)),
            pl.BlockSpec((2, c, GLA_HEADS * c), lambda s: (0, 0, 0)),
            s0_spec(0, fwd), s0_spec(1, bwd),
        ],
        out_specs=[out_spec(fwd), out_spec(bwd), sfin_spec(fwd), sfin_spec(bwd)],
        out_shape=[out_sds, out_sds, state_sds, state_sds],
        scratch_shapes=[pltpu.VMEM((2, GLA_GROUPS, GLA_HEADS * GLA_DV, GLA_HEADS * GLA_DK), F32)],
        compiler_params=_cparams(("arbitrary",)),
        name="gla",
    )(z3, z3, z3, z3, z3, z3, z3, z3, w2p, ba, seg, levmap, s0t, s0t)
    flat = lambda a: a.reshape((-1,) + a.shape[2:])
    return flat(o_f), flat(o_b), flat(sf), flat(sb)


def _cast_kernel(*refs):
    n = len(refs) // 2
    for src, dst in zip(refs[:n], refs[n:]):
        dst[...] = src[...].astype(dst.dtype)


def _cast_bf16(*weights):
    spec = lambda w: pl.BlockSpec((None,) + w.shape[1:], lambda l: (l, 0, 0))
    return pl.pallas_call(
        _cast_kernel,
        grid=(DEPTH,),
        in_specs=[spec(w) for w in weights],
        out_specs=[spec(w) for w in weights],
        out_shape=[jax.ShapeDtypeStruct(w.shape, BF16) for w in weights],
        compiler_params=_cparams(("arbitrary",)),
        name="cast_bf16",
    )(*weights)


def _post_mixer_kernel(*refs, nk, n_x, n_y):
    x_refs, refs = refs[:n_x], refs[n_x:]
    (of_ref, obk_ref, ar_ref, gates_ref, bc_ref, bl_ref, cc_ref, cl_ref, wpa_ref, wpb_ref, wpc_ref,
     wo_ref, gn_ref, g2_ref, mod_ref, w1_ref, w2_ref) = refs[:17]
    y_refs = refs[17:17 + n_y]
    w1_scr, w2_scr, h_scr, x1_scr, acc_scr = refs[17 + n_y:]
    s = pl.program_id(0)
    d = D_MODEL
    tm = of_ref.shape[0]
    th = w1_ref.shape[1]
    ctx = jnp.maximum(s - (nk - 1), 0) < T_CTX // tm

    def put_y(val):
        if n_y == 1:
            y_refs[0][...] = val
        else:
            @pl.when(ctx)
            def _():
                y_refs[0][...] = val

            @pl.when(jnp.logical_not(ctx))
            def _():
                y_refs[1][...] = val

    def merged_residual():
        o = of_ref[...] + obk_ref[...]
        heads = []
        for h in range(GLA_HEADS):
            oh = o[:, GLA_DV * h:GLA_DV * (h + 1)]
            ms = jnp.mean(oh * oh, axis=-1, keepdims=True)
            heads.append(oh * lax.rsqrt(ms + EPS) * gn_ref[...])
        oa = (jnp.concatenate(heads, axis=1) * _silu(ar_ref[...].astype(F32))).astype(BF16)
        ob = jnp.where(ctx, bc_ref[...], bl_ref[...])
        oc = jnp.where(ctx, cc_ref[...], cl_ref[...])
        gate = lambda j: _sigmoid(gates_ref[:, j * d:(j + 1) * d].astype(F32))
        merged = (gate(0) * _dot(oa, wpa_ref[...]) + gate(1) * _dot(ob, wpb_ref[...])
                  + gate(2) * _dot(oc, wpc_ref[...]))
        return (_token_rows(x_refs, ctx)
                + mod_ref[:, 2 * d:3 * d] * _dot(merged.astype(BF16), wo_ref[...]))

    def normed(x1):
        return _norm_mod(x1, g2_ref[...], mod_ref[:, 3 * d:4 * d], mod_ref[:, 4 * d:5 * d]).astype(BF16)

    def act(u):
        return jnp.square(jnp.maximum(u, 0.0)).astype(BF16)

    @pl.when(s == 0)
    def _():
        x1 = merged_residual()
        x1_scr[...] = x1
        h_scr[...] = normed(x1)

    @pl.when(s < nk)
    def _():
        col = pl.multiple_of(s * th, th)
        w1 = w1_ref[...].astype(BF16)
        w2 = w2_ref[...].astype(BF16)
        w1_scr[:, pl.ds(col, th)] = w1
        w2_scr[pl.ds(col, th), :] = w2
        part = _dot(act(_dot(h_scr[...], w1)), w2)

        @pl.when(s == 0)
        def _():
            acc_scr[...] = part

        @pl.when(s > 0)
        def _():
            acc_scr[...] += part

        @pl.when(s == nk - 1)
        def _():
            put_y(x1_scr[...] + mod_ref[:, 5 * d:6 * d] * acc_scr[...])

    @pl.when(s >= nk)
    def _():
        x1 = merged_residual()
        out = _dot(act(_dot(normed(x1), w1_scr[...])), w2_scr[...])
        put_y(x1 + mod_ref[:, 5 * d:6 * d] * out)


def _post_mixer(x_parts, o_fwd, o_bwd, z, ob_ctx, ob_lat, oc_ctx, oc_lat, wpa, wpb, wpc, wo, gn, norm2_w,
                mod4, w1, w2, l, split_out):
    tm, th = 256, 512
    nk = MLP_HIDDEN // th
    ctx_blocks = T_CTX // tm
    row = lambda s: jnp.maximum(s - (nk - 1), 0)
    chunk = lambda s: jnp.minimum(s, nk - 1)
    ctx_idx = lambda s: (jnp.minimum(row(s), ctx_blocks - 1), 0)
    lat_idx = lambda s: (jnp.maximum(row(s) - ctx_blocks, 0), 0)
    rows = lambda w, j=0: pl.BlockSpec((tm, w), lambda s: (row(s), j))
    resident = lambda k: pl.BlockSpec((None, k, D_MODEL), lambda s: (l, 0, 0),
                                      pipeline_mode=pl.Buffered(1))
    if split_out:
        out_specs = [pl.BlockSpec((tm, D_MODEL), ctx_idx), pl.BlockSpec((tm, D_MODEL), lat_idx)]
        out_shape = [jax.ShapeDtypeStruct((T_CTX, D_MODEL), F32), jax.ShapeDtypeStruct((T_LAT, D_MODEL), F32)]
    else:
        out_specs = [rows(D_MODEL)]
        out_shape = [jax.ShapeDtypeStruct((T_ALL, D_MODEL), F32)]
    return pl.pallas_call(
        functools.partial(_post_mixer_kernel, nk=nk, n_x=len(x_parts), n_y=len(out_specs)),
        grid=(nk - 1 + T_ALL // tm,),
        in_specs=_token_specs(x_parts, tm, row) + [
            rows(512), rows(512), rows(512, Z_AR // 512), rows(3 * D_MODEL),
            pl.BlockSpec((tm, 512), ctx_idx), pl.BlockSpec((tm, 512), lat_idx),
            pl.BlockSpec((tm, 512), ctx_idx), pl.BlockSpec((tm, 512), lat_idx),
            resident(512), resident(512), resident(512), resident(D_MODEL),
            pl.BlockSpec((None, 1, GLA_DV), lambda s: (l, 0, 0)),
            pl.BlockSpec((None, 1, D_MODEL), lambda s: (l, 0, 0)),
            pl.BlockSpec((None, None, 1, 6 * D_MODEL),
                         lambda s: (l, _group_of_rows(row(s), tm), 0, 0)),
            pl.BlockSpec((None, D_MODEL, th), lambda s: (l, 0, chunk(s))),
            pl.BlockSpec((None, th, D_MODEL), lambda s: (l, chunk(s), 0)),
        ],
        out_specs=out_specs,
        out_shape=out_shape,
        scratch_shapes=[pltpu.VMEM((D_MODEL, MLP_HIDDEN), BF16), pltpu.VMEM((MLP_HIDDEN, D_MODEL), BF16),
                        pltpu.VMEM((tm, D_MODEL), BF16), pltpu.VMEM((tm, D_MODEL), F32),
                        pltpu.VMEM((tm, D_MODEL), F32)],
        compiler_params=_cparams(("arbitrary",)),
        name="post_mixer",
    )(*x_parts, o_fwd, o_bwd, z, z, ob_ctx, ob_lat, oc_ctx, oc_lat, wpa, wpb, wpc, wo, gn, norm2_w, mod4,
      w1, w2)


def _rope_tables():
    t = jnp.arange(LAT_LEN)
    row = (t // GRID_W).astype(F32)
    col = (t % GRID_W).astype(F32)
    nf = HEAD_DIM // 4
    inv_freq = ROPE_BASE ** (-jnp.arange(nf, dtype=F32) / nf)
    ang_r = row[:, None] * inv_freq[None, :]
    ang_c = col[:, None] * inv_freq[None, :]
    cos = jnp.concatenate([jnp.cos(ang_r)] * 2 + [jnp.cos(ang_c)] * 2, axis=1)
    sin = jnp.concatenate([-jnp.sin(ang_r), jnp.sin(ang_r), -jnp.sin(ang_c), jnp.sin(ang_c)], axis=1)
    return jnp.tile(cos, (1, 2)), jnp.tile(sin, (1, 2))


def kernel(x_prompt, x_sample, state_gla, cache_swa_k, cache_swa_v, cache_na_k, cache_na_v, c,
           c_ctx, w_mod, b_mod, norm1, norm2, w_in, w_a2_f, b_a_f, w_a2_b, b_a_b, gla_onorm,
           qn_swa, kn_swa, sink_swa, qn_na, kn_na, rpb_na, w_pa, w_pb, w_pc, w_o, w_fc1, w_fc2):
    d = D_MODEL
    x_parts = (x_prompt.reshape(T_CTX, d), x_sample.reshape(T_LAT, d))

    cond8 = jnp.zeros((8, d), F32).at[0].set(c_ctx).at[1:1 + N_LAT_SEQ].set(c)
    mod4 = _modulation(cond8, w_mod, b_mod).reshape(DEPTH, 8, 1, 6 * d)

    w_in_p = _w_in_layout(jnp.swapaxes(w_in, 1, 2))
    wpa, wpb, wpc, wo = _cast_bf16(w_pa, w_pb, w_pc, w_o)
    norm1r = norm1.reshape(DEPTH, 1, d)
    norm2r = norm2.reshape(DEPTH, 1, d)
    gnr = gla_onorm.reshape(DEPTH, 1, GLA_DV)
    w2p = jnp.zeros((DEPTH, 2, 128, GLA_HEADS * GLA_DK), F32)
    w2p = w2p.at[:, 0, 0:GLA_LOWRANK].set(w_a2_f).at[:, 1, GLA_LOWRANK:2 * GLA_LOWRANK].set(w_a2_b)
    w2p = w2p.astype(BF16)
    ba = jnp.stack([b_a_f, b_a_b], axis=1).reshape(DEPTH, 2, 1, GLA_HEADS * GLA_DK)
    seg_np, lev_np = _gla_constants()
    seg = jnp.asarray(seg_np, BF16)
    levmap = jnp.asarray(lev_np)
    s0t = jnp.swapaxes(state_gla, -1, -2)
    cos_t, sin_t = _rope_tables()
    rp_cyc = _na_table_rows(rpb_na)
    feat_major = lambda a: a.transpose(0, 1, 3, 4, 2).reshape(N_LAT_SEQ, DEPTH, -1, PAST_LEN)
    csk, csv, cnk, cnv = (feat_major(a) for a in (cache_swa_k, cache_swa_v, cache_na_k, cache_na_v))

    st_l = []
    caches = None
    for l in range(DEPTH):
        z = _inproj(x_parts, norm1r, mod4, w_in_p, l)
        wqb = jnp.tile(qn_swa[l], 8)[None, :]
        wkb = jnp.tile(kn_swa[l], 2)[None, :]
        wqc = jnp.tile(qn_na[l], 8)[None, :]
        wkc = jnp.tile(kn_na[l], 8)[None, :]
        qb, kb, vb, qc, kc, vc = _attn_prep(z, wqb, wkb, wqc, wkc, cos_t, sin_t)
        o_fwd, o_bwd, sfin_f, sfin_b = _gla(z, w2p, ba, seg, levmap, s0t, l)
        ob_ctx, oc_ctx, *caches = _ctx_attention(sink_swa, z, wqb, wkb, wqc, wkc, caches, l)
        ob_lat = _swa_attention(sink_swa, qb, kb, vb, csk, csv, l)
        oc_lat = _na_attention(qc, kc, vc, cnk, cnv, rp_cyc, l)
        x_parts = _post_mixer(x_parts, o_fwd, o_bwd, z, ob_ctx, ob_lat, oc_ctx, oc_lat, wpa, wpb, wpc, wo,
                              gnr, norm2r, mod4, w_fc1, w_fc2, l, split_out=(l == DEPTH - 1))

        st_l += [sfin_f, sfin_b]

    y_prompt = x_parts[0].reshape(N_CTX_SEQ, CTX_LEN, d)
    y_sample = x_parts[1].reshape(N_LAT_SEQ, LAT_LEN, d)
    new_state = jnp.stack(st_l, axis=1).reshape(
        N_CTX_SEQ, DEPTH, 2, GLA_HEADS, GLA_DK, GLA_DV)

    swa_k, swa_v, na_k, na_v = caches
    kv_shape = lambda heads: (N_CTX_SEQ, DEPTH, CTX_LEN, heads, HEAD_DIM)
    return (y_prompt, y_sample, new_state,
            swa_k.reshape(kv_shape(SWA_KV_HEADS)), swa_v.reshape(kv_shape(SWA_KV_HEADS)),
            na_k.reshape(kv_shape(NA_HEADS)), na_v.reshape(kv_shape(NA_HEADS)))
```

```python
import functools

import numpy as np
import jax
import jax.numpy as jnp
from jax import lax
from jax.experimental import pallas as pl
from jax.experimental.pallas import tpu as pltpu

F32 = jnp.float32
BF16 = jnp.bfloat16

D_MODEL = 1024
DEPTH = 4
N_CTX_SEQ = 16
CTX_LEN = 256
N_LAT_SEQ = 2
LAT_LEN = 1024
PAST_LEN = 512
T_CTX = N_CTX_SEQ * CTX_LEN
T_LAT = N_LAT_SEQ * LAT_LEN
T_ALL = T_CTX + T_LAT
GRID_W = 64
HEAD_DIM = 64
GLA_HEADS = 4
GLA_DK = 64
GLA_DV = 128
GLA_LOWRANK = 16
GLA_TAU = 16.0
SWA_Q_HEADS = 8
SWA_KV_HEADS = 2
SWA_GROUP = 4
SWA_WINDOW = 128
NA_HEADS = 8
NA_ROWS = 8
NA_COLS = 16
MLP_HIDDEN = 4 * D_MODEL
ROPE_BASE = 10000.0
EPS = 1e-6
LOG2E = 1.4426950408889634
QK_SCALE = HEAD_DIM ** -0.5 * LOG2E

Z_GATES = 0
Z_GLA = 3072
Z_AQ, Z_AK, Z_AV, Z_AR = 3072, 3328, 3584, 4096
Z_ATT = 4608
ATT_W = 2304
A_BQ, A_BK, A_BV, A_CQ, A_CK, A_CV = 0, 512, 640, 768, 1280, 1792
Z_LR = 6912
Z_W = 7168

GLA_C = 128
GLA_LEVELS = 7
GLA_GROWS = (GLA_LEVELS + 2) * GLA_C + 16

VMEM_LIMIT = 56 * 1024 * 1024


def _cparams(sem):
    return pltpu.CompilerParams(dimension_semantics=sem, vmem_limit_bytes=VMEM_LIMIT)


def _sigmoid(x):
    return 1.0 / (1.0 + jnp.exp(-x))


def _silu(x):
    return x * _sigmoid(x)


def _nt_dot(a, b):
    return lax.dot_general(a, b, (((1,), (1,)), ((), ())), preferred_element_type=F32)


def _tn_dot(a, b):
    return lax.dot_general(a, b, (((0,), (0,)), ((), ())), preferred_element_type=F32)


def _dot(a, b):
    return jnp.dot(a, b, preferred_element_type=F32)


def _mod_kernel(cond_ref, w_ref, b_ref, o_ref):
    s = _silu(cond_ref[...]).astype(BF16)
    o_ref[...] = _dot(s, w_ref[...].astype(BF16)) + b_ref[...]


def _modulation(cond8, w_mod, b_mod):
    tn = 2048
    return pl.pallas_call(
        _mod_kernel,
        grid=(DEPTH, 6 * D_MODEL // tn),
        in_specs=[
            pl.BlockSpec((8, D_MODEL), lambda l, j: (0, 0)),
            pl.BlockSpec((None, D_MODEL, tn), lambda l, j: (l, 0, j)),
            pl.BlockSpec((None, 1, tn), lambda l, j: (l, 0, j)),
        ],
        out_specs=pl.BlockSpec((None, 8, tn), lambda l, j: (l, 0, j)),
        out_shape=jax.ShapeDtypeStruct((DEPTH, 8, 6 * D_MODEL), F32),
        compiler_params=_cparams(("arbitrary", "arbitrary")),
        name="modulation",
    )(cond8, w_mod, b_mod.reshape(DEPTH, 1, 6 * D_MODEL))


def _group_of_rows(row_block, rows_per_block):
    first = row_block * rows_per_block
    return jnp.maximum(first - T_CTX + LAT_LEN, 0) // LAT_LEN


def _norm_mod(x, g, shift, scale):
    ms = jnp.mean(x * x, axis=-1, keepdims=True)
    y = x * lax.rsqrt(ms + EPS) * g
    return y * (1.0 + scale) + shift


W_IN_SEGMENTS = ((3872, 6944), (0, 1536), (1568, 3872), (1536, 1568))
W_IN_COLS = 6944


def _w_in_layout_kernel(w_ref, o_ref):
    dst = 0
    for lo, hi in W_IN_SEGMENTS:
        o_ref[dst:dst + hi - lo, :] = w_ref[lo:hi, :].astype(BF16)
        dst += hi - lo
    o_ref[dst:, :] = jnp.zeros((Z_W - dst, o_ref.shape[1]), BF16)


def _w_in_layout(w_in_t):
    tk = 512
    return pl.pallas_call(
        _w_in_layout_kernel,
        grid=(DEPTH, D_MODEL // tk),
        in_specs=[pl.BlockSpec((None, W_IN_COLS, tk), lambda l, r: (l, 0, r))],
        out_specs=pl.BlockSpec((None, Z_W, tk), lambda l, r: (l, 0, r)),
        out_shape=jax.ShapeDtypeStruct((DEPTH, Z_W, D_MODEL), BF16),
        compiler_params=_cparams(("arbitrary", "arbitrary")),
        name="w_in_layout",
    )(w_in_t)


def _token_rows(x_refs, ctx):
    if len(x_refs) == 1:
        return x_refs[0][...]
    return jnp.where(ctx, x_refs[0][...], x_refs[1][...])


def _token_specs(x_parts, tm, row):
    if len(x_parts) == 1:
        return [pl.BlockSpec((tm, D_MODEL), lambda s: (row(s), 0))]
    ctx_blocks = T_CTX // tm
    return [pl.BlockSpec((tm, D_MODEL), lambda s: (jnp.minimum(row(s), ctx_blocks - 1), 0)),
            pl.BlockSpec((tm, D_MODEL), lambda s: (jnp.maximum(row(s) - ctx_blocks, 0), 0))]


def _inproj_kernel(*refs):
    *x_refs, g_ref, mod_ref, w_ref, z_ref = refs
    ctx = pl.program_id(0) < T_CTX // z_ref.shape[0]
    h = _norm_mod(_token_rows(x_refs, ctx), g_ref[...],
                  mod_ref[:, 0:D_MODEL], mod_ref[:, D_MODEL:2 * D_MODEL])
    z_ref[...] = _nt_dot(h.astype(BF16), w_ref[...]).astype(z_ref.dtype)


def _inproj(x_parts, norm_w, mod4, w_in_p, l):
    tm = 512
    return pl.pallas_call(
        _inproj_kernel,
        grid=(T_ALL // tm,),
        in_specs=_token_specs(x_parts, tm, lambda i: i) + [
            pl.BlockSpec((None, 1, D_MODEL), lambda i: (l, 0, 0)),
            pl.BlockSpec((None, None, 1, 6 * D_MODEL), lambda i: (l, _group_of_rows(i, tm), 0, 0)),
            pl.BlockSpec((None, Z_W, D_MODEL), lambda i: (l, 0, 0),
                         pipeline_mode=pl.Buffered(1)),
        ],
        out_specs=pl.BlockSpec((tm, Z_W), lambda i: (i, 0)),
        out_shape=jax.ShapeDtypeStruct((T_ALL, Z_W), BF16),
        compiler_params=_cparams(("arbitrary",)),
        name="inproj",
    )(*x_parts, norm_w, mod4, w_in_p)


def _head_norm_pairs(z_ref, groups):
    same_head = (lax.broadcasted_iota(jnp.int32, (128, 128), 0) // HEAD_DIM
                 == lax.broadcasted_iota(jnp.int32, (128, 128), 1) // HEAD_DIM)
    ones = jnp.where(same_head, 1.0, 0.0).astype(BF16)
    tiles = [[(z_ref[:, col + 128 * p: col + 128 * (p + 1)].astype(F32), w_ref[:, 128 * p:128 * (p + 1)])
              for p in range(width // 128)] for col, width, w_ref in groups]
    sums = [[_dot((x * x).astype(BF16), ones) for x, _ in group] for group in tiles]
    return [[x * lax.rsqrt(ss * (1.0 / HEAD_DIM) + EPS) * w for (x, w), ss in zip(group, gs)]
            for group, gs in zip(tiles, sums)]


def _rope(x, cos, sin_signed):
    n = x.shape[-1]
    lane = lax.broadcasted_iota(jnp.int32, (1, n), 1)
    first = (lane % 32) < 16
    partner = jnp.where(first, pltpu.roll(x, n - 16, 1), pltpu.roll(x, 16, 1))
    return x * cos + partner * sin_signed


def _prep_kernel(z_ref, wqb_ref, wkb_ref, wqc_ref, wkc_ref, cos_ref, sin_ref,
                 qb_ref, kb_ref, vb_ref, qc_ref, kc_ref, vc_ref):
    qscale = QK_SCALE

    qb, kb, qc, kc = _head_norm_pairs(z_ref, [(A_BQ, 512, wqb_ref), (A_BK, 128, wkb_ref),
                                              (A_CQ, 512, wqc_ref), (A_CK, 512, wkc_ref)])

    vb_ref[...] = z_ref[:, A_BV:A_BV + 128].astype(F32).T.astype(BF16)
    vc_ref[...] = z_ref[:, A_CV:A_CV + 512]
    for p in range(4):
        qc_ref[:, 128 * p:128 * (p + 1)] = (qc[p] * qscale).astype(BF16)
        kc_ref[:, 128 * p:128 * (p + 1)] = kc[p].astype(BF16)

    cos = cos_ref[...]
    sin = sin_ref[...]
    for p in range(4):
        qb_ref[128 * p:128 * (p + 1), :] = (_rope(qb[p], cos, sin) * qscale).T.astype(BF16)
    kb_ref[...] = _rope(kb[0], cos, sin).astype(BF16)


def _attn_prep(z, wqb, wkb, wqc, wkc, cos_t, sin_t):
    tm = 1024
    lat_blocks = LAT_LEN // tm
    rope_idx = lambda i: (i % lat_blocks, 0)
    row = lambda w: pl.BlockSpec((tm, w), lambda i: (i, 0))
    col = lambda w: pl.BlockSpec((w, tm), lambda i: (0, i))
    cst = lambda w: pl.BlockSpec((1, w), lambda i: (0, 0))
    sds = lambda w: jax.ShapeDtypeStruct((T_LAT, w), BF16)
    sds_t = lambda w: jax.ShapeDtypeStruct((w, T_LAT), BF16)
    return pl.pallas_call(
        _prep_kernel,
        grid=(T_LAT // tm,),
        in_specs=[
            pl.BlockSpec((tm, ATT_W), lambda i: (T_CTX // tm + i, Z_ATT // ATT_W)),
            cst(512), cst(128), cst(512), cst(512),
            pl.BlockSpec((tm, 128), rope_idx),
            pl.BlockSpec((tm, 128), rope_idx),
        ],
        out_specs=[col(512), row(128), col(128), row(512), row(512), row(512)],
        out_shape=[sds_t(512), sds(128), sds_t(128), sds(512), sds(512), sds(512)],
        compiler_params=_cparams(("arbitrary",)),
        name="attn_prep",
    )(z, wqb, wkb, wqc, wkc, cos_t, sin_t)


def _half_rows(t, half):
    z = jnp.zeros((HEAD_DIM, t.shape[1]), t.dtype)
    return jnp.concatenate([t[:HEAD_DIM], z] if half == 0 else [z, t[HEAD_DIM:]], axis=0)


CTX_SEQS_PER_STEP = 2


def _ctx_attn_kernel(sink_ref, z_ref, wqb_ref, wkb_ref, wqc_ref, wkc_ref, *refs, layer):
    ob_ref, oc_ref = refs[-6:-4]
    cache_refs = refs[-4:]
    first_layer = len(refs) == 6
    if first_layer:
        for ref in cache_refs:
            ref[...] = jnp.zeros(ref.shape, F32)
    kbf_ref, vbf_ref, kcf_ref, vcf_ref = cache_refs
    n = CTX_LEN
    nseq = CTX_SEQS_PER_STEP
    qscale = QK_SCALE

    def put(ref, seq, val, cols=slice(None)):
        if first_layer:
            ref[seq, layer, :, cols] = val
        else:
            ref[seq, :, cols] = val

    def seq_rows(x, seq):
        return x[n * seq:n * (seq + 1)]

    def seq_cols(x, seq):
        return x[:, n * seq:n * (seq + 1)]

    def pair(col, p):
        return z_ref[:, col + 128 * p: col + 128 * (p + 1)]

    qb_n, kb_n, qc_n, kc_n = _head_norm_pairs(z_ref, [(A_BQ, 512, wqb_ref), (A_BK, 128, wkb_ref),
                                                      (A_CQ, 512, wqc_ref), (A_CK, 512, wkc_ref)])

    def q_t(tile):
        return (tile * qscale).T.astype(BF16)

    def v_t(col, p):
        return pair(col, p).astype(F32).T.astype(BF16)

    for seq in range(nseq):
        put(vbf_ref, seq, seq_rows(z_ref[:, A_BV:A_BV + 128], seq).astype(F32))
        put(vcf_ref, seq, seq_rows(z_ref[:, A_CV:A_CV + 512], seq).astype(F32))
        put(kbf_ref, seq, seq_rows(kb_n[0], seq))
        for p in range(NA_HEADS // 2):
            put(kcf_ref, seq, seq_rows(kc_n[p], seq), slice(128 * p, 128 * (p + 1)))

    kb = kb_n[0].astype(BF16)
    kc = [t.astype(BF16) for t in kc_n]
    qb_t = [q_t(tile) for tile in qb_n]
    qc_t = [q_t(tile) for tile in qc_n]
    scores = []
    for seq in range(nseq):
        for kvh in range(SWA_KV_HEADS):
            placed = []
            for h in range(SWA_GROUP * kvh, SWA_GROUP * (kvh + 1)):
                rows = seq_cols(qb_t[h // 2][HEAD_DIM * (h % 2):HEAD_DIM * (h % 2 + 1)], seq)
                z = jnp.zeros_like(rows)
                placed.append(jnp.concatenate([rows, z] if kvh == 0 else [z, rows], axis=0))
            scores.append(_dot(seq_rows(kb, seq), jnp.concatenate(placed, axis=1)))
        for p in range(NA_HEADS // 2):
            q = seq_cols(qc_t[p], seq)
            q2 = jnp.concatenate([_half_rows(q, 0), _half_rows(q, 1)], axis=1)
            scores.append(_dot(seq_rows(kc[p], seq), q2))
    s_t = jnp.concatenate(scores, axis=1)

    sink = jnp.concatenate(
        ([jnp.full((1, n), sink_ref[layer, h] * LOG2E, F32) for h in range(SWA_Q_HEADS)]
         + [jnp.full((1, n * NA_HEADS), -jnp.inf, F32)]) * nseq, axis=1)
    mx = jnp.maximum(s_t.max(axis=0, keepdims=True), sink)
    p_t = jnp.exp2(s_t - mx)
    inv = 1.0 / (p_t.sum(axis=0, keepdims=True) + jnp.exp2(sink - mx))
    p_t = p_t.astype(BF16)

    nb = SWA_Q_HEADS * n
    per_seq = (SWA_Q_HEADS + NA_HEADS) * n
    vb_t = v_t(A_BV, 0)
    vc_t = [v_t(A_CV, p) for p in range(NA_HEADS // 2)]
    for seq in range(nseq):
        c0 = per_seq * seq
        o = _dot(seq_cols(vb_t, seq), p_t[:, c0:c0 + nb]) * inv[:, c0:c0 + nb]
        outs = [o[HEAD_DIM * (h // SWA_GROUP):HEAD_DIM * (h // SWA_GROUP + 1), n * h:n * (h + 1)]
                for h in range(SWA_Q_HEADS)]
        ob_ref[n * seq:n * (seq + 1), :] = jnp.concatenate(outs, axis=0).T.astype(BF16)
        outs = []
        for p in range(NA_HEADS // 2):
            cols = slice(c0 + nb + 2 * n * p, c0 + nb + 2 * n * (p + 1))
            o = _dot(seq_cols(vc_t[p], seq), p_t[:, cols]) * inv[:, cols]
            outs += [o[:HEAD_DIM, :n], o[HEAD_DIM:, n:]]
        oc_ref[n * seq:n * (seq + 1), :] = jnp.concatenate(outs, axis=0).T.astype(BF16)


def _ctx_attention(sink, z, wqb, wkb, wqc, wkc, caches, l):
    nseq = CTX_SEQS_PER_STEP
    blk = lambda w, j=0: pl.BlockSpec((nseq * CTX_LEN, w), lambda b: (b, j))
    cst = lambda w: pl.BlockSpec((1, w), lambda b: (0, 0))
    sds = lambda w: jax.ShapeDtypeStruct((T_CTX, w), BF16)
    cache_w = (128, 128, 512, 512)
    if caches is None:
        cache_blk = [pl.BlockSpec((nseq, DEPTH, CTX_LEN, w), lambda b: (b, 0, 0, 0)) for w in cache_w]
    else:
        cache_blk = [pl.BlockSpec((nseq, None, CTX_LEN, w), lambda b: (b, l, 0, 0)) for w in cache_w]
    cache_sds = [jax.ShapeDtypeStruct((N_CTX_SEQ, DEPTH, CTX_LEN, w), F32) for w in cache_w]
    carried = [] if caches is None else list(caches)
    n_in = 6
    return pl.pallas_call(
        functools.partial(_ctx_attn_kernel, layer=l),
        grid=(N_CTX_SEQ // nseq,),
        in_specs=[pl.BlockSpec(memory_space=pltpu.SMEM), blk(ATT_W, Z_ATT // ATT_W),
                  cst(512), cst(128), cst(512), cst(512)]
                 + [pl.BlockSpec(memory_space=pl.ANY)] * len(carried),
        out_specs=[blk(512), blk(512)] + cache_blk,
        out_shape=[sds(512), sds(512)] + cache_sds,
        input_output_aliases={n_in + j: 2 + j for j in range(len(carried))},
        compiler_params=_cparams(("arbitrary",)),
        name="ctx_attention",
    )(sink, z, wqb, wkb, wqc, wkc, *carried)


SWA_QBLK = 128
SWA_SPAN = SWA_QBLK + 2 * SWA_WINDOW
SWA_BLOCKS_PER_STEP = 4


def _swa_kernel(sink_ref, qt_ref, k_ref, vt_ref, kc_ref, vc_ref, o_ref, kctx_scr, vctx_scr, *, layer):
    step = pl.program_id(1)
    nq = SWA_QBLK

    @pl.when(step == 0)
    def _():
        kctx_scr[...] = kc_ref[...].T.astype(BF16)
        vctx_scr[...] = vc_ref[...].astype(BF16)

    kctx = kctx_scr[...]
    starts, s_lat, s_ctx = [], [], []
    for blk in range(SWA_BLOCKS_PER_STEP):
        i = step * SWA_BLOCKS_PER_STEP + blk
        start = pl.multiple_of(jnp.clip(nq * (i - 1), 0, LAT_LEN - SWA_SPAN), nq)
        starts.append(start)
        kwin = k_ref[pl.ds(start, SWA_SPAN), :]
        kpos = start + lax.broadcasted_iota(jnp.int32, (SWA_SPAN, nq), 0)
        qpos = nq * i + lax.broadcasted_iota(jnp.int32, (SWA_SPAN, nq), 1)
        band = jnp.where(jnp.abs(kpos - qpos) <= SWA_WINDOW, 0.0, -jnp.inf)
        for kvh in range(SWA_KV_HEADS):
            placed = []
            for h in range(SWA_GROUP * kvh, SWA_GROUP * (kvh + 1)):
                rows = qt_ref[HEAD_DIM * h:HEAD_DIM * (h + 1), nq * blk:nq * (blk + 1)]
                z = jnp.zeros_like(rows)
                placed.append(jnp.concatenate([rows, z] if kvh == 0 else [z, rows], axis=0))
            q4 = jnp.concatenate(placed, axis=1)
            s_lat.append(_dot(kwin, q4) + jnp.concatenate([band] * SWA_GROUP, axis=1))
            s_ctx.append(_dot(kctx, q4))
    s_lat = jnp.concatenate(s_lat, axis=1)
    s_ctx = jnp.concatenate(s_ctx, axis=1)

    sink = jnp.concatenate(
        [jnp.full((1, nq), sink_ref[layer, h] * LOG2E, F32) for h in range(SWA_Q_HEADS)]
        * SWA_BLOCKS_PER_STEP, axis=1)
    mx = jnp.maximum(jnp.maximum(s_lat.max(axis=0, keepdims=True),
                                 s_ctx.max(axis=0, keepdims=True)), sink)
    p_lat = jnp.exp2(s_lat - mx)
    p_ctx = jnp.exp2(s_ctx - mx)
    inv = 1.0 / (p_lat.sum(axis=0, keepdims=True) + p_ctx.sum(axis=0, keepdims=True)
                 + jnp.exp2(sink - mx))
    p_lat = p_lat.astype(BF16)
    o_ctx = _dot(vctx_scr[...], p_ctx.astype(BF16))
    width = SWA_Q_HEADS * nq
    for blk, start in enumerate(starts):
        cols = slice(width * blk, width * (blk + 1))
        vwin_t = vt_ref[:, pl.ds(start, SWA_SPAN)]
        o = (_dot(vwin_t, p_lat[:, cols]) + o_ctx[:, cols]) * inv[:, cols]
        outs = [o[HEAD_DIM * (h // SWA_GROUP):HEAD_DIM * (h // SWA_GROUP + 1), nq * h:nq * (h + 1)]
                for h in range(SWA_Q_HEADS)]
        o_ref[nq * blk:nq * (blk + 1), :] = jnp.concatenate(outs, axis=0).T.astype(BF16)


def _swa_attention(sink, qb_t, kb, vb_t, cache_k, cache_v, l):
    rows = SWA_QBLK * SWA_BLOCKS_PER_STEP
    steps = LAT_LEN // rows
    cache = pl.BlockSpec((None, None, 128, PAST_LEN), lambda b, i: (b, l, 0, 0))
    return pl.pallas_call(
        functools.partial(_swa_kernel, layer=l),
        grid=(N_LAT_SEQ, steps),
        in_specs=[pl.BlockSpec(memory_space=pltpu.SMEM),
                  pl.BlockSpec((512, rows), lambda b, i: (0, b * steps + i)),
                  pl.BlockSpec((LAT_LEN, 128), lambda b, i: (b, 0)),
                  pl.BlockSpec((128, LAT_LEN), lambda b, i: (0, b)),
                  cache, cache],
        out_specs=pl.BlockSpec((rows, 512), lambda b, i: (b * steps + i, 0)),
        out_shape=jax.ShapeDtypeStruct((T_LAT, 512), BF16),
        scratch_shapes=[pltpu.VMEM((PAST_LEN, 128), BF16), pltpu.VMEM((128, PAST_LEN), BF16)],
        compiler_params=_cparams(("arbitrary", "arbitrary")),
        name="swa_attention",
    )(sink, qb_t, kb, vb_t, cache_k, cache_v)


LAT_ROWS = LAT_LEN // GRID_W
NA_WIN_ROWS = min(NA_ROWS, LAT_ROWS)
NA_KEYS = NA_WIN_ROWS * GRID_W
NA_ROWS_PER_STEP = 8


def _na_kernel(q_ref, k_ref, v_ref, kc_ref, vc_ref, rp_ref, o_ref, kctx_scr, vctx_scr, bias_ref):
    step = pl.program_id(1)
    low = lax.broadcasted_iota(jnp.int32, (1, 128), 1) < HEAD_DIM

    @pl.when(step == 0)
    def _():
        kctx_scr[...] = kc_ref[...].astype(BF16)
        vctx_scr[...] = vc_ref[...].astype(BF16)
        cq = lax.broadcasted_iota(jnp.int32, (GRID_W, 128), 0)
        ck = lax.broadcasted_iota(jnp.int32, (GRID_W, 128), 1) % GRID_W
        cs = jnp.clip(cq - NA_COLS // 2, 0, GRID_W - NA_COLS)
        window = jnp.where((ck >= cs) & (ck < cs + NA_COLS), 0.0, -jnp.inf)
        for h in range(NA_HEADS):
            rows = [jnp.broadcast_to(rp_ref[h, dr:dr + 1, :], (GRID_W, 128))
                    for dr in range(2 * NA_ROWS - 1)]
            left = [pltpu.roll(x, 0, 1, stride=1, stride_axis=0) for x in rows[:-1]]
            right = [pltpu.roll(x, GRID_W, 1, stride=1, stride_axis=0) for x in rows[1:]]
            for j in range(2 * NA_ROWS - 2):
                bias_ref[h, j] = jnp.where(low, left[j], right[j]) * LOG2E + window

    npair = NA_HEADS // 2
    windows = []
    s_lat, s_ctx = [], []
    for rr in range(NA_ROWS_PER_STEP):
        r = step * NA_ROWS_PER_STEP + rr
        rs = jnp.clip(r - NA_ROWS // 2, 0, LAT_ROWS - NA_WIN_ROWS)
        start = pl.multiple_of(rs * GRID_W, GRID_W)
        base = rs - r + NA_ROWS - 1
        windows.append(start)
        for p in range(npair):
            sl = slice(128 * p, 128 * (p + 1))
            q = q_ref[GRID_W * rr:GRID_W * (rr + 1), sl]
            zero = jnp.zeros_like(q)
            q2 = jnp.concatenate([jnp.where(low, q, zero), jnp.where(low, zero, q)], axis=0)
            bias = jnp.concatenate(
                [jnp.concatenate([bias_ref[2 * p + hh, base + 2 * w] for w in range(NA_WIN_ROWS // 2)],
                                 axis=1) for hh in range(2)], axis=0)
            s_lat.append(_nt_dot(q2, k_ref[pl.ds(start, NA_KEYS), sl]) + bias)
            s_ctx.append(_dot(q2, kctx_scr[sl, :]))
    s_lat = jnp.concatenate(s_lat, axis=0)
    s_ctx = jnp.concatenate(s_ctx, axis=0)
    mx = jnp.maximum(s_lat.max(axis=-1, keepdims=True), s_ctx.max(axis=-1, keepdims=True))
    p_lat = jnp.exp2(s_lat - mx)
    p_ctx = jnp.exp2(s_ctx - mx)
    inv = 1.0 / (p_lat.sum(axis=-1, keepdims=True) + p_ctx.sum(axis=-1, keepdims=True))
    p_lat = p_lat.astype(BF16)
    p_ctx = p_ctx.astype(BF16)
    for rr, start in enumerate(windows):
        tiles = []
        for p in range(npair):
            sl = slice(128 * p, 128 * (p + 1))
            first = 2 * GRID_W * (npair * rr + p)
            rows = slice(first, first + 2 * GRID_W)
            o2 = (_dot(p_lat[rows], v_ref[pl.ds(start, NA_KEYS), sl])
                  + _nt_dot(p_ctx[rows], vctx_scr[sl, :])) * inv[rows]
            tiles.append(jnp.where(low, o2[:GRID_W], o2[GRID_W:]))
        o_ref[GRID_W * rr:GRID_W * (rr + 1), :] = jnp.concatenate(tiles, axis=1).astype(BF16)


def _na_attention(qc, kc, vc, cache_k, cache_v, rp_cyc, l):
    steps = LAT_ROWS // NA_ROWS_PER_STEP
    rows = NA_ROWS_PER_STEP * GRID_W
    kv = pl.BlockSpec((LAT_LEN, 512), lambda b, r: (b, 0))
    cache = pl.BlockSpec((None, None, 512, PAST_LEN), lambda b, r: (b, l, 0, 0))
    return pl.pallas_call(
        _na_kernel,
        grid=(N_LAT_SEQ, steps),
        in_specs=[pl.BlockSpec((rows, 512), lambda b, r: (b * steps + r, 0)),
                  kv, kv, cache, cache,
                  pl.BlockSpec((None, NA_HEADS, 2 * NA_ROWS - 1, 128), lambda b, r: (l, 0, 0, 0))],
        out_specs=pl.BlockSpec((rows, 512), lambda b, r: (b * steps + r, 0)),
        out_shape=jax.ShapeDtypeStruct((T_LAT, 512), BF16),
        scratch_shapes=[pltpu.VMEM((512, PAST_LEN), BF16), pltpu.VMEM((512, PAST_LEN), BF16),
                        pltpu.VMEM((NA_HEADS, 2 * NA_ROWS - 2, GRID_W, 128), F32)],
        compiler_params=_cparams(("arbitrary", "arbitrary")),
        name="na_attention",
    )(qc, kc, vc, cache_k, cache_v, rp_cyc)


def _na_table_rows(rpb):
    pad = jnp.zeros(rpb.shape[:-1] + (128 - (2 * NA_COLS - 1),), F32)
    return jnp.concatenate([rpb[..., NA_COLS - 1:], pad, rpb[..., :NA_COLS - 1]], axis=-1)


def _gla_constants():
    c = GLA_C
    t = np.arange(c)[:, None]
    u = np.arange(c)[None, :]
    blocks = []
    for k in range(GLA_LEVELS):
        b = 1 << k
        m = ((t >> k) | 1) * b - 1
        query = ((t >> k) & 1) == 1
        blocks.append(np.where(query, (u > m) & (u <= t), (u > t) & (u <= m)))
    blocks.append(u <= t)
    blocks.append(u > t)
    blocks.append(np.ones((16, c), bool))
    fwd = np.concatenate(blocks, axis=0).astype(np.float32)
    bwd_blocks = [blk[::-1, ::-1] for blk in blocks]
    bwd = np.concatenate(bwd_blocks, axis=0).astype(np.float32)
    seg = np.stack([fwd, bwd])
    seg = np.concatenate([seg, seg], axis=-1)

    s = np.arange(c)[None, :]
    x = t ^ s
    lev = np.where(x == 0, GLA_LEVELS, np.floor(np.log2(np.maximum(x, 1))).astype(np.int64))
    lev_f = np.where(s <= t, lev, -1)
    lev_b = np.where(s >= t, lev, -1)
    levmap = np.stack([np.tile(lev_f, (1, GLA_HEADS)), np.tile(lev_b, (1, GLA_HEADS))])
    return seg, levmap.astype(np.int32)


def _gla_chunks(chains):
    c = GLA_C
    lane_head = lax.broadcasted_iota(jnp.int32, (1, GLA_HEADS * GLA_DK), 1) // GLA_DK
    row_head = lax.broadcasted_iota(jnp.int32, (GLA_HEADS * GLA_DV, 1), 0) // GLA_DV
    head_sel = [jnp.where(lane_head == h, 1.0, 0.0).astype(BF16) for h in range(GLA_HEADS)]

    def operands(q, k, e=None):
        if e is not None:
            q, k = q * e, k * e
        return q, jnp.concatenate([k * head_sel[h] for h in range(GLA_HEADS)], axis=0)

    segs = []
    for (_, _, _, lr_ref, w2_ref, ba_ref, seg_ref, *_) in chains:
        x = _dot(lr_ref[...], w2_ref[...]) + ba_ref[...]
        la = (jnp.minimum(x, 0.0) - jnp.log(1.0 + jnp.exp(-jnp.abs(x)))) * (1.0 / GLA_TAU)
        la_hi = la.astype(BF16)
        la_lo = (la - la_hi.astype(F32)).astype(BF16)
        segs.append(_dot(seg_ref[...], jnp.concatenate([la_hi, la_lo], axis=0)))

    def decay(seg, block):
        return jnp.exp(seg[c * block:c * (block + 1)]).astype(BF16)

    qs = [ch[0][...] * (GLA_DK ** -0.5) for ch in chains]
    ks = [ch[1][...] for ch in chains]
    order = [GLA_LEVELS] + list(range(GLA_LEVELS))
    prepare = lambda lvl: [operands(qs[n], ks[n], None if lvl == GLA_LEVELS else decay(segs[n], lvl))
                           for n in range(len(chains))]
    attn = [None] * len(chains)
    ops = prepare(order[0])
    for i, lvl in enumerate(order):
        scores = [_nt_dot(q, kbd) for q, kbd in ops]
        if i + 1 < len(order):
            ops = prepare(order[i + 1])
        for n, ch in enumerate(chains):
            attn[n] = jnp.where(ch[7][...] == lvl, scores[n], 0.0 if attn[n] is None else attn[n])

    new_states = []
    for n, (_, _, v_ref, _, _, _, _, _, o_ref, st_ref) in enumerate(chains):
        seg = segs[n]
        v = v_ref[...]
        a = attn[n].astype(BF16)
        tot = seg[c * (GLA_LEVELS + 2):c * (GLA_LEVELS + 2) + 1]
        q_in = qs[n] * decay(seg, GLA_LEVELS)
        k_in = ks[n] * decay(seg, GLA_LEVELS + 1)
        state = st_ref[...]
        o_inter = _nt_dot(q_in, state.astype(BF16))
        for h in range(GLA_HEADS):
            sl = slice(GLA_DV * h, GLA_DV * (h + 1))
            o_ref[:, sl] = o_inter[:, sl] + _dot(a[:, c * h:c * (h + 1)], v[:, sl])
        upd = _tn_dot(v, k_in)
        new_state = state * jnp.exp(tot) + jnp.where(row_head == lane_head, upd, 0.0)
        st_ref[...] = new_state
        new_states.append(new_state)
    return new_states


GLA_GROUPS = 3
GLA_GROUP_ROWS = T_ALL // GLA_GROUPS
GLA_CTX_GROUPS = T_CTX // GLA_GROUP_ROWS
GLA_STEPS = GLA_GROUP_ROWS // GLA_C
GLA_CTX_CHUNKS = CTX_LEN // GLA_C
GLA_LAT_CHUNKS = LAT_LEN // GLA_C


def _gla_kernel(qf, kf, vf, lrf, qb, kb, vb, lrb, w2_ref, ba_ref, seg_ref, lev_ref, s0f, s0b,
                of_ref, ob_ref, sff_ref, sfb_ref, states):
    step = pl.program_id(0)

    @pl.when(step % GLA_CTX_CHUNKS == 0)
    def _():
        for d in range(2):
            for g in range(GLA_CTX_GROUPS):
                states[d, g] = jnp.zeros(states.shape[2:], F32)

    @pl.when(step % GLA_LAT_CHUNKS == 0)
    def _():
        for d, s0 in enumerate((s0f, s0b)):
            for h in range(GLA_HEADS):
                row = [jnp.zeros((GLA_DV, GLA_DK), F32)] * GLA_HEADS
                row[h] = s0[h]
                states[d, GLA_GROUPS - 1, GLA_DV * h:GLA_DV * (h + 1), :] = jnp.concatenate(row, axis=1)

    chains = []
    for g in range(GLA_GROUPS):
        chains.append((qf.at[g], kf.at[g], vf.at[g], lrf.at[g], w2_ref.at[0], ba_ref.at[0],
                       seg_ref.at[0], lev_ref.at[0], of_ref.at[g], states.at[0, g]))
        chains.append((qb.at[g], kb.at[g], vb.at[g], lrb.at[g], w2_ref.at[1], ba_ref.at[1],
                       seg_ref.at[1], lev_ref.at[1], ob_ref.at[g], states.at[1, g]))
    new_states = _gla_chunks(chains)

    @pl.when(step % GLA_CTX_CHUNKS == GLA_CTX_CHUNKS - 1)
    def _():
        for g in range(GLA_CTX_GROUPS):
            for d, out in enumerate((sff_ref, sfb_ref)):
                st = new_states[2 * g + d]
                for h in range(GLA_HEADS):
                    out[g, h] = st[GLA_DV * h:GLA_DV * (h + 1), GLA_DK * h:GLA_DK * (h + 1)].T


def _gla(z, w2p, ba, seg, levmap, s0t, l):
    c = GLA_C
    z3 = z.reshape(GLA_GROUPS, GLA_GROUP_ROWS, Z_W)
    fwd = lambda s: s
    bwd = lambda s: GLA_STEPS - 1 - s
    seqs_per_group = GLA_GROUP_ROWS // CTX_LEN

    def chunk_specs(pos):
        return [pl.BlockSpec((GLA_GROUPS, c, 256), lambda s: (0, pos(s), Z_AQ // 256)),
                pl.BlockSpec((GLA_GROUPS, c, 256), lambda s: (0, pos(s), Z_AK // 256)),
                pl.BlockSpec((GLA_GROUPS, c, 512), lambda s: (0, pos(s), Z_AV // 512)),
                pl.BlockSpec((GLA_GROUPS, c, 128), lambda s: (0, pos(s), Z_LR // 128))]

    def s0_spec(d, pos):
        return pl.BlockSpec((None, None, None, GLA_HEADS, GLA_DV, GLA_DK),
                            lambda s: (pos(s) // GLA_LAT_CHUNKS, l, d, 0, 0, 0))

    def out_spec(pos):
        return pl.BlockSpec((GLA_GROUPS, c, 512), lambda s: (0, pos(s), 0))

    def sfin_spec(pos):
        return pl.BlockSpec((GLA_CTX_GROUPS, None, GLA_HEADS, GLA_DK, GLA_DV),
                            lambda s: (0, pos(s) // GLA_CTX_CHUNKS, 0, 0, 0))

    state_sds = jax.ShapeDtypeStruct((GLA_CTX_GROUPS, seqs_per_group, GLA_HEADS, GLA_DK, GLA_DV), F32)
    out_sds = jax.ShapeDtypeStruct((GLA_GROUPS, GLA_GROUP_ROWS, 512), F32)
    o_f, o_b, sf, sb = pl.pallas_call(
        _gla_kernel,
        grid=(GLA_STEPS,),
        in_specs=chunk_specs(fwd) + chunk_specs(bwd) + [
            pl.BlockSpec((None, 2, 128, 256), lambda s: (l, 0, 0, 0)),
            pl.BlockSpec((None, 2, 1, 256), lambda s: (l, 0, 0, 0)),
            pl.BlockSpec((2, GLA_GROWS, 2 * c), lambda s: (0, 0, 0)),
            pl.BlockSpec((2, c, GLA_HEADS * c), lambda s: (0, 0, 0)),
            s0_spec(0, fwd), s0_spec(1, bwd),
        ],
        out_specs=[out_spec(fwd), out_spec(bwd), sfin_spec(fwd), sfin_spec(bwd)],
        out_shape=[out_sds, out_sds, state_sds, state_sds],
        scratch_shapes=[pltpu.VMEM((2, GLA_GROUPS, GLA_HEADS * GLA_DV, GLA_HEADS * GLA_DK), F32)],
        compiler_params=_cparams(("arbitrary",)),
        name="gla",
    )(z3, z3, z3, z3, z3, z3, z3, z3, w2p, ba, seg, levmap, s0t, s0t)
    flat = lambda a: a.reshape((-1,) + a.shape[2:])
    return flat(o_f), flat(o_b), flat(sf), flat(sb)


def _cast_kernel(*refs):
    n = len(refs) // 2
    for src, dst in zip(refs[:n], refs[n:]):
        dst[...] = src[...].astype(dst.dtype)


def _cast_bf16(*weights):
    spec = lambda w: pl.BlockSpec((None,) + w.shape[1:], lambda l: (l, 0, 0))
    return pl.pallas_call(
        _cast_kernel,
        grid=(DEPTH,),
        in_specs=[spec(w) for w in weights],
        out_specs=[spec(w) for w in weights],
        out_shape=[jax.ShapeDtypeStruct(w.shape, BF16) for w in weights],
        compiler_params=_cparams(("arbitrary",)),
        name="cast_bf16",
    )(*weights)


def _post_mixer_kernel(*refs, nk, n_x, n_y):
    x_refs, refs = refs[:n_x], refs[n_x:]
    (of_ref, obk_ref, ar_ref, gates_ref, bc_ref, bl_ref, cc_ref, cl_ref, wpa_ref, wpb_ref, wpc_ref,
     wo_ref, gn_ref, g2_ref, mod_ref, w1_ref, w2_ref) = refs[:17]
    y_refs = refs[17:17 + n_y]
    w1_scr, w2_scr, h_scr, x1_scr, acc_scr = refs[17 + n_y:]
    s = pl.program_id(0)
    d = D_MODEL
    tm = of_ref.shape[0]
    th = w1_ref.shape[1]
    ctx = jnp.maximum(s - (nk - 1), 0) < T_CTX // tm

    def put_y(val):
        if n_y == 1:
            y_refs[0][...] = val
        else:
            @pl.when(ctx)
            def _():
                y_refs[0][...] = val

            @pl.when(jnp.logical_not(ctx))
            def _():
                y_refs[1][...] = val

    def merged_residual():
        o = of_ref[...] + obk_ref[...]
        heads = []
        for h in range(GLA_HEADS):
            oh = o[:, GLA_DV * h:GLA_DV * (h + 1)]
            ms = jnp.mean(oh * oh, axis=-1, keepdims=True)
            heads.append(oh * lax.rsqrt(ms + EPS) * gn_ref[...])
        oa = (jnp.concatenate(heads, axis=1) * _silu(ar_ref[...].astype(F32))).astype(BF16)
        ob = jnp.where(ctx, bc_ref[...], bl_ref[...])
        oc = jnp.where(ctx, cc_ref[...], cl_ref[...])
        gate = lambda j: _sigmoid(gates_ref[:, j * d:(j + 1) * d].astype(F32))
        merged = (gate(0) * _dot(oa, wpa_ref[...]) + gate(1) * _dot(ob, wpb_ref[...])
                  + gate(2) * _dot(oc, wpc_ref[...]))
        return (_token_rows(x_refs, ctx)
                + mod_ref[:, 2 * d:3 * d] * _dot(merged.astype(BF16), wo_ref[...]))

    def normed(x1):
        return _norm_mod(x1, g2_ref[...], mod_ref[:, 3 * d:4 * d], mod_ref[:, 4 * d:5 * d]).astype(BF16)

    def act(u):
        return jnp.square(jnp.maximum(u, 0.0)).astype(BF16)

    @pl.when(s == 0)
    def _():
        x1 = merged_residual()
        x1_scr[...] = x1
        h_scr[...] = normed(x1)

    @pl.when(s < nk)
    def _():
        col = pl.multiple_of(s * th, th)
        w1 = w1_ref[...].astype(BF16)
        w2 = w2_ref[...].astype(BF16)
        w1_scr[:, pl.ds(col, th)] = w1
        w2_scr[pl.ds(col, th), :] = w2
        part = _dot(act(_dot(h_scr[...], w1)), w2)

        @pl.when(s == 0)
        def _():
            acc_scr[...] = part

        @pl.when(s > 0)
        def _():
            acc_scr[...] += part

        @pl.when(s == nk - 1)
        def _():
            put_y(x1_scr[...] + mod_ref[:, 5 * d:6 * d] * acc_scr[...])

    @pl.when(s >= nk)
    def _():
        x1 = merged_residual()
        out = _dot(act(_dot(normed(x1), w1_scr[...])), w2_scr[...])
        put_y(x1 + mod_ref[:, 5 * d:6 * d] * out)


def _post_mixer(x_parts, o_fwd, o_bwd, z, ob_ctx, ob_lat, oc_ctx, oc_lat, wpa, wpb, wpc, wo, gn, norm2_w,
                mod4, w1, w2, l, split_out):
    tm, th = 256, 512
    nk = MLP_HIDDEN // th
    ctx_blocks = T_CTX // tm
    row = lambda s: jnp.maximum(s - (nk - 1), 0)
    chunk = lambda s: jnp.minimum(s, nk - 1)
    ctx_idx = lambda s: (jnp.minimum(row(s), ctx_blocks - 1), 0)
    lat_idx = lambda s: (jnp.maximum(row(s) - ctx_blocks, 0), 0)
    rows = lambda w, j=0: pl.BlockSpec((tm, w), lambda s: (row(s), j))
    resident = lambda k: pl.BlockSpec((None, k, D_MODEL), lambda s: (l, 0, 0),
                                      pipeline_mode=pl.Buffered(1))
    if split_out:
        out_specs = [pl.BlockSpec((tm, D_MODEL), ctx_idx), pl.BlockSpec((tm, D_MODEL), lat_idx)]
        out_shape = [jax.ShapeDtypeStruct((T_CTX, D_MODEL), F32), jax.ShapeDtypeStruct((T_LAT, D_MODEL), F32)]
    else:
        out_specs = [rows(D_MODEL)]
        out_shape = [jax.ShapeDtypeStruct((T_ALL, D_MODEL), F32)]
    return pl.pallas_call(
        functools.partial(_post_mixer_kernel, nk=nk, n_x=len(x_parts), n_y=len(out_specs)),
        grid=(nk - 1 + T_ALL // tm,),
        in_specs=_token_specs(x_parts, tm, row) + [
            rows(512), rows(512), rows(512, Z_AR // 512), rows(3 * D_MODEL),
            pl.BlockSpec((tm, 512), ctx_idx), pl.BlockSpec((tm, 512), lat_idx),
            pl.BlockSpec((tm, 512), ctx_idx), pl.BlockSpec((tm, 512), lat_idx),
            resident(512), resident(512), resident(512), resident(D_MODEL),
            pl.BlockSpec((None, 1, GLA_DV), lambda s: (l, 0, 0)),
            pl.BlockSpec((None, 1, D_MODEL), lambda s: (l, 0, 0)),
            pl.BlockSpec((None, None, 1, 6 * D_MODEL),
                         lambda s: (l, _group_of_rows(row(s), tm), 0, 0)),
            pl.BlockSpec((None, D_MODEL, th), lambda s: (l, 0, chunk(s))),
            pl.BlockSpec((None, th, D_MODEL), lambda s: (l, chunk(s), 0)),
        ],
        out_specs=out_specs,
        out_shape=out_shape,
        scratch_shapes=[pltpu.VMEM((D_MODEL, MLP_HIDDEN), BF16), pltpu.VMEM((MLP_HIDDEN, D_MODEL), BF16),
                        pltpu.VMEM((tm, D_MODEL), BF16), pltpu.VMEM((tm, D_MODEL), F32),
                        pltpu.VMEM((tm, D_MODEL), F32)],
        compiler_params=_cparams(("arbitrary",)),
        name="post_mixer",
    )(*x_parts, o_fwd, o_bwd, z, z, ob_ctx, ob_lat, oc_ctx, oc_lat, wpa, wpb, wpc, wo, gn, norm2_w, mod4,
      w1, w2)


def _rope_tables():
    t = jnp.arange(LAT_LEN)
    row = (t // GRID_W).astype(F32)
    col = (t % GRID_W).astype(F32)
    nf = HEAD_DIM // 4
    inv_freq = ROPE_BASE ** (-jnp.arange(nf, dtype=F32) / nf)
    ang_r = row[:, None] * inv_freq[None, :]
    ang_c = col[:, None] * inv_freq[None, :]
    cos = jnp.concatenate([jnp.cos(ang_r)] * 2 + [jnp.cos(ang_c)] * 2, axis=1)
    sin = jnp.concatenate([-jnp.sin(ang_r), jnp.sin(ang_r), -jnp.sin(ang_c), jnp.sin(ang_c)], axis=1)
    return jnp.tile(cos, (1, 2)), jnp.tile(sin, (1, 2))


def kernel(x_prompt, x_sample, state_gla, cache_swa_k, cache_swa_v, cache_na_k, cache_na_v, c,
           c_ctx, w_mod, b_mod, norm1, norm2, w_in, w_a2_f, b_a_f, w_a2_b, b_a_b, gla_onorm,
           qn_swa, kn_swa, sink_swa, qn_na, kn_na, rpb_na, w_pa, w_pb, w_pc, w_o, w_fc1, w_fc2):
    d = D_MODEL
    x_parts = (x_prompt.reshape(T_CTX, d), x_sample.reshape(T_LAT, d))

    cond8 = jnp.zeros((8, d), F32).at[0].set(c_ctx).at[1:1 + N_LAT_SEQ].set(c)
    mod4 = _modulation(cond8, w_mod, b_mod).reshape(DEPTH, 8, 1, 6 * d)

    w_in_p = _w_in_layout(jnp.swapaxes(w_in, 1, 2))
    wpa, wpb, wpc, wo = _cast_bf16(w_pa, w_pb, w_pc, w_o)
    norm1r = norm1.reshape(DEPTH, 1, d)
    norm2r = norm2.reshape(DEPTH, 1, d)
    gnr = gla_onorm.reshape(DEPTH, 1, GLA_DV)
    w2p = jnp.zeros((DEPTH, 2, 128, GLA_HEADS * GLA_DK), F32)
    w2p = w2p.at[:, 0, 0:GLA_LOWRANK].set(w_a2_f).at[:, 1, GLA_LOWRANK:2 * GLA_LOWRANK].set(w_a2_b)
    w2p = w2p.astype(BF16)
    ba = jnp.stack([b_a_f, b_a_b], axis=1).reshape(DEPTH, 2, 1, GLA_HEADS * GLA_DK)
    seg_np, lev_np = _gla_constants()
    seg = jnp.asarray(seg_np, BF16)
    levmap = jnp.asarray(lev_np)
    s0t = jnp.swapaxes(state_gla, -1, -2)
    cos_t, sin_t = _rope_tables()
    rp_cyc = _na_table_rows(rpb_na)
    feat_major = lambda a: a.transpose(0, 1, 3, 4, 2).reshape(N_LAT_SEQ, DEPTH, -1, PAST_LEN)
    csk, csv, cnk, cnv = (feat_major(a) for a in (cache_swa_k, cache_swa_v, cache_na_k, cache_na_v))

    st_l = []
    caches = None
    for l in range(DEPTH):
        z = _inproj(x_parts, norm1r, mod4, w_in_p, l)
        wqb = jnp.tile(qn_swa[l], 8)[None, :]
        wkb = jnp.tile(kn_swa[l], 2)[None, :]
        wqc = jnp.tile(qn_na[l], 8)[None, :]
        wkc = jnp.tile(kn_na[l], 8)[None, :]
        qb, kb, vb, qc, kc, vc = _attn_prep(z, wqb, wkb, wqc, wkc, cos_t, sin_t)
        o_fwd, o_bwd, sfin_f, sfin_b = _gla(z, w2p, ba, seg, levmap, s0t, l)
        ob_ctx, oc_ctx, *caches = _ctx_attention(sink_swa, z, wqb, wkb, wqc, wkc, caches, l)
        ob_lat = _swa_attention(sink_swa, qb, kb, vb, csk, csv, l)
        oc_lat = _na_attention(qc, kc, vc, cnk, cnv, rp_cyc, l)
        x_parts = _post_mixer(x_parts, o_fwd, o_bwd, z, ob_ctx, ob_lat, oc_ctx, oc_lat, wpa, wpb, wpc, wo,
                              gnr, norm2r, mod4, w_fc1, w_fc2, l, split_out=(l == DEPTH - 1))

        st_l += [sfin_f, sfin_b]

    y_prompt = x_parts[0].reshape(N_CTX_SEQ, CTX_LEN, d)
    y_sample = x_parts[1].reshape(N_LAT_SEQ, LAT_LEN, d)
    new_state = jnp.stack(st_l, axis=1).reshape(
        N_CTX_SEQ, DEPTH, 2, GLA_HEADS, GLA_DK, GLA_DV)

    swa_k, swa_v, na_k, na_v = caches
    kv_shape = lambda heads: (N_CTX_SEQ, DEPTH, CTX_LEN, heads, HEAD_DIM)
    return (y_prompt, y_sample, new_state,
            swa_k.reshape(kv_shape(SWA_KV_HEADS)), swa_v.reshape(kv_shape(SWA_KV_HEADS)),
            na_k.reshape(kv_shape(NA_HEADS)), na_v.reshape(kv_shape(NA_HEADS)))
```
